```python
import jax, jax.numpy as jnp
from jax import lax
import numpy as np

D_MODEL = 1024
BATCH = 32
SEQ = 2048
DEPTH = 4
DEC_BATCH = 8
DEC_SEQ = 32
PAST_LEN = 2048

CHUNK = 64
N_A = DEPTH // 2
N_B = DEPTH - N_A
GLA_HEADS = 4
GLA_DK = D_MODEL // (2 * GLA_HEADS)
GLA_DV = D_MODEL // GLA_HEADS
GLA_QK = GLA_HEADS * GLA_DK
GLA_V = GLA_HEADS * GLA_DV
GLA_GATE_RANK = 16
GLA_GATE_TEMP = 16.0
SWA_HEAD_DIM = 64
SWA_Q_HEADS = D_MODEL // SWA_HEAD_DIM
SWA_KV_HEADS = 4
SWA_GROUP = SWA_Q_HEADS // SWA_KV_HEADS
WINDOW = 128
WINDOW_CHUNKS = WINDOW // CHUNK
N_MEM = 256
MEM_HEADS = 4
MEM_HEAD_DIM = D_MODEL // MEM_HEADS
N_EXPERTS = 32
TOP_K = 4
D_EXPERT = D_MODEL
SWIGLU_ALPHA = 1.702
SWIGLU_LIMIT = 7.0
MOE_BLOCK = 128
DEEPNORM_ALPHA = (2 * DEPTH) ** 0.25
DEEPNORM_BETA = (8 * DEPTH) ** -0.25
LN_EPS = 1e-5
NEG_INF = -1e30

kernel_name = "yoco_gla_swa_sink_moe_stream_step"


def layer_norm(x, g, b):
    xf = x.astype(jnp.float32)
    mu = xf.mean(-1, keepdims=True)
    var = jnp.square(xf - mu).mean(-1, keepdims=True)
    return ((xf - mu) * lax.rsqrt(var + LN_EPS) * g.astype(jnp.float32) + b.astype(jnp.float32)).astype(x.dtype)


def _pad_len(L):
    return -(-L // CHUNK) * CHUNK


def gla_scan(q, k, v, log_a, s0):
    bsz, L, H, _ = q.shape
    Lp = _pad_len(L)
    nb = Lp // CHUNK

    def blocks(t):
        t = jnp.pad(t.astype(jnp.float32), ((0, 0), (0, Lp - L), (0, 0), (0, 0)))
        return t.reshape(bsz, nb, CHUNK, H, t.shape[-1]).transpose(1, 0, 3, 2, 4)

    causal = jnp.tril(jnp.ones((CHUNK, CHUNK), bool))[None, None, :, :, None]

    def step(S, blk):
        qb, kb, vb, gb = blk
        b = jnp.cumsum(gb, axis=2)
        b_end = b[:, :, -1:, :]
        decay = jnp.where(causal, jnp.exp(jnp.minimum(b[:, :, :, None, :] - b[:, :, None, :, :], 0.0)), 0.0)
        att = jnp.einsum('bhtk,bhsk,bhtsk->bhts', qb, kb, decay)
        o = jnp.einsum('bhts,bhsv->bhtv', att, vb) + jnp.einsum('bhtk,bhkv->bhtv', qb * jnp.exp(b), S)
        S = jnp.exp(b_end[:, :, 0, :])[..., None] * S + jnp.einsum('bhsk,bhsv->bhkv', kb * jnp.exp(b_end - b), vb)
        return S, o

    S, o = lax.scan(step, s0.astype(jnp.float32), (blocks(q), blocks(k), blocks(v), blocks(log_a)))
    o = o.transpose(1, 0, 3, 2, 4).reshape(bsz, Lp, H, -1)[:, :L]
    return o, S


def gla_mixer(x, s0, w_in, w_g2, b_g, gn_g, w_o):
    bsz, L, _ = x.shape
    q, k, v, r, g_lr = jnp.split(x @ w_in, [GLA_QK, 2 * GLA_QK, 2 * GLA_QK + GLA_V, 2 * GLA_QK + 2 * GLA_V], axis=-1)
    log_a = jax.nn.log_sigmoid((g_lr @ w_g2 + b_g).astype(jnp.float32)) / GLA_GATE_TEMP
    heads = lambda t, d: t.reshape(bsz, L, GLA_HEADS, d)
    o, s = gla_scan(heads(q, GLA_DK) * (GLA_DK ** -0.5), heads(k, GLA_DK), heads(v, GLA_DV), heads(log_a, GLA_DK), s0)
    mu = o.mean(-1, keepdims=True)
    var = jnp.square(o - mu).mean(-1, keepdims=True)
    o = (o - mu) * lax.rsqrt(var + LN_EPS) * gn_g.astype(jnp.float32)
    o = o.reshape(bsz, L, GLA_V).astype(x.dtype) * jax.nn.silu(r)
    return o @ w_o, s.astype(s0.dtype)


def swa_mixer(x, k_all, v_all, past_valid, w_q, sinks, w_o):
    bsz, L, _ = x.shape
    Lp = _pad_len(L)
    nb = Lp // CHUNK
    kw = (WINDOW_CHUNKS + 1) * CHUNK
    q = (x @ w_q).reshape(bsz, L, SWA_KV_HEADS, SWA_GROUP, SWA_HEAD_DIM) * (SWA_HEAD_DIM ** -0.5)
    q = jnp.pad(q, ((0, 0), (0, Lp - L), (0, 0), (0, 0), (0, 0))).reshape(bsz, nb, CHUNK, SWA_KV_HEADS, SWA_GROUP, SWA_HEAD_DIM)

    def band(t):
        t = jnp.pad(t, ((0, 0), (0, Lp - L), (0, 0), (0, 0)))
        t = t.reshape(bsz, WINDOW_CHUNKS + nb, CHUNK, SWA_KV_HEADS, SWA_HEAD_DIM)
        return jnp.concatenate([t[:, j:j + nb] for j in range(WINDOW_CHUNKS + 1)], axis=2)

    kb, vb = band(k_all), band(v_all)
    pos = jnp.arange(nb)[:, None] * CHUNK + jnp.arange(kw)[None, :]
    valid = jnp.where(pos < WINDOW, past_valid, pos - WINDOW < L)
    s = jnp.einsum('bnqhgd,bnkhd->bnhgqk', q, kb).astype(jnp.float32)
    s = jnp.where(valid[None, :, None, None, None, :], s, NEG_INF)
    sink = sinks.astype(jnp.float32).reshape(SWA_KV_HEADS, SWA_GROUP)[None, None, :, :, None, None]
    m = jnp.maximum(s.max(-1, keepdims=True), sink)
    p = jnp.exp(s - m)
    p = p / (p.sum(-1, keepdims=True) + jnp.exp(sink - m))
    o = jnp.einsum('bnhgqk,bnkhd->bnqhgd', p.astype(vb.dtype), vb).reshape(bsz, Lp, SWA_Q_HEADS * SWA_HEAD_DIM)[:, :L]
    return o @ w_o


def mem_xattn(x, mk, mv, w_q, w_o):
    bsz, L, _ = x.shape
    q = (x @ w_q).reshape(bsz, L, MEM_HEADS, MEM_HEAD_DIM) * (MEM_HEAD_DIM ** -0.5)
    p = jax.nn.softmax(jnp.einsum('blhd,bmhd->bhlm', q, mk).astype(jnp.float32), axis=-1)
    o = jnp.einsum('bhlm,bmhd->blhd', p.astype(mv.dtype), mv).reshape(bsz, L, D_MODEL)
    return o @ w_o


def moe(x, w_r, b_r, w1, b1, w2, b2):
    bsz, L, D = x.shape
    T = bsz * L
    xt = x.reshape(T, D)
    logits = (xt @ w_r + b_r).astype(jnp.float32)
    top_v, top_i = lax.top_k(logits, TOP_K)
    gates = jax.nn.softmax(top_v, axis=-1)
    N = T * TOP_K
    e_flat = top_i.reshape(N)
    tok_flat = jnp.repeat(jnp.arange(T, dtype=jnp.int32), TOP_K)
    g_flat = gates.reshape(N)
    order = jnp.argsort(e_flat)
    e_sorted = e_flat[order]
    counts = jnp.bincount(e_flat, length=N_EXPERTS)
    padded = (counts + MOE_BLOCK - 1) // MOE_BLOCK * MOE_BLOCK
    pad_end = jnp.cumsum(padded)
    pad_start = pad_end - padded
    start = jnp.cumsum(counts) - counts
    dest = pad_start[e_sorted] + jnp.arange(N) - start[e_sorted]
    P = -(-(N + N_EXPERTS * (MOE_BLOCK - 1)) // MOE_BLOCK) * MOE_BLOCK
    nb = P // MOE_BLOCK
    slot_tok = jnp.full((P,), T, jnp.int32).at[dest].set(tok_flat[order])
    slot_gate = jnp.zeros((P,), jnp.float32).at[dest].set(g_flat[order])
    block_e = jnp.minimum(jnp.searchsorted(pad_end, jnp.arange(nb) * MOE_BLOCK, side='right'), N_EXPERTS - 1)
    x_pad = jnp.concatenate([xt, jnp.zeros((1, D), xt.dtype)], axis=0)

    def expert_block(args):
        toks, e = args
        h = x_pad[toks] @ w1[e] + b1[e]
        glu = jnp.minimum(h[:, 0::2], SWIGLU_LIMIT)
        lin = jnp.clip(h[:, 1::2], -SWIGLU_LIMIT, SWIGLU_LIMIT)
        a = glu * jax.nn.sigmoid(SWIGLU_ALPHA * glu) * (lin + 1.0)
        return a @ w2[e] + b2[e]

    out = lax.map(expert_block, (slot_tok.reshape(nb, MOE_BLOCK), block_e)).reshape(P, D)
    out = out * slot_gate[:, None].astype(out.dtype)
    y = jnp.zeros((T + 1, D), out.dtype).at[slot_tok].add(out)[:T]
    return y.reshape(bsz, L, D)


def trunk(x, gla_s0, swa_past_k, swa_past_v, past_valid, mem_k, mem_v, w):
    bsz, L, _ = x.shape
    gla_states = []
    k_all = v_all = new_k = new_v = None
    for layer in range(DEPTH):
        if layer < N_A:
            i = layer
            h, s = gla_mixer(x, gla_s0[i], w['gla_w_in'][i], w['gla_w_g2'][i], w['gla_b_g'][i], w['gla_gn_g'][i], w['gla_w_o'][i])
            gla_states.append(s)
        else:
            j = layer - N_A
            if j == 0:
                k_new, v_new = jnp.split(x @ w['kv_w'], 2, axis=-1)
                k_new = k_new.reshape(bsz, L, SWA_KV_HEADS, SWA_HEAD_DIM)
                v_new = v_new.reshape(bsz, L, SWA_KV_HEADS, SWA_HEAD_DIM)
                k_all = jnp.concatenate([swa_past_k.astype(k_new.dtype), k_new], axis=1)
                v_all = jnp.concatenate([swa_past_v.astype(v_new.dtype), v_new], axis=1)
                new_k, new_v = k_all[:, -WINDOW:], v_all[:, -WINDOW:]
            h = swa_mixer(x, k_all, v_all, past_valid, w['swa_w_q'][j], w['swa_sinks'][j], w['swa_w_o'][j])
        x = layer_norm(DEEPNORM_ALPHA * x + h, w['ln_g'][layer, 0], w['ln_b'][layer, 0])
        h = mem_xattn(x, mem_k[layer], mem_v[layer], w['mem_w_q'][layer], w['mem_w_o'][layer])
        x = layer_norm(DEEPNORM_ALPHA * x + h, w['ln_g'][layer, 1], w['ln_b'][layer, 1])
        h = moe(x, w['moe_w_r'][layer], w['moe_b_r'][layer], w['moe_w1'][layer], w['moe_b1'][layer], w['moe_w2'][layer], w['moe_b2'][layer])
        x = layer_norm(DEEPNORM_ALPHA * x + h, w['ln_g'][layer, 2], w['ln_b'][layer, 2])
    return x, jnp.stack(gla_states), new_k, new_v


def setup_inputs(seed: int = 0) -> dict:
    key = jax.random.key(seed)
    ks = iter(jax.random.split(key, 40))
    nrm = lambda shape, scale=1.0: jax.random.normal(next(ks), shape, jnp.float32) * scale
    D = D_MODEL
    return {
        "x_prompt": nrm((BATCH, SEQ, D)),
        "x_sample": nrm((DEC_BATCH, DEC_SEQ, D)),
        "state_gla": nrm((N_A, DEC_BATCH, GLA_HEADS, GLA_DK, GLA_DV)),
        "cache_swa_k": nrm((DEC_BATCH, WINDOW, SWA_KV_HEADS, SWA_HEAD_DIM)),
        "cache_swa_v": nrm((DEC_BATCH, WINDOW, SWA_KV_HEADS, SWA_HEAD_DIM)),
        "cache_mem_k": nrm((DEPTH, DEC_BATCH, N_MEM, MEM_HEADS, MEM_HEAD_DIM)),
        "cache_mem_v": nrm((DEPTH, DEC_BATCH, N_MEM, MEM_HEADS, MEM_HEAD_DIM)),
        "mem_prompt": nrm((BATCH, N_MEM, D)),
        "gla_w_in": nrm((N_A, D, 2 * GLA_QK + 2 * GLA_V + GLA_GATE_RANK), D ** -0.5),
        "gla_w_g2": nrm((N_A, GLA_GATE_RANK, GLA_QK), GLA_GATE_RANK ** -0.5),
        "gla_b_g": nrm((N_A, GLA_QK), 0.1),
        "gla_gn_g": 1.0 + nrm((N_A, GLA_HEADS, GLA_DV), 0.02),
        "gla_w_o": nrm((N_A, GLA_V, D), GLA_V ** -0.5 * DEEPNORM_BETA),
        "kv_w": nrm((D, 2 * SWA_KV_HEADS * SWA_HEAD_DIM), D ** -0.5),
        "swa_w_q": nrm((N_B, D, SWA_Q_HEADS * SWA_HEAD_DIM), D ** -0.5),
        "swa_sinks": nrm((N_B, SWA_Q_HEADS), 0.5),
        "swa_w_o": nrm((N_B, SWA_Q_HEADS * SWA_HEAD_DIM, D), (SWA_Q_HEADS * SWA_HEAD_DIM) ** -0.5 * DEEPNORM_BETA),
        "mem_w_q": nrm((DEPTH, D, D), D ** -0.5),
        "mem_w_kv": nrm((DEPTH, D, 2 * D), D ** -0.5),
        "mem_w_o": nrm((DEPTH, D, D), D ** -0.5 * DEEPNORM_BETA),
        "moe_w_r": nrm((DEPTH, D, N_EXPERTS), D ** -0.5),
        "moe_b_r": nrm((DEPTH, N_EXPERTS), 0.01),
        "moe_w1": nrm((DEPTH, N_EXPERTS, D, 2 * D_EXPERT), D ** -0.5),
        "moe_b1": nrm((DEPTH, N_EXPERTS, 2 * D_EXPERT), 0.01),
        "moe_w2": nrm((DEPTH, N_EXPERTS, D_EXPERT, D), D_EXPERT ** -0.5 * DEEPNORM_BETA),
        "moe_b2": nrm((DEPTH, N_EXPERTS, D), 0.01),
        "ln_g": 1.0 + nrm((DEPTH, 3, D), 0.02),
        "ln_b": nrm((DEPTH, 3, D), 0.02),
    }


def reference(x_prompt, x_sample, state_gla, cache_swa_k, cache_swa_v, cache_mem_k, cache_mem_v, mem_prompt,
              gla_w_in, gla_w_g2, gla_b_g, gla_gn_g, gla_w_o, kv_w, swa_w_q, swa_sinks, swa_w_o,
              mem_w_q, mem_w_kv, mem_w_o, moe_w_r, moe_b_r, moe_w1, moe_b1, moe_w2, moe_b2, ln_g, ln_b):
    w = dict(gla_w_in=gla_w_in, gla_w_g2=gla_w_g2, gla_b_g=gla_b_g, gla_gn_g=gla_gn_g, gla_w_o=gla_w_o,
             kv_w=kv_w, swa_w_q=swa_w_q, swa_sinks=swa_sinks, swa_w_o=swa_w_o,
             mem_w_q=mem_w_q, mem_w_o=mem_w_o, moe_w_r=moe_w_r, moe_b_r=moe_b_r,
             moe_w1=moe_w1, moe_b1=moe_b1, moe_w2=moe_w2, moe_b2=moe_b2, ln_g=ln_g, ln_b=ln_b)
    bp = x_prompt.shape[0]
    mem_kv = jnp.einsum('bmd,lde->lbme', mem_prompt, mem_w_kv)
    mem_k_p = mem_kv[..., :D_MODEL].reshape(DEPTH, bp, N_MEM, MEM_HEADS, MEM_HEAD_DIM)
    mem_v_p = mem_kv[..., D_MODEL:].reshape(DEPTH, bp, N_MEM, MEM_HEADS, MEM_HEAD_DIM)
    gla0 = jnp.zeros((N_A, bp, GLA_HEADS, GLA_DK, GLA_DV), x_prompt.dtype)
    zero_win = jnp.zeros((bp, WINDOW, SWA_KV_HEADS, SWA_HEAD_DIM), x_prompt.dtype)
    y_prompt, gla_state_p, swa_k_p, swa_v_p = trunk(x_prompt, gla0, zero_win, zero_win, False, mem_k_p, mem_v_p, w)
    y_sample, gla_state_s, swa_k_s, swa_v_s = trunk(x_sample, state_gla, cache_swa_k, cache_swa_v, True, cache_mem_k, cache_mem_v, w)
    return (y_prompt, y_sample, gla_state_p, gla_state_s, swa_k_p, swa_v_p, swa_k_s, swa_v_s, mem_k_p, mem_v_p)
```

```python
import functools

import jax
import jax.numpy as jnp
from jax import lax
from jax.experimental import pallas as pl
from jax.experimental.pallas import tpu as pltpu

F32 = jnp.float32
BF16 = jnp.bfloat16

D_MODEL = 1024
DEPTH = 4
CHUNK = 64
N_A = DEPTH // 2
GLA_HEADS = 4
GLA_DK = D_MODEL // (2 * GLA_HEADS)
GLA_DV = D_MODEL // GLA_HEADS
GLA_QK = GLA_HEADS * GLA_DK
GLA_V = GLA_HEADS * GLA_DV
GLA_GATE_RANK = 16
GLA_GATE_TEMP = 16.0
SWA_HEAD_DIM = 64
SWA_Q_HEADS = D_MODEL // SWA_HEAD_DIM
SWA_KV_HEADS = 4
SWA_GROUP = SWA_Q_HEADS // SWA_KV_HEADS
SWA_KV = SWA_KV_HEADS * SWA_HEAD_DIM
WINDOW = 128
WINDOW_CHUNKS = WINDOW // CHUNK
N_MEM = 256
MEM_HEADS = 4
MEM_HEAD_DIM = D_MODEL // MEM_HEADS
N_EXPERTS = 32
TOP_K = 4
D_EXPERT = D_MODEL
SWIGLU_ALPHA = 1.702
SWIGLU_LIMIT = 7.0
DEEPNORM_ALPHA = (2 * DEPTH) ** 0.25
LN_EPS = 1e-5
NEG_INF = -1e30

LANES = 128
VMEM_LIMIT = 56 * 1024 * 1024


def _params(*sem):
    return pltpu.CompilerParams(dimension_semantics=sem, vmem_limit_bytes=VMEM_LIMIT)


def _row_tile(n_rows, want):
    t = min(want, n_rows)
    while n_rows % t:
        t //= 2
    return t


def _full(shape):
    nd = len(shape)
    return pl.BlockSpec(shape, lambda *_: (0,) * nd)


def _bdot(a, b):
    return jnp.dot(a.astype(BF16), b.astype(BF16), preferred_element_type=F32)


def _bdot_nt(a, b):
    return lax.dot_general(a.astype(BF16), b.astype(BF16), (((1,), (1,)), ((), ())),
                           preferred_element_type=F32)


def _bdot_tn(a, b):
    return lax.dot_general(a.astype(BF16), b.astype(BF16), (((0,), (0,)), ((), ())),
                           preferred_element_type=F32)


def _layer_norm(z, g, b):
    mu = jnp.mean(z, axis=-1, keepdims=True)
    zc = z - mu
    var = jnp.mean(zc * zc, axis=-1, keepdims=True)
    return zc * lax.rsqrt(var + LN_EPS) * g + b


def _linear_kernel(x_ref, w_ref, o_ref):
    o_ref[...] = _bdot(x_ref[...], w_ref[...])


def _linear(x, w, tm=512):
    T, K = x.shape
    N = w.shape[1]
    tm = _row_tile(T, tm)
    return pl.pallas_call(
        _linear_kernel,
        out_shape=jax.ShapeDtypeStruct((T, N), F32),
        grid=(T // tm,),
        in_specs=[pl.BlockSpec((tm, K), lambda i: (i, 0)), _full((K, N))],
        out_specs=pl.BlockSpec((tm, N), lambda i: (i, 0)),
        compiler_params=_params("parallel"),
        name="linear",
    )(x, w)


def _gla_in_kernel(x_ref, w_ref, wg_ref, wg2_ref, bg_ref, q_ref, k_ref, v_ref, r_ref, la_ref):
    xb = x_ref[...].astype(BF16)
    y = jnp.dot(xb, w_ref[...], preferred_element_type=F32)
    q_ref[...] = y[:, :GLA_QK] * (GLA_DK ** -0.5)
    k_ref[...] = y[:, GLA_QK:2 * GLA_QK]
    v_ref[...] = y[:, 2 * GLA_QK:2 * GLA_QK + GLA_V]
    r_ref[...] = y[:, 2 * GLA_QK + GLA_V:]
    g_lr = jnp.dot(xb, wg_ref[...], preferred_element_type=F32)
    z = _bdot(g_lr, wg2_ref[...]) + bg_ref[...]
    log_sig = jnp.minimum(z, 0.0) - jnp.log(1.0 + jnp.exp(-jnp.abs(z)))
    la_ref[...] = log_sig / GLA_GATE_TEMP


def _gla_in(x, w_main, w_g, w_g2, b_g, tm=512):
    T = x.shape[0]
    tm = _row_tile(T, tm)
    n_main = w_main.shape[1]
    row = lambda n: pl.BlockSpec((tm, n), lambda i: (i, 0))
    return pl.pallas_call(
        _gla_in_kernel,
        out_shape=(jax.ShapeDtypeStruct((T, GLA_QK), F32), jax.ShapeDtypeStruct((T, GLA_QK), F32),
                   jax.ShapeDtypeStruct((T, GLA_V), F32), jax.ShapeDtypeStruct((T, GLA_V), F32),
                   jax.ShapeDtypeStruct((T, GLA_QK), F32)),
        grid=(T // tm,),
        in_specs=[row(D_MODEL), _full((D_MODEL, n_main)), _full((D_MODEL, LANES)),
                  _full((LANES, GLA_QK)), _full((1, GLA_QK))],
        out_specs=(row(GLA_QK), row(GLA_QK), row(GLA_V), row(GLA_V), row(GLA_QK)),
        compiler_params=_params("parallel"),
        name="gla_in",
    )(x, w_main, w_g, w_g2, b_g)


def _gla_scan_kernel(q_ref, k_ref, v_ref, la_ref, s0_ref, o_ref, s_ref, st_scr, b_scr, k_scr, *, n_chunks):
    @pl.when(pl.program_id(1) == 0)
    def _():
        st_scr[...] = s0_ref[0]

    row = lax.broadcasted_iota(jnp.int32, (CHUNK, CHUNK), 0)
    col = lax.broadcasted_iota(jnp.int32, (CHUNK, CHUNK), 1)
    tril = (row >= col).astype(BF16)
    lane = lax.broadcasted_iota(jnp.int32, (CHUNK, LANES), 1)
    row_l = lax.broadcasted_iota(jnp.int32, (CHUNK, LANES), 0)

    def chunk_body(c, carry):
        t0 = pl.multiple_of(c * CHUNK, CHUNK)
        for h in range(GLA_HEADS):
            ks = slice(h * GLA_DK, (h + 1) * GLA_DK)
            vs = slice(h * GLA_DV, (h + 1) * GLA_DV)
            q = q_ref[0, pl.ds(t0, CHUNK), ks]
            k = k_ref[0, pl.ds(t0, CHUNK), ks]
            g = la_ref[0, pl.ds(t0, CHUNK), ks]
            v = v_ref[0, pl.ds(t0, CHUNK), vs]
            g1 = g.astype(BF16)
            r1 = g - g1.astype(F32)
            g2 = r1.astype(BF16)
            g3 = (r1 - g2.astype(F32)).astype(BF16)
            b = (jnp.dot(tril, g1, preferred_element_type=F32)
                 + jnp.dot(tril, g2, preferred_element_type=F32)
                 + jnp.dot(tril, g3, preferred_element_type=F32))
            b_scr[...] = b
            k_scr[...] = k
            b_end = b[CHUNK - 1:CHUNK, :]

            def col_body(s, att):
                bs = b_scr[pl.ds(s, 1), :]
                k_s = k_scr[pl.ds(s, 1), :]
                e = jnp.exp(jnp.minimum(b - bs, 0.0))
                red = jnp.sum(q * e * k_s, axis=-1, keepdims=True)
                return jnp.where(lane == s, red, att)

            att = lax.fori_loop(0, CHUNK, col_body, jnp.zeros((CHUNK, LANES), F32))
            att = jnp.where(row_l >= lane, att, 0.0)
            st = st_scr[h]
            o = _bdot(att[:, :CHUNK], v) + _bdot_nt(q * jnp.exp(b), st)
            o_ref[0, pl.ds(t0, CHUNK), vs] = o
            kd = k * jnp.exp(b_end - b)
            st_scr[h] = st * jnp.exp(b_end) + _bdot_tn(v, kd)
        return carry

    lax.fori_loop(0, n_chunks, chunk_body, 0)

    @pl.when(pl.program_id(1) == pl.num_programs(1) - 1)
    def _():
        s_ref[0] = st_scr[...]


def _gla_scan(q, k, v, la, s0t, tl=256):
    B, Lp, _ = q.shape
    tl = _row_tile(Lp, tl)
    qk_spec = pl.BlockSpec((1, tl, GLA_QK), lambda b, i: (b, i, 0))
    v_spec = pl.BlockSpec((1, tl, GLA_V), lambda b, i: (b, i, 0))
    s_spec = pl.BlockSpec((1, GLA_HEADS, GLA_DV, GLA_DK), lambda b, i: (b, 0, 0, 0))
    return pl.pallas_call(
        functools.partial(_gla_scan_kernel, n_chunks=tl // CHUNK),
        out_shape=(jax.ShapeDtypeStruct((B, Lp, GLA_V), F32),
                   jax.ShapeDtypeStruct((B, GLA_HEADS, GLA_DV, GLA_DK), F32)),
        grid=(B, Lp // tl),
        in_specs=[qk_spec, qk_spec, v_spec, qk_spec, s_spec],
        out_specs=(v_spec, s_spec),
        scratch_shapes=[pltpu.VMEM((GLA_HEADS, GLA_DV, GLA_DK), F32), pltpu.VMEM((CHUNK, GLA_DK), F32),
                        pltpu.VMEM((CHUNK, GLA_DK), F32)],
        compiler_params=_params("parallel", "arbitrary"),
        name="gla_scan",
    )(q, k, v, la, s0t)


def _gla_out_kernel(o_ref, r_ref, x_ref, gn_ref, wo_ref, g_ref, b_ref, y_ref):
    parts = []
    for h in range(GLA_HEADS):
        vs = slice(h * GLA_DV, (h + 1) * GLA_DV)
        o = o_ref[:, vs]
        mu = jnp.mean(o, axis=-1, keepdims=True)
        oc = o - mu
        var = jnp.mean(oc * oc, axis=-1, keepdims=True)
        parts.append(oc * lax.rsqrt(var + LN_EPS) * gn_ref[:, vs])
    o = jnp.concatenate(parts, axis=-1)
    r = r_ref[...]
    o = o * (r * jax.nn.sigmoid(r))
    h = _bdot(o, wo_ref[...])
    y_ref[...] = _layer_norm(DEEPNORM_ALPHA * x_ref[...] + h, g_ref[...], b_ref[...])


def _gla_out(o, r, x, gn_g, w_o, ln_g, ln_b, tm=512):
    T = x.shape[0]
    tm = _row_tile(T, tm)
    row = pl.BlockSpec((tm, D_MODEL), lambda i: (i, 0))
    vec = _full((1, D_MODEL))
    return pl.pallas_call(
        _gla_out_kernel,
        out_shape=jax.ShapeDtypeStruct((T, D_MODEL), F32),
        grid=(T // tm,),
        in_specs=[row, row, row, vec, _full((GLA_V, D_MODEL)), vec, vec],
        out_specs=row,
        compiler_params=_params("parallel"),
        name="gla_out",
    )(o, r, x, gn_g, w_o, ln_g, ln_b)


def _swa_kernel(x_ref, k_ref, v_ref, wq_ref, sink_ref, wo_ref, g_ref, b_ref, y_ref, o_scr,
                *, n_chunks, seq_len, past_valid):
    x = x_ref[0]
    q = _bdot(x, wq_ref[...]) * (SWA_HEAD_DIM ** -0.5)
    kw = (WINDOW_CHUNKS + 1) * CHUNK
    tile = pl.program_id(1)
    for c in range(n_chunks):
        n = tile * n_chunks + c
        r0 = pl.multiple_of(n * CHUNK, CHUNK)
        pos = n * CHUNK + lax.broadcasted_iota(jnp.int32, (1, kw), 1)
        if past_valid:
            valid = pos - WINDOW < seq_len
        else:
            valid = (pos >= WINDOW) & (pos - WINDOW < seq_len)
        kc = k_ref[0, pl.ds(r0, kw), :]
        vc = v_ref[0, pl.ds(r0, kw), :]
        for g in range(SWA_KV_HEADS):
            kg = kc[:, g * SWA_HEAD_DIM:(g + 1) * SWA_HEAD_DIM]
            vg = vc[:, g * SWA_HEAD_DIM:(g + 1) * SWA_HEAD_DIM]
            heads = [g * SWA_GROUP + i for i in range(SWA_GROUP)]
            qg = jnp.concatenate(
                [q[c * CHUNK:(c + 1) * CHUNK, h * SWA_HEAD_DIM:(h + 1) * SWA_HEAD_DIM] for h in heads], axis=0)
            sink = jnp.concatenate(
                [jnp.broadcast_to(sink_ref[h:h + 1, :], (CHUNK, 1)) for h in heads], axis=0)
            s = _bdot_nt(qg, kg)
            s = jnp.where(valid, s, NEG_INF)
            m = jnp.maximum(jnp.max(s, axis=-1, keepdims=True), sink)
            p = jnp.exp(s - m)
            p = p / (jnp.sum(p, axis=-1, keepdims=True) + jnp.exp(sink - m))
            og = _bdot(p, vg)
            for i, h in enumerate(heads):
                o_scr[c * CHUNK:(c + 1) * CHUNK, h * SWA_HEAD_DIM:(h + 1) * SWA_HEAD_DIM] = \
                    og[i * CHUNK:(i + 1) * CHUNK, :]
    h_out = _bdot(o_scr[...], wo_ref[...])
    y_ref[0] = _layer_norm(DEEPNORM_ALPHA * x + h_out, g_ref[...], b_ref[...])


def _swa(x, k_all, v_all, w_q, sinks, w_o, ln_g, ln_b, seq_len, past_valid, tl=256):
    B, Lp, _ = x.shape
    tl = _row_tile(Lp, tl)
    x_spec = pl.BlockSpec((1, tl, D_MODEL), lambda b, i: (b, i, 0))
    kv_spec = pl.BlockSpec((1, WINDOW + Lp, SWA_KV), lambda b, i: (b, 0, 0))
    vec = _full((1, D_MODEL))
    return pl.pallas_call(
        functools.partial(_swa_kernel, n_chunks=tl // CHUNK, seq_len=seq_len, past_valid=past_valid),
        out_shape=jax.ShapeDtypeStruct((B, Lp, D_MODEL), F32),
        grid=(B, Lp // tl),
        in_specs=[x_spec, kv_spec, kv_spec, _full((D_MODEL, D_MODEL)), _full((SWA_Q_HEADS, 1)),
                  _full((D_MODEL, D_MODEL)), vec, vec],
        out_specs=x_spec,
        scratch_shapes=[pltpu.VMEM((tl, D_MODEL), F32)],
        compiler_params=_params("parallel", "arbitrary"),
        name="swa",
    )(x, k_all, v_all, w_q, sinks, w_o, ln_g, ln_b)


def _mem_kernel(x_ref, mk_ref, mv_ref, wq_ref, wo_ref, g_ref, b_ref, y_ref):
    x = x_ref[0]
    q = _bdot(x, wq_ref[...]) * (MEM_HEAD_DIM ** -0.5)
    parts = []
    for h in range(MEM_HEADS):
        hs = slice(h * MEM_HEAD_DIM, (h + 1) * MEM_HEAD_DIM)
        s = _bdot_nt(q[:, hs], mk_ref[0, :, hs])
        m = jnp.max(s, axis=-1, keepdims=True)
        p = jnp.exp(s - m)
        p = p / jnp.sum(p, axis=-1, keepdims=True)
        parts.append(_bdot(p, mv_ref[0, :, hs]))
    o = jnp.concatenate(parts, axis=-1)
    h_out = _bdot(o, wo_ref[...])
    y_ref[0] = _layer_norm(DEEPNORM_ALPHA * x + h_out, g_ref[...], b_ref[...])


def _mem_xattn(x, mk, mv, w_q, w_o, ln_g, ln_b, tl=512):
    B, L, _ = x.shape
    tl = _row_tile(L, tl)
    x_spec = pl.BlockSpec((1, tl, D_MODEL), lambda b, i: (b, i, 0))
    m_spec = pl.BlockSpec((1, N_MEM, D_MODEL), lambda b, i: (b, 0, 0))
    vec = _full((1, D_MODEL))
    return pl.pallas_call(
        _mem_kernel,
        out_shape=jax.ShapeDtypeStruct((B, L, D_MODEL), F32),
        grid=(B, L // tl),
        in_specs=[x_spec, m_spec, m_spec, _full((D_MODEL, D_MODEL)), _full((D_MODEL, D_MODEL)), vec, vec],
        out_specs=x_spec,
        compiler_params=_params("parallel", "arbitrary"),
        name="mem_xattn",
    )(x, mk, mv, w_q, w_o, ln_g, ln_b)


def _router_kernel(x_ref, wr_ref, br_ref, idx_ref, gate_ref, pos_ref, cnt_ref, cnt_scr, *, tm):
    @pl.when(pl.program_id(0) == 0)
    def _():
        cnt_scr[...] = jnp.zeros_like(cnt_scr)

    logits = _bdot(x_ref[...], wr_ref[...]) + br_ref[...]
    lane = lax.broadcasted_iota(jnp.int32, (tm, LANES), 1)
    sel = jnp.zeros((tm, LANES), F32)
    idx_out = jnp.zeros((tm, LANES), jnp.int32)
    val_out = jnp.zeros((tm, LANES), F32)
    l = logits
    idxs = []
    for j in range(TOP_K):
        m = jnp.max(l, axis=-1, keepdims=True)
        idx = jnp.min(jnp.where(l == m, lane, LANES), axis=-1, keepdims=True)
        hit = lane == idx
        sel = jnp.where(hit, 1.0, sel)
        l = jnp.where(hit, -jnp.inf, l)
        idx_out = jnp.where(lane == j, idx, idx_out)
        val_out = jnp.where(lane == j, m, val_out)
        idxs.append(idx)
    e = jnp.where(lane < TOP_K, jnp.exp(val_out - val_out[:, 0:1]), 0.0)
    gates = e / jnp.sum(e, axis=-1, keepdims=True)
    r = lax.broadcasted_iota(jnp.int32, (tm, tm), 0)
    c = lax.broadcasted_iota(jnp.int32, (tm, tm), 1)
    before = jnp.dot((r > c).astype(BF16), sel.astype(BF16), preferred_element_type=F32) + cnt_scr[...]
    pos_out = jnp.zeros((tm, LANES), F32)
    for j in range(TOP_K):
        pj = jnp.sum(jnp.where(lane == idxs[j], before, 0.0), axis=-1, keepdims=True)
        pos_out = jnp.where(lane == j, pj, pos_out)
    cnt_scr[...] = cnt_scr[...] + jnp.sum(sel, axis=0, keepdims=True)
    idx_ref[...] = idx_out[:, :TOP_K]
    gate_ref[...] = gates[:, :TOP_K]
    pos_ref[...] = pos_out[:, :TOP_K].astype(jnp.int32)
    cnt_ref[...] = cnt_scr[...].astype(jnp.int32)


def _router(x, w_r, b_r, tm=512):
    T = x.shape[0]
    tm = _row_tile(T, tm)
    small = pl.BlockSpec((tm, TOP_K), lambda i: (i, 0))
    return pl.pallas_call(
        functools.partial(_router_kernel, tm=tm),
        out_shape=(jax.ShapeDtypeStruct((T, TOP_K), jnp.int32), jax.ShapeDtypeStruct((T, TOP_K), F32),
                   jax.ShapeDtypeStruct((T, TOP_K), jnp.int32), jax.ShapeDtypeStruct((1, LANES), jnp.int32)),
        grid=(T // tm,),
        in_specs=[pl.BlockSpec((tm, D_MODEL), lambda i: (i, 0)), _full((D_MODEL, LANES)), _full((1, LANES))],
        out_specs=(small, small, small, _full((1, LANES))),
        scratch_shapes=[pltpu.VMEM((1, LANES), F32)],
        compiler_params=_params("arbitrary"),
        name="moe_router",
    )(x, w_r, b_r)


def _dispatch_kernel(dest_ref, x_ref, init_ref, xs_ref, sem, *, tm):
    del init_ref

    def row_copy(r, d):
        return pltpu.make_async_copy(x_ref.at[pl.ds(r, 1), :], xs_ref.at[pl.ds(d, 1), :], sem)

    def issue(r, carry):
        for j in range(TOP_K):
            row_copy(r, dest_ref[0, j, r]).start()
        return carry

    lax.fori_loop(0, tm, issue, 0)

    def drain(r, carry):
        for j in range(TOP_K):
            row_copy(r, dest_ref[0, j, r]).wait()
        return carry

    lax.fori_loop(0, tm, drain, 0)


def _dispatch(x, dest_blocks, n_slots, tm):
    T = x.shape[0]
    init = jnp.zeros((n_slots, D_MODEL), F32)
    return pl.pallas_call(
        functools.partial(_dispatch_kernel, tm=tm),
        out_shape=jax.ShapeDtypeStruct((n_slots, D_MODEL), F32),
        grid=(T // tm,),
        in_specs=[pl.BlockSpec((1, TOP_K, tm), lambda i: (i, 0, 0), memory_space=pltpu.SMEM),
                  pl.BlockSpec((tm, D_MODEL), lambda i: (i, 0)),
                  pl.BlockSpec(memory_space=pl.ANY)],
        out_specs=pl.BlockSpec(memory_space=pl.ANY),
        scratch_shapes=[pltpu.SemaphoreType.DMA],
        input_output_aliases={2: 0},
        compiler_params=_params("arbitrary"),
        name="moe_dispatch",
    )(dest_blocks, x, init)


def _collect_kernel(idx_ref, src_ref, out_ref, sem, *, bm):
    def row_copy(r):
        return pltpu.make_async_copy(src_ref.at[pl.ds(idx_ref[0, 0, r], 1), :], out_ref.at[pl.ds(r, 1), :], sem)

    def issue(r, carry):
        row_copy(r).start()
        return carry

    lax.fori_loop(0, bm, issue, 0)

    def drain(r, carry):
        row_copy(r).wait()
        return carry

    lax.fori_loop(0, bm, drain, 0)


def _collect(src, idx, bm=512):
    M = idx.shape[0]
    bm = _row_tile(M, bm)
    return pl.pallas_call(
        functools.partial(_collect_kernel, bm=bm),
        out_shape=jax.ShapeDtypeStruct((M, D_MODEL), F32),
        grid=(M // bm,),
        in_specs=[pl.BlockSpec((1, 1, bm), lambda i: (i, 0, 0), memory_space=pltpu.SMEM),
                  pl.BlockSpec(memory_space=pl.ANY)],
        out_specs=pl.BlockSpec((bm, D_MODEL), lambda i: (i, 0)),
        scratch_shapes=[pltpu.SemaphoreType.DMA],
        compiler_params=_params("arbitrary"),
        name="moe_collect",
    )(idx.reshape(M // bm, 1, bm), src)


def _expert_kernel(be_ref, nb_ref, xs_ref, w1_ref, b1_ref, w2_ref, b2_ref, o_ref):
    @pl.when(pl.program_id(0) < nb_ref[0])
    def _():
        h = _bdot(xs_ref[...], w1_ref[0]) + b1_ref[0]
        glu = jnp.minimum(h[:, :D_EXPERT], SWIGLU_LIMIT)
        lin = jnp.clip(h[:, D_EXPERT:], -SWIGLU_LIMIT, SWIGLU_LIMIT)
        a = glu * jax.nn.sigmoid(SWIGLU_ALPHA * glu) * (lin + 1.0)
        o_ref[...] = _bdot(a, w2_ref[0]) + b2_ref[0]


def _experts(xs, block_e, n_used, w1, b1, w2, b2, blk):
    P = xs.shape[0]
    nb = P // blk

    def blk_map(i, be, nu):
        return (jnp.minimum(i, nu[0] - 1), 0)

    def w_map(i, be, nu):
        return (be[jnp.minimum(i, nu[0] - 1)], 0, 0)

    return pl.pallas_call(
        _expert_kernel,
        out_shape=jax.ShapeDtypeStruct((P, D_MODEL), F32),
        grid_spec=pltpu.PrefetchScalarGridSpec(
            num_scalar_prefetch=2,
            grid=(nb,),
            in_specs=[pl.BlockSpec((blk, D_MODEL), blk_map),
                      pl.BlockSpec((1, D_MODEL, 2 * D_EXPERT), w_map),
                      pl.BlockSpec((1, 1, 2 * D_EXPERT), w_map),
                      pl.BlockSpec((1, D_EXPERT, D_MODEL), w_map),
                      pl.BlockSpec((1, 1, D_MODEL), w_map)],
            out_specs=pl.BlockSpec((blk, D_MODEL), blk_map)),
        compiler_params=_params("arbitrary"),
        name="moe_experts",
    )(block_e, n_used, xs, w1, b1, w2, b2)


def _combine_kernel(y4_ref, gate_ref, x_ref, g_ref, b_ref, y_ref):
    gates = gate_ref[...]
    acc = DEEPNORM_ALPHA * x_ref[...]
    for j in range(TOP_K):
        acc = acc + gates[:, j:j + 1] * y4_ref[j]
    y_ref[...] = _layer_norm(acc, g_ref[...], b_ref[...])


def _combine(y4, gates, x, ln_g, ln_b, tm=512):
    T = x.shape[0]
    tm = _row_tile(T, tm)
    row = pl.BlockSpec((tm, D_MODEL), lambda i: (i, 0))
    vec = _full((1, D_MODEL))
    return pl.pallas_call(
        _combine_kernel,
        out_shape=jax.ShapeDtypeStruct((T, D_MODEL), F32),
        grid=(T // tm,),
        in_specs=[pl.BlockSpec((TOP_K, tm, D_MODEL), lambda i: (0, i, 0)),
                  pl.BlockSpec((tm, TOP_K), lambda i: (i, 0)), row, vec, vec],
        out_specs=row,
        compiler_params=_params("parallel"),
        name="moe_combine",
    )(y4, gates, x, ln_g, ln_b)


def _moe(x, w_r, b_r, w1, b1, w2, b2, ln_g, ln_b):
    T = x.shape[0]
    blk = 512 if T >= 4096 else 128
    tm = _row_tile(T, 512)
    n_assign = T * TOP_K
    n_blocks = -(-(n_assign + N_EXPERTS * (blk - 1)) // blk)
    n_slots = n_blocks * blk

    top_i, gates, pos, counts = _router(x, w_r, b_r)
    counts = counts[0, :N_EXPERTS]
    padded = (counts + blk - 1) // blk * blk
    pad_end = jnp.cumsum(padded)
    pad_start = pad_end - padded
    dest = pad_start[top_i] + pos
    block_e = jnp.minimum(
        jnp.searchsorted(pad_end, jnp.arange(n_blocks, dtype=jnp.int32) * blk, side='right'),
        N_EXPERTS - 1).astype(jnp.int32)
    n_used = (pad_end[-1:] // blk).astype(jnp.int32)
    dest_km = dest.T
    dest_blocks = dest_km.reshape(TOP_K, T // tm, tm).transpose(1, 0, 2)

    xs = _dispatch(x, dest_blocks, n_slots, tm)
    out = _experts(xs, block_e, n_used, w1, b1, w2, b2, blk)
    y4 = _collect(out, dest_km.reshape(n_assign)).reshape(TOP_K, T, D_MODEL)
    return _combine(y4, gates, x, ln_g, ln_b)


def _pad_rows(t, n):
    return t if n == 0 else jnp.pad(t, ((0, 0), (0, n), (0, 0)))


def _trunk(x, gla_s0, past_k, past_v, past_valid, mem_k, mem_v, w):
    B, L, _ = x.shape
    Lp = -(-L // CHUNK) * CHUNK
    T = B * L
    vec = lambda a: a.reshape(1, -1)
    gla_states = []
    k_all = v_all = new_k = new_v = None
    for layer in range(DEPTH):
        lg, lb = w['ln_g'][layer], w['ln_b'][layer]
        xt = x.reshape(T, D_MODEL)
        if layer < N_A:
            i = layer
            q, k, v, r, la = _gla_in(xt, w['gla_w_main'][i], w['gla_w_g'][i], w['gla_w_g2'][i], vec(w['gla_b_g'][i]))
            seq = lambda t: _pad_rows(t.reshape(B, L, -1), Lp - L)
            s0t = jnp.swapaxes(gla_s0[i], -1, -2)
            o, st = _gla_scan(seq(q), seq(k), seq(v), seq(la), s0t)
            gla_states.append(jnp.swapaxes(st, -1, -2))
            o = o[:, :L].reshape(T, GLA_V)
            xt = _gla_out(o, r, xt, vec(w['gla_gn_g'][i]), w['gla_w_o'][i], vec(lg[0]), vec(lb[0]))
            x = xt.reshape(B, L, D_MODEL)
        else:
            j = layer - N_A
            if j == 0:
                kv = _linear(xt, w['kv_w']).reshape(B, L, 2 * SWA_KV)
                k_full = jnp.concatenate([past_k, kv[..., :SWA_KV]], axis=1)
                v_full = jnp.concatenate([past_v, kv[..., SWA_KV:]], axis=1)
                new_k, new_v = k_full[:, -WINDOW:], v_full[:, -WINDOW:]
                k_all, v_all = _pad_rows(k_full, Lp - L), _pad_rows(v_full, Lp - L)
            xp = _swa(_pad_rows(x, Lp - L), k_all, v_all, w['swa_w_q'][j], w['swa_sinks'][j].reshape(-1, 1),
                      w['swa_w_o'][j], vec(lg[0]), vec(lb[0]), L, past_valid)
            x = xp[:, :L]
        x = _mem_xattn(x, mem_k[layer], mem_v[layer], w['mem_w_q'][layer], w['mem_w_o'][layer],
                       vec(lg[1]), vec(lb[1]))
        xt = _moe(x.reshape(T, D_MODEL), w['moe_w_r'][layer], w['moe_b_r'][layer], w['moe_w1'][layer],
                  w['moe_b1'][layer], w['moe_w2'][layer], w['moe_b2'][layer], vec(lg[2]), vec(lb[2]))
        x = xt.reshape(B, L, D_MODEL)
    return x, jnp.stack(gla_states), new_k, new_v


def _prep_weights(gla_w_in, gla_w_g2, gla_b_g, gla_gn_g, gla_w_o, kv_w, swa_w_q, swa_sinks, swa_w_o,
                  mem_w_q, mem_w_kv, mem_w_o, moe_w_r, moe_b_r, moe_w1, moe_b1, moe_w2, moe_b2, ln_g, ln_b):
    n_main = 2 * GLA_QK + 2 * GLA_V
    pad_c = lambda a, n: jnp.pad(a, [(0, 0)] * (a.ndim - 1) + [(0, n - a.shape[-1])])
    w = dict(
        gla_w_main=gla_w_in[:, :, :n_main].astype(BF16),
        gla_w_g=pad_c(gla_w_in[:, :, n_main:], LANES).astype(BF16),
        gla_w_g2=jnp.pad(gla_w_g2, ((0, 0), (0, LANES - GLA_GATE_RANK), (0, 0))).astype(BF16),
        gla_b_g=gla_b_g, gla_gn_g=gla_gn_g.reshape(N_A, GLA_V), gla_w_o=gla_w_o.astype(BF16),
        kv_w=kv_w.astype(BF16), swa_w_q=swa_w_q.astype(BF16), swa_sinks=swa_sinks, swa_w_o=swa_w_o.astype(BF16),
        mem_w_q=mem_w_q.astype(BF16), mem_w_kv=mem_w_kv.astype(BF16), mem_w_o=mem_w_o.astype(BF16),
        moe_w_r=pad_c(moe_w_r, LANES).astype(BF16),
        moe_b_r=jnp.pad(moe_b_r, ((0, 0), (0, LANES - N_EXPERTS)), constant_values=-jnp.inf).reshape(DEPTH, 1, LANES),
        moe_w1=jnp.concatenate([moe_w1[..., 0::2], moe_w1[..., 1::2]], axis=-1).astype(BF16),
        moe_b1=jnp.concatenate([moe_b1[..., 0::2], moe_b1[..., 1::2]], axis=-1).reshape(DEPTH, N_EXPERTS, 1, -1),
        moe_w2=moe_w2.astype(BF16), moe_b2=moe_b2.reshape(DEPTH, N_EXPERTS, 1, D_MODEL),
        ln_g=ln_g, ln_b=ln_b)
    return w


def kernel(x_prompt, x_sample, state_gla, cache_swa_k, cache_swa_v, cache_mem_k, cache_mem_v, mem_prompt, gla_w_in, gla_w_g2, gla_b_g, gla_gn_g, gla_w_o, kv_w, swa_w_q, swa_sinks, swa_w_o, mem_w_q, mem_w_kv, mem_w_o, moe_w_r, moe_b_r, moe_w1, moe_b1, moe_w2, moe_b2, ln_g, ln_b):
    w = _prep_weights(gla_w_in, gla_w_g2, gla_b_g, gla_gn_g, gla_w_o, kv_w, swa_w_q, swa_sinks, swa_w_o,
                      mem_w_q, mem_w_kv, mem_w_o, moe_w_r, moe_b_r, moe_w1, moe_b1, moe_w2, moe_b2, ln_g, ln_b)
    bp = x_prompt.shape[0]
    bs = x_sample.shape[0]
    mem_flat = mem_prompt.reshape(bp * N_MEM, D_MODEL)
    mem_kv = jnp.stack([_linear(mem_flat, w['mem_w_kv'][l]) for l in range(DEPTH)])
    mem_k_p = mem_kv[..., :D_MODEL].reshape(DEPTH, bp, N_MEM, D_MODEL)
    mem_v_p = mem_kv[..., D_MODEL:].reshape(DEPTH, bp, N_MEM, D_MODEL)
    gla0 = jnp.zeros((N_A, bp, GLA_HEADS, GLA_DK, GLA_DV), F32)
    zero_win = jnp.zeros((bp, WINDOW, SWA_KV), F32)
    y_p, gla_p, k_p, v_p = _trunk(x_prompt, gla0, zero_win, zero_win, False, mem_k_p, mem_v_p, w)
    y_s, gla_s, k_s, v_s = _trunk(x_sample, state_gla, cache_swa_k.reshape(bs, WINDOW, SWA_KV),
                                  cache_swa_v.reshape(bs, WINDOW, SWA_KV), True,
                                  cache_mem_k.reshape(DEPTH, bs, N_MEM, D_MODEL),
                                  cache_mem_v.reshape(DEPTH, bs, N_MEM, D_MODEL), w)
    heads4 = lambda t: t.reshape(t.shape[0], WINDOW, SWA_KV_HEADS, SWA_HEAD_DIM)
    mem5 = lambda t: t.reshape(DEPTH, bp, N_MEM, MEM_HEADS, MEM_HEAD_DIM)
    return (y_p, y_s, gla_p, gla_s, heads4(k_p), heads4(v_p), heads4(k_s), heads4(v_s), mem5(mem_k_p), mem5(mem_v_p))
```

```python
import functools

import jax
import jax.numpy as jnp
from jax import lax
from jax.experimental import pallas as pl
from jax.experimental.pallas import tpu as pltpu

F32 = jnp.float32
BF16 = jnp.bfloat16

D_MODEL = 1024
DEPTH = 4
CHUNK = 64
N_A = DEPTH // 2
GLA_HEADS = 4
GLA_DK = D_MODEL // (2 * GLA_HEADS)
GLA_DV = D_MODEL // GLA_HEADS
GLA_QK = GLA_HEADS * GLA_DK
GLA_V = GLA_HEADS * GLA_DV
GLA_GATE_RANK = 16
GLA_GATE_TEMP = 16.0
SWA_HEAD_DIM = 64
SWA_Q_HEADS = D_MODEL // SWA_HEAD_DIM
SWA_KV_HEADS = 4
SWA_GROUP = SWA_Q_HEADS // SWA_KV_HEADS
SWA_KV = SWA_KV_HEADS * SWA_HEAD_DIM
WINDOW = 128
WINDOW_CHUNKS = WINDOW // CHUNK
N_MEM = 256
MEM_HEADS = 4
MEM_HEAD_DIM = D_MODEL // MEM_HEADS
N_EXPERTS = 32
TOP_K = 4
D_EXPERT = D_MODEL
SWIGLU_ALPHA = 1.702
SWIGLU_LIMIT = 7.0
DEEPNORM_ALPHA = (2 * DEPTH) ** 0.25
LN_EPS = 1e-5
NEG_INF = -1e30

LANES = 128
VMEM_LIMIT = 56 * 1024 * 1024


def _params(*sem):
    return pltpu.CompilerParams(dimension_semantics=sem, vmem_limit_bytes=VMEM_LIMIT)


def _row_tile(n_rows, want):
    t = min(want, n_rows)
    while n_rows % t:
        t //= 2
    return t


def _full(shape):
    nd = len(shape)
    return pl.BlockSpec(shape, lambda *_: (0,) * nd)


def _bdot(a, b):
    return jnp.dot(a.astype(BF16), b.astype(BF16), preferred_element_type=F32)


def _bdot_nt(a, b):
    return lax.dot_general(a.astype(BF16), b.astype(BF16), (((1,), (1,)), ((), ())),
                           preferred_element_type=F32)


def _bdot_tn(a, b):
    return lax.dot_general(a.astype(BF16), b.astype(BF16), (((0,), (0,)), ((), ())),
                           preferred_element_type=F32)


def _layer_norm(z, g, b):
    mu = jnp.mean(z, axis=-1, keepdims=True)
    zc = z - mu
    var = jnp.mean(zc * zc, axis=-1, keepdims=True)
    return zc * lax.rsqrt(var + LN_EPS) * g + b


def _linear_kernel(x_ref, w_ref, o_ref):
    o_ref[...] = _bdot(x_ref[...], w_ref[...])


def _linear(x, w, tm=512):
    T, K = x.shape
    N = w.shape[1]
    tm = _row_tile(T, tm)
    return pl.pallas_call(
        _linear_kernel,
        out_shape=jax.ShapeDtypeStruct((T, N), F32),
        grid=(T // tm,),
        in_specs=[pl.BlockSpec((tm, K), lambda i: (i, 0)), _full((K, N))],
        out_specs=pl.BlockSpec((tm, N), lambda i: (i, 0)),
        compiler_params=_params("parallel"),
        name="linear",
    )(x, w)


MXU_DIM = 256


def _w1_relayout_kernel(w_ref, o_ref):
    half = MXU_DIM // 2
    r = lax.broadcasted_iota(jnp.int32, (MXU_DIM, MXU_DIM), 0)
    c = lax.broadcasted_iota(jnp.int32, (MXU_DIM, MXU_DIM), 1)
    src = jnp.where(c < half, 2 * c, 2 * (c - half) + 1)
    perm = (r == src).astype(BF16)
    n_out = o_ref.shape[-1] // 2
    for j in range(w_ref.shape[-1] // MXU_DIM):
        y = jnp.dot(w_ref[0, :, j * MXU_DIM:(j + 1) * MXU_DIM].astype(BF16), perm, preferred_element_type=F32)
        o_ref[0, :, j * half:(j + 1) * half] = y[:, :half].astype(BF16)
        o_ref[0, :, n_out + j * half:n_out + (j + 1) * half] = y[:, half:].astype(BF16)


def _w1_relayout(w1):
    E, D, N = w1.shape
    spec = pl.BlockSpec((1, D, N), lambda e: (e, 0, 0))
    return pl.pallas_call(
        _w1_relayout_kernel,
        out_shape=jax.ShapeDtypeStruct((E, D, N), BF16),
        grid=(E,),
        in_specs=[spec],
        out_specs=spec,
        compiler_params=_params("parallel"),
        name="w1_relayout",
    )(w1)


def _gla_in_kernel(x_ref, w_ref, wg_ref, wg2_ref, bg_ref, q_ref, k_ref, v_ref, r_ref, b_ref,
                   *, tm, seq_len, seq_pad):
    xb = x_ref[...].astype(BF16)
    y = jnp.dot(xb, w_ref[...], preferred_element_type=F32)
    q_ref[...] = y[:, :GLA_QK] * (GLA_DK ** -0.5)
    k_ref[...] = y[:, GLA_QK:2 * GLA_QK]
    v_ref[...] = y[:, 2 * GLA_QK:2 * GLA_QK + GLA_V]
    r_ref[...] = y[:, 2 * GLA_QK + GLA_V:]
    g_lr = jnp.dot(xb, wg_ref[...], preferred_element_type=F32)
    z = _bdot(g_lr, wg2_ref[...]) + bg_ref[...]
    log_a = (jnp.minimum(z, 0.0) - jnp.log(1.0 + jnp.exp(-jnp.abs(z)))) / GLA_GATE_TEMP
    if seq_len < seq_pad:
        pos = (pl.program_id(0) * tm + lax.broadcasted_iota(jnp.int32, (tm, 1), 0)) % seq_pad
        log_a = jnp.where(pos < seq_len, log_a, 0.0)
    row = lax.broadcasted_iota(jnp.int32, (tm, tm), 0)
    col = lax.broadcasted_iota(jnp.int32, (tm, tm), 1)
    tri = ((row // CHUNK == col // CHUNK) & (row >= col)).astype(BF16)
    g1 = log_a.astype(BF16)
    r1 = log_a - g1.astype(F32)
    g2 = r1.astype(BF16)
    g3 = (r1 - g2.astype(F32)).astype(BF16)
    b_ref[...] = (jnp.dot(tri, g1, preferred_element_type=F32) + jnp.dot(tri, g2, preferred_element_type=F32)
                  + jnp.dot(tri, g3, preferred_element_type=F32))


def _gla_in(x, w_main, w_g, w_g2, b_g, seq_len, seq_pad, tm=512):
    T = x.shape[0]
    tm = _row_tile(T, tm)
    assert tm % CHUNK == 0 and seq_pad % CHUNK == 0
    n_main = w_main.shape[1]
    row = lambda n: pl.BlockSpec((tm, n), lambda i: (i, 0))
    return pl.pallas_call(
        functools.partial(_gla_in_kernel, tm=tm, seq_len=seq_len, seq_pad=seq_pad),
        out_shape=(jax.ShapeDtypeStruct((T, GLA_QK), F32), jax.ShapeDtypeStruct((T, GLA_QK), F32),
                   jax.ShapeDtypeStruct((T, GLA_V), F32), jax.ShapeDtypeStruct((T, GLA_V), F32),
                   jax.ShapeDtypeStruct((T, GLA_QK), F32)),
        grid=(T // tm,),
        in_specs=[row(D_MODEL), _full((D_MODEL, n_main)), _full((D_MODEL, LANES)),
                  _full((LANES, GLA_QK)), _full((1, GLA_QK))],
        out_specs=(row(GLA_QK), row(GLA_QK), row(GLA_V), row(GLA_V), row(GLA_QK)),
        compiler_params=_params("parallel"),
        name="gla_in",
    )(x, w_main, w_g, w_g2, b_g)


SUBLANES = 8


def _gla_intra(q, k, b):
    n_grp = CHUNK // SUBLANES
    lane = lax.broadcasted_iota(jnp.int32, (SUBLANES, LANES), 1)
    row = lax.broadcasted_iota(jnp.int32, (SUBLANES, LANES), 0)
    grp = lambda t, i: t[i * SUBLANES:(i + 1) * SUBLANES]
    att = [jnp.zeros((SUBLANES, LANES), F32) for _ in range(n_grp)]
    for s in range(CHUNK):
        b_s = b[s:s + 1, :]
        k_s = k[s:s + 1, :]
        for i in range(s // SUBLANES, n_grp):
            e = jnp.exp(grp(b, i) - b_s)
            red = jnp.sum(grp(q, i) * e * k_s, axis=-1, keepdims=True)
            att[i] = jnp.where(lane == s, red, att[i])
    att = [jnp.where(row + i * SUBLANES >= lane, att[i], 0.0) for i in range(n_grp)]
    return jnp.concatenate(att, axis=0)


def _gla_scan_kernel(q_ref, k_ref, v_ref, b_ref, s0_ref, o_ref, s_ref, st_scr, *, n_chunks):
    @pl.when(pl.program_id(1) == 0)
    def _():
        st_scr[...] = s0_ref[0]

    for c in range(n_chunks):
        ts = slice(c * CHUNK, (c + 1) * CHUNK)
        for h in range(GLA_HEADS):
            ks = slice(h * GLA_DK, (h + 1) * GLA_DK)
            vs = slice(h * GLA_DV, (h + 1) * GLA_DV)
            q = q_ref[0, ts, ks]
            k = k_ref[0, ts, ks]
            b = b_ref[0, ts, ks]
            v = v_ref[0, ts, vs]
            att = _gla_intra(q, k, b)
            b_end = b[CHUNK - 1:CHUNK, :]
            st = st_scr[h]
            o_ref[0, ts, vs] = _bdot(att[:, :CHUNK], v) + _bdot_nt(q * jnp.exp(b), st)
            st_scr[h] = st * jnp.exp(b_end) + _bdot_tn(v, k * jnp.exp(b_end - b))

    @pl.when(pl.program_id(1) == pl.num_programs(1) - 1)
    def _():
        s_ref[0] = st_scr[...]


def _gla_scan(q, k, v, la, s0t, tl=128):
    B, Lp, _ = q.shape
    tl = _row_tile(Lp, tl)
    qk_spec = pl.BlockSpec((1, tl, GLA_QK), lambda b, i: (b, i, 0))
    v_spec = pl.BlockSpec((1, tl, GLA_V), lambda b, i: (b, i, 0))
    s_spec = pl.BlockSpec((1, GLA_HEADS, GLA_DV, GLA_DK), lambda b, i: (b, 0, 0, 0))
    return pl.pallas_call(
        functools.partial(_gla_scan_kernel, n_chunks=tl // CHUNK),
        out_shape=(jax.ShapeDtypeStruct((B, Lp, GLA_V), F32),
                   jax.ShapeDtypeStruct((B, GLA_HEADS, GLA_DV, GLA_DK), F32)),
        grid=(B, Lp // tl),
        in_specs=[qk_spec, qk_spec, v_spec, qk_spec, s_spec],
        out_specs=(v_spec, s_spec),
        scratch_shapes=[pltpu.VMEM((GLA_HEADS, GLA_DV, GLA_DK), F32)],
        compiler_params=_params("parallel", "arbitrary"),
        name="gla_scan",
    )(q, k, v, la, s0t)


def _gla_out_kernel(o_ref, r_ref, x_ref, gn_ref, wo_ref, g_ref, b_ref, y_ref):
    parts = []
    for h in range(GLA_HEADS):
        vs = slice(h * GLA_DV, (h + 1) * GLA_DV)
        o = o_ref[:, vs]
        mu = jnp.mean(o, axis=-1, keepdims=True)
        oc = o - mu
        var = jnp.mean(oc * oc, axis=-1, keepdims=True)
        parts.append(oc * lax.rsqrt(var + LN_EPS) * gn_ref[:, vs])
    o = jnp.concatenate(parts, axis=-1)
    r = r_ref[...]
    o = o * (r * jax.nn.sigmoid(r))
    h = _bdot(o, wo_ref[...])
    y_ref[...] = _layer_norm(DEEPNORM_ALPHA * x_ref[...] + h, g_ref[...], b_ref[...])


def _gla_out(o, r, x, gn_g, w_o, ln_g, ln_b, tm=512):
    T = x.shape[0]
    tm = _row_tile(T, tm)
    row = pl.BlockSpec((tm, D_MODEL), lambda i: (i, 0))
    vec = _full((1, D_MODEL))
    return pl.pallas_call(
        _gla_out_kernel,
        out_shape=jax.ShapeDtypeStruct((T, D_MODEL), F32),
        grid=(T // tm,),
        in_specs=[row, row, row, vec, _full((GLA_V, D_MODEL)), vec, vec],
        out_specs=row,
        compiler_params=_params("parallel"),
        name="gla_out",
    )(o, r, x, gn_g, w_o, ln_g, ln_b)


def _swa_kernel(x_ref, k_ref, v_ref, wq_ref, sink_ref, wo_ref, g_ref, b_ref, y_ref,
                *, tl, seq_len, past_valid):
    x = x_ref[0]
    q = (_bdot(x, wq_ref[...]) * (SWA_HEAD_DIM ** -0.5)).astype(BF16)
    kw = tl + WINDOW
    r0 = pl.multiple_of(pl.program_id(1) * tl, tl)
    kt = k_ref[0, pl.ds(r0, kw), :].astype(BF16)
    vt = v_ref[0, pl.ds(r0, kw), :].astype(BF16)
    q_row = lax.broadcasted_iota(jnp.int32, (tl, kw), 0)
    k_col = lax.broadcasted_iota(jnp.int32, (tl, kw), 1)
    band_lo = (q_row // CHUNK) * CHUNK
    pos = r0 + k_col
    ok = (k_col >= band_lo) & (k_col < band_lo + WINDOW + CHUNK) & (pos - WINDOW < seq_len)
    if not past_valid:
        ok = ok & (pos >= WINDOW)
    zeros = jnp.zeros((kw, SWA_HEAD_DIM), BF16)
    pair_out = []
    for g in range(SWA_KV_HEADS):
        kg = kt[:, g * SWA_HEAD_DIM:(g + 1) * SWA_HEAD_DIM]
        vg = vt[:, g * SWA_HEAD_DIM:(g + 1) * SWA_HEAD_DIM]
        k_pad = (jnp.concatenate([kg, zeros], axis=1), jnp.concatenate([zeros, kg], axis=1))
        v_pad = (jnp.concatenate([vg, zeros], axis=1), jnp.concatenate([zeros, vg], axis=1))
        for pair in range(g * SWA_GROUP // 2, (g + 1) * SWA_GROUP // 2):
            qp = q[:, pair * LANES:(pair + 1) * LANES]
            acc = None
            for j in range(2):
                h = 2 * pair + j
                sink = sink_ref[h:h + 1, :]
                s = jnp.where(ok, _bdot_nt(qp, k_pad[j]), NEG_INF)
                m = jnp.maximum(jnp.max(s, axis=-1, keepdims=True), sink)
                e = jnp.exp(s - m)
                p = e * (1.0 / (jnp.sum(e, axis=-1, keepdims=True) + jnp.exp(sink - m)))
                o = _bdot(p, v_pad[j])
                acc = o if acc is None else acc + o
            pair_out.append(acc)
    h_out = _bdot(jnp.concatenate(pair_out, axis=1), wo_ref[...])
    y_ref[0] = _layer_norm(DEEPNORM_ALPHA * x + h_out, g_ref[...], b_ref[...])


def _swa(x, k_all, v_all, w_q, sinks, w_o, ln_g, ln_b, seq_len, past_valid, tl=256):
    B, Lp, _ = x.shape
    tl = _row_tile(Lp, tl)
    x_spec = pl.BlockSpec((1, tl, D_MODEL), lambda b, i: (b, i, 0))
    kv_spec = pl.BlockSpec((1, WINDOW + Lp, SWA_KV), lambda b, i: (b, 0, 0))
    vec = _full((1, D_MODEL))
    return pl.pallas_call(
        functools.partial(_swa_kernel, tl=tl, seq_len=seq_len, past_valid=past_valid),
        out_shape=jax.ShapeDtypeStruct((B, Lp, D_MODEL), F32),
        grid=(B, Lp // tl),
        in_specs=[x_spec, kv_spec, kv_spec, _full((D_MODEL, D_MODEL)), _full((SWA_Q_HEADS, 1)),
                  _full((D_MODEL, D_MODEL)), vec, vec],
        out_specs=x_spec,
        compiler_params=_params("parallel", "arbitrary"),
        name="swa",
    )(x, k_all, v_all, w_q, sinks, w_o, ln_g, ln_b)


def _mem_kernel(x_ref, mk_ref, mv_ref, wq_ref, wo_ref, g_ref, b_ref, y_ref):
    x = x_ref[0]
    q = _bdot(x, wq_ref[...]) * (MEM_HEAD_DIM ** -0.5)
    parts = []
    for h in range(MEM_HEADS):
        hs = slice(h * MEM_HEAD_DIM, (h + 1) * MEM_HEAD_DIM)
        s = _bdot_nt(q[:, hs], mk_ref[0, :, hs])
        m = jnp.max(s, axis=-1, keepdims=True)
        p = jnp.exp(s - m)
        p = p / jnp.sum(p, axis=-1, keepdims=True)
        parts.append(_bdot(p, mv_ref[0, :, hs]))
    o = jnp.concatenate(parts, axis=-1)
    h_out = _bdot(o, wo_ref[...])
    y_ref[0] = _layer_norm(DEEPNORM_ALPHA * x + h_out, g_ref[...], b_ref[...])


def _mem_xattn(x, mk, mv, w_q, w_o, ln_g, ln_b, tl=512):
    B, L, _ = x.shape
    tl = _row_tile(L, tl)
    x_spec = pl.BlockSpec((1, tl, D_MODEL), lambda b, i: (b, i, 0))
    m_spec = pl.BlockSpec((1, N_MEM, D_MODEL), lambda b, i: (b, 0, 0))
    vec = _full((1, D_MODEL))
    return pl.pallas_call(
        _mem_kernel,
        out_shape=jax.ShapeDtypeStruct((B, L, D_MODEL), F32),
        grid=(B, L // tl),
        in_specs=[x_spec, m_spec, m_spec, _full((D_MODEL, D_MODEL)), _full((D_MODEL, D_MODEL)), vec, vec],
        out_specs=x_spec,
        compiler_params=_params("parallel", "arbitrary"),
        name="mem_xattn",
    )(x, mk, mv, w_q, w_o, ln_g, ln_b)


def _router_kernel(x_ref, wr_ref, br_ref, idx_ref, gate_ref, pos_ref, cnt_ref, cnt_scr, *, tm):
    @pl.when(pl.program_id(0) == 0)
    def _():
        cnt_scr[...] = jnp.zeros_like(cnt_scr)

    logits = _bdot(x_ref[...], wr_ref[...]) + br_ref[...]
    lane = lax.broadcasted_iota(jnp.int32, (tm, LANES), 1)
    sel = jnp.zeros((tm, LANES), F32)
    idx_out = jnp.zeros((tm, LANES), jnp.int32)
    val_out = jnp.zeros((tm, LANES), F32)
    l = logits
    idxs = []
    for j in range(TOP_K):
        m = jnp.max(l, axis=-1, keepdims=True)
        idx = jnp.min(jnp.where(l == m, lane, LANES), axis=-1, keepdims=True)
        hit = lane == idx
        sel = jnp.where(hit, 1.0, sel)
        l = jnp.where(hit, -jnp.inf, l)
        idx_out = jnp.where(lane == j, idx, idx_out)
        val_out = jnp.where(lane == j, m, val_out)
        idxs.append(idx)
    e = jnp.where(lane < TOP_K, jnp.exp(val_out - val_out[:, 0:1]), 0.0)
    gates = e / jnp.sum(e, axis=-1, keepdims=True)
    r = lax.broadcasted_iota(jnp.int32, (tm, tm), 0)
    c = lax.broadcasted_iota(jnp.int32, (tm, tm), 1)
    before = jnp.dot((r > c).astype(BF16), sel.astype(BF16), preferred_element_type=F32) + cnt_scr[...]
    pos_out = jnp.zeros((tm, LANES), F32)
    for j in range(TOP_K):
        pj = jnp.sum(jnp.where(lane == idxs[j], before, 0.0), axis=-1, keepdims=True)
        pos_out = jnp.where(lane == j, pj, pos_out)
    cnt_scr[...] = cnt_scr[...] + jnp.sum(sel, axis=0, keepdims=True)
    idx_ref[...] = idx_out[:, :TOP_K]
    gate_ref[...] = gates[:, :TOP_K]
    pos_ref[...] = pos_out[:, :TOP_K].astype(jnp.int32)
    cnt_ref[...] = cnt_scr[...].astype(jnp.int32)


def _router(x, w_r, b_r, tm=512):
    T = x.shape[0]
    tm = _row_tile(T, tm)
    small = pl.BlockSpec((tm, TOP_K), lambda i: (i, 0))
    return pl.pallas_call(
        functools.partial(_router_kernel, tm=tm),
        out_shape=(jax.ShapeDtypeStruct((T, TOP_K), jnp.int32), jax.ShapeDtypeStruct((T, TOP_K), F32),
                   jax.ShapeDtypeStruct((T, TOP_K), jnp.int32), jax.ShapeDtypeStruct((1, LANES), jnp.int32)),
        grid=(T // tm,),
        in_specs=[pl.BlockSpec((tm, D_MODEL), lambda i: (i, 0)), _full((D_MODEL, LANES)), _full((1, LANES))],
        out_specs=(small, small, small, _full((1, LANES))),
        scratch_shapes=[pltpu.VMEM((1, LANES), F32)],
        compiler_params=_params("arbitrary"),
        name="moe_router",
    )(x, w_r, b_r)


def _dispatch_kernel(dest_ref, x_ref, init_ref, xs_ref, sem, *, tm):
    del init_ref

    def row_copy(r, d):
        return pltpu.make_async_copy(x_ref.at[pl.ds(r, 1), :], xs_ref.at[pl.ds(d, 1), :], sem)

    def issue(r, carry):
        for j in range(TOP_K):
            row_copy(r, dest_ref[0, j, r]).start()
        return carry

    lax.fori_loop(0, tm, issue, 0)

    def drain(r, carry):
        for j in range(TOP_K):
            row_copy(r, dest_ref[0, j, r]).wait()
        return carry

    lax.fori_loop(0, tm, drain, 0)


def _dispatch(x, dest_blocks, n_slots, tm):
    T = x.shape[0]
    init = jnp.zeros((n_slots, D_MODEL), F32)
    return pl.pallas_call(
        functools.partial(_dispatch_kernel, tm=tm),
        out_shape=jax.ShapeDtypeStruct((n_slots, D_MODEL), F32),
        grid=(T // tm,),
        in_specs=[pl.BlockSpec((1, TOP_K, tm), lambda i: (i, 0, 0), memory_space=pltpu.SMEM),
                  pl.BlockSpec((tm, D_MODEL), lambda i: (i, 0)),
                  pl.BlockSpec(memory_space=pl.ANY)],
        out_specs=pl.BlockSpec(memory_space=pl.ANY),
        scratch_shapes=[pltpu.SemaphoreType.DMA],
        input_output_aliases={2: 0},
        compiler_params=_params("arbitrary"),
        name="moe_dispatch",
    )(dest_blocks, x, init)


def _collect_kernel(idx_ref, src_ref, out_ref, sem, *, bm):
    def row_copy(r):
        return pltpu.make_async_copy(src_ref.at[pl.ds(idx_ref[0, 0, r], 1), :], out_ref.at[pl.ds(r, 1), :], sem)

    def issue(r, carry):
        row_copy(r).start()
        return carry

    lax.fori_loop(0, bm, issue, 0)

    def drain(r, carry):
        row_copy(r).wait()
        return carry

    lax.fori_loop(0, bm, drain, 0)


def _collect(src, idx, bm=512):
    M = idx.shape[0]
    bm = _row_tile(M, bm)
    return pl.pallas_call(
        functools.partial(_collect_kernel, bm=bm),
        out_shape=jax.ShapeDtypeStruct((M, D_MODEL), F32),
        grid=(M // bm,),
        in_specs=[pl.BlockSpec((1, 1, bm), lambda i: (i, 0, 0), memory_space=pltpu.SMEM),
                  pl.BlockSpec(memory_space=pl.ANY)],
        out_specs=pl.BlockSpec((bm, D_MODEL), lambda i: (i, 0)),
        scratch_shapes=[pltpu.SemaphoreType.DMA],
        compiler_params=_params("arbitrary"),
        name="moe_collect",
    )(idx.reshape(M // bm, 1, bm), src)


def _expert_kernel(be_ref, nb_ref, xs_ref, w1_ref, b1_ref, w2_ref, b2_ref, o_ref):
    @pl.when(pl.program_id(0) < nb_ref[0])
    def _():
        h = _bdot(xs_ref[...], w1_ref[0]) + b1_ref[0]
        glu = jnp.minimum(h[:, :D_EXPERT], SWIGLU_LIMIT)
        lin = jnp.clip(h[:, D_EXPERT:], -SWIGLU_LIMIT, SWIGLU_LIMIT)
        a = glu * jax.nn.sigmoid(SWIGLU_ALPHA * glu) * (lin + 1.0)
        o_ref[...] = _bdot(a, w2_ref[0]) + b2_ref[0]


def _experts(xs, block_e, n_used, w1, b1, w2, b2, blk):
    P = xs.shape[0]
    nb = P // blk

    def blk_map(i, be, nu):
        return (jnp.minimum(i, nu[0] - 1), 0)

    def w_map(i, be, nu):
        return (be[jnp.minimum(i, nu[0] - 1)], 0, 0)

    return pl.pallas_call(
        _expert_kernel,
        out_shape=jax.ShapeDtypeStruct((P, D_MODEL), F32),
        grid_spec=pltpu.PrefetchScalarGridSpec(
            num_scalar_prefetch=2,
            grid=(nb,),
            in_specs=[pl.BlockSpec((blk, D_MODEL), blk_map),
                      pl.BlockSpec((1, D_MODEL, 2 * D_EXPERT), w_map),
                      pl.BlockSpec((1, 1, 2 * D_EXPERT), w_map),
                      pl.BlockSpec((1, D_EXPERT, D_MODEL), w_map),
                      pl.BlockSpec((1, 1, D_MODEL), w_map)],
            out_specs=pl.BlockSpec((blk, D_MODEL), blk_map)),
        compiler_params=_params("arbitrary"),
        name="moe_experts",
    )(block_e, n_used, xs, w1, b1, w2, b2)


def _combine_kernel(y4_ref, gate_ref, x_ref, g_ref, b_ref, y_ref):
    gates = gate_ref[...]
    acc = DEEPNORM_ALPHA * x_ref[...]
    for j in range(TOP_K):
        acc = acc + gates[:, j:j + 1] * y4_ref[j]
    y_ref[...] = _layer_norm(acc, g_ref[...], b_ref[...])


def _combine(y4, gates, x, ln_g, ln_b, tm=512):
    T = x.shape[0]
    tm = _row_tile(T, tm)
    row = pl.BlockSpec((tm, D_MODEL), lambda i: (i, 0))
    vec = _full((1, D_MODEL))
    return pl.pallas_call(
        _combine_kernel,
        out_shape=jax.ShapeDtypeStruct((T, D_MODEL), F32),
        grid=(T // tm,),
        in_specs=[pl.BlockSpec((TOP_K, tm, D_MODEL), lambda i: (0, i, 0)),
                  pl.BlockSpec((tm, TOP_K), lambda i: (i, 0)), row, vec, vec],
        out_specs=row,
        compiler_params=_params("parallel"),
        name="moe_combine",
    )(y4, gates, x, ln_g, ln_b)


def _moe(x, w_r, b_r, w1, b1, w2, b2, ln_g, ln_b):
    T = x.shape[0]
    blk = 512 if T >= 4096 else 128
    tm = _row_tile(T, 512)
    n_assign = T * TOP_K
    n_blocks = -(-(n_assign + N_EXPERTS * (blk - 1)) // blk)
    n_slots = n_blocks * blk

    top_i, gates, pos, counts = _router(x, w_r, b_r)
    counts = counts[0, :N_EXPERTS]
    padded = (counts + blk - 1) // blk * blk
    pad_end = jnp.cumsum(padded)
    pad_start = pad_end - padded
    dest = pad_start[top_i] + pos
    block_start = jnp.arange(n_blocks, dtype=jnp.int32) * blk
    block_e = jnp.minimum(jnp.sum(pad_end[None, :] <= block_start[:, None], axis=1), N_EXPERTS - 1).astype(jnp.int32)
    n_used = (pad_end[-1:] // blk).astype(jnp.int32)
    dest_km = dest.T
    dest_blocks = dest_km.reshape(TOP_K, T // tm, tm).transpose(1, 0, 2)

    xs = _dispatch(x, dest_blocks, n_slots, tm)
    out = _experts(xs, block_e, n_used, w1, b1, w2, b2, blk)
    y4 = _collect(out, dest_km.reshape(n_assign)).reshape(TOP_K, T, D_MODEL)
    return _combine(y4, gates, x, ln_g, ln_b)


def _pad_rows(t, n):
    return t if n == 0 else jnp.pad(t, ((0, 0), (0, n), (0, 0)))


def _trunk(x, gla_s0, past_k, past_v, past_valid, mem_k, mem_v, w):
    B, L, _ = x.shape
    Lp = -(-L // CHUNK) * CHUNK
    T = B * L
    vec = lambda a: a.reshape(1, -1)
    gla_states = []
    k_all = v_all = new_k = new_v = None
    for layer in range(DEPTH):
        lg, lb = w['ln_g'][layer], w['ln_b'][layer]
        xt = x.reshape(T, D_MODEL)
        if layer < N_A:
            i = layer
            xt = _pad_rows(x, Lp - L).reshape(B * Lp, D_MODEL)
            q, k, v, r, b = _gla_in(xt, w['gla_w_main'][i], w['gla_w_g'][i], w['gla_w_g2'][i],
                                    vec(w['gla_b_g'][i]), L, Lp)
            seq = lambda t: t.reshape(B, Lp, -1)
            s0t = jnp.swapaxes(gla_s0[i], -1, -2)
            o, st = _gla_scan(seq(q), seq(k), seq(v), seq(b), s0t)
            gla_states.append(jnp.swapaxes(st, -1, -2))
            xt = _gla_out(o.reshape(B * Lp, GLA_V), r, xt, vec(w['gla_gn_g'][i]), w['gla_w_o'][i],
                          vec(lg[0]), vec(lb[0]))
            x = xt.reshape(B, Lp, D_MODEL)[:, :L]
        else:
            j = layer - N_A
            if j == 0:
                kv = _linear(xt, w['kv_w']).reshape(B, L, 2 * SWA_KV)
                k_full = jnp.concatenate([past_k, kv[..., :SWA_KV]], axis=1)
                v_full = jnp.concatenate([past_v, kv[..., SWA_KV:]], axis=1)
                new_k, new_v = k_full[:, -WINDOW:], v_full[:, -WINDOW:]
                k_all, v_all = _pad_rows(k_full, Lp - L), _pad_rows(v_full, Lp - L)
            xp = _swa(_pad_rows(x, Lp - L), k_all, v_all, w['swa_w_q'][j], w['swa_sinks'][j].reshape(-1, 1),
                      w['swa_w_o'][j], vec(lg[0]), vec(lb[0]), L, past_valid)
            x = xp[:, :L]
        x = _mem_xattn(x, mem_k[layer], mem_v[layer], w['mem_w_q'][layer], w['mem_w_o'][layer],
                       vec(lg[1]), vec(lb[1]))
        xt = _moe(x.reshape(T, D_MODEL), w['moe_w_r'][layer], w['moe_b_r'][layer], w['moe_w1'][layer],
                  w['moe_b1'][layer], w['moe_w2'][layer], w['moe_b2'][layer], vec(lg[2]), vec(lb[2]))
        x = xt.reshape(B, L, D_MODEL)
    return x, jnp.stack(gla_states), new_k, new_v


def _prep_weights(gla_w_in, gla_w_g2, gla_b_g, gla_gn_g, gla_w_o, kv_w, swa_w_q, swa_sinks, swa_w_o,
                  mem_w_q, mem_w_kv, mem_w_o, moe_w_r, moe_b_r, moe_w1, moe_b1, moe_w2, moe_b2, ln_g, ln_b):
    n_main = 2 * GLA_QK + 2 * GLA_V
    pad_c = lambda a, n: jnp.pad(a, [(0, 0)] * (a.ndim - 1) + [(0, n - a.shape[-1])])
    w = dict(
        gla_w_main=gla_w_in[:, :, :n_main].astype(BF16),
        gla_w_g=pad_c(gla_w_in[:, :, n_main:], LANES).astype(BF16),
        gla_w_g2=jnp.pad(gla_w_g2, ((0, 0), (0, LANES - GLA_GATE_RANK), (0, 0))).astype(BF16),
        gla_b_g=gla_b_g, gla_gn_g=gla_gn_g.reshape(N_A, GLA_V), gla_w_o=gla_w_o.astype(BF16),
        kv_w=kv_w.astype(BF16), swa_w_q=swa_w_q.astype(BF16), swa_sinks=swa_sinks, swa_w_o=swa_w_o.astype(BF16),
        mem_w_q=mem_w_q.astype(BF16), mem_w_kv=mem_w_kv.astype(BF16), mem_w_o=mem_w_o.astype(BF16),
        moe_w_r=pad_c(moe_w_r, LANES).astype(BF16),
        moe_b_r=jnp.pad(moe_b_r, ((0, 0), (0, LANES - N_EXPERTS)), constant_values=-jnp.inf).reshape(DEPTH, 1, LANES),
        moe_w1=_w1_relayout(moe_w1.reshape(DEPTH * N_EXPERTS, D_MODEL, 2 * D_EXPERT)).reshape(moe_w1.shape),
        moe_b1=jnp.concatenate([moe_b1[..., 0::2], moe_b1[..., 1::2]], axis=-1).reshape(DEPTH, N_EXPERTS, 1, -1),
        moe_w2=moe_w2.astype(BF16), moe_b2=moe_b2.reshape(DEPTH, N_EXPERTS, 1, D_MODEL),
        ln_g=ln_g, ln_b=ln_b)
    return w


def kernel(x_prompt, x_sample, state_gla, cache_swa_k, cache_swa_v, cache_mem_k, cache_mem_v, mem_prompt, gla_w_in, gla_w_g2, gla_b_g, gla_gn_g, gla_w_o, kv_w, swa_w_q, swa_sinks, swa_w_o, mem_w_q, mem_w_kv, mem_w_o, moe_w_r, moe_b_r, moe_w1, moe_b1, moe_w2, moe_b2, ln_g, ln_b):
    w = _prep_weights(gla_w_in, gla_w_g2, gla_b_g, gla_gn_g, gla_w_o, kv_w, swa_w_q, swa_sinks, swa_w_o,
                      mem_w_q, mem_w_kv, mem_w_o, moe_w_r, moe_b_r, moe_w1, moe_b1, moe_w2, moe_b2, ln_g, ln_b)
    bp = x_prompt.shape[0]
    bs = x_sample.shape[0]
    mem_flat = mem_prompt.reshape(bp * N_MEM, D_MODEL)
    mem_kv = jnp.stack([_linear(mem_flat, w['mem_w_kv'][l]) for l in range(DEPTH)])
    mem_k_p = mem_kv[..., :D_MODEL].reshape(DEPTH, bp, N_MEM, D_MODEL)
    mem_v_p = mem_kv[..., D_MODEL:].reshape(DEPTH, bp, N_MEM, D_MODEL)
    gla0 = jnp.zeros((N_A, bp, GLA_HEADS, GLA_DK, GLA_DV), F32)
    zero_win = jnp.zeros((bp, WINDOW, SWA_KV), F32)
    y_p, gla_p, k_p, v_p = _trunk(x_prompt, gla0, zero_win, zero_win, False, mem_k_p, mem_v_p, w)
    y_s, gla_s, k_s, v_s = _trunk(x_sample, state_gla, cache_swa_k.reshape(bs, WINDOW, SWA_KV),
                                  cache_swa_v.reshape(bs, WINDOW, SWA_KV), True,
                                  cache_mem_k.reshape(DEPTH, bs, N_MEM, D_MODEL),
                                  cache_mem_v.reshape(DEPTH, bs, N_MEM, D_MODEL), w)
    heads4 = lambda t: t.reshape(t.shape[0], WINDOW, SWA_KV_HEADS, SWA_HEAD_DIM)
    mem5 = lambda t: t.reshape(DEPTH, bp, N_MEM, MEM_HEADS, MEM_HEAD_DIM)
    return (y_p, y_s, gla_p, gla_s, heads4(k_p), heads4(v_p), heads4(k_s), heads4(v_s), mem5(mem_k_p), mem5(mem_v_p))
```

```python
import functools

import jax
import jax.numpy as jnp
from jax import lax
from jax.experimental import pallas as pl
from jax.experimental.pallas import tpu as pltpu

F32 = jnp.float32
BF16 = jnp.bfloat16

D_MODEL = 1024
DEPTH = 4
CHUNK = 64
N_A = DEPTH // 2
GLA_HEADS = 4
GLA_DK = D_MODEL // (2 * GLA_HEADS)
GLA_DV = D_MODEL // GLA_HEADS
GLA_QK = GLA_HEADS * GLA_DK
GLA_V = GLA_HEADS * GLA_DV
GLA_GATE_RANK = 16
GLA_GATE_TEMP = 16.0
SWA_HEAD_DIM = 64
SWA_Q_HEADS = D_MODEL // SWA_HEAD_DIM
SWA_KV_HEADS = 4
SWA_GROUP = SWA_Q_HEADS // SWA_KV_HEADS
SWA_KV = SWA_KV_HEADS * SWA_HEAD_DIM
WINDOW = 128
WINDOW_CHUNKS = WINDOW // CHUNK
N_MEM = 256
MEM_HEADS = 4
MEM_HEAD_DIM = D_MODEL // MEM_HEADS
N_EXPERTS = 32
TOP_K = 4
D_EXPERT = D_MODEL
SWIGLU_ALPHA = 1.702
SWIGLU_LIMIT = 7.0
DEEPNORM_ALPHA = (2 * DEPTH) ** 0.25
LN_EPS = 1e-5
NEG_INF = -1e30

LANES = 128
VMEM_LIMIT = 56 * 1024 * 1024


def _params(*sem):
    return pltpu.CompilerParams(dimension_semantics=sem, vmem_limit_bytes=VMEM_LIMIT)


def _row_tile(n_rows, want):
    t = min(want, n_rows)
    while n_rows % t:
        t //= 2
    return t


def _full(shape):
    nd = len(shape)
    return pl.BlockSpec(shape, lambda *_: (0,) * nd)


def _bdot(a, b):
    return jnp.dot(a.astype(BF16), b.astype(BF16), preferred_element_type=F32)


def _bdot_nt(a, b):
    return lax.dot_general(a.astype(BF16), b.astype(BF16), (((1,), (1,)), ((), ())),
                           preferred_element_type=F32)


def _bdot_tn(a, b):
    return lax.dot_general(a.astype(BF16), b.astype(BF16), (((0,), (0,)), ((), ())),
                           preferred_element_type=F32)


def _layer_norm(z, g, b):
    mu = jnp.mean(z, axis=-1, keepdims=True)
    zc = z - mu
    var = jnp.mean(zc * zc, axis=-1, keepdims=True)
    return zc * lax.rsqrt(var + LN_EPS) * g + b


def _linear_kernel(x_ref, w_ref, o_ref):
    o_ref[...] = _bdot(x_ref[...], w_ref[...])


def _linear(x, w, tm=512):
    T, K = x.shape
    N = w.shape[1]
    tm = _row_tile(T, tm)
    return pl.pallas_call(
        _linear_kernel,
        out_shape=jax.ShapeDtypeStruct((T, N), F32),
        grid=(T // tm,),
        in_specs=[pl.BlockSpec((tm, K), lambda i: (i, 0)), _full((K, N))],
        out_specs=pl.BlockSpec((tm, N), lambda i: (i, 0)),
        compiler_params=_params("parallel"),
        name="linear",
    )(x, w)


MXU_DIM = 256


def _w1_relayout_kernel(w_ref, o_ref):
    half = MXU_DIM // 2
    r = lax.broadcasted_iota(jnp.int32, (MXU_DIM, MXU_DIM), 0)
    c = lax.broadcasted_iota(jnp.int32, (MXU_DIM, MXU_DIM), 1)
    src = jnp.where(c < half, 2 * c, 2 * (c - half) + 1)
    perm = (r == src).astype(BF16)
    n_out = o_ref.shape[-1] // 2
    for j in range(w_ref.shape[-1] // MXU_DIM):
        y = jnp.dot(w_ref[0, :, j * MXU_DIM:(j + 1) * MXU_DIM].astype(BF16), perm, preferred_element_type=F32)
        o_ref[0, :, j * half:(j + 1) * half] = y[:, :half].astype(BF16)
        o_ref[0, :, n_out + j * half:n_out + (j + 1) * half] = y[:, half:].astype(BF16)


def _w1_relayout(w1):
    E, D, N = w1.shape
    spec = pl.BlockSpec((1, D, N), lambda e: (e, 0, 0))
    return pl.pallas_call(
        _w1_relayout_kernel,
        out_shape=jax.ShapeDtypeStruct((E, D, N), BF16),
        grid=(E,),
        in_specs=[spec],
        out_specs=spec,
        compiler_params=_params("parallel"),
        name="w1_relayout",
    )(w1)


def _gla_in_kernel(x_ref, w_ref, wg_ref, wg2_ref, bg_ref, q_ref, k_ref, v_ref, r_ref, b_ref,
                   *, tm, seq_len, seq_pad):
    xb = x_ref[...].astype(BF16)
    y = jnp.dot(xb, w_ref[...], preferred_element_type=F32)
    q_ref[...] = y[:, :GLA_QK] * (GLA_DK ** -0.5)
    k_ref[...] = y[:, GLA_QK:2 * GLA_QK]
    v_ref[...] = y[:, 2 * GLA_QK:2 * GLA_QK + GLA_V]
    r_ref[...] = y[:, 2 * GLA_QK + GLA_V:]
    g_lr = jnp.dot(xb, wg_ref[...], preferred_element_type=F32)
    z = _bdot(g_lr, wg2_ref[...]) + bg_ref[...]
    log_a = (jnp.minimum(z, 0.0) - jnp.log(1.0 + jnp.exp(-jnp.abs(z)))) / GLA_GATE_TEMP
    if seq_len < seq_pad:
        pos = (pl.program_id(0) * tm + lax.broadcasted_iota(jnp.int32, (tm, 1), 0)) % seq_pad
        log_a = jnp.where(pos < seq_len, log_a, 0.0)
    row = lax.broadcasted_iota(jnp.int32, (tm, tm), 0)
    col = lax.broadcasted_iota(jnp.int32, (tm, tm), 1)
    tri = ((row // CHUNK == col // CHUNK) & (row >= col)).astype(BF16)
    g1 = log_a.astype(BF16)
    r1 = log_a - g1.astype(F32)
    g2 = r1.astype(BF16)
    g3 = (r1 - g2.astype(F32)).astype(BF16)
    b_ref[...] = (jnp.dot(tri, g1, preferred_element_type=F32) + jnp.dot(tri, g2, preferred_element_type=F32)
                  + jnp.dot(tri, g3, preferred_element_type=F32))


def _gla_in(x, w_main, w_g, w_g2, b_g, seq_len, seq_pad, tm=512):
    T = x.shape[0]
    tm = _row_tile(T, tm)
    assert tm % CHUNK == 0 and seq_pad % CHUNK == 0
    n_main = w_main.shape[1]
    row = lambda n: pl.BlockSpec((tm, n), lambda i: (i, 0))
    return pl.pallas_call(
        functools.partial(_gla_in_kernel, tm=tm, seq_len=seq_len, seq_pad=seq_pad),
        out_shape=(jax.ShapeDtypeStruct((T, GLA_QK), F32), jax.ShapeDtypeStruct((T, GLA_QK), F32),
                   jax.ShapeDtypeStruct((T, GLA_V), F32), jax.ShapeDtypeStruct((T, GLA_V), F32),
                   jax.ShapeDtypeStruct((T, GLA_QK), F32)),
        grid=(T // tm,),
        in_specs=[row(D_MODEL), _full((D_MODEL, n_main)), _full((D_MODEL, LANES)),
                  _full((LANES, GLA_QK)), _full((1, GLA_QK))],
        out_specs=(row(GLA_QK), row(GLA_QK), row(GLA_V), row(GLA_V), row(GLA_QK)),
        compiler_params=_params("parallel"),
        name="gla_in",
    )(x, w_main, w_g, w_g2, b_g)


SUBLANES = 8


def _gla_intra(q, k, b):
    n_grp = CHUNK // SUBLANES
    lane = lax.broadcasted_iota(jnp.int32, (SUBLANES, LANES), 1)
    row = lax.broadcasted_iota(jnp.int32, (SUBLANES, LANES), 0)
    grp = lambda t, i: t[i * SUBLANES:(i + 1) * SUBLANES]
    att = [jnp.zeros((SUBLANES, LANES), F32) for _ in range(n_grp)]
    for s in range(CHUNK):
        b_s = b[s:s + 1, :]
        k_s = k[s:s + 1, :]
        for i in range(s // SUBLANES, n_grp):
            e = jnp.exp(grp(b, i) - b_s)
            red = jnp.sum(grp(q, i) * e * k_s, axis=-1, keepdims=True)
            att[i] = jnp.where(lane == s, red, att[i])
    att = [jnp.where(row + i * SUBLANES >= lane, att[i], 0.0) for i in range(n_grp)]
    return jnp.concatenate(att, axis=0)


def _gla_scan_kernel(q_ref, k_ref, v_ref, b_ref, s0_ref, o_ref, s_ref, st_scr, *, n_chunks):
    @pl.when(pl.program_id(1) == 0)
    def _():
        st_scr[...] = s0_ref[0]

    for c in range(n_chunks):
        ts = slice(c * CHUNK, (c + 1) * CHUNK)
        for h in range(GLA_HEADS):
            ks = slice(h * GLA_DK, (h + 1) * GLA_DK)
            vs = slice(h * GLA_DV, (h + 1) * GLA_DV)
            q = q_ref[0, ts, ks]
            k = k_ref[0, ts, ks]
            b = b_ref[0, ts, ks]
            v = v_ref[0, ts, vs]
            att = _gla_intra(q, k, b)
            b_end = b[CHUNK - 1:CHUNK, :]
            st = st_scr[h]
            o_ref[0, ts, vs] = _bdot(att[:, :CHUNK], v) + _bdot_nt(q * jnp.exp(b), st)
            st_scr[h] = st * jnp.exp(b_end) + _bdot_tn(v, k * jnp.exp(b_end - b))

    @pl.when(pl.program_id(1) == pl.num_programs(1) - 1)
    def _():
        s_ref[0] = st_scr[...]


def _gla_scan(q, k, v, la, s0t, tl=128):
    B, Lp, _ = q.shape
    tl = _row_tile(Lp, tl)
    qk_spec = pl.BlockSpec((1, tl, GLA_QK), lambda b, i: (b, i, 0))
    v_spec = pl.BlockSpec((1, tl, GLA_V), lambda b, i: (b, i, 0))
    s_spec = pl.BlockSpec((1, GLA_HEADS, GLA_DV, GLA_DK), lambda b, i: (b, 0, 0, 0))
    return pl.pallas_call(
        functools.partial(_gla_scan_kernel, n_chunks=tl // CHUNK),
        out_shape=(jax.ShapeDtypeStruct((B, Lp, GLA_V), F32),
                   jax.ShapeDtypeStruct((B, GLA_HEADS, GLA_DV, GLA_DK), F32)),
        grid=(B, Lp // tl),
        in_specs=[qk_spec, qk_spec, v_spec, qk_spec, s_spec],
        out_specs=(v_spec, s_spec),
        scratch_shapes=[pltpu.VMEM((GLA_HEADS, GLA_DV, GLA_DK), F32)],
        compiler_params=_params("parallel", "arbitrary"),
        name="gla_scan",
    )(q, k, v, la, s0t)


def _gla_out_kernel(o_ref, r_ref, x_ref, gn_ref, wo_ref, g_ref, b_ref, y_ref):
    parts = []
    for h in range(GLA_HEADS):
        vs = slice(h * GLA_DV, (h + 1) * GLA_DV)
        o = o_ref[:, vs]
        mu = jnp.mean(o, axis=-1, keepdims=True)
        oc = o - mu
        var = jnp.mean(oc * oc, axis=-1, keepdims=True)
        parts.append(oc * lax.rsqrt(var + LN_EPS) * gn_ref[:, vs])
    o = jnp.concatenate(parts, axis=-1)
    r = r_ref[...]
    o = o * (r * jax.nn.sigmoid(r))
    h = _bdot(o, wo_ref[...])
    y_ref[...] = _layer_norm(DEEPNORM_ALPHA * x_ref[...] + h, g_ref[...], b_ref[...])


def _gla_out(o, r, x, gn_g, w_o, ln_g, ln_b, tm=512):
    T = x.shape[0]
    tm = _row_tile(T, tm)
    row = pl.BlockSpec((tm, D_MODEL), lambda i: (i, 0))
    vec = _full((1, D_MODEL))
    return pl.pallas_call(
        _gla_out_kernel,
        out_shape=jax.ShapeDtypeStruct((T, D_MODEL), F32),
        grid=(T // tm,),
        in_specs=[row, row, row, vec, _full((GLA_V, D_MODEL)), vec, vec],
        out_specs=row,
        compiler_params=_params("parallel"),
        name="gla_out",
    )(o, r, x, gn_g, w_o, ln_g, ln_b)


def _swa_kernel(x_ref, k_ref, v_ref, wq_ref, sink_ref, wo_ref, g_ref, b_ref, y_ref,
                *, tl, seq_len, past_valid):
    x = x_ref[0]
    q = (_bdot(x, wq_ref[...]) * (SWA_HEAD_DIM ** -0.5)).astype(BF16)
    kw = tl + WINDOW
    r0 = pl.multiple_of(pl.program_id(1) * tl, tl)
    kt = k_ref[0, pl.ds(r0, kw), :].astype(BF16)
    vt = v_ref[0, pl.ds(r0, kw), :].astype(BF16)
    q_row = lax.broadcasted_iota(jnp.int32, (tl, kw), 0)
    k_col = lax.broadcasted_iota(jnp.int32, (tl, kw), 1)
    band_lo = (q_row // CHUNK) * CHUNK
    pos = r0 + k_col
    ok = (k_col >= band_lo) & (k_col < band_lo + WINDOW + CHUNK) & (pos - WINDOW < seq_len)
    if not past_valid:
        ok = ok & (pos >= WINDOW)
    zeros = jnp.zeros((kw, SWA_HEAD_DIM), BF16)
    pair_out = []
    for g in range(SWA_KV_HEADS):
        kg = kt[:, g * SWA_HEAD_DIM:(g + 1) * SWA_HEAD_DIM]
        vg = vt[:, g * SWA_HEAD_DIM:(g + 1) * SWA_HEAD_DIM]
        k_pad = (jnp.concatenate([kg, zeros], axis=1), jnp.concatenate([zeros, kg], axis=1))
        v_pad = (jnp.concatenate([vg, zeros], axis=1), jnp.concatenate([zeros, vg], axis=1))
        for pair in range(g * SWA_GROUP // 2, (g + 1) * SWA_GROUP // 2):
            qp = q[:, pair * LANES:(pair + 1) * LANES]
            acc = None
            for j in range(2):
                h = 2 * pair + j
                sink = sink_ref[h:h + 1, :]
                s = jnp.where(ok, _bdot_nt(qp, k_pad[j]), NEG_INF)
                m = jnp.maximum(jnp.max(s, axis=-1, keepdims=True), sink)
                e = jnp.exp(s - m)
                p = e * (1.0 / (jnp.sum(e, axis=-1, keepdims=True) + jnp.exp(sink - m)))
                o = _bdot(p, v_pad[j])
                acc = o if acc is None else acc + o
            pair_out.append(acc)
    h_out = _bdot(jnp.concatenate(pair_out, axis=1), wo_ref[...])
    y_ref[0] = _layer_norm(DEEPNORM_ALPHA * x + h_out, g_ref[...], b_ref[...])


def _swa(x, k_all, v_all, w_q, sinks, w_o, ln_g, ln_b, seq_len, past_valid, tl=256):
    B, Lp, _ = x.shape
    tl = _row_tile(Lp, tl)
    x_spec = pl.BlockSpec((1, tl, D_MODEL), lambda b, i: (b, i, 0))
    kv_spec = pl.BlockSpec((1, WINDOW + Lp, SWA_KV), lambda b, i: (b, 0, 0))
    vec = _full((1, D_MODEL))
    return pl.pallas_call(
        functools.partial(_swa_kernel, tl=tl, seq_len=seq_len, past_valid=past_valid),
        out_shape=jax.ShapeDtypeStruct((B, Lp, D_MODEL), F32),
        grid=(B, Lp // tl),
        in_specs=[x_spec, kv_spec, kv_spec, _full((D_MODEL, D_MODEL)), _full((SWA_Q_HEADS, 1)),
                  _full((D_MODEL, D_MODEL)), vec, vec],
        out_specs=x_spec,
        compiler_params=_params("parallel", "arbitrary"),
        name="swa",
    )(x, k_all, v_all, w_q, sinks, w_o, ln_g, ln_b)


def _mem_kernel(x_ref, mk_ref, mv_ref, wq_ref, wo_ref, g_ref, b_ref, y_ref):
    x = x_ref[0]
    q = _bdot(x, wq_ref[...]) * (MEM_HEAD_DIM ** -0.5)
    parts = []
    for h in range(MEM_HEADS):
        hs = slice(h * MEM_HEAD_DIM, (h + 1) * MEM_HEAD_DIM)
        s = _bdot_nt(q[:, hs], mk_ref[0, :, hs])
        m = jnp.max(s, axis=-1, keepdims=True)
        p = jnp.exp(s - m)
        p = p / jnp.sum(p, axis=-1, keepdims=True)
        parts.append(_bdot(p, mv_ref[0, :, hs]))
    o = jnp.concatenate(parts, axis=-1)
    h_out = _bdot(o, wo_ref[...])
    y_ref[0] = _layer_norm(DEEPNORM_ALPHA * x + h_out, g_ref[...], b_ref[...])


def _mem_xattn(x, mk, mv, w_q, w_o, ln_g, ln_b, tl=512):
    B, L, _ = x.shape
    tl = _row_tile(L, tl)
    x_spec = pl.BlockSpec((1, tl, D_MODEL), lambda b, i: (b, i, 0))
    m_spec = pl.BlockSpec((1, N_MEM, D_MODEL), lambda b, i: (b, 0, 0))
    vec = _full((1, D_MODEL))
    return pl.pallas_call(
        _mem_kernel,
        out_shape=jax.ShapeDtypeStruct((B, L, D_MODEL), F32),
        grid=(B, L // tl),
        in_specs=[x_spec, m_spec, m_spec, _full((D_MODEL, D_MODEL)), _full((D_MODEL, D_MODEL)), vec, vec],
        out_specs=x_spec,
        compiler_params=_params("parallel", "arbitrary"),
        name="mem_xattn",
    )(x, mk, mv, w_q, w_o, ln_g, ln_b)


def _router_kernel(x_ref, wr_ref, br_ref, idx_ref, gate_ref, pos_ref, cnt_ref, cnt_scr, *, tm):
    @pl.when(pl.program_id(0) == 0)
    def _():
        cnt_scr[...] = jnp.zeros_like(cnt_scr)

    logits = _bdot(x_ref[...], wr_ref[...]) + br_ref[...]
    lane = lax.broadcasted_iota(jnp.int32, (tm, LANES), 1)
    sel = jnp.zeros((tm, LANES), F32)
    idx_out = jnp.zeros((tm, LANES), jnp.int32)
    val_out = jnp.zeros((tm, LANES), F32)
    l = logits
    idxs = []
    for j in range(TOP_K):
        m = jnp.max(l, axis=-1, keepdims=True)
        idx = jnp.min(jnp.where(l == m, lane, LANES), axis=-1, keepdims=True)
        hit = lane == idx
        sel = jnp.where(hit, 1.0, sel)
        l = jnp.where(hit, -jnp.inf, l)
        idx_out = jnp.where(lane == j, idx, idx_out)
        val_out = jnp.where(lane == j, m, val_out)
        idxs.append(idx)
    e = jnp.where(lane < TOP_K, jnp.exp(val_out - val_out[:, 0:1]), 0.0)
    gates = e / jnp.sum(e, axis=-1, keepdims=True)
    r = lax.broadcasted_iota(jnp.int32, (tm, tm), 0)
    c = lax.broadcasted_iota(jnp.int32, (tm, tm), 1)
    before = jnp.dot((r > c).astype(BF16), sel.astype(BF16), preferred_element_type=F32) + cnt_scr[...]
    pos_out = jnp.zeros((tm, LANES), F32)
    for j in range(TOP_K):
        pj = jnp.sum(jnp.where(lane == idxs[j], before, 0.0), axis=-1, keepdims=True)
        pos_out = jnp.where(lane == j, pj, pos_out)
    cnt_scr[...] = cnt_scr[...] + jnp.sum(sel, axis=0, keepdims=True)
    idx_ref[...] = idx_out[:, :TOP_K]
    gate_ref[...] = gates[:, :TOP_K]
    pos_ref[...] = pos_out[:, :TOP_K].astype(jnp.int32)
    cnt_ref[...] = cnt_scr[...].astype(jnp.int32)


def _router(x, w_r, b_r, tm=512):
    T = x.shape[0]
    tm = _row_tile(T, tm)
    small = pl.BlockSpec((tm, TOP_K), lambda i: (i, 0))
    return pl.pallas_call(
        functools.partial(_router_kernel, tm=tm),
        out_shape=(jax.ShapeDtypeStruct((T, TOP_K), jnp.int32), jax.ShapeDtypeStruct((T, TOP_K), F32),
                   jax.ShapeDtypeStruct((T, TOP_K), jnp.int32), jax.ShapeDtypeStruct((1, LANES), jnp.int32)),
        grid=(T // tm,),
        in_specs=[pl.BlockSpec((tm, D_MODEL), lambda i: (i, 0)), _full((D_MODEL, LANES)), _full((1, LANES))],
        out_specs=(small, small, small, _full((1, LANES))),
        scratch_shapes=[pltpu.VMEM((1, LANES), F32)],
        compiler_params=_params("arbitrary"),
        name="moe_router",
    )(x, w_r, b_r)


ISSUE_UNROLL = 8


def _dispatch_kernel(pad_end_ref, dest_ref, x_ref, xs_ref, zero_scr, sem, zsem, *, tm, blk):
    @pl.when(pl.program_id(0) == 0)
    def _():
        zero_scr[...] = jnp.zeros_like(zero_scr)

        def fill(e):
            start = pl.multiple_of(pad_end_ref[e] - blk, blk)
            return pltpu.make_async_copy(zero_scr, xs_ref.at[pl.ds(start, blk), :], zsem)

        def has_block(e):
            return pad_end_ref[e] > (pad_end_ref[e - 1] if e else 0)

        for e in range(N_EXPERTS):
            pl.when(has_block(e))(lambda e=e: fill(e).start())
        for e in range(N_EXPERTS):
            pl.when(has_block(e))(lambda e=e: fill(e).wait())

    def issue(g, carry):
        base = pl.multiple_of(g * (ISSUE_UNROLL // TOP_K), ISSUE_UNROLL // TOP_K)
        for u in range(ISSUE_UNROLL // TOP_K):
            r = base + u
            for j in range(TOP_K):
                pltpu.make_async_copy(x_ref.at[pl.ds(r, 1), :], xs_ref.at[pl.ds(dest_ref[0, j, r], 1), :], sem).start()
        return carry

    lax.fori_loop(0, tm * TOP_K // ISSUE_UNROLL, issue, 0)
    for j in range(TOP_K):
        pltpu.make_async_copy(x_ref, xs_ref.at[pl.ds(0, tm), :], sem).wait()


def _dispatch(x, dest_blocks, pad_end, n_slots, tm, blk):
    T = x.shape[0]
    return pl.pallas_call(
        functools.partial(_dispatch_kernel, tm=tm, blk=blk),
        out_shape=jax.ShapeDtypeStruct((n_slots, D_MODEL), F32),
        grid_spec=pltpu.PrefetchScalarGridSpec(
            num_scalar_prefetch=1,
            grid=(T // tm,),
            in_specs=[pl.BlockSpec((1, TOP_K, tm), lambda i, pe: (i, 0, 0), memory_space=pltpu.SMEM),
                      pl.BlockSpec((tm, D_MODEL), lambda i, pe: (i, 0))],
            out_specs=pl.BlockSpec(memory_space=pl.ANY),
            scratch_shapes=[pltpu.VMEM((blk, D_MODEL), F32), pltpu.SemaphoreType.DMA, pltpu.SemaphoreType.DMA]),
        compiler_params=_params("arbitrary"),
        name="moe_dispatch",
    )(pad_end, dest_blocks, x)


def _collect_kernel(idx_ref, src_ref, out_ref, sem, *, bm):
    def issue(g, carry):
        base = pl.multiple_of(g * ISSUE_UNROLL, ISSUE_UNROLL)
        for u in range(ISSUE_UNROLL):
            r = base + u
            pltpu.make_async_copy(src_ref.at[pl.ds(idx_ref[0, 0, r], 1), :], out_ref.at[pl.ds(r, 1), :], sem).start()
        return carry

    lax.fori_loop(0, bm // ISSUE_UNROLL, issue, 0)
    pltpu.make_async_copy(src_ref.at[pl.ds(0, bm), :], out_ref, sem).wait()


def _collect(src, idx, bm=512):
    M = idx.shape[0]
    bm = _row_tile(M, bm)
    return pl.pallas_call(
        functools.partial(_collect_kernel, bm=bm),
        out_shape=jax.ShapeDtypeStruct((M, D_MODEL), F32),
        grid=(M // bm,),
        in_specs=[pl.BlockSpec((1, 1, bm), lambda i: (i, 0, 0), memory_space=pltpu.SMEM),
                  pl.BlockSpec(memory_space=pl.ANY)],
        out_specs=pl.BlockSpec((bm, D_MODEL), lambda i: (i, 0)),
        scratch_shapes=[pltpu.SemaphoreType.DMA],
        compiler_params=_params("arbitrary"),
        name="moe_collect",
    )(idx.reshape(M // bm, 1, bm), src)


def _expert_kernel(be_ref, nb_ref, xs_ref, w1_ref, b1_ref, w2_ref, b2_ref, o_ref):
    @pl.when(pl.program_id(0) < nb_ref[0])
    def _():
        h = _bdot(xs_ref[...], w1_ref[0]) + b1_ref[0]
        glu = jnp.minimum(h[:, :D_EXPERT], SWIGLU_LIMIT)
        lin = jnp.clip(h[:, D_EXPERT:], -SWIGLU_LIMIT, SWIGLU_LIMIT)
        a = glu * jax.nn.sigmoid(SWIGLU_ALPHA * glu) * (lin + 1.0)
        o_ref[...] = _bdot(a, w2_ref[0]) + b2_ref[0]


def _experts(xs, block_e, n_used, w1, b1, w2, b2, blk):
    P = xs.shape[0]
    nb = P // blk

    def blk_map(i, be, nu):
        return (jnp.minimum(i, nu[0] - 1), 0)

    def w_map(i, be, nu):
        return (be[jnp.minimum(i, nu[0] - 1)], 0, 0)

    return pl.pallas_call(
        _expert_kernel,
        out_shape=jax.ShapeDtypeStruct((P, D_MODEL), F32),
        grid_spec=pltpu.PrefetchScalarGridSpec(
            num_scalar_prefetch=2,
            grid=(nb,),
            in_specs=[pl.BlockSpec((blk, D_MODEL), blk_map),
                      pl.BlockSpec((1, D_MODEL, 2 * D_EXPERT), w_map),
                      pl.BlockSpec((1, 1, 2 * D_EXPERT), w_map),
                      pl.BlockSpec((1, D_EXPERT, D_MODEL), w_map),
                      pl.BlockSpec((1, 1, D_MODEL), w_map)],
            out_specs=pl.BlockSpec((blk, D_MODEL), blk_map)),
        compiler_params=_params("arbitrary"),
        name="moe_experts",
    )(block_e, n_used, xs, w1, b1, w2, b2)


def _combine_kernel(y4_ref, gate_ref, x_ref, g_ref, b_ref, y_ref):
    gates = gate_ref[...]
    acc = DEEPNORM_ALPHA * x_ref[...]
    for j in range(TOP_K):
        acc = acc + gates[:, j:j + 1] * y4_ref[j]
    y_ref[...] = _layer_norm(acc, g_ref[...], b_ref[...])


def _combine(y4, gates, x, ln_g, ln_b, tm=512):
    T = x.shape[0]
    tm = _row_tile(T, tm)
    row = pl.BlockSpec((tm, D_MODEL), lambda i: (i, 0))
    vec = _full((1, D_MODEL))
    return pl.pallas_call(
        _combine_kernel,
        out_shape=jax.ShapeDtypeStruct((T, D_MODEL), F32),
        grid=(T // tm,),
        in_specs=[pl.BlockSpec((TOP_K, tm, D_MODEL), lambda i: (0, i, 0)),
                  pl.BlockSpec((tm, TOP_K), lambda i: (i, 0)), row, vec, vec],
        out_specs=row,
        compiler_params=_params("parallel"),
        name="moe_combine",
    )(y4, gates, x, ln_g, ln_b)


def _moe(x, w_r, b_r, w1, b1, w2, b2, ln_g, ln_b):
    T = x.shape[0]
    blk = 512 if T >= 4096 else 128
    tm = _row_tile(T, 512)
    n_assign = T * TOP_K
    n_blocks = -(-(n_assign + N_EXPERTS * (blk - 1)) // blk)
    n_slots = n_blocks * blk

    top_i, gates, pos, counts = _router(x, w_r, b_r)
    counts = counts[0, :N_EXPERTS]
    padded = (counts + blk - 1) // blk * blk
    pad_end = jnp.cumsum(padded)
    pad_start = pad_end - padded
    dest = pad_start[top_i] + pos
    block_start = jnp.arange(n_blocks, dtype=jnp.int32) * blk
    block_e = jnp.minimum(jnp.sum(pad_end[None, :] <= block_start[:, None], axis=1), N_EXPERTS - 1).astype(jnp.int32)
    n_used = (pad_end[-1:] // blk).astype(jnp.int32)
    dest_km = dest.T
    dest_blocks = dest_km.reshape(TOP_K, T // tm, tm).transpose(1, 0, 2)

    xs = _dispatch(x, dest_blocks, pad_end.astype(jnp.int32), n_slots, tm, blk)
    out = _experts(xs, block_e, n_used, w1, b1, w2, b2, blk)
    y4 = _collect(out, dest_km.reshape(n_assign)).reshape(TOP_K, T, D_MODEL)
    return _combine(y4, gates, x, ln_g, ln_b)


def _pad_rows(t, n):
    return t if n == 0 else jnp.pad(t, ((0, 0), (0, n), (0, 0)))


def _trunk(x, gla_s0, past_k, past_v, past_valid, mem_k, mem_v, w):
    B, L, _ = x.shape
    Lp = -(-L // CHUNK) * CHUNK
    T = B * L
    vec = lambda a: a.reshape(1, -1)
    gla_states = []
    k_all = v_all = new_k = new_v = None
    for layer in range(DEPTH):
        lg, lb = w['ln_g'][layer], w['ln_b'][layer]
        xt = x.reshape(T, D_MODEL)
        if layer < N_A:
            i = layer
            xt = _pad_rows(x, Lp - L).reshape(B * Lp, D_MODEL)
            q, k, v, r, b = _gla_in(xt, w['gla_w_main'][i], w['gla_w_g'][i], w['gla_w_g2'][i],
                                    vec(w['gla_b_g'][i]), L, Lp)
            seq = lambda t: t.reshape(B, Lp, -1)
            s0t = jnp.swapaxes(gla_s0[i], -1, -2)
            o, st = _gla_scan(seq(q), seq(k), seq(v), seq(b), s0t)
            gla_states.append(jnp.swapaxes(st, -1, -2))
            xt = _gla_out(o.reshape(B * Lp, GLA_V), r, xt, vec(w['gla_gn_g'][i]), w['gla_w_o'][i],
                          vec(lg[0]), vec(lb[0]))
            x = xt.reshape(B, Lp, D_MODEL)[:, :L]
        else:
            j = layer - N_A
            if j == 0:
                kv = _linear(xt, w['kv_w']).reshape(B, L, 2 * SWA_KV)
                k_full = jnp.concatenate([past_k, kv[..., :SWA_KV]], axis=1)
                v_full = jnp.concatenate([past_v, kv[..., SWA_KV:]], axis=1)
                new_k, new_v = k_full[:, -WINDOW:], v_full[:, -WINDOW:]
                k_all, v_all = _pad_rows(k_full, Lp - L), _pad_rows(v_full, Lp - L)
            xp = _swa(_pad_rows(x, Lp - L), k_all, v_all, w['swa_w_q'][j], w['swa_sinks'][j].reshape(-1, 1),
                      w['swa_w_o'][j], vec(lg[0]), vec(lb[0]), L, past_valid)
            x = xp[:, :L]
        x = _mem_xattn(x, mem_k[layer], mem_v[layer], w['mem_w_q'][layer], w['mem_w_o'][layer],
                       vec(lg[1]), vec(lb[1]))
        xt = _moe(x.reshape(T, D_MODEL), w['moe_w_r'][layer], w['moe_b_r'][layer], w['moe_w1'][layer],
                  w['moe_b1'][layer], w['moe_w2'][layer], w['moe_b2'][layer], vec(lg[2]), vec(lb[2]))
        x = xt.reshape(B, L, D_MODEL)
    return x, jnp.stack(gla_states), new_k, new_v


def _prep_weights(gla_w_in, gla_w_g2, gla_b_g, gla_gn_g, gla_w_o, kv_w, swa_w_q, swa_sinks, swa_w_o,
                  mem_w_q, mem_w_kv, mem_w_o, moe_w_r, moe_b_r, moe_w1, moe_b1, moe_w2, moe_b2, ln_g, ln_b):
    n_main = 2 * GLA_QK + 2 * GLA_V
    pad_c = lambda a, n: jnp.pad(a, [(0, 0)] * (a.ndim - 1) + [(0, n - a.shape[-1])])
    w = dict(
        gla_w_main=gla_w_in[:, :, :n_main].astype(BF16),
        gla_w_g=pad_c(gla_w_in[:, :, n_main:], LANES).astype(BF16),
        gla_w_g2=jnp.pad(gla_w_g2, ((0, 0), (0, LANES - GLA_GATE_RANK), (0, 0))).astype(BF16),
        gla_b_g=gla_b_g, gla_gn_g=gla_gn_g.reshape(N_A, GLA_V), gla_w_o=gla_w_o.astype(BF16),
        kv_w=kv_w.astype(BF16), swa_w_q=swa_w_q.astype(BF16), swa_sinks=swa_sinks, swa_w_o=swa_w_o.astype(BF16),
        mem_w_q=mem_w_q.astype(BF16), mem_w_kv=mem_w_kv.astype(BF16), mem_w_o=mem_w_o.astype(BF16),
        moe_w_r=pad_c(moe_w_r, LANES).astype(BF16),
        moe_b_r=jnp.pad(moe_b_r, ((0, 0), (0, LANES - N_EXPERTS)), constant_values=-jnp.inf).reshape(DEPTH, 1, LANES),
        moe_w1=_w1_relayout(moe_w1.reshape(DEPTH * N_EXPERTS, D_MODEL, 2 * D_EXPERT)).reshape(moe_w1.shape),
        moe_b1=jnp.concatenate([moe_b1[..., 0::2], moe_b1[..., 1::2]], axis=-1).reshape(DEPTH, N_EXPERTS, 1, -1),
        moe_w2=moe_w2.astype(BF16), moe_b2=moe_b2.reshape(DEPTH, N_EXPERTS, 1, D_MODEL),
        ln_g=ln_g, ln_b=ln_b)
    return w


def kernel(x_prompt, x_sample, state_gla, cache_swa_k, cache_swa_v, cache_mem_k, cache_mem_v, mem_prompt, gla_w_in, gla_w_g2, gla_b_g, gla_gn_g, gla_w_o, kv_w, swa_w_q, swa_sinks, swa_w_o, mem_w_q, mem_w_kv, mem_w_o, moe_w_r, moe_b_r, moe_w1, moe_b1, moe_w2, moe_b2, ln_g, ln_b):
    w = _prep_weights(gla_w_in, gla_w_g2, gla_b_g, gla_gn_g, gla_w_o, kv_w, swa_w_q, swa_sinks, swa_w_o,
                      mem_w_q, mem_w_kv, mem_w_o, moe_w_r, moe_b_r, moe_w1, moe_b1, moe_w2, moe_b2, ln_g, ln_b)
    bp = x_prompt.shape[0]
    bs = x_sample.shape[0]
    mem_flat = mem_prompt.reshape(bp * N_MEM, D_MODEL)
    mem_kv = jnp.stack([_linear(mem_flat, w['mem_w_kv'][l]) for l in range(DEPTH)])
    mem_k_p = mem_kv[..., :D_MODEL].reshape(DEPTH, bp, N_MEM, D_MODEL)
    mem_v_p = mem_kv[..., D_MODEL:].reshape(DEPTH, bp, N_MEM, D_MODEL)
    gla0 = jnp.zeros((N_A, bp, GLA_HEADS, GLA_DK, GLA_DV), F32)
    zero_win = jnp.zeros((bp, WINDOW, SWA_KV), F32)
    y_p, gla_p, k_p, v_p = _trunk(x_prompt, gla0, zero_win, zero_win, False, mem_k_p, mem_v_p, w)
    y_s, gla_s, k_s, v_s = _trunk(x_sample, state_gla, cache_swa_k.reshape(bs, WINDOW, SWA_KV),
                                  cache_swa_v.reshape(bs, WINDOW, SWA_KV), True,
                                  cache_mem_k.reshape(DEPTH, bs, N_MEM, D_MODEL),
                                  cache_mem_v.reshape(DEPTH, bs, N_MEM, D_MODEL), w)
    heads4 = lambda t: t.reshape(t.shape[0], WINDOW, SWA_KV_HEADS, SWA_HEAD_DIM)
    mem5 = lambda t: t.reshape(DEPTH, bp, N_MEM, MEM_HEADS, MEM_HEAD_DIM)
    return (y_p, y_s, gla_p, gla_s, heads4(k_p), heads4(v_p), heads4(k_s), heads4(v_s), mem5(mem_k_p), mem5(mem_v_p))
```

```python
import functools

import jax
import jax.numpy as jnp
from jax import lax
from jax.experimental import pallas as pl
from jax.experimental.pallas import tpu as pltpu

F32 = jnp.float32
BF16 = jnp.bfloat16

D_MODEL = 1024
DEPTH = 4
CHUNK = 64
N_A = DEPTH // 2
GLA_HEADS = 4
GLA_DK = D_MODEL // (2 * GLA_HEADS)
GLA_DV = D_MODEL // GLA_HEADS
GLA_QK = GLA_HEADS * GLA_DK
GLA_V = GLA_HEADS * GLA_DV
GLA_GATE_RANK = 16
GLA_GATE_TEMP = 16.0
SWA_HEAD_DIM = 64
SWA_Q_HEADS = D_MODEL // SWA_HEAD_DIM
SWA_KV_HEADS = 4
SWA_GROUP = SWA_Q_HEADS // SWA_KV_HEADS
SWA_KV = SWA_KV_HEADS * SWA_HEAD_DIM
WINDOW = 128
WINDOW_CHUNKS = WINDOW // CHUNK
N_MEM = 256
MEM_HEADS = 4
MEM_HEAD_DIM = D_MODEL // MEM_HEADS
N_EXPERTS = 32
TOP_K = 4
D_EXPERT = D_MODEL
SWIGLU_ALPHA = 1.702
SWIGLU_LIMIT = 7.0
DEEPNORM_ALPHA = (2 * DEPTH) ** 0.25
LN_EPS = 1e-5
NEG_INF = -1e30

LANES = 128
VMEM_LIMIT = 56 * 1024 * 1024


def _params(*sem):
    return pltpu.CompilerParams(dimension_semantics=sem, vmem_limit_bytes=VMEM_LIMIT)


def _row_tile(n_rows, want):
    t = min(want, n_rows)
    while n_rows % t:
        t //= 2
    return t


def _full(shape):
    nd = len(shape)
    return pl.BlockSpec(shape, lambda *_: (0,) * nd)


def _bdot(a, b):
    return jnp.dot(a.astype(BF16), b.astype(BF16), preferred_element_type=F32)


def _bdot_nt(a, b):
    return lax.dot_general(a.astype(BF16), b.astype(BF16), (((1,), (1,)), ((), ())),
                           preferred_element_type=F32)


def _bdot_tn(a, b):
    return lax.dot_general(a.astype(BF16), b.astype(BF16), (((0,), (0,)), ((), ())),
                           preferred_element_type=F32)


def _layer_norm(z, g, b):
    mu = jnp.mean(z, axis=-1, keepdims=True)
    zc = z - mu
    var = jnp.mean(zc * zc, axis=-1, keepdims=True)
    return zc * lax.rsqrt(var + LN_EPS) * g + b


def _linear_kernel(x_ref, w_ref, o_ref):
    o_ref[...] = _bdot(x_ref[...], w_ref[...])


def _linear(x, w, tm=512):
    T, K = x.shape
    N = w.shape[1]
    tm = _row_tile(T, tm)
    return pl.pallas_call(
        _linear_kernel,
        out_shape=jax.ShapeDtypeStruct((T, N), F32),
        grid=(T // tm,),
        in_specs=[pl.BlockSpec((tm, K), lambda i: (i, 0)), _full((K, N))],
        out_specs=pl.BlockSpec((tm, N), lambda i: (i, 0)),
        compiler_params=_params("parallel"),
        name="linear",
    )(x, w)


MXU_DIM = 256


def _w1_relayout_kernel(w_ref, o_ref):
    half = MXU_DIM // 2
    r = lax.broadcasted_iota(jnp.int32, (MXU_DIM, MXU_DIM), 0)
    c = lax.broadcasted_iota(jnp.int32, (MXU_DIM, MXU_DIM), 1)
    src = jnp.where(c < half, 2 * c, 2 * (c - half) + 1)
    perm = (r == src).astype(BF16)
    n_out = o_ref.shape[-1] // 2
    for j in range(w_ref.shape[-1] // MXU_DIM):
        y = jnp.dot(w_ref[0, :, j * MXU_DIM:(j + 1) * MXU_DIM].astype(BF16), perm, preferred_element_type=F32)
        o_ref[0, :, j * half:(j + 1) * half] = y[:, :half].astype(BF16)
        o_ref[0, :, n_out + j * half:n_out + (j + 1) * half] = y[:, half:].astype(BF16)


def _w1_relayout(w1):
    E, D, N = w1.shape
    spec = pl.BlockSpec((1, D, N), lambda e: (e, 0, 0))
    return pl.pallas_call(
        _w1_relayout_kernel,
        out_shape=jax.ShapeDtypeStruct((E, D, N), BF16),
        grid=(E,),
        in_specs=[spec],
        out_specs=spec,
        compiler_params=_params("parallel"),
        name="w1_relayout",
    )(w1)


def _gla_in_kernel(x_ref, w_ref, wg_ref, wg2_ref, bg_ref, q_ref, k_ref, v_ref, r_ref, b_ref,
                   *, tm, seq_len, seq_pad):
    xb = x_ref[...].astype(BF16)
    y = jnp.dot(xb, w_ref[...], preferred_element_type=F32)
    q_ref[...] = y[:, :GLA_QK] * (GLA_DK ** -0.5)
    k_ref[...] = y[:, GLA_QK:2 * GLA_QK]
    v_ref[...] = y[:, 2 * GLA_QK:2 * GLA_QK + GLA_V]
    r_ref[...] = y[:, 2 * GLA_QK + GLA_V:]
    g_lr = jnp.dot(xb, wg_ref[...], preferred_element_type=F32)
    z = _bdot(g_lr, wg2_ref[...]) + bg_ref[...]
    log_a = (jnp.minimum(z, 0.0) - jnp.log(1.0 + jnp.exp(-jnp.abs(z)))) / GLA_GATE_TEMP
    if seq_len < seq_pad:
        pos = (pl.program_id(0) * tm + lax.broadcasted_iota(jnp.int32, (tm, 1), 0)) % seq_pad
        log_a = jnp.where(pos < seq_len, log_a, 0.0)
    row = lax.broadcasted_iota(jnp.int32, (tm, tm), 0)
    col = lax.broadcasted_iota(jnp.int32, (tm, tm), 1)
    tri = ((row // CHUNK == col // CHUNK) & (row >= col)).astype(BF16)
    g1 = log_a.astype(BF16)
    r1 = log_a - g1.astype(F32)
    g2 = r1.astype(BF16)
    g3 = (r1 - g2.astype(F32)).astype(BF16)
    b_ref[...] = (jnp.dot(tri, g1, preferred_element_type=F32) + jnp.dot(tri, g2, preferred_element_type=F32)
                  + jnp.dot(tri, g3, preferred_element_type=F32))


def _gla_in(x, w_main, w_g, w_g2, b_g, seq_len, seq_pad, tm=512):
    T = x.shape[0]
    tm = _row_tile(T, tm)
    assert tm % CHUNK == 0 and seq_pad % CHUNK == 0
    n_main = w_main.shape[1]
    row = lambda n: pl.BlockSpec((tm, n), lambda i: (i, 0))
    return pl.pallas_call(
        functools.partial(_gla_in_kernel, tm=tm, seq_len=seq_len, seq_pad=seq_pad),
        out_shape=(jax.ShapeDtypeStruct((T, GLA_QK), F32), jax.ShapeDtypeStruct((T, GLA_QK), F32),
                   jax.ShapeDtypeStruct((T, GLA_V), F32), jax.ShapeDtypeStruct((T, GLA_V), F32),
                   jax.ShapeDtypeStruct((T, GLA_QK), F32)),
        grid=(T // tm,),
        in_specs=[row(D_MODEL), _full((D_MODEL, n_main)), _full((D_MODEL, LANES)),
                  _full((LANES, GLA_QK)), _full((1, GLA_QK))],
        out_specs=(row(GLA_QK), row(GLA_QK), row(GLA_V), row(GLA_V), row(GLA_QK)),
        compiler_params=_params("parallel"),
        name="gla_in",
    )(x, w_main, w_g, w_g2, b_g)


SUBLANES = 8


def _gla_intra(q, k, b):
    n_grp = CHUNK // SUBLANES
    lane = lax.broadcasted_iota(jnp.int32, (SUBLANES, LANES), 1)
    row = lax.broadcasted_iota(jnp.int32, (SUBLANES, LANES), 0)
    grp = lambda t, i: t[i * SUBLANES:(i + 1) * SUBLANES]
    att = [jnp.zeros((SUBLANES, LANES), F32) for _ in range(n_grp)]
    for s in range(CHUNK):
        b_s = b[s:s + 1, :]
        k_s = k[s:s + 1, :]
        for i in range(s // SUBLANES, n_grp):
            e = jnp.exp(grp(b, i) - b_s)
            red = jnp.sum(grp(q, i) * e * k_s, axis=-1, keepdims=True)
            att[i] = jnp.where(lane == s, red, att[i])
    att = [jnp.where(row + i * SUBLANES >= lane, att[i], 0.0) for i in range(n_grp)]
    return jnp.concatenate(att, axis=0)


def _gla_scan_kernel(q_ref, k_ref, v_ref, b_ref, s0_ref, o_ref, s_ref, st_scr, *, n_chunks):
    @pl.when(pl.program_id(1) == 0)
    def _():
        st_scr[...] = s0_ref[0]

    for c in range(n_chunks):
        ts = slice(c * CHUNK, (c + 1) * CHUNK)
        for h in range(GLA_HEADS):
            ks = slice(h * GLA_DK, (h + 1) * GLA_DK)
            vs = slice(h * GLA_DV, (h + 1) * GLA_DV)
            q = q_ref[0, ts, ks]
            k = k_ref[0, ts, ks]
            b = b_ref[0, ts, ks]
            v = v_ref[0, ts, vs]
            att = _gla_intra(q, k, b)
            b_end = b[CHUNK - 1:CHUNK, :]
            st = st_scr[h]
            o_ref[0, ts, vs] = _bdot(att[:, :CHUNK], v) + _bdot_nt(q * jnp.exp(b), st)
            st_scr[h] = st * jnp.exp(b_end) + _bdot_tn(v, k * jnp.exp(b_end - b))

    @pl.when(pl.program_id(1) == pl.num_programs(1) - 1)
    def _():
        s_ref[0] = st_scr[...]


def _gla_scan(q, k, v, la, s0t, tl=128):
    B, Lp, _ = q.shape
    tl = _row_tile(Lp, tl)
    qk_spec = pl.BlockSpec((1, tl, GLA_QK), lambda b, i: (b, i, 0))
    v_spec = pl.BlockSpec((1, tl, GLA_V), lambda b, i: (b, i, 0))
    s_spec = pl.BlockSpec((1, GLA_HEADS, GLA_DV, GLA_DK), lambda b, i: (b, 0, 0, 0))
    return pl.pallas_call(
        functools.partial(_gla_scan_kernel, n_chunks=tl // CHUNK),
        out_shape=(jax.ShapeDtypeStruct((B, Lp, GLA_V), F32),
                   jax.ShapeDtypeStruct((B, GLA_HEADS, GLA_DV, GLA_DK), F32)),
        grid=(B, Lp // tl),
        in_specs=[qk_spec, qk_spec, v_spec, qk_spec, s_spec],
        out_specs=(v_spec, s_spec),
        scratch_shapes=[pltpu.VMEM((GLA_HEADS, GLA_DV, GLA_DK), F32)],
        compiler_params=_params("parallel", "arbitrary"),
        name="gla_scan",
    )(q, k, v, la, s0t)


def _gla_out_kernel(o_ref, r_ref, x_ref, gn_ref, wo_ref, g_ref, b_ref, y_ref):
    parts = []
    for h in range(GLA_HEADS):
        vs = slice(h * GLA_DV, (h + 1) * GLA_DV)
        o = o_ref[:, vs]
        mu = jnp.mean(o, axis=-1, keepdims=True)
        oc = o - mu
        var = jnp.mean(oc * oc, axis=-1, keepdims=True)
        parts.append(oc * lax.rsqrt(var + LN_EPS) * gn_ref[:, vs])
    o = jnp.concatenate(parts, axis=-1)
    r = r_ref[...]
    o = o * (r * jax.nn.sigmoid(r))
    h = _bdot(o, wo_ref[...])
    y_ref[...] = _layer_norm(DEEPNORM_ALPHA * x_ref[...] + h, g_ref[...], b_ref[...])


def _gla_out(o, r, x, gn_g, w_o, ln_g, ln_b, tm=512):
    T = x.shape[0]
    tm = _row_tile(T, tm)
    row = pl.BlockSpec((tm, D_MODEL), lambda i: (i, 0))
    vec = _full((1, D_MODEL))
    return pl.pallas_call(
        _gla_out_kernel,
        out_shape=jax.ShapeDtypeStruct((T, D_MODEL), F32),
        grid=(T // tm,),
        in_specs=[row, row, row, vec, _full((GLA_V, D_MODEL)), vec, vec],
        out_specs=row,
        compiler_params=_params("parallel"),
        name="gla_out",
    )(o, r, x, gn_g, w_o, ln_g, ln_b)


def _swa_kernel(x_ref, k_ref, v_ref, wq_ref, sink_ref, wo_ref, g_ref, b_ref, y_ref,
                *, tl, seq_len, past_valid):
    x = x_ref[0]
    q = (_bdot(x, wq_ref[...]) * (SWA_HEAD_DIM ** -0.5)).astype(BF16)
    kw = tl + WINDOW
    r0 = pl.multiple_of(pl.program_id(1) * tl, tl)
    kt = k_ref[0, pl.ds(r0, kw), :].astype(BF16)
    vt = v_ref[0, pl.ds(r0, kw), :].astype(BF16)
    q_row = lax.broadcasted_iota(jnp.int32, (tl, kw), 0)
    k_col = lax.broadcasted_iota(jnp.int32, (tl, kw), 1)
    band_lo = (q_row // CHUNK) * CHUNK
    pos = r0 + k_col
    ok = (k_col >= band_lo) & (k_col < band_lo + WINDOW + CHUNK) & (pos - WINDOW < seq_len)
    if not past_valid:
        ok = ok & (pos >= WINDOW)
    zeros = jnp.zeros((kw, SWA_HEAD_DIM), BF16)
    pair_out = []
    for g in range(SWA_KV_HEADS):
        kg = kt[:, g * SWA_HEAD_DIM:(g + 1) * SWA_HEAD_DIM]
        vg = vt[:, g * SWA_HEAD_DIM:(g + 1) * SWA_HEAD_DIM]
        k_pad = (jnp.concatenate([kg, zeros], axis=1), jnp.concatenate([zeros, kg], axis=1))
        v_pad = (jnp.concatenate([vg, zeros], axis=1), jnp.concatenate([zeros, vg], axis=1))
        for pair in range(g * SWA_GROUP // 2, (g + 1) * SWA_GROUP // 2):
            qp = q[:, pair * LANES:(pair + 1) * LANES]
            acc = None
            for j in range(2):
                h = 2 * pair + j
                sink = sink_ref[h:h + 1, :]
                s = jnp.where(ok, _bdot_nt(qp, k_pad[j]), NEG_INF)
                m = jnp.maximum(jnp.max(s, axis=-1, keepdims=True), sink)
                e = jnp.exp(s - m)
                p = e * (1.0 / (jnp.sum(e, axis=-1, keepdims=True) + jnp.exp(sink - m)))
                o = _bdot(p, v_pad[j])
                acc = o if acc is None else acc + o
            pair_out.append(acc)
    h_out = _bdot(jnp.concatenate(pair_out, axis=1), wo_ref[...])
    y_ref[0] = _layer_norm(DEEPNORM_ALPHA * x + h_out, g_ref[...], b_ref[...])


def _swa(x, k_all, v_all, w_q, sinks, w_o, ln_g, ln_b, seq_len, past_valid, tl=256):
    B, Lp, _ = x.shape
    tl = _row_tile(Lp, tl)
    x_spec = pl.BlockSpec((1, tl, D_MODEL), lambda b, i: (b, i, 0))
    kv_spec = pl.BlockSpec((1, WINDOW + Lp, SWA_KV), lambda b, i: (b, 0, 0))
    vec = _full((1, D_MODEL))
    return pl.pallas_call(
        functools.partial(_swa_kernel, tl=tl, seq_len=seq_len, past_valid=past_valid),
        out_shape=jax.ShapeDtypeStruct((B, Lp, D_MODEL), F32),
        grid=(B, Lp // tl),
        in_specs=[x_spec, kv_spec, kv_spec, _full((D_MODEL, D_MODEL)), _full((SWA_Q_HEADS, 1)),
                  _full((D_MODEL, D_MODEL)), vec, vec],
        out_specs=x_spec,
        compiler_params=_params("parallel", "arbitrary"),
        name="swa",
    )(x, k_all, v_all, w_q, sinks, w_o, ln_g, ln_b)


def _mem_kernel(x_ref, mk_ref, mv_ref, wq_ref, wo_ref, g_ref, b_ref, y_ref):
    x = x_ref[0]
    q = _bdot(x, wq_ref[...]) * (MEM_HEAD_DIM ** -0.5)
    parts = []
    for h in range(MEM_HEADS):
        hs = slice(h * MEM_HEAD_DIM, (h + 1) * MEM_HEAD_DIM)
        s = _bdot_nt(q[:, hs], mk_ref[0, :, hs])
        m = jnp.max(s, axis=-1, keepdims=True)
        p = jnp.exp(s - m)
        p = p / jnp.sum(p, axis=-1, keepdims=True)
        parts.append(_bdot(p, mv_ref[0, :, hs]))
    o = jnp.concatenate(parts, axis=-1)
    h_out = _bdot(o, wo_ref[...])
    y_ref[0] = _layer_norm(DEEPNORM_ALPHA * x + h_out, g_ref[...], b_ref[...])


def _mem_xattn(x, mk, mv, w_q, w_o, ln_g, ln_b, tl=512):
    B, L, _ = x.shape
    tl = _row_tile(L, tl)
    x_spec = pl.BlockSpec((1, tl, D_MODEL), lambda b, i: (b, i, 0))
    m_spec = pl.BlockSpec((1, N_MEM, D_MODEL), lambda b, i: (b, 0, 0))
    vec = _full((1, D_MODEL))
    return pl.pallas_call(
        _mem_kernel,
        out_shape=jax.ShapeDtypeStruct((B, L, D_MODEL), F32),
        grid=(B, L // tl),
        in_specs=[x_spec, m_spec, m_spec, _full((D_MODEL, D_MODEL)), _full((D_MODEL, D_MODEL)), vec, vec],
        out_specs=x_spec,
        compiler_params=_params("parallel", "arbitrary"),
        name="mem_xattn",
    )(x, mk, mv, w_q, w_o, ln_g, ln_b)


def _router_kernel(x_ref, wr_ref, br_ref, idx_ref, gate_ref, pos_ref, cnt_ref, cnt_scr, *, tm):
    @pl.when(pl.program_id(0) == 0)
    def _():
        cnt_scr[...] = jnp.zeros_like(cnt_scr)

    logits = _bdot(x_ref[...], wr_ref[...]) + br_ref[...]
    lane = lax.broadcasted_iota(jnp.int32, (tm, LANES), 1)
    sel = jnp.zeros((tm, LANES), F32)
    idx_out = jnp.zeros((tm, LANES), jnp.int32)
    val_out = jnp.zeros((tm, LANES), F32)
    l = logits
    idxs = []
    for j in range(TOP_K):
        m = jnp.max(l, axis=-1, keepdims=True)
        idx = jnp.min(jnp.where(l == m, lane, LANES), axis=-1, keepdims=True)
        hit = lane == idx
        sel = jnp.where(hit, 1.0, sel)
        l = jnp.where(hit, -jnp.inf, l)
        idx_out = jnp.where(lane == j, idx, idx_out)
        val_out = jnp.where(lane == j, m, val_out)
        idxs.append(idx)
    e = jnp.where(lane < TOP_K, jnp.exp(val_out - val_out[:, 0:1]), 0.0)
    gates = e / jnp.sum(e, axis=-1, keepdims=True)
    r = lax.broadcasted_iota(jnp.int32, (tm, tm), 0)
    c = lax.broadcasted_iota(jnp.int32, (tm, tm), 1)
    before = jnp.dot((r > c).astype(BF16), sel.astype(BF16), preferred_element_type=F32) + cnt_scr[...]
    pos_out = jnp.zeros((tm, LANES), F32)
    for j in range(TOP_K):
        pj = jnp.sum(jnp.where(lane == idxs[j], before, 0.0), axis=-1, keepdims=True)
        pos_out = jnp.where(lane == j, pj, pos_out)
    cnt_scr[...] = cnt_scr[...] + jnp.sum(sel, axis=0, keepdims=True)
    idx_ref[...] = idx_out[:, :TOP_K]
    gate_ref[...] = gates[:, :TOP_K]
    pos_ref[...] = pos_out[:, :TOP_K].astype(jnp.int32)
    cnt_ref[...] = cnt_scr[...].astype(jnp.int32)


def _router(x, w_r, b_r, tm=512):
    T = x.shape[0]
    tm = _row_tile(T, tm)
    small = pl.BlockSpec((tm, TOP_K), lambda i: (i, 0))
    return pl.pallas_call(
        functools.partial(_router_kernel, tm=tm),
        out_shape=(jax.ShapeDtypeStruct((T, TOP_K), jnp.int32), jax.ShapeDtypeStruct((T, TOP_K), F32),
                   jax.ShapeDtypeStruct((T, TOP_K), jnp.int32), jax.ShapeDtypeStruct((1, LANES), jnp.int32)),
        grid=(T // tm,),
        in_specs=[pl.BlockSpec((tm, D_MODEL), lambda i: (i, 0)), _full((D_MODEL, LANES)), _full((1, LANES))],
        out_specs=(small, small, small, _full((1, LANES))),
        scratch_shapes=[pltpu.VMEM((1, LANES), F32)],
        compiler_params=_params("arbitrary"),
        name="moe_router",
    )(x, w_r, b_r)


ISSUE_UNROLL = 8


def _dispatch_kernel(pad_end_ref, dest_ref, x_ref, xs_ref, zero_scr, sem, zsem, *, tm, blk):
    @pl.when(pl.program_id(0) == 0)
    def _():
        zero_scr[...] = jnp.zeros_like(zero_scr)

        def fill(e):
            start = pl.multiple_of(pad_end_ref[e] - blk, blk)
            return pltpu.make_async_copy(zero_scr, xs_ref.at[pl.ds(start, blk), :], zsem)

        def has_block(e):
            return pad_end_ref[e] > (pad_end_ref[e - 1] if e else 0)

        for e in range(N_EXPERTS):
            pl.when(has_block(e))(lambda e=e: fill(e).start())
        for e in range(N_EXPERTS):
            pl.when(has_block(e))(lambda e=e: fill(e).wait())

    def issue(g, carry):
        base = pl.multiple_of(g * (ISSUE_UNROLL // TOP_K), ISSUE_UNROLL // TOP_K)
        for u in range(ISSUE_UNROLL // TOP_K):
            r = base + u
            for j in range(TOP_K):
                pltpu.make_async_copy(x_ref.at[pl.ds(r, 1), :], xs_ref.at[pl.ds(dest_ref[0, j, r], 1), :], sem).start()
        return carry

    lax.fori_loop(0, tm * TOP_K // ISSUE_UNROLL, issue, 0)
    for j in range(TOP_K):
        pltpu.make_async_copy(x_ref, xs_ref.at[pl.ds(0, tm), :], sem).wait()


def _dispatch(x, dest_blocks, pad_end, n_slots, tm, blk):
    T = x.shape[0]
    return pl.pallas_call(
        functools.partial(_dispatch_kernel, tm=tm, blk=blk),
        out_shape=jax.ShapeDtypeStruct((n_slots, D_MODEL), F32),
        grid_spec=pltpu.PrefetchScalarGridSpec(
            num_scalar_prefetch=1,
            grid=(T // tm,),
            in_specs=[pl.BlockSpec((1, TOP_K, tm), lambda i, pe: (i, 0, 0), memory_space=pltpu.SMEM),
                      pl.BlockSpec((tm, D_MODEL), lambda i, pe: (i, 0))],
            out_specs=pl.BlockSpec(memory_space=pl.ANY),
            scratch_shapes=[pltpu.VMEM((blk, D_MODEL), F32), pltpu.SemaphoreType.DMA, pltpu.SemaphoreType.DMA]),
        compiler_params=_params("arbitrary"),
        name="moe_dispatch",
    )(pad_end, dest_blocks, x)


def _collect_combine_kernel(idx_ref, idx_next_ref, src_ref, gate_ref, x_ref, g_ref, b_ref, y_ref, buf, sem, *, tm):
    i = pl.program_id(0)
    slot = i % 2

    def request(ids, s):
        def body(g, carry):
            base = pl.multiple_of(g * (ISSUE_UNROLL // TOP_K), ISSUE_UNROLL // TOP_K)
            for u in range(ISSUE_UNROLL // TOP_K):
                r = base + u
                for j in range(TOP_K):
                    pltpu.make_async_copy(src_ref.at[pl.ds(ids[0, j, r], 1), :],
                                          buf.at[s, j, pl.ds(r, 1), :], sem.at[s]).start()
            return carry
        lax.fori_loop(0, tm * TOP_K // ISSUE_UNROLL, body, 0)

    @pl.when(i == 0)
    def _():
        request(idx_ref, 0)

    @pl.when(i + 1 < pl.num_programs(0))
    def _():
        request(idx_next_ref, 1 - slot)

    for j in range(TOP_K):
        pltpu.make_async_copy(src_ref.at[pl.ds(0, tm), :], buf.at[slot, j], sem.at[slot]).wait()
    gates = gate_ref[...]
    acc = DEEPNORM_ALPHA * x_ref[...]
    for j in range(TOP_K):
        acc = acc + gates[:, j:j + 1] * buf[slot, j]
    y_ref[...] = _layer_norm(acc, g_ref[...], b_ref[...])


def _collect_combine(src, idx_blocks, gates, x, ln_g, ln_b, tm):
    T = x.shape[0]
    nt = T // tm
    row = pl.BlockSpec((tm, D_MODEL), lambda i: (i, 0))
    vec = _full((1, D_MODEL))
    return pl.pallas_call(
        functools.partial(_collect_combine_kernel, tm=tm),
        out_shape=jax.ShapeDtypeStruct((T, D_MODEL), F32),
        grid=(nt,),
        in_specs=[pl.BlockSpec((1, TOP_K, tm), lambda i: (i, 0, 0), memory_space=pltpu.SMEM),
                  pl.BlockSpec((1, TOP_K, tm), lambda i: (jnp.minimum(i + 1, nt - 1), 0, 0), memory_space=pltpu.SMEM),
                  pl.BlockSpec(memory_space=pl.ANY),
                  pl.BlockSpec((tm, TOP_K), lambda i: (i, 0)), row, vec, vec],
        out_specs=row,
        scratch_shapes=[pltpu.VMEM((2, TOP_K, tm, D_MODEL), F32), pltpu.SemaphoreType.DMA((2,))],
        compiler_params=_params("arbitrary"),
        name="moe_collect_combine",
    )(idx_blocks, idx_blocks, src, gates, x, ln_g, ln_b)


def _expert_kernel(be_ref, nb_ref, xs_ref, w1_ref, b1_ref, w2_ref, b2_ref, o_ref):
    @pl.when(pl.program_id(0) < nb_ref[0])
    def _():
        h = _bdot(xs_ref[...], w1_ref[0]) + b1_ref[0]
        glu = jnp.minimum(h[:, :D_EXPERT], SWIGLU_LIMIT)
        lin = jnp.clip(h[:, D_EXPERT:], -SWIGLU_LIMIT, SWIGLU_LIMIT)
        a = glu * jax.nn.sigmoid(SWIGLU_ALPHA * glu) * (lin + 1.0)
        o_ref[...] = _bdot(a, w2_ref[0]) + b2_ref[0]


def _experts(xs, block_e, n_used, w1, b1, w2, b2, blk):
    P = xs.shape[0]
    nb = P // blk

    def blk_map(i, be, nu):
        return (jnp.minimum(i, nu[0] - 1), 0)

    def w_map(i, be, nu):
        return (be[jnp.minimum(i, nu[0] - 1)], 0, 0)

    return pl.pallas_call(
        _expert_kernel,
        out_shape=jax.ShapeDtypeStruct((P, D_MODEL), F32),
        grid_spec=pltpu.PrefetchScalarGridSpec(
            num_scalar_prefetch=2,
            grid=(nb,),
            in_specs=[pl.BlockSpec((blk, D_MODEL), blk_map),
                      pl.BlockSpec((1, D_MODEL, 2 * D_EXPERT), w_map),
                      pl.BlockSpec((1, 1, 2 * D_EXPERT), w_map),
                      pl.BlockSpec((1, D_EXPERT, D_MODEL), w_map),
                      pl.BlockSpec((1, 1, D_MODEL), w_map)],
            out_specs=pl.BlockSpec((blk, D_MODEL), blk_map)),
        compiler_params=_params("arbitrary"),
        name="moe_experts",
    )(block_e, n_used, xs, w1, b1, w2, b2)


def _moe(x, w_r, b_r, w1, b1, w2, b2, ln_g, ln_b):
    T = x.shape[0]
    blk = 512 if T >= 4096 else 128
    tm = _row_tile(T, 512)
    n_assign = T * TOP_K
    n_blocks = -(-(n_assign + N_EXPERTS * (blk - 1)) // blk)
    n_slots = n_blocks * blk

    top_i, gates, pos, counts = _router(x, w_r, b_r)
    counts = counts[0, :N_EXPERTS]
    padded = (counts + blk - 1) // blk * blk
    pad_end = jnp.cumsum(padded)
    pad_start = pad_end - padded
    dest = pad_start[top_i] + pos
    block_start = jnp.arange(n_blocks, dtype=jnp.int32) * blk
    block_e = jnp.minimum(jnp.sum(pad_end[None, :] <= block_start[:, None], axis=1), N_EXPERTS - 1).astype(jnp.int32)
    n_used = (pad_end[-1:] // blk).astype(jnp.int32)
    dest_km = dest.T
    dest_blocks = dest_km.reshape(TOP_K, T // tm, tm).transpose(1, 0, 2)

    xs = _dispatch(x, dest_blocks, pad_end.astype(jnp.int32), n_slots, tm, blk)
    out = _experts(xs, block_e, n_used, w1, b1, w2, b2, blk)
    return _collect_combine(out, dest_blocks, gates, x, ln_g, ln_b, tm)


def _pad_rows(t, n):
    return t if n == 0 else jnp.pad(t, ((0, 0), (0, n), (0, 0)))


def _trunk(x, gla_s0, past_k, past_v, past_valid, mem_k, mem_v, w):
    B, L, _ = x.shape
    Lp = -(-L // CHUNK) * CHUNK
    T = B * L
    vec = lambda a: a.reshape(1, -1)
    gla_states = []
    k_all = v_all = new_k = new_v = None
    for layer in range(DEPTH):
        lg, lb = w['ln_g'][layer], w['ln_b'][layer]
        xt = x.reshape(T, D_MODEL)
        if layer < N_A:
            i = layer
            xt = _pad_rows(x, Lp - L).reshape(B * Lp, D_MODEL)
            q, k, v, r, b = _gla_in(xt, w['gla_w_main'][i], w['gla_w_g'][i], w['gla_w_g2'][i],
                                    vec(w['gla_b_g'][i]), L, Lp)
            seq = lambda t: t.reshape(B, Lp, -1)
            s0t = jnp.swapaxes(gla_s0[i], -1, -2)
            o, st = _gla_scan(seq(q), seq(k), seq(v), seq(b), s0t)
            gla_states.append(jnp.swapaxes(st, -1, -2))
            xt = _gla_out(o.reshape(B * Lp, GLA_V), r, xt, vec(w['gla_gn_g'][i]), w['gla_w_o'][i],
                          vec(lg[0]), vec(lb[0]))
            x = xt.reshape(B, Lp, D_MODEL)[:, :L]
        else:
            j = layer - N_A
            if j == 0:
                kv = _linear(xt, w['kv_w']).reshape(B, L, 2 * SWA_KV)
                k_full = jnp.concatenate([past_k, kv[..., :SWA_KV]], axis=1)
                v_full = jnp.concatenate([past_v, kv[..., SWA_KV:]], axis=1)
                new_k, new_v = k_full[:, -WINDOW:], v_full[:, -WINDOW:]
                k_all, v_all = _pad_rows(k_full, Lp - L), _pad_rows(v_full, Lp - L)
            xp = _swa(_pad_rows(x, Lp - L), k_all, v_all, w['swa_w_q'][j], w['swa_sinks'][j].reshape(-1, 1),
                      w['swa_w_o'][j], vec(lg[0]), vec(lb[0]), L, past_valid)
            x = xp[:, :L]
        x = _mem_xattn(x, mem_k[layer], mem_v[layer], w['mem_w_q'][layer], w['mem_w_o'][layer],
                       vec(lg[1]), vec(lb[1]))
        xt = _moe(x.reshape(T, D_MODEL), w['moe_w_r'][layer], w['moe_b_r'][layer], w['moe_w1'][layer],
                  w['moe_b1'][layer], w['moe_w2'][layer], w['moe_b2'][layer], vec(lg[2]), vec(lb[2]))
        x = xt.reshape(B, L, D_MODEL)
    return x, jnp.stack(gla_states), new_k, new_v


def _prep_weights(gla_w_in, gla_w_g2, gla_b_g, gla_gn_g, gla_w_o, kv_w, swa_w_q, swa_sinks, swa_w_o,
                  mem_w_q, mem_w_kv, mem_w_o, moe_w_r, moe_b_r, moe_w1, moe_b1, moe_w2, moe_b2, ln_g, ln_b):
    n_main = 2 * GLA_QK + 2 * GLA_V
    pad_c = lambda a, n: jnp.pad(a, [(0, 0)] * (a.ndim - 1) + [(0, n - a.shape[-1])])
    w = dict(
        gla_w_main=gla_w_in[:, :, :n_main].astype(BF16),
        gla_w_g=pad_c(gla_w_in[:, :, n_main:], LANES).astype(BF16),
        gla_w_g2=jnp.pad(gla_w_g2, ((0, 0), (0, LANES - GLA_GATE_RANK), (0, 0))).astype(BF16),
        gla_b_g=gla_b_g, gla_gn_g=gla_gn_g.reshape(N_A, GLA_V), gla_w_o=gla_w_o.astype(BF16),
        kv_w=kv_w.astype(BF16), swa_w_q=swa_w_q.astype(BF16), swa_sinks=swa_sinks, swa_w_o=swa_w_o.astype(BF16),
        mem_w_q=mem_w_q.astype(BF16), mem_w_kv=mem_w_kv.astype(BF16), mem_w_o=mem_w_o.astype(BF16),
        moe_w_r=pad_c(moe_w_r, LANES).astype(BF16),
        moe_b_r=jnp.pad(moe_b_r, ((0, 0), (0, LANES - N_EXPERTS)), constant_values=-jnp.inf).reshape(DEPTH, 1, LANES),
        moe_w1=_w1_relayout(moe_w1.reshape(DEPTH * N_EXPERTS, D_MODEL, 2 * D_EXPERT)).reshape(moe_w1.shape),
        moe_b1=jnp.concatenate([moe_b1[..., 0::2], moe_b1[..., 1::2]], axis=-1).reshape(DEPTH, N_EXPERTS, 1, -1),
        moe_w2=moe_w2.astype(BF16), moe_b2=moe_b2.reshape(DEPTH, N_EXPERTS, 1, D_MODEL),
        ln_g=ln_g, ln_b=ln_b)
    return w


def kernel(x_prompt, x_sample, state_gla, cache_swa_k, cache_swa_v, cache_mem_k, cache_mem_v, mem_prompt, gla_w_in, gla_w_g2, gla_b_g, gla_gn_g, gla_w_o, kv_w, swa_w_q, swa_sinks, swa_w_o, mem_w_q, mem_w_kv, mem_w_o, moe_w_r, moe_b_r, moe_w1, moe_b1, moe_w2, moe_b2, ln_g, ln_b):
    w = _prep_weights(gla_w_in, gla_w_g2, gla_b_g, gla_gn_g, gla_w_o, kv_w, swa_w_q, swa_sinks, swa_w_o,
                      mem_w_q, mem_w_kv, mem_w_o, moe_w_r, moe_b_r, moe_w1, moe_b1, moe_w2, moe_b2, ln_g, ln_b)
    bp = x_prompt.shape[0]
    bs = x_sample.shape[0]
    mem_flat = mem_prompt.reshape(bp * N_MEM, D_MODEL)
    mem_kv = jnp.stack([_linear(mem_flat, w['mem_w_kv'][l]) for l in range(DEPTH)])
    mem_k_p = mem_kv[..., :D_MODEL].reshape(DEPTH, bp, N_MEM, D_MODEL)
    mem_v_p = mem_kv[..., D_MODEL:].reshape(DEPTH, bp, N_MEM, D_MODEL)
    gla0 = jnp.zeros((N_A, bp, GLA_HEADS, GLA_DK, GLA_DV), F32)
    zero_win = jnp.zeros((bp, WINDOW, SWA_KV), F32)
    y_p, gla_p, k_p, v_p = _trunk(x_prompt, gla0, zero_win, zero_win, False, mem_k_p, mem_v_p, w)
    y_s, gla_s, k_s, v_s = _trunk(x_sample, state_gla, cache_swa_k.reshape(bs, WINDOW, SWA_KV),
                                  cache_swa_v.reshape(bs, WINDOW, SWA_KV), True,
                                  cache_mem_k.reshape(DEPTH, bs, N_MEM, D_MODEL),
                                  cache_mem_v.reshape(DEPTH, bs, N_MEM, D_MODEL), w)
    heads4 = lambda t: t.reshape(t.shape[0], WINDOW, SWA_KV_HEADS, SWA_HEAD_DIM)
    mem5 = lambda t: t.reshape(DEPTH, bp, N_MEM, MEM_HEADS, MEM_HEAD_DIM)
    return (y_p, y_s, gla_p, gla_s, heads4(k_p), heads4(v_p), heads4(k_s), heads4(v_s), mem5(mem_k_p), mem5(mem_v_p))
```

```python
import functools

import jax
import jax.numpy as jnp
from jax import lax
from jax.experimental import pallas as pl
from jax.experimental.pallas import tpu as pltpu

F32 = jnp.float32
BF16 = jnp.bfloat16

D_MODEL = 1024
DEPTH = 4
CHUNK = 64
N_A = DEPTH // 2
GLA_HEADS = 4
GLA_DK = D_MODEL // (2 * GLA_HEADS)
GLA_DV = D_MODEL // GLA_HEADS
GLA_QK = GLA_HEADS * GLA_DK
GLA_V = GLA_HEADS * GLA_DV
GLA_GATE_RANK = 16
GLA_GATE_TEMP = 16.0
SWA_HEAD_DIM = 64
SWA_Q_HEADS = D_MODEL // SWA_HEAD_DIM
SWA_KV_HEADS = 4
SWA_GROUP = SWA_Q_HEADS // SWA_KV_HEADS
SWA_KV = SWA_KV_HEADS * SWA_HEAD_DIM
WINDOW = 128
WINDOW_CHUNKS = WINDOW // CHUNK
N_MEM = 256
MEM_HEADS = 4
MEM_HEAD_DIM = D_MODEL // MEM_HEADS
N_EXPERTS = 32
TOP_K = 4
D_EXPERT = D_MODEL
SWIGLU_ALPHA = 1.702
SWIGLU_LIMIT = 7.0
DEEPNORM_ALPHA = (2 * DEPTH) ** 0.25
LN_EPS = 1e-5
NEG_INF = -1e30

LANES = 128
VMEM_LIMIT = 56 * 1024 * 1024


def _params(*sem):
    return pltpu.CompilerParams(dimension_semantics=sem, vmem_limit_bytes=VMEM_LIMIT)


def _row_tile(n_rows, want):
    t = min(want, n_rows)
    while n_rows % t:
        t //= 2
    return t


def _full(shape):
    nd = len(shape)
    return pl.BlockSpec(shape, lambda *_: (0,) * nd)


def _bdot(a, b):
    return jnp.dot(a.astype(BF16), b.astype(BF16), preferred_element_type=F32)


def _bdot_nt(a, b):
    return lax.dot_general(a.astype(BF16), b.astype(BF16), (((1,), (1,)), ((), ())),
                           preferred_element_type=F32)


def _bdot_tn(a, b):
    return lax.dot_general(a.astype(BF16), b.astype(BF16), (((0,), (0,)), ((), ())),
                           preferred_element_type=F32)


def _layer_norm(z, g, b):
    mu = jnp.mean(z, axis=-1, keepdims=True)
    zc = z - mu
    var = jnp.mean(zc * zc, axis=-1, keepdims=True)
    return zc * lax.rsqrt(var + LN_EPS) * g + b


def _linear_kernel(x_ref, w_ref, o_ref):
    o_ref[...] = _bdot(x_ref[...], w_ref[...])


def _linear(x, w, tm=512):
    T, K = x.shape
    N = w.shape[1]
    tm = _row_tile(T, tm)
    return pl.pallas_call(
        _linear_kernel,
        out_shape=jax.ShapeDtypeStruct((T, N), F32),
        grid=(T // tm,),
        in_specs=[pl.BlockSpec((tm, K), lambda i: (i, 0)), _full((K, N))],
        out_specs=pl.BlockSpec((tm, N), lambda i: (i, 0)),
        compiler_params=_params("parallel"),
        name="linear",
    )(x, w)


MXU_DIM = 256


def _w1_relayout_kernel(w_ref, o_ref):
    half = MXU_DIM // 2
    r = lax.broadcasted_iota(jnp.int32, (MXU_DIM, MXU_DIM), 0)
    c = lax.broadcasted_iota(jnp.int32, (MXU_DIM, MXU_DIM), 1)
    src = jnp.where(c < half, 2 * c, 2 * (c - half) + 1)
    perm = (r == src).astype(BF16)
    n_out = o_ref.shape[-1] // 2
    for j in range(w_ref.shape[-1] // MXU_DIM):
        y = jnp.dot(w_ref[0, :, j * MXU_DIM:(j + 1) * MXU_DIM].astype(BF16), perm, preferred_element_type=F32)
        o_ref[0, :, j * half:(j + 1) * half] = y[:, :half].astype(BF16)
        o_ref[0, :, n_out + j * half:n_out + (j + 1) * half] = y[:, half:].astype(BF16)


def _w1_relayout(w1):
    E, D, N = w1.shape
    spec = pl.BlockSpec((1, D, N), lambda e: (e, 0, 0))
    return pl.pallas_call(
        _w1_relayout_kernel,
        out_shape=jax.ShapeDtypeStruct((E, D, N), BF16),
        grid=(E,),
        in_specs=[spec],
        out_specs=spec,
        compiler_params=_params("parallel"),
        name="w1_relayout",
    )(w1)


def _gla_in_kernel(x_ref, w_ref, wg_ref, wg2_ref, bg_ref, q_ref, k_ref, v_ref, r_ref, b_ref,
                   *, tm, seq_len, seq_pad):
    xb = x_ref[...].astype(BF16)
    y = jnp.dot(xb, w_ref[...], preferred_element_type=F32)
    q_ref[...] = y[:, :GLA_QK] * (GLA_DK ** -0.5)
    k_ref[...] = y[:, GLA_QK:2 * GLA_QK]
    v_ref[...] = y[:, 2 * GLA_QK:2 * GLA_QK + GLA_V]
    r_ref[...] = y[:, 2 * GLA_QK + GLA_V:]
    g_lr = jnp.dot(xb, wg_ref[...], preferred_element_type=F32)
    z = _bdot(g_lr, wg2_ref[...]) + bg_ref[...]
    log_a = (jnp.minimum(z, 0.0) - jnp.log(1.0 + jnp.exp(-jnp.abs(z)))) / GLA_GATE_TEMP
    if seq_len < seq_pad:
        pos = (pl.program_id(0) * tm + lax.broadcasted_iota(jnp.int32, (tm, 1), 0)) % seq_pad
        log_a = jnp.where(pos < seq_len, log_a, 0.0)
    row = lax.broadcasted_iota(jnp.int32, (tm, tm), 0)
    col = lax.broadcasted_iota(jnp.int32, (tm, tm), 1)
    tri = ((row // CHUNK == col // CHUNK) & (row >= col)).astype(BF16)
    g1 = log_a.astype(BF16)
    r1 = log_a - g1.astype(F32)
    g2 = r1.astype(BF16)
    g3 = (r1 - g2.astype(F32)).astype(BF16)
    b_ref[...] = (jnp.dot(tri, g1, preferred_element_type=F32) + jnp.dot(tri, g2, preferred_element_type=F32)
                  + jnp.dot(tri, g3, preferred_element_type=F32))


def _gla_in(x, w_main, w_g, w_g2, b_g, seq_len, seq_pad, tm=512):
    T = x.shape[0]
    tm = _row_tile(T, tm)
    assert tm % CHUNK == 0 and seq_pad % CHUNK == 0
    n_main = w_main.shape[1]
    row = lambda n: pl.BlockSpec((tm, n), lambda i: (i, 0))
    return pl.pallas_call(
        functools.partial(_gla_in_kernel, tm=tm, seq_len=seq_len, seq_pad=seq_pad),
        out_shape=(jax.ShapeDtypeStruct((T, GLA_QK), F32), jax.ShapeDtypeStruct((T, GLA_QK), F32),
                   jax.ShapeDtypeStruct((T, GLA_V), F32), jax.ShapeDtypeStruct((T, GLA_V), F32),
                   jax.ShapeDtypeStruct((T, GLA_QK), F32)),
        grid=(T // tm,),
        in_specs=[row(D_MODEL), _full((D_MODEL, n_main)), _full((D_MODEL, LANES)),
                  _full((LANES, GLA_QK)), _full((1, GLA_QK))],
        out_specs=(row(GLA_QK), row(GLA_QK), row(GLA_V), row(GLA_V), row(GLA_QK)),
        compiler_params=_params("parallel"),
        name="gla_in",
    )(x, w_main, w_g, w_g2, b_g)


SUBLANES = 8


def _gla_intra(q, k, b):
    n_grp = CHUNK // SUBLANES
    lane = lax.broadcasted_iota(jnp.int32, (SUBLANES, LANES), 1)
    row = lax.broadcasted_iota(jnp.int32, (SUBLANES, LANES), 0)
    grp = lambda t, i: t[i * SUBLANES:(i + 1) * SUBLANES]
    att = [jnp.zeros((SUBLANES, LANES), F32) for _ in range(n_grp)]
    for s in range(CHUNK):
        b_s = b[s:s + 1, :]
        k_s = k[s:s + 1, :]
        for i in range(s // SUBLANES, n_grp):
            e = jnp.exp(grp(b, i) - b_s)
            red = jnp.sum(grp(q, i) * e * k_s, axis=-1, keepdims=True)
            att[i] = jnp.where(lane == s, red, att[i])
    att = [jnp.where(row + i * SUBLANES >= lane, att[i], 0.0) for i in range(n_grp)]
    return jnp.concatenate(att, axis=0)


def _gla_scan_kernel(q_ref, k_ref, v_ref, b_ref, s0_ref, o_ref, s_ref, st_scr, *, n_chunks):
    @pl.when(pl.program_id(1) == 0)
    def _():
        st_scr[...] = s0_ref[0]

    for c in range(n_chunks):
        ts = slice(c * CHUNK, (c + 1) * CHUNK)
        for h in range(GLA_HEADS):
            ks = slice(h * GLA_DK, (h + 1) * GLA_DK)
            vs = slice(h * GLA_DV, (h + 1) * GLA_DV)
            q = q_ref[0, ts, ks]
            k = k_ref[0, ts, ks]
            b = b_ref[0, ts, ks]
            v = v_ref[0, ts, vs]
            att = _gla_intra(q, k, b)
            b_end = b[CHUNK - 1:CHUNK, :]
            st = st_scr[h]
            o_ref[0, ts, vs] = _bdot(att[:, :CHUNK], v) + _bdot_nt(q * jnp.exp(b), st)
            st_scr[h] = st * jnp.exp(b_end) + _bdot_tn(v, k * jnp.exp(b_end - b))

    @pl.when(pl.program_id(1) == pl.num_programs(1) - 1)
    def _():
        s_ref[0] = st_scr[...]


def _gla_scan(q, k, v, la, s0t, tl=128):
    B, Lp, _ = q.shape
    tl = _row_tile(Lp, tl)
    qk_spec = pl.BlockSpec((1, tl, GLA_QK), lambda b, i: (b, i, 0))
    v_spec = pl.BlockSpec((1, tl, GLA_V), lambda b, i: (b, i, 0))
    s_spec = pl.BlockSpec((1, GLA_HEADS, GLA_DV, GLA_DK), lambda b, i: (b, 0, 0, 0))
    return pl.pallas_call(
        functools.partial(_gla_scan_kernel, n_chunks=tl // CHUNK),
        out_shape=(jax.ShapeDtypeStruct((B, Lp, GLA_V), F32),
                   jax.ShapeDtypeStruct((B, GLA_HEADS, GLA_DV, GLA_DK), F32)),
        grid=(B, Lp // tl),
        in_specs=[qk_spec, qk_spec, v_spec, qk_spec, s_spec],
        out_specs=(v_spec, s_spec),
        scratch_shapes=[pltpu.VMEM((GLA_HEADS, GLA_DV, GLA_DK), F32)],
        compiler_params=_params("parallel", "arbitrary"),
        name="gla_scan",
    )(q, k, v, la, s0t)


def _gla_out_kernel(o_ref, r_ref, x_ref, gn_ref, wo_ref, g_ref, b_ref, y_ref):
    parts = []
    for h in range(GLA_HEADS):
        vs = slice(h * GLA_DV, (h + 1) * GLA_DV)
        o = o_ref[:, vs]
        mu = jnp.mean(o, axis=-1, keepdims=True)
        oc = o - mu
        var = jnp.mean(oc * oc, axis=-1, keepdims=True)
        parts.append(oc * lax.rsqrt(var + LN_EPS) * gn_ref[:, vs])
    o = jnp.concatenate(parts, axis=-1)
    r = r_ref[...]
    o = o * (r * jax.nn.sigmoid(r))
    h = _bdot(o, wo_ref[...])
    y_ref[...] = _layer_norm(DEEPNORM_ALPHA * x_ref[...] + h, g_ref[...], b_ref[...])


def _gla_out(o, r, x, gn_g, w_o, ln_g, ln_b, tm=512):
    T = x.shape[0]
    tm = _row_tile(T, tm)
    row = pl.BlockSpec((tm, D_MODEL), lambda i: (i, 0))
    vec = _full((1, D_MODEL))
    return pl.pallas_call(
        _gla_out_kernel,
        out_shape=jax.ShapeDtypeStruct((T, D_MODEL), F32),
        grid=(T // tm,),
        in_specs=[row, row, row, vec, _full((GLA_V, D_MODEL)), vec, vec],
        out_specs=row,
        compiler_params=_params("parallel"),
        name="gla_out",
    )(o, r, x, gn_g, w_o, ln_g, ln_b)


def _swa_kernel(x_ref, k_ref, v_ref, wq_ref, sink_ref, wo_ref, g_ref, b_ref, y_ref,
                *, tl, seq_len, past_valid):
    x = x_ref[0]
    q = (_bdot(x, wq_ref[...]) * (SWA_HEAD_DIM ** -0.5)).astype(BF16)
    kw = tl + WINDOW
    r0 = pl.multiple_of(pl.program_id(1) * tl, tl)
    kt = k_ref[0, pl.ds(r0, kw), :].astype(BF16)
    vt = v_ref[0, pl.ds(r0, kw), :].astype(BF16)
    q_row = lax.broadcasted_iota(jnp.int32, (tl, kw), 0)
    k_col = lax.broadcasted_iota(jnp.int32, (tl, kw), 1)
    band_lo = (q_row // CHUNK) * CHUNK
    pos = r0 + k_col
    ok = (k_col >= band_lo) & (k_col < band_lo + WINDOW + CHUNK) & (pos - WINDOW < seq_len)
    if not past_valid:
        ok = ok & (pos >= WINDOW)
    zeros = jnp.zeros((kw, SWA_HEAD_DIM), BF16)
    pair_out = []
    for g in range(SWA_KV_HEADS):
        kg = kt[:, g * SWA_HEAD_DIM:(g + 1) * SWA_HEAD_DIM]
        vg = vt[:, g * SWA_HEAD_DIM:(g + 1) * SWA_HEAD_DIM]
        k_pad = (jnp.concatenate([kg, zeros], axis=1), jnp.concatenate([zeros, kg], axis=1))
        v_pad = (jnp.concatenate([vg, zeros], axis=1), jnp.concatenate([zeros, vg], axis=1))
        for pair in range(g * SWA_GROUP // 2, (g + 1) * SWA_GROUP // 2):
            qp = q[:, pair * LANES:(pair + 1) * LANES]
            sinks = [sink_ref[2 * pair + j:2 * pair + j + 1, :] for j in range(2)]
            s = [jnp.where(ok, _bdot_nt(qp, k_pad[j]), NEG_INF) for j in range(2)]
            m = [jnp.maximum(jnp.max(s[j], axis=-1, keepdims=True), sinks[j]) for j in range(2)]
            e = [jnp.exp(s[j] - m[j]) for j in range(2)]
            o = [_bdot(e[j], v_pad[j]) for j in range(2)]
            inv = [1.0 / (jnp.sum(e[j], axis=-1, keepdims=True) + jnp.exp(sinks[j] - m[j])) for j in range(2)]
            pair_out.append(o[0] * inv[0] + o[1] * inv[1])
    h_out = _bdot(jnp.concatenate(pair_out, axis=1), wo_ref[...])
    y_ref[0] = _layer_norm(DEEPNORM_ALPHA * x + h_out, g_ref[...], b_ref[...])


def _swa(x, k_all, v_all, w_q, sinks, w_o, ln_g, ln_b, seq_len, past_valid, tl=256):
    B, Lp, _ = x.shape
    tl = _row_tile(Lp, tl)
    x_spec = pl.BlockSpec((1, tl, D_MODEL), lambda b, i: (b, i, 0))
    kv_spec = pl.BlockSpec((1, WINDOW + Lp, SWA_KV), lambda b, i: (b, 0, 0))
    vec = _full((1, D_MODEL))
    return pl.pallas_call(
        functools.partial(_swa_kernel, tl=tl, seq_len=seq_len, past_valid=past_valid),
        out_shape=jax.ShapeDtypeStruct((B, Lp, D_MODEL), F32),
        grid=(B, Lp // tl),
        in_specs=[x_spec, kv_spec, kv_spec, _full((D_MODEL, D_MODEL)), _full((SWA_Q_HEADS, 1)),
                  _full((D_MODEL, D_MODEL)), vec, vec],
        out_specs=x_spec,
        compiler_params=_params("parallel", "arbitrary"),
        name="swa",
    )(x, k_all, v_all, w_q, sinks, w_o, ln_g, ln_b)


def _mem_kernel(x_ref, mk_ref, mv_ref, wq_ref, wo_ref, g_ref, b_ref, y_ref):
    x = x_ref[0]
    q = _bdot(x, wq_ref[...]) * (MEM_HEAD_DIM ** -0.5)
    parts = []
    for h in range(MEM_HEADS):
        hs = slice(h * MEM_HEAD_DIM, (h + 1) * MEM_HEAD_DIM)
        s = _bdot_nt(q[:, hs], mk_ref[0, :, hs])
        m = jnp.max(s, axis=-1, keepdims=True)
        e = jnp.exp(s - m)
        parts.append(_bdot(e, mv_ref[0, :, hs]) * (1.0 / jnp.sum(e, axis=-1, keepdims=True)))
    o = jnp.concatenate(parts, axis=-1)
    h_out = _bdot(o, wo_ref[...])
    y_ref[0] = _layer_norm(DEEPNORM_ALPHA * x + h_out, g_ref[...], b_ref[...])


def _mem_xattn(x, mk, mv, w_q, w_o, ln_g, ln_b, tl=512):
    B, L, _ = x.shape
    tl = _row_tile(L, tl)
    x_spec = pl.BlockSpec((1, tl, D_MODEL), lambda b, i: (b, i, 0))
    m_spec = pl.BlockSpec((1, N_MEM, D_MODEL), lambda b, i: (b, 0, 0))
    vec = _full((1, D_MODEL))
    return pl.pallas_call(
        _mem_kernel,
        out_shape=jax.ShapeDtypeStruct((B, L, D_MODEL), F32),
        grid=(B, L // tl),
        in_specs=[x_spec, m_spec, m_spec, _full((D_MODEL, D_MODEL)), _full((D_MODEL, D_MODEL)), vec, vec],
        out_specs=x_spec,
        compiler_params=_params("parallel", "arbitrary"),
        name="mem_xattn",
    )(x, mk, mv, w_q, w_o, ln_g, ln_b)


def _router_kernel(x_ref, wr_ref, br_ref, idx_ref, gate_ref, pos_ref, cnt_ref, cnt_scr, *, tm):
    @pl.when(pl.program_id(0) == 0)
    def _():
        cnt_scr[...] = jnp.zeros_like(cnt_scr)

    logits = _bdot(x_ref[...], wr_ref[...]) + br_ref[...]
    lane = lax.broadcasted_iota(jnp.int32, (tm, LANES), 1)
    sel = jnp.zeros((tm, LANES), F32)
    idx_out = jnp.zeros((tm, LANES), jnp.int32)
    val_out = jnp.zeros((tm, LANES), F32)
    l = logits
    idxs = []
    for j in range(TOP_K):
        m = jnp.max(l, axis=-1, keepdims=True)
        idx = jnp.min(jnp.where(l == m, lane, LANES), axis=-1, keepdims=True)
        hit = lane == idx
        sel = jnp.where(hit, 1.0, sel)
        l = jnp.where(hit, -jnp.inf, l)
        idx_out = jnp.where(lane == j, idx, idx_out)
        val_out = jnp.where(lane == j, m, val_out)
        idxs.append(idx)
    e = jnp.where(lane < TOP_K, jnp.exp(val_out - val_out[:, 0:1]), 0.0)
    gates = e / jnp.sum(e, axis=-1, keepdims=True)
    r = lax.broadcasted_iota(jnp.int32, (tm, tm), 0)
    c = lax.broadcasted_iota(jnp.int32, (tm, tm), 1)
    before = jnp.dot((r > c).astype(BF16), sel.astype(BF16), preferred_element_type=F32) + cnt_scr[...]
    pos_out = jnp.zeros((tm, LANES), F32)
    for j in range(TOP_K):
        pj = jnp.sum(jnp.where(lane == idxs[j], before, 0.0), axis=-1, keepdims=True)
        pos_out = jnp.where(lane == j, pj, pos_out)
    cnt_scr[...] = cnt_scr[...] + jnp.sum(sel, axis=0, keepdims=True)
    idx_ref[...] = idx_out[:, :TOP_K]
    gate_ref[...] = gates[:, :TOP_K]
    pos_ref[...] = pos_out[:, :TOP_K].astype(jnp.int32)
    cnt_ref[...] = cnt_scr[...].astype(jnp.int32)


def _router(x, w_r, b_r, tm=512):
    T = x.shape[0]
    tm = _row_tile(T, tm)
    small = pl.BlockSpec((tm, TOP_K), lambda i: (i, 0))
    return pl.pallas_call(
        functools.partial(_router_kernel, tm=tm),
        out_shape=(jax.ShapeDtypeStruct((T, TOP_K), jnp.int32), jax.ShapeDtypeStruct((T, TOP_K), F32),
                   jax.ShapeDtypeStruct((T, TOP_K), jnp.int32), jax.ShapeDtypeStruct((1, LANES), jnp.int32)),
        grid=(T // tm,),
        in_specs=[pl.BlockSpec((tm, D_MODEL), lambda i: (i, 0)), _full((D_MODEL, LANES)), _full((1, LANES))],
        out_specs=(small, small, small, _full((1, LANES))),
        scratch_shapes=[pltpu.VMEM((1, LANES), F32)],
        compiler_params=_params("arbitrary"),
        name="moe_router",
    )(x, w_r, b_r)


ISSUE_UNROLL = 8


def _dispatch_kernel(pad_end_ref, dest_ref, x_ref, xs_ref, zero_scr, sem, zsem, *, tm, blk):
    @pl.when(pl.program_id(0) == 0)
    def _():
        zero_scr[...] = jnp.zeros_like(zero_scr)

        def fill(e):
            start = pl.multiple_of(pad_end_ref[e] - blk, blk)
            return pltpu.make_async_copy(zero_scr, xs_ref.at[pl.ds(start, blk), :], zsem)

        def has_block(e):
            return pad_end_ref[e] > (pad_end_ref[e - 1] if e else 0)

        for e in range(N_EXPERTS):
            pl.when(has_block(e))(lambda e=e: fill(e).start())
        for e in range(N_EXPERTS):
            pl.when(has_block(e))(lambda e=e: fill(e).wait())

    def issue(g, carry):
        base = pl.multiple_of(g * (ISSUE_UNROLL // TOP_K), ISSUE_UNROLL // TOP_K)
        for u in range(ISSUE_UNROLL // TOP_K):
            r = base + u
            for j in range(TOP_K):
                pltpu.make_async_copy(x_ref.at[pl.ds(r, 1), :], xs_ref.at[pl.ds(dest_ref[0, j, r], 1), :],
                                      sem).start(priority=j % 2)
        return carry

    lax.fori_loop(0, tm * TOP_K // ISSUE_UNROLL, issue, 0)
    for j in range(TOP_K):
        pltpu.make_async_copy(x_ref, xs_ref.at[pl.ds(0, tm), :], sem).wait()


def _dispatch(x, dest_blocks, pad_end, n_slots, tm, blk):
    T = x.shape[0]
    return pl.pallas_call(
        functools.partial(_dispatch_kernel, tm=tm, blk=blk),
        out_shape=jax.ShapeDtypeStruct((n_slots, D_MODEL), F32),
        grid_spec=pltpu.PrefetchScalarGridSpec(
            num_scalar_prefetch=1,
            grid=(T // tm,),
            in_specs=[pl.BlockSpec((1, TOP_K, tm), lambda i, pe: (i, 0, 0), memory_space=pltpu.SMEM),
                      pl.BlockSpec((tm, D_MODEL), lambda i, pe: (i, 0))],
            out_specs=pl.BlockSpec(memory_space=pl.ANY),
            scratch_shapes=[pltpu.VMEM((blk, D_MODEL), F32), pltpu.SemaphoreType.DMA, pltpu.SemaphoreType.DMA]),
        compiler_params=_params("arbitrary"),
        name="moe_dispatch",
    )(pad_end, dest_blocks, x)


def _collect_combine_kernel(idx_ref, idx_next_ref, src_ref, gate_ref, x_ref, g_ref, b_ref, y_ref, buf, sem, *, tm):
    i = pl.program_id(0)
    slot = i % 2

    def request(ids, s):
        def body(g, carry):
            base = pl.multiple_of(g * (ISSUE_UNROLL // TOP_K), ISSUE_UNROLL // TOP_K)
            for u in range(ISSUE_UNROLL // TOP_K):
                r = base + u
                for j in range(TOP_K):
                    pltpu.make_async_copy(src_ref.at[pl.ds(ids[0, j, r], 1), :],
                                          buf.at[s, j, pl.ds(r, 1), :], sem.at[s]).start(priority=j % 2)
            return carry
        lax.fori_loop(0, tm * TOP_K // ISSUE_UNROLL, body, 0)

    @pl.when(i == 0)
    def _():
        request(idx_ref, 0)

    @pl.when(i + 1 < pl.num_programs(0))
    def _():
        request(idx_next_ref, 1 - slot)

    for j in range(TOP_K):
        pltpu.make_async_copy(src_ref.at[pl.ds(0, tm), :], buf.at[slot, j], sem.at[slot]).wait()
    gates = gate_ref[...]
    acc = DEEPNORM_ALPHA * x_ref[...]
    for j in range(TOP_K):
        acc = acc + gates[:, j:j + 1] * buf[slot, j]
    y_ref[...] = _layer_norm(acc, g_ref[...], b_ref[...])


def _collect_combine(src, idx_blocks, gates, x, ln_g, ln_b, tm):
    T = x.shape[0]
    nt = T // tm
    row = pl.BlockSpec((tm, D_MODEL), lambda i: (i, 0))
    vec = _full((1, D_MODEL))
    return pl.pallas_call(
        functools.partial(_collect_combine_kernel, tm=tm),
        out_shape=jax.ShapeDtypeStruct((T, D_MODEL), F32),
        grid=(nt,),
        in_specs=[pl.BlockSpec((1, TOP_K, tm), lambda i: (i, 0, 0), memory_space=pltpu.SMEM),
                  pl.BlockSpec((1, TOP_K, tm), lambda i: (jnp.minimum(i + 1, nt - 1), 0, 0), memory_space=pltpu.SMEM),
                  pl.BlockSpec(memory_space=pl.ANY),
                  pl.BlockSpec((tm, TOP_K), lambda i: (i, 0)), row, vec, vec],
        out_specs=row,
        scratch_shapes=[pltpu.VMEM((2, TOP_K, tm, D_MODEL), F32), pltpu.SemaphoreType.DMA((2,))],
        compiler_params=_params("arbitrary"),
        name="moe_collect_combine",
    )(idx_blocks, idx_blocks, src, gates, x, ln_g, ln_b)


def _expert_kernel(be_ref, nb_ref, xs_ref, w1_ref, b1_ref, w2_ref, b2_ref, o_ref):
    @pl.when(pl.program_id(0) < nb_ref[0])
    def _():
        h = _bdot(xs_ref[...], w1_ref[0]) + b1_ref[0]
        glu = jnp.minimum(h[:, :D_EXPERT], SWIGLU_LIMIT)
        lin = jnp.clip(h[:, D_EXPERT:], -SWIGLU_LIMIT, SWIGLU_LIMIT)
        a = glu * jax.nn.sigmoid(SWIGLU_ALPHA * glu) * (lin + 1.0)
        o_ref[...] = _bdot(a, w2_ref[0]) + b2_ref[0]


def _experts(xs, block_e, n_used, w1, b1, w2, b2, blk):
    P = xs.shape[0]
    nb = P // blk

    def blk_map(i, be, nu):
        return (jnp.minimum(i, nu[0] - 1), 0)

    def w_map(i, be, nu):
        return (be[jnp.minimum(i, nu[0] - 1)], 0, 0)

    return pl.pallas_call(
        _expert_kernel,
        out_shape=jax.ShapeDtypeStruct((P, D_MODEL), F32),
        grid_spec=pltpu.PrefetchScalarGridSpec(
            num_scalar_prefetch=2,
            grid=(nb,),
            in_specs=[pl.BlockSpec((blk, D_MODEL), blk_map),
                      pl.BlockSpec((1, D_MODEL, 2 * D_EXPERT), w_map),
                      pl.BlockSpec((1, 1, 2 * D_EXPERT), w_map),
                      pl.BlockSpec((1, D_EXPERT, D_MODEL), w_map),
                      pl.BlockSpec((1, 1, D_MODEL), w_map)],
            out_specs=pl.BlockSpec((blk, D_MODEL), blk_map)),
        compiler_params=_params("arbitrary"),
        name="moe_experts",
    )(block_e, n_used, xs, w1, b1, w2, b2)


def _moe(x, w_r, b_r, w1, b1, w2, b2, ln_g, ln_b):
    T = x.shape[0]
    blk = 512 if T >= 4096 else 128
    tm = _row_tile(T, 512)
    n_assign = T * TOP_K
    n_blocks = -(-(n_assign + N_EXPERTS * (blk - 1)) // blk)
    n_slots = n_blocks * blk

    top_i, gates, pos, counts = _router(x, w_r, b_r)
    counts = counts[0, :N_EXPERTS]
    padded = (counts + blk - 1) // blk * blk
    pad_end = jnp.cumsum(padded)
    pad_start = pad_end - padded
    dest = pad_start[top_i] + pos
    block_start = jnp.arange(n_blocks, dtype=jnp.int32) * blk
    block_e = jnp.minimum(jnp.sum(pad_end[None, :] <= block_start[:, None], axis=1), N_EXPERTS - 1).astype(jnp.int32)
    n_used = (pad_end[-1:] // blk).astype(jnp.int32)
    dest_km = dest.T
    dest_blocks = dest_km.reshape(TOP_K, T // tm, tm).transpose(1, 0, 2)

    xs = _dispatch(x, dest_blocks, pad_end.astype(jnp.int32), n_slots, tm, blk)
    out = _experts(xs, block_e, n_used, w1, b1, w2, b2, blk)
    return _collect_combine(out, dest_blocks, gates, x, ln_g, ln_b, tm)


def _pad_rows(t, n):
    return t if n == 0 else jnp.pad(t, ((0, 0), (0, n), (0, 0)))


def _trunk(x, gla_s0, past_k, past_v, past_valid, mem_k, mem_v, w):
    B, L, _ = x.shape
    Lp = -(-L // CHUNK) * CHUNK
    T = B * L
    vec = lambda a: a.reshape(1, -1)
    gla_states = []
    k_all = v_all = new_k = new_v = None
    for layer in range(DEPTH):
        lg, lb = w['ln_g'][layer], w['ln_b'][layer]
        xt = x.reshape(T, D_MODEL)
        if layer < N_A:
            i = layer
            xt = _pad_rows(x, Lp - L).reshape(B * Lp, D_MODEL)
            q, k, v, r, b = _gla_in(xt, w['gla_w_main'][i], w['gla_w_g'][i], w['gla_w_g2'][i],
                                    vec(w['gla_b_g'][i]), L, Lp)
            seq = lambda t: t.reshape(B, Lp, -1)
            s0t = jnp.swapaxes(gla_s0[i], -1, -2)
            o, st = _gla_scan(seq(q), seq(k), seq(v), seq(b), s0t)
            gla_states.append(jnp.swapaxes(st, -1, -2))
            xt = _gla_out(o.reshape(B * Lp, GLA_V), r, xt, vec(w['gla_gn_g'][i]), w['gla_w_o'][i],
                          vec(lg[0]), vec(lb[0]))
            x = xt.reshape(B, Lp, D_MODEL)[:, :L]
        else:
            j = layer - N_A
            if j == 0:
                kv = _linear(xt, w['kv_w']).reshape(B, L, 2 * SWA_KV)
                k_full = jnp.concatenate([past_k, kv[..., :SWA_KV]], axis=1)
                v_full = jnp.concatenate([past_v, kv[..., SWA_KV:]], axis=1)
                new_k, new_v = k_full[:, -WINDOW:], v_full[:, -WINDOW:]
                k_all, v_all = _pad_rows(k_full, Lp - L), _pad_rows(v_full, Lp - L)
            xp = _swa(_pad_rows(x, Lp - L), k_all, v_all, w['swa_w_q'][j], w['swa_sinks'][j].reshape(-1, 1),
                      w['swa_w_o'][j], vec(lg[0]), vec(lb[0]), L, past_valid)
            x = xp[:, :L]
        x = _mem_xattn(x, mem_k[layer], mem_v[layer], w['mem_w_q'][layer], w['mem_w_o'][layer],
                       vec(lg[1]), vec(lb[1]))
        xt = _moe(x.reshape(T, D_MODEL), w['moe_w_r'][layer], w['moe_b_r'][layer], w['moe_w1'][layer],
                  w['moe_b1'][layer], w['moe_w2'][layer], w['moe_b2'][layer], vec(lg[2]), vec(lb[2]))
        x = xt.reshape(B, L, D_MODEL)
    return x, jnp.stack(gla_states), new_k, new_v


def _prep_weights(gla_w_in, gla_w_g2, gla_b_g, gla_gn_g, gla_w_o, kv_w, swa_w_q, swa_sinks, swa_w_o,
                  mem_w_q, mem_w_kv, mem_w_o, moe_w_r, moe_b_r, moe_w1, moe_b1, moe_w2, moe_b2, ln_g, ln_b):
    n_main = 2 * GLA_QK + 2 * GLA_V
    pad_c = lambda a, n: jnp.pad(a, [(0, 0)] * (a.ndim - 1) + [(0, n - a.shape[-1])])
    w = dict(
        gla_w_main=gla_w_in[:, :, :n_main].astype(BF16),
        gla_w_g=pad_c(gla_w_in[:, :, n_main:], LANES).astype(BF16),
        gla_w_g2=jnp.pad(gla_w_g2, ((0, 0), (0, LANES - GLA_GATE_RANK), (0, 0))).astype(BF16),
        gla_b_g=gla_b_g, gla_gn_g=gla_gn_g.reshape(N_A, GLA_V), gla_w_o=gla_w_o.astype(BF16),
        kv_w=kv_w.astype(BF16), swa_w_q=swa_w_q.astype(BF16), swa_sinks=swa_sinks, swa_w_o=swa_w_o.astype(BF16),
        mem_w_q=mem_w_q.astype(BF16), mem_w_kv=mem_w_kv.astype(BF16), mem_w_o=mem_w_o.astype(BF16),
        moe_w_r=pad_c(moe_w_r, LANES).astype(BF16),
        moe_b_r=jnp.pad(moe_b_r, ((0, 0), (0, LANES - N_EXPERTS)), constant_values=-jnp.inf).reshape(DEPTH, 1, LANES),
        moe_w1=_w1_relayout(moe_w1.reshape(DEPTH * N_EXPERTS, D_MODEL, 2 * D_EXPERT)).reshape(moe_w1.shape),
        moe_b1=jnp.concatenate([moe_b1[..., 0::2], moe_b1[..., 1::2]], axis=-1).reshape(DEPTH, N_EXPERTS, 1, -1),
        moe_w2=moe_w2.astype(BF16), moe_b2=moe_b2.reshape(DEPTH, N_EXPERTS, 1, D_MODEL),
        ln_g=ln_g, ln_b=ln_b)
    return w


def kernel(x_prompt, x_sample, state_gla, cache_swa_k, cache_swa_v, cache_mem_k, cache_mem_v, mem_prompt, gla_w_in, gla_w_g2, gla_b_g, gla_gn_g, gla_w_o, kv_w, swa_w_q, swa_sinks, swa_w_o, mem_w_q, mem_w_kv, mem_w_o, moe_w_r, moe_b_r, moe_w1, moe_b1, moe_w2, moe_b2, ln_g, ln_b):
    w = _prep_weights(gla_w_in, gla_w_g2, gla_b_g, gla_gn_g, gla_w_o, kv_w, swa_w_q, swa_sinks, swa_w_o,
                      mem_w_q, mem_w_kv, mem_w_o, moe_w_r, moe_b_r, moe_w1, moe_b1, moe_w2, moe_b2, ln_g, ln_b)
    bp = x_prompt.shape[0]
    bs = x_sample.shape[0]
    mem_flat = mem_prompt.reshape(bp * N_MEM, D_MODEL)
    mem_kv = jnp.stack([_linear(mem_flat, w['mem_w_kv'][l]) for l in range(DEPTH)])
    mem_k_p = mem_kv[..., :D_MODEL].reshape(DEPTH, bp, N_MEM, D_MODEL)
    mem_v_p = mem_kv[..., D_MODEL:].reshape(DEPTH, bp, N_MEM, D_MODEL)
    gla0 = jnp.zeros((N_A, bp, GLA_HEADS, GLA_DK, GLA_DV), F32)
    zero_win = jnp.zeros((bp, WINDOW, SWA_KV), F32)
    y_p, gla_p, k_p, v_p = _trunk(x_prompt, gla0, zero_win, zero_win, False, mem_k_p, mem_v_p, w)
    y_s, gla_s, k_s, v_s = _trunk(x_sample, state_gla, cache_swa_k.reshape(bs, WINDOW, SWA_KV),
                                  cache_swa_v.reshape(bs, WINDOW, SWA_KV), True,
                                  cache_mem_k.reshape(DEPTH, bs, N_MEM, D_MODEL),
                                  cache_mem_v.reshape(DEPTH, bs, N_MEM, D_MODEL), w)
    heads4 = lambda t: t.reshape(t.shape[0], WINDOW, SWA_KV_HEADS, SWA_HEAD_DIM)
    mem5 = lambda t: t.reshape(DEPTH, bp, N_MEM, MEM_HEADS, MEM_HEAD_DIM)
    return (y_p, y_s, gla_p, gla_s, heads4(k_p), heads4(v_p), heads4(k_s), heads4(v_s), mem5(mem_k_p), mem5(mem_v_p))
```

```python
import functools

import jax
import jax.numpy as jnp
from jax import lax
from jax.experimental import pallas as pl
from jax.experimental.pallas import tpu as pltpu

F32 = jnp.float32
BF16 = jnp.bfloat16

D_MODEL = 1024
DEPTH = 4
CHUNK = 64
N_A = DEPTH // 2
GLA_HEADS = 4
GLA_DK = D_MODEL // (2 * GLA_HEADS)
GLA_DV = D_MODEL // GLA_HEADS
GLA_QK = GLA_HEADS * GLA_DK
GLA_V = GLA_HEADS * GLA_DV
GLA_GATE_RANK = 16
GLA_GATE_TEMP = 16.0
SWA_HEAD_DIM = 64
SWA_Q_HEADS = D_MODEL // SWA_HEAD_DIM
SWA_KV_HEADS = 4
SWA_GROUP = SWA_Q_HEADS // SWA_KV_HEADS
SWA_KV = SWA_KV_HEADS * SWA_HEAD_DIM
WINDOW = 128
WINDOW_CHUNKS = WINDOW // CHUNK
N_MEM = 256
MEM_HEADS = 4
MEM_HEAD_DIM = D_MODEL // MEM_HEADS
N_EXPERTS = 32
TOP_K = 4
D_EXPERT = D_MODEL
SWIGLU_ALPHA = 1.702
SWIGLU_LIMIT = 7.0
DEEPNORM_ALPHA = (2 * DEPTH) ** 0.25
LN_EPS = 1e-5
NEG_INF = -1e30

LANES = 128
VMEM_LIMIT = 56 * 1024 * 1024


def _params(*sem):
    return pltpu.CompilerParams(dimension_semantics=sem, vmem_limit_bytes=VMEM_LIMIT)


def _row_tile(n_rows, want):
    t = min(want, n_rows)
    while n_rows % t:
        t //= 2
    return t


def _full(shape):
    nd = len(shape)
    return pl.BlockSpec(shape, lambda *_: (0,) * nd)


def _bdot(a, b):
    return jnp.dot(a.astype(BF16), b.astype(BF16), preferred_element_type=F32)


def _bdot_nt(a, b):
    return lax.dot_general(a.astype(BF16), b.astype(BF16), (((1,), (1,)), ((), ())),
                           preferred_element_type=F32)


def _bdot_tn(a, b):
    return lax.dot_general(a.astype(BF16), b.astype(BF16), (((0,), (0,)), ((), ())),
                           preferred_element_type=F32)


def _layer_norm(z, g, b):
    mu = jnp.mean(z, axis=-1, keepdims=True)
    zc = z - mu
    var = jnp.mean(zc * zc, axis=-1, keepdims=True)
    return zc * lax.rsqrt(var + LN_EPS) * g + b


ROW_TILES = D_MODEL // LANES


def _rows_to_tiles(ref, x):
    n = x.shape[0]
    for c in range(ROW_TILES):
        ref[pl.ds(c, n, stride=ROW_TILES), :] = x[:, c * LANES:(c + 1) * LANES]


def _tiles_to_rows(ref, n):
    return jnp.concatenate([ref[pl.ds(c, n, stride=ROW_TILES), :] for c in range(ROW_TILES)], axis=1)


def _linear_kernel(x_ref, w_ref, o_ref):
    o_ref[...] = _bdot(x_ref[...], w_ref[...])


def _linear(x, w, tm=512):
    T, K = x.shape
    N = w.shape[1]
    tm = _row_tile(T, tm)
    return pl.pallas_call(
        _linear_kernel,
        out_shape=jax.ShapeDtypeStruct((T, N), F32),
        grid=(T // tm,),
        in_specs=[pl.BlockSpec((tm, K), lambda i: (i, 0)), _full((K, N))],
        out_specs=pl.BlockSpec((tm, N), lambda i: (i, 0)),
        compiler_params=_params("parallel"),
        name="linear",
    )(x, w)


MXU_DIM = 256


def _w1_relayout_kernel(w_ref, o_ref):
    half = MXU_DIM // 2
    r = lax.broadcasted_iota(jnp.int32, (MXU_DIM, MXU_DIM), 0)
    c = lax.broadcasted_iota(jnp.int32, (MXU_DIM, MXU_DIM), 1)
    src = jnp.where(c < half, 2 * c, 2 * (c - half) + 1)
    perm = (r == src).astype(BF16)
    n_out = o_ref.shape[-1] // 2
    for j in range(w_ref.shape[-1] // MXU_DIM):
        y = jnp.dot(w_ref[0, :, j * MXU_DIM:(j + 1) * MXU_DIM].astype(BF16), perm, preferred_element_type=F32)
        o_ref[0, :, j * half:(j + 1) * half] = y[:, :half].astype(BF16)
        o_ref[0, :, n_out + j * half:n_out + (j + 1) * half] = y[:, half:].astype(BF16)


def _w1_relayout(w1):
    E, D, N = w1.shape
    spec = pl.BlockSpec((1, D, N), lambda e: (e, 0, 0))
    return pl.pallas_call(
        _w1_relayout_kernel,
        out_shape=jax.ShapeDtypeStruct((E, D, N), BF16),
        grid=(E,),
        in_specs=[spec],
        out_specs=spec,
        compiler_params=_params("parallel"),
        name="w1_relayout",
    )(w1)


def _gla_in_kernel(x_ref, w_ref, wg_ref, wg2_ref, bg_ref, q_ref, k_ref, v_ref, r_ref, b_ref,
                   *, tm, seq_len, seq_pad):
    xb = x_ref[...].astype(BF16)
    y = jnp.dot(xb, w_ref[...], preferred_element_type=F32)
    q_ref[...] = y[:, :GLA_QK] * (GLA_DK ** -0.5)
    k_ref[...] = y[:, GLA_QK:2 * GLA_QK]
    v_ref[...] = y[:, 2 * GLA_QK:2 * GLA_QK + GLA_V]
    r_ref[...] = y[:, 2 * GLA_QK + GLA_V:]
    g_lr = jnp.dot(xb, wg_ref[...], preferred_element_type=F32)
    z = _bdot(g_lr, wg2_ref[...]) + bg_ref[...]
    log_a = (jnp.minimum(z, 0.0) - jnp.log(1.0 + jnp.exp(-jnp.abs(z)))) / GLA_GATE_TEMP
    if seq_len < seq_pad:
        pos = (pl.program_id(0) * tm + lax.broadcasted_iota(jnp.int32, (tm, 1), 0)) % seq_pad
        log_a = jnp.where(pos < seq_len, log_a, 0.0)
    row = lax.broadcasted_iota(jnp.int32, (tm, tm), 0)
    col = lax.broadcasted_iota(jnp.int32, (tm, tm), 1)
    tri = ((row // CHUNK == col // CHUNK) & (row >= col)).astype(BF16)
    g1 = log_a.astype(BF16)
    r1 = log_a - g1.astype(F32)
    g2 = r1.astype(BF16)
    g3 = (r1 - g2.astype(F32)).astype(BF16)
    b_ref[...] = (jnp.dot(tri, g1, preferred_element_type=F32) + jnp.dot(tri, g2, preferred_element_type=F32)
                  + jnp.dot(tri, g3, preferred_element_type=F32))


def _gla_in(x, w_main, w_g, w_g2, b_g, seq_len, seq_pad, tm=512):
    T = x.shape[0]
    tm = _row_tile(T, tm)
    assert tm % CHUNK == 0 and seq_pad % CHUNK == 0
    n_main = w_main.shape[1]
    row = lambda n: pl.BlockSpec((tm, n), lambda i: (i, 0))
    return pl.pallas_call(
        functools.partial(_gla_in_kernel, tm=tm, seq_len=seq_len, seq_pad=seq_pad),
        out_shape=(jax.ShapeDtypeStruct((T, GLA_QK), F32), jax.ShapeDtypeStruct((T, GLA_QK), F32),
                   jax.ShapeDtypeStruct((T, GLA_V), F32), jax.ShapeDtypeStruct((T, GLA_V), F32),
                   jax.ShapeDtypeStruct((T, GLA_QK), F32)),
        grid=(T // tm,),
        in_specs=[row(D_MODEL), _full((D_MODEL, n_main)), _full((D_MODEL, LANES)),
                  _full((LANES, GLA_QK)), _full((1, GLA_QK))],
        out_specs=(row(GLA_QK), row(GLA_QK), row(GLA_V), row(GLA_V), row(GLA_QK)),
        compiler_params=_params("parallel"),
        name="gla_in",
    )(x, w_main, w_g, w_g2, b_g)


SUBLANES = 8


def _gla_intra(q, k, b):
    n_grp = CHUNK // SUBLANES
    lane = lax.broadcasted_iota(jnp.int32, (SUBLANES, LANES), 1)
    row = lax.broadcasted_iota(jnp.int32, (SUBLANES, LANES), 0)
    grp = lambda t, i: t[i * SUBLANES:(i + 1) * SUBLANES]
    att = [jnp.zeros((SUBLANES, LANES), F32) for _ in range(n_grp)]
    for s in range(CHUNK):
        b_s = b[s:s + 1, :]
        k_s = k[s:s + 1, :]
        for i in range(s // SUBLANES, n_grp):
            e = jnp.exp(grp(b, i) - b_s)
            red = jnp.sum(grp(q, i) * e * k_s, axis=-1, keepdims=True)
            att[i] = jnp.where(lane == s, red, att[i])
    att = [jnp.where(row + i * SUBLANES >= lane, att[i], 0.0) for i in range(n_grp)]
    return jnp.concatenate(att, axis=0)


def _gla_scan_kernel(q_ref, k_ref, v_ref, b_ref, s0_ref, o_ref, s_ref, st_scr, *, n_chunks):
    @pl.when(pl.program_id(1) == 0)
    def _():
        st_scr[...] = s0_ref[0]

    for c in range(n_chunks):
        ts = slice(c * CHUNK, (c + 1) * CHUNK)
        for h in range(GLA_HEADS):
            ks = slice(h * GLA_DK, (h + 1) * GLA_DK)
            vs = slice(h * GLA_DV, (h + 1) * GLA_DV)
            q = q_ref[0, ts, ks]
            k = k_ref[0, ts, ks]
            b = b_ref[0, ts, ks]
            v = v_ref[0, ts, vs]
            att = _gla_intra(q, k, b)
            b_end = b[CHUNK - 1:CHUNK, :]
            st = st_scr[h]
            o_ref[0, ts, vs] = _bdot(att[:, :CHUNK], v) + _bdot_nt(q * jnp.exp(b), st)
            st_scr[h] = st * jnp.exp(b_end) + _bdot_tn(v, k * jnp.exp(b_end - b))

    @pl.when(pl.program_id(1) == pl.num_programs(1) - 1)
    def _():
        s_ref[0] = st_scr[...]


def _gla_scan(q, k, v, la, s0t, tl=128):
    B, Lp, _ = q.shape
    tl = _row_tile(Lp, tl)
    qk_spec = pl.BlockSpec((1, tl, GLA_QK), lambda b, i: (b, i, 0))
    v_spec = pl.BlockSpec((1, tl, GLA_V), lambda b, i: (b, i, 0))
    s_spec = pl.BlockSpec((1, GLA_HEADS, GLA_DV, GLA_DK), lambda b, i: (b, 0, 0, 0))
    return pl.pallas_call(
        functools.partial(_gla_scan_kernel, n_chunks=tl // CHUNK),
        out_shape=(jax.ShapeDtypeStruct((B, Lp, GLA_V), F32),
                   jax.ShapeDtypeStruct((B, GLA_HEADS, GLA_DV, GLA_DK), F32)),
        grid=(B, Lp // tl),
        in_specs=[qk_spec, qk_spec, v_spec, qk_spec, s_spec],
        out_specs=(v_spec, s_spec),
        scratch_shapes=[pltpu.VMEM((GLA_HEADS, GLA_DV, GLA_DK), F32)],
        compiler_params=_params("parallel", "arbitrary"),
        name="gla_scan",
    )(q, k, v, la, s0t)


def _gla_out_kernel(o_ref, r_ref, x_ref, gn_ref, wo_ref, g_ref, b_ref, y_ref):
    parts = []
    for h in range(GLA_HEADS):
        vs = slice(h * GLA_DV, (h + 1) * GLA_DV)
        o = o_ref[:, vs]
        mu = jnp.mean(o, axis=-1, keepdims=True)
        oc = o - mu
        var = jnp.mean(oc * oc, axis=-1, keepdims=True)
        parts.append(oc * lax.rsqrt(var + LN_EPS) * gn_ref[:, vs])
    o = jnp.concatenate(parts, axis=-1)
    r = r_ref[...]
    o = o * (r * jax.nn.sigmoid(r))
    h = _bdot(o, wo_ref[...])
    y_ref[...] = _layer_norm(DEEPNORM_ALPHA * x_ref[...] + h, g_ref[...], b_ref[...])


def _gla_out(o, r, x, gn_g, w_o, ln_g, ln_b, tm=512):
    T = x.shape[0]
    tm = _row_tile(T, tm)
    row = pl.BlockSpec((tm, D_MODEL), lambda i: (i, 0))
    vec = _full((1, D_MODEL))
    return pl.pallas_call(
        _gla_out_kernel,
        out_shape=jax.ShapeDtypeStruct((T, D_MODEL), F32),
        grid=(T // tm,),
        in_specs=[row, row, row, vec, _full((GLA_V, D_MODEL)), vec, vec],
        out_specs=row,
        compiler_params=_params("parallel"),
        name="gla_out",
    )(o, r, x, gn_g, w_o, ln_g, ln_b)


def _swa_kernel(x_ref, k_ref, v_ref, wq_ref, sink_ref, wo_ref, g_ref, b_ref, y_ref,
                *, tl, seq_len, past_valid):
    x = x_ref[0]
    q = (_bdot(x, wq_ref[...]) * (SWA_HEAD_DIM ** -0.5)).astype(BF16)
    kw = tl + WINDOW
    r0 = pl.multiple_of(pl.program_id(1) * tl, tl)
    kt = k_ref[0, pl.ds(r0, kw), :].astype(BF16)
    vt = v_ref[0, pl.ds(r0, kw), :].astype(BF16)
    q_row = lax.broadcasted_iota(jnp.int32, (tl, kw), 0)
    k_col = lax.broadcasted_iota(jnp.int32, (tl, kw), 1)
    band_lo = (q_row // CHUNK) * CHUNK
    pos = r0 + k_col
    ok = (k_col >= band_lo) & (k_col < band_lo + WINDOW + CHUNK) & (pos - WINDOW < seq_len)
    if not past_valid:
        ok = ok & (pos >= WINDOW)
    zeros = jnp.zeros((kw, SWA_HEAD_DIM), BF16)
    pair_out = []
    for g in range(SWA_KV_HEADS):
        kg = kt[:, g * SWA_HEAD_DIM:(g + 1) * SWA_HEAD_DIM]
        vg = vt[:, g * SWA_HEAD_DIM:(g + 1) * SWA_HEAD_DIM]
        k_pad = (jnp.concatenate([kg, zeros], axis=1), jnp.concatenate([zeros, kg], axis=1))
        v_pad = (jnp.concatenate([vg, zeros], axis=1), jnp.concatenate([zeros, vg], axis=1))
        for pair in range(g * SWA_GROUP // 2, (g + 1) * SWA_GROUP // 2):
            qp = q[:, pair * LANES:(pair + 1) * LANES]
            sinks = [sink_ref[2 * pair + j:2 * pair + j + 1, :] for j in range(2)]
            s = [jnp.where(ok, _bdot_nt(qp, k_pad[j]), NEG_INF) for j in range(2)]
            m = [jnp.maximum(jnp.max(s[j], axis=-1, keepdims=True), sinks[j]) for j in range(2)]
            e = [jnp.exp(s[j] - m[j]) for j in range(2)]
            o = [_bdot(e[j], v_pad[j]) for j in range(2)]
            inv = [1.0 / (jnp.sum(e[j], axis=-1, keepdims=True) + jnp.exp(sinks[j] - m[j])) for j in range(2)]
            pair_out.append(o[0] * inv[0] + o[1] * inv[1])
    h_out = _bdot(jnp.concatenate(pair_out, axis=1), wo_ref[...])
    y_ref[0] = _layer_norm(DEEPNORM_ALPHA * x + h_out, g_ref[...], b_ref[...])


def _swa(x, k_all, v_all, w_q, sinks, w_o, ln_g, ln_b, seq_len, past_valid, tl=256):
    B, Lp, _ = x.shape
    tl = _row_tile(Lp, tl)
    x_spec = pl.BlockSpec((1, tl, D_MODEL), lambda b, i: (b, i, 0))
    kv_spec = pl.BlockSpec((1, WINDOW + Lp, SWA_KV), lambda b, i: (b, 0, 0))
    vec = _full((1, D_MODEL))
    return pl.pallas_call(
        functools.partial(_swa_kernel, tl=tl, seq_len=seq_len, past_valid=past_valid),
        out_shape=jax.ShapeDtypeStruct((B, Lp, D_MODEL), F32),
        grid=(B, Lp // tl),
        in_specs=[x_spec, kv_spec, kv_spec, _full((D_MODEL, D_MODEL)), _full((SWA_Q_HEADS, 1)),
                  _full((D_MODEL, D_MODEL)), vec, vec],
        out_specs=x_spec,
        compiler_params=_params("parallel", "arbitrary"),
        name="swa",
    )(x, k_all, v_all, w_q, sinks, w_o, ln_g, ln_b)


def _mem_kernel(x_ref, mk_ref, mv_ref, wq_ref, wo_ref, g_ref, b_ref, y_ref):
    x = x_ref[0]
    q = _bdot(x, wq_ref[...]) * (MEM_HEAD_DIM ** -0.5)
    parts = []
    for h in range(MEM_HEADS):
        hs = slice(h * MEM_HEAD_DIM, (h + 1) * MEM_HEAD_DIM)
        s = _bdot_nt(q[:, hs], mk_ref[0, :, hs])
        m = jnp.max(s, axis=-1, keepdims=True)
        e = jnp.exp(s - m)
        parts.append(_bdot(e, mv_ref[0, :, hs]) * (1.0 / jnp.sum(e, axis=-1, keepdims=True)))
    o = jnp.concatenate(parts, axis=-1)
    h_out = _bdot(o, wo_ref[...])
    y_ref[0] = _layer_norm(DEEPNORM_ALPHA * x + h_out, g_ref[...], b_ref[...])


def _mem_xattn(x, mk, mv, w_q, w_o, ln_g, ln_b, tl=512):
    B, L, _ = x.shape
    tl = _row_tile(L, tl)
    x_spec = pl.BlockSpec((1, tl, D_MODEL), lambda b, i: (b, i, 0))
    m_spec = pl.BlockSpec((1, N_MEM, D_MODEL), lambda b, i: (b, 0, 0))
    vec = _full((1, D_MODEL))
    return pl.pallas_call(
        _mem_kernel,
        out_shape=jax.ShapeDtypeStruct((B, L, D_MODEL), F32),
        grid=(B, L // tl),
        in_specs=[x_spec, m_spec, m_spec, _full((D_MODEL, D_MODEL)), _full((D_MODEL, D_MODEL)), vec, vec],
        out_specs=x_spec,
        compiler_params=_params("parallel", "arbitrary"),
        name="mem_xattn",
    )(x, mk, mv, w_q, w_o, ln_g, ln_b)


def _router_kernel(x_ref, wr_ref, br_ref, idx_ref, gate_ref, pos_ref, cnt_ref, cnt_scr, *, tm):
    @pl.when(pl.program_id(0) == 0)
    def _():
        cnt_scr[...] = jnp.zeros_like(cnt_scr)

    logits = _bdot(x_ref[...], wr_ref[...]) + br_ref[...]
    lane = lax.broadcasted_iota(jnp.int32, (tm, LANES), 1)
    sel = jnp.zeros((tm, LANES), F32)
    idx_out = jnp.zeros((tm, LANES), jnp.int32)
    val_out = jnp.zeros((tm, LANES), F32)
    l = logits
    idxs = []
    for j in range(TOP_K):
        m = jnp.max(l, axis=-1, keepdims=True)
        idx = jnp.min(jnp.where(l == m, lane, LANES), axis=-1, keepdims=True)
        hit = lane == idx
        sel = jnp.where(hit, 1.0, sel)
        l = jnp.where(hit, -jnp.inf, l)
        idx_out = jnp.where(lane == j, idx, idx_out)
        val_out = jnp.where(lane == j, m, val_out)
        idxs.append(idx)
    e = jnp.where(lane < TOP_K, jnp.exp(val_out - val_out[:, 0:1]), 0.0)
    gates = e / jnp.sum(e, axis=-1, keepdims=True)
    r = lax.broadcasted_iota(jnp.int32, (tm, tm), 0)
    c = lax.broadcasted_iota(jnp.int32, (tm, tm), 1)
    before = jnp.dot((r > c).astype(BF16), sel.astype(BF16), preferred_element_type=F32) + cnt_scr[...]
    pos_out = jnp.zeros((tm, LANES), F32)
    for j in range(TOP_K):
        pj = jnp.sum(jnp.where(lane == idxs[j], before, 0.0), axis=-1, keepdims=True)
        pos_out = jnp.where(lane == j, pj, pos_out)
    cnt_scr[...] = cnt_scr[...] + jnp.sum(sel, axis=0, keepdims=True)
    idx_ref[...] = idx_out[:, :TOP_K]
    gate_ref[...] = gates[:, :TOP_K]
    pos_ref[...] = pos_out[:, :TOP_K].astype(jnp.int32)
    cnt_ref[...] = cnt_scr[...].astype(jnp.int32)


def _router(x, w_r, b_r, tm=512):
    T = x.shape[0]
    tm = _row_tile(T, tm)
    small = pl.BlockSpec((tm, TOP_K), lambda i: (i, 0))
    return pl.pallas_call(
        functools.partial(_router_kernel, tm=tm),
        out_shape=(jax.ShapeDtypeStruct((T, TOP_K), jnp.int32), jax.ShapeDtypeStruct((T, TOP_K), F32),
                   jax.ShapeDtypeStruct((T, TOP_K), jnp.int32), jax.ShapeDtypeStruct((1, LANES), jnp.int32)),
        grid=(T // tm,),
        in_specs=[pl.BlockSpec((tm, D_MODEL), lambda i: (i, 0)), _full((D_MODEL, LANES)), _full((1, LANES))],
        out_specs=(small, small, small, _full((1, LANES))),
        scratch_shapes=[pltpu.VMEM((1, LANES), F32)],
        compiler_params=_params("arbitrary"),
        name="moe_router",
    )(x, w_r, b_r)


ISSUE_UNROLL = 8


def _dispatch_kernel(pad_end_ref, dest_ref, x_ref, xs_ref, rows_scr, zero_scr, sem, zsem, *, tm, blk):
    @pl.when(pl.program_id(0) == 0)
    def _():
        zero_scr[...] = jnp.zeros_like(zero_scr)

        def fill(e):
            start = pl.multiple_of((pad_end_ref[e] - blk) * ROW_TILES, blk * ROW_TILES)
            return pltpu.make_async_copy(zero_scr, xs_ref.at[pl.ds(start, blk * ROW_TILES), :], zsem)

        def has_block(e):
            return pad_end_ref[e] > (pad_end_ref[e - 1] if e else 0)

        for e in range(N_EXPERTS):
            pl.when(has_block(e))(lambda e=e: fill(e).start())
        for e in range(N_EXPERTS):
            pl.when(has_block(e))(lambda e=e: fill(e).wait())

    _rows_to_tiles(rows_scr, x_ref[...])

    def tile(ref, row):
        return ref.at[pl.ds(pl.multiple_of(row * ROW_TILES, ROW_TILES), ROW_TILES), :]

    def issue(g, carry):
        base = pl.multiple_of(g * (ISSUE_UNROLL // TOP_K), ISSUE_UNROLL // TOP_K)
        for u in range(ISSUE_UNROLL // TOP_K):
            r = base + u
            for j in range(TOP_K):
                pltpu.make_async_copy(tile(rows_scr, r), tile(xs_ref, dest_ref[0, j, r]), sem).start(priority=j % 2)
        return carry

    lax.fori_loop(0, tm * TOP_K // ISSUE_UNROLL, issue, 0)
    for j in range(TOP_K):
        pltpu.make_async_copy(rows_scr, xs_ref.at[pl.ds(0, tm * ROW_TILES), :], sem).wait()


def _dispatch(x, dest_blocks, pad_end, n_slots, tm, blk):
    T = x.shape[0]
    return pl.pallas_call(
        functools.partial(_dispatch_kernel, tm=tm, blk=blk),
        out_shape=jax.ShapeDtypeStruct((n_slots * ROW_TILES, LANES), F32),
        grid_spec=pltpu.PrefetchScalarGridSpec(
            num_scalar_prefetch=1,
            grid=(T // tm,),
            in_specs=[pl.BlockSpec((1, TOP_K, tm), lambda i, pe: (i, 0, 0), memory_space=pltpu.SMEM),
                      pl.BlockSpec((tm, D_MODEL), lambda i, pe: (i, 0))],
            out_specs=pl.BlockSpec(memory_space=pl.ANY),
            scratch_shapes=[pltpu.VMEM((tm * ROW_TILES, LANES), F32), pltpu.VMEM((blk * ROW_TILES, LANES), F32),
                            pltpu.SemaphoreType.DMA, pltpu.SemaphoreType.DMA]),
        compiler_params=_params("arbitrary"),
        name="moe_dispatch",
    )(pad_end, dest_blocks, x)


def _collect_combine_kernel(idx_ref, idx_next_ref, src_ref, gate_ref, x_ref, g_ref, b_ref, y_ref, buf, sem, *, tm):
    i = pl.program_id(0)
    slot = i % 2

    def tile(ref, row):
        return ref.at[pl.ds(pl.multiple_of(row * ROW_TILES, ROW_TILES), ROW_TILES), :]

    def request(ids, s):
        def body(g, carry):
            base = pl.multiple_of(g * (ISSUE_UNROLL // TOP_K), ISSUE_UNROLL // TOP_K)
            for u in range(ISSUE_UNROLL // TOP_K):
                r = base + u
                for j in range(TOP_K):
                    pltpu.make_async_copy(tile(src_ref, ids[0, j, r]), tile(buf.at[s, j], r),
                                          sem.at[s]).start(priority=j % 2)
            return carry
        lax.fori_loop(0, tm * TOP_K // ISSUE_UNROLL, body, 0)

    @pl.when(i == 0)
    def _():
        request(idx_ref, 0)

    @pl.when(i + 1 < pl.num_programs(0))
    def _():
        request(idx_next_ref, 1 - slot)

    for j in range(TOP_K):
        pltpu.make_async_copy(src_ref.at[pl.ds(0, tm * ROW_TILES), :], buf.at[slot, j], sem.at[slot]).wait()
    gates = gate_ref[...]
    acc = DEEPNORM_ALPHA * x_ref[...]
    for j in range(TOP_K):
        acc = acc + gates[:, j:j + 1] * _tiles_to_rows(buf.at[slot, j], tm)
    y_ref[...] = _layer_norm(acc, g_ref[...], b_ref[...])


def _collect_combine(src, idx_blocks, gates, x, ln_g, ln_b, tm):
    T = x.shape[0]
    nt = T // tm
    row = pl.BlockSpec((tm, D_MODEL), lambda i: (i, 0))
    vec = _full((1, D_MODEL))
    return pl.pallas_call(
        functools.partial(_collect_combine_kernel, tm=tm),
        out_shape=jax.ShapeDtypeStruct((T, D_MODEL), F32),
        grid=(nt,),
        in_specs=[pl.BlockSpec((1, TOP_K, tm), lambda i: (i, 0, 0), memory_space=pltpu.SMEM),
                  pl.BlockSpec((1, TOP_K, tm), lambda i: (jnp.minimum(i + 1, nt - 1), 0, 0), memory_space=pltpu.SMEM),
                  pl.BlockSpec(memory_space=pl.ANY),
                  pl.BlockSpec((tm, TOP_K), lambda i: (i, 0)), row, vec, vec],
        out_specs=row,
        scratch_shapes=[pltpu.VMEM((2, TOP_K, tm * ROW_TILES, LANES), F32), pltpu.SemaphoreType.DMA((2,))],
        compiler_params=_params("arbitrary"),
        name="moe_collect_combine",
    )(idx_blocks, idx_blocks, src, gates, x, ln_g, ln_b)


def _expert_kernel(be_ref, nb_ref, xs_ref, w1_ref, b1_ref, w2_ref, b2_ref, o_ref, *, blk):
    @pl.when(pl.program_id(0) < nb_ref[0])
    def _():
        h = _bdot(_tiles_to_rows(xs_ref, blk), w1_ref[0]) + b1_ref[0]
        glu = jnp.minimum(h[:, :D_EXPERT], SWIGLU_LIMIT)
        lin = jnp.clip(h[:, D_EXPERT:], -SWIGLU_LIMIT, SWIGLU_LIMIT)
        a = glu * jax.nn.sigmoid(SWIGLU_ALPHA * glu) * (lin + 1.0)
        _rows_to_tiles(o_ref, _bdot(a, w2_ref[0]) + b2_ref[0])


def _experts(xs, block_e, n_used, w1, b1, w2, b2, blk):
    P = xs.shape[0] // ROW_TILES
    nb = P // blk

    def blk_map(i, be, nu):
        return (jnp.minimum(i, nu[0] - 1), 0)

    def w_map(i, be, nu):
        return (be[jnp.minimum(i, nu[0] - 1)], 0, 0)

    return pl.pallas_call(
        functools.partial(_expert_kernel, blk=blk),
        out_shape=jax.ShapeDtypeStruct((P * ROW_TILES, LANES), F32),
        grid_spec=pltpu.PrefetchScalarGridSpec(
            num_scalar_prefetch=2,
            grid=(nb,),
            in_specs=[pl.BlockSpec((blk * ROW_TILES, LANES), blk_map),
                      pl.BlockSpec((1, D_MODEL, 2 * D_EXPERT), w_map),
                      pl.BlockSpec((1, 1, 2 * D_EXPERT), w_map),
                      pl.BlockSpec((1, D_EXPERT, D_MODEL), w_map),
                      pl.BlockSpec((1, 1, D_MODEL), w_map)],
            out_specs=pl.BlockSpec((blk * ROW_TILES, LANES), blk_map)),
        compiler_params=_params("arbitrary"),
        name="moe_experts",
    )(block_e, n_used, xs, w1, b1, w2, b2)


def _moe(x, w_r, b_r, w1, b1, w2, b2, ln_g, ln_b):
    T = x.shape[0]
    blk = 512 if T >= 4096 else 128
    tm = _row_tile(T, 512)
    n_assign = T * TOP_K
    n_blocks = -(-(n_assign + N_EXPERTS * (blk - 1)) // blk)
    n_slots = n_blocks * blk

    top_i, gates, pos, counts = _router(x, w_r, b_r)
    counts = counts[0, :N_EXPERTS]
    padded = (counts + blk - 1) // blk * blk
    pad_end = jnp.cumsum(padded)
    pad_start = pad_end - padded
    dest = pad_start[top_i] + pos
    block_start = jnp.arange(n_blocks, dtype=jnp.int32) * blk
    block_e = jnp.minimum(jnp.sum(pad_end[None, :] <= block_start[:, None], axis=1), N_EXPERTS - 1).astype(jnp.int32)
    n_used = (pad_end[-1:] // blk).astype(jnp.int32)
    dest_km = dest.T
    dest_blocks = dest_km.reshape(TOP_K, T // tm, tm).transpose(1, 0, 2)

    xs = _dispatch(x, dest_blocks, pad_end.astype(jnp.int32), n_slots, tm, blk)
    out = _experts(xs, block_e, n_used, w1, b1, w2, b2, blk)
    return _collect_combine(out, dest_blocks, gates, x, ln_g, ln_b, tm)


def _pad_rows(t, n):
    return t if n == 0 else jnp.pad(t, ((0, 0), (0, n), (0, 0)))


def _trunk(x, gla_s0, past_k, past_v, past_valid, mem_k, mem_v, w):
    B, L, _ = x.shape
    Lp = -(-L // CHUNK) * CHUNK
    T = B * L
    vec = lambda a: a.reshape(1, -1)
    gla_states = []
    k_all = v_all = new_k = new_v = None
    for layer in range(DEPTH):
        lg, lb = w['ln_g'][layer], w['ln_b'][layer]
        xt = x.reshape(T, D_MODEL)
        if layer < N_A:
            i = layer
            xt = _pad_rows(x, Lp - L).reshape(B * Lp, D_MODEL)
            q, k, v, r, b = _gla_in(xt, w['gla_w_main'][i], w['gla_w_g'][i], w['gla_w_g2'][i],
                                    vec(w['gla_b_g'][i]), L, Lp)
            seq = lambda t: t.reshape(B, Lp, -1)
            s0t = jnp.swapaxes(gla_s0[i], -1, -2)
            o, st = _gla_scan(seq(q), seq(k), seq(v), seq(b), s0t)
            gla_states.append(jnp.swapaxes(st, -1, -2))
            xt = _gla_out(o.reshape(B * Lp, GLA_V), r, xt, vec(w['gla_gn_g'][i]), w['gla_w_o'][i],
                          vec(lg[0]), vec(lb[0]))
            x = xt.reshape(B, Lp, D_MODEL)[:, :L]
        else:
            j = layer - N_A
            if j == 0:
                kv = _linear(xt, w['kv_w']).reshape(B, L, 2 * SWA_KV)
                k_full = jnp.concatenate([past_k, kv[..., :SWA_KV]], axis=1)
                v_full = jnp.concatenate([past_v, kv[..., SWA_KV:]], axis=1)
                new_k, new_v = k_full[:, -WINDOW:], v_full[:, -WINDOW:]
                k_all, v_all = _pad_rows(k_full, Lp - L), _pad_rows(v_full, Lp - L)
            xp = _swa(_pad_rows(x, Lp - L), k_all, v_all, w['swa_w_q'][j], w['swa_sinks'][j].reshape(-1, 1),
                      w['swa_w_o'][j], vec(lg[0]), vec(lb[0]), L, past_valid)
            x = xp[:, :L]
        x = _mem_xattn(x, mem_k[layer], mem_v[layer], w['mem_w_q'][layer], w['mem_w_o'][layer],
                       vec(lg[1]), vec(lb[1]))
        xt = _moe(x.reshape(T, D_MODEL), w['moe_w_r'][layer], w['moe_b_r'][layer], w['moe_w1'][layer],
                  w['moe_b1'][layer], w['moe_w2'][layer], w['moe_b2'][layer], vec(lg[2]), vec(lb[2]))
        x = xt.reshape(B, L, D_MODEL)
    return x, jnp.stack(gla_states), new_k, new_v


def _prep_weights(gla_w_in, gla_w_g2, gla_b_g, gla_gn_g, gla_w_o, kv_w, swa_w_q, swa_sinks, swa_w_o,
                  mem_w_q, mem_w_kv, mem_w_o, moe_w_r, moe_b_r, moe_w1, moe_b1, moe_w2, moe_b2, ln_g, ln_b):
    n_main = 2 * GLA_QK + 2 * GLA_V
    pad_c = lambda a, n: jnp.pad(a, [(0, 0)] * (a.ndim - 1) + [(0, n - a.shape[-1])])
    w = dict(
        gla_w_main=gla_w_in[:, :, :n_main].astype(BF16),
        gla_w_g=pad_c(gla_w_in[:, :, n_main:], LANES).astype(BF16),
        gla_w_g2=jnp.pad(gla_w_g2, ((0, 0), (0, LANES - GLA_GATE_RANK), (0, 0))).astype(BF16),
        gla_b_g=gla_b_g, gla_gn_g=gla_gn_g.reshape(N_A, GLA_V), gla_w_o=gla_w_o.astype(BF16),
        kv_w=kv_w.astype(BF16), swa_w_q=swa_w_q.astype(BF16), swa_sinks=swa_sinks, swa_w_o=swa_w_o.astype(BF16),
        mem_w_q=mem_w_q.astype(BF16), mem_w_kv=mem_w_kv.astype(BF16), mem_w_o=mem_w_o.astype(BF16),
        moe_w_r=pad_c(moe_w_r, LANES).astype(BF16),
        moe_b_r=jnp.pad(moe_b_r, ((0, 0), (0, LANES - N_EXPERTS)), constant_values=-jnp.inf).reshape(DEPTH, 1, LANES),
        moe_w1=_w1_relayout(moe_w1.reshape(DEPTH * N_EXPERTS, D_MODEL, 2 * D_EXPERT)).reshape(moe_w1.shape),
        moe_b1=jnp.concatenate([moe_b1[..., 0::2], moe_b1[..., 1::2]], axis=-1).reshape(DEPTH, N_EXPERTS, 1, -1),
        moe_w2=moe_w2.astype(BF16), moe_b2=moe_b2.reshape(DEPTH, N_EXPERTS, 1, D_MODEL),
        ln_g=ln_g, ln_b=ln_b)
    return w


def kernel(x_prompt, x_sample, state_gla, cache_swa_k, cache_swa_v, cache_mem_k, cache_mem_v, mem_prompt, gla_w_in, gla_w_g2, gla_b_g, gla_gn_g, gla_w_o, kv_w, swa_w_q, swa_sinks, swa_w_o, mem_w_q, mem_w_kv, mem_w_o, moe_w_r, moe_b_r, moe_w1, moe_b1, moe_w2, moe_b2, ln_g, ln_b):
    w = _prep_weights(gla_w_in, gla_w_g2, gla_b_g, gla_gn_g, gla_w_o, kv_w, swa_w_q, swa_sinks, swa_w_o,
                      mem_w_q, mem_w_kv, mem_w_o, moe_w_r, moe_b_r, moe_w1, moe_b1, moe_w2, moe_b2, ln_g, ln_b)
    bp = x_prompt.shape[0]
    bs = x_sample.shape[0]
    mem_flat = mem_prompt.reshape(bp * N_MEM, D_MODEL)
    mem_kv = jnp.stack([_linear(mem_flat, w['mem_w_kv'][l]) for l in range(DEPTH)])
    mem_k_p = mem_kv[..., :D_MODEL].reshape(DEPTH, bp, N_MEM, D_MODEL)
    mem_v_p = mem_kv[..., D_MODEL:].reshape(DEPTH, bp, N_MEM, D_MODEL)
    gla0 = jnp.zeros((N_A, bp, GLA_HEADS, GLA_DK, GLA_DV), F32)
    zero_win = jnp.zeros((bp, WINDOW, SWA_KV), F32)
    y_p, gla_p, k_p, v_p = _trunk(x_prompt, gla0, zero_win, zero_win, False, mem_k_p, mem_v_p, w)
    y_s, gla_s, k_s, v_s = _trunk(x_sample, state_gla, cache_swa_k.reshape(bs, WINDOW, SWA_KV),
                                  cache_swa_v.reshape(bs, WINDOW, SWA_KV), True,
                                  cache_mem_k.reshape(DEPTH, bs, N_MEM, D_MODEL),
                                  cache_mem_v.reshape(DEPTH, bs, N_MEM, D_MODEL), w)
    heads4 = lambda t: t.reshape(t.shape[0], WINDOW, SWA_KV_HEADS, SWA_HEAD_DIM)
    mem5 = lambda t: t.reshape(DEPTH, bp, N_MEM, MEM_HEADS, MEM_HEAD_DIM)
    return (y_p, y_s, gla_p, gla_s, heads4(k_p), heads4(v_p), heads4(k_s), heads4(v_s), mem5(mem_k_p), mem5(mem_v_p))
```

```python
import functools

import jax
import jax.numpy as jnp
from jax import lax
from jax.experimental import pallas as pl
from jax.experimental.pallas import tpu as pltpu

F32 = jnp.float32
BF16 = jnp.bfloat16

D_MODEL = 1024
DEPTH = 4
CHUNK = 64
N_A = DEPTH // 2
GLA_HEADS = 4
GLA_DK = D_MODEL // (2 * GLA_HEADS)
GLA_DV = D_MODEL // GLA_HEADS
GLA_QK = GLA_HEADS * GLA_DK
GLA_V = GLA_HEADS * GLA_DV
GLA_GATE_RANK = 16
GLA_GATE_TEMP = 16.0
SWA_HEAD_DIM = 64
SWA_Q_HEADS = D_MODEL // SWA_HEAD_DIM
SWA_KV_HEADS = 4
SWA_GROUP = SWA_Q_HEADS // SWA_KV_HEADS
SWA_KV = SWA_KV_HEADS * SWA_HEAD_DIM
WINDOW = 128
WINDOW_CHUNKS = WINDOW // CHUNK
N_MEM = 256
MEM_HEADS = 4
MEM_HEAD_DIM = D_MODEL // MEM_HEADS
N_EXPERTS = 32
TOP_K = 4
D_EXPERT = D_MODEL
SWIGLU_ALPHA = 1.702
SWIGLU_LIMIT = 7.0
DEEPNORM_ALPHA = (2 * DEPTH) ** 0.25
LN_EPS = 1e-5
NEG_INF = -1e30

LANES = 128
VMEM_LIMIT = 56 * 1024 * 1024


def _params(*sem):
    return pltpu.CompilerParams(dimension_semantics=sem, vmem_limit_bytes=VMEM_LIMIT)


def _row_tile(n_rows, want):
    t = min(want, n_rows)
    while n_rows % t:
        t //= 2
    return t


def _full(shape):
    nd = len(shape)
    return pl.BlockSpec(shape, lambda *_: (0,) * nd)


def _bdot(a, b):
    return jnp.dot(a.astype(BF16), b.astype(BF16), preferred_element_type=F32)


def _bdot_nt(a, b):
    return lax.dot_general(a.astype(BF16), b.astype(BF16), (((1,), (1,)), ((), ())),
                           preferred_element_type=F32)


def _bdot_tn(a, b):
    return lax.dot_general(a.astype(BF16), b.astype(BF16), (((0,), (0,)), ((), ())),
                           preferred_element_type=F32)


def _layer_norm(z, g, b):
    mu = jnp.mean(z, axis=-1, keepdims=True)
    zc = z - mu
    var = jnp.mean(zc * zc, axis=-1, keepdims=True)
    return zc * lax.rsqrt(var + LN_EPS) * g + b


ROW_TILES = D_MODEL // LANES


def _rows_to_tiles(ref, x):
    n = x.shape[0]
    for c in range(ROW_TILES):
        ref[pl.ds(c, n, stride=ROW_TILES), :] = x[:, c * LANES:(c + 1) * LANES]


def _tiles_to_rows(ref, n):
    return jnp.concatenate([ref[pl.ds(c, n, stride=ROW_TILES), :] for c in range(ROW_TILES)], axis=1)


def _linear_kv_kernel(x_ref, w_ref, k_ref, v_ref):
    y = _bdot(x_ref[...], w_ref[0])
    half = y.shape[1] // 2
    k_ref[0] = y[:, :half]
    v_ref[0] = y[:, half:]


def _linear_kv(x, w, tm=512):
    T, K = x.shape
    G, _, N2 = w.shape
    tm = _row_tile(T, tm)
    out = jax.ShapeDtypeStruct((G, T, N2 // 2), F32)
    o_spec = pl.BlockSpec((1, tm, N2 // 2), lambda g, i: (g, i, 0))
    return pl.pallas_call(
        _linear_kv_kernel,
        out_shape=(out, out),
        grid=(G, T // tm),
        in_specs=[pl.BlockSpec((tm, K), lambda g, i: (i, 0)), pl.BlockSpec((1, K, N2), lambda g, i: (g, 0, 0))],
        out_specs=(o_spec, o_spec),
        compiler_params=_params("parallel", "parallel"),
        name="linear_kv",
    )(x, w)


MXU_DIM = 256


def _w1_relayout_kernel(w_ref, o_ref):
    half = MXU_DIM // 2
    r = lax.broadcasted_iota(jnp.int32, (MXU_DIM, MXU_DIM), 0)
    c = lax.broadcasted_iota(jnp.int32, (MXU_DIM, MXU_DIM), 1)
    src = jnp.where(c < half, 2 * c, 2 * (c - half) + 1)
    perm = (r == src).astype(BF16)
    n_out = o_ref.shape[-1] // 2
    for j in range(w_ref.shape[-1] // MXU_DIM):
        y = jnp.dot(w_ref[0, :, j * MXU_DIM:(j + 1) * MXU_DIM].astype(BF16), perm, preferred_element_type=F32)
        o_ref[0, :, j * half:(j + 1) * half] = y[:, :half].astype(BF16)
        o_ref[0, :, n_out + j * half:n_out + (j + 1) * half] = y[:, half:].astype(BF16)


def _w1_relayout(w1):
    E, D, N = w1.shape
    spec = pl.BlockSpec((1, D, N), lambda e: (e, 0, 0))
    return pl.pallas_call(
        _w1_relayout_kernel,
        out_shape=jax.ShapeDtypeStruct((E, D, N), BF16),
        grid=(E,),
        in_specs=[spec],
        out_specs=spec,
        compiler_params=_params("parallel"),
        name="w1_relayout",
    )(w1)


def _gla_in_kernel(x_ref, w_ref, wg_ref, wg2_ref, bg_ref, q_ref, k_ref, v_ref, r_ref, b_ref,
                   *, tm, seq_len, seq_pad):
    xb = x_ref[...].astype(BF16)
    y = jnp.dot(xb, w_ref[...], preferred_element_type=F32)
    q_ref[...] = y[:, :GLA_QK] * (GLA_DK ** -0.5)
    k_ref[...] = y[:, GLA_QK:2 * GLA_QK]
    v_ref[...] = y[:, 2 * GLA_QK:2 * GLA_QK + GLA_V]
    r_ref[...] = y[:, 2 * GLA_QK + GLA_V:]
    g_lr = jnp.dot(xb, wg_ref[...], preferred_element_type=F32)
    z = _bdot(g_lr, wg2_ref[...]) + bg_ref[...]
    log_a = (jnp.minimum(z, 0.0) - jnp.log(1.0 + jnp.exp(-jnp.abs(z)))) / GLA_GATE_TEMP
    if seq_len < seq_pad:
        pos = (pl.program_id(0) * tm + lax.broadcasted_iota(jnp.int32, (tm, 1), 0)) % seq_pad
        log_a = jnp.where(pos < seq_len, log_a, 0.0)
    row = lax.broadcasted_iota(jnp.int32, (tm, tm), 0)
    col = lax.broadcasted_iota(jnp.int32, (tm, tm), 1)
    tri = ((row // CHUNK == col // CHUNK) & (row >= col)).astype(BF16)
    g1 = log_a.astype(BF16)
    r1 = log_a - g1.astype(F32)
    g2 = r1.astype(BF16)
    g3 = (r1 - g2.astype(F32)).astype(BF16)
    b_ref[...] = (jnp.dot(tri, g1, preferred_element_type=F32) + jnp.dot(tri, g2, preferred_element_type=F32)
                  + jnp.dot(tri, g3, preferred_element_type=F32))


def _gla_in(x, w_main, w_g, w_g2, b_g, seq_len, seq_pad, tm=512):
    T = x.shape[0]
    tm = _row_tile(T, tm)
    assert tm % CHUNK == 0 and seq_pad % CHUNK == 0
    n_main = w_main.shape[1]
    row = lambda n: pl.BlockSpec((tm, n), lambda i: (i, 0))
    return pl.pallas_call(
        functools.partial(_gla_in_kernel, tm=tm, seq_len=seq_len, seq_pad=seq_pad),
        out_shape=(jax.ShapeDtypeStruct((T, GLA_QK), F32), jax.ShapeDtypeStruct((T, GLA_QK), F32),
                   jax.ShapeDtypeStruct((T, GLA_V), F32), jax.ShapeDtypeStruct((T, GLA_V), F32),
                   jax.ShapeDtypeStruct((T, GLA_QK), F32)),
        grid=(T // tm,),
        in_specs=[row(D_MODEL), _full((D_MODEL, n_main)), _full((D_MODEL, LANES)),
                  _full((LANES, GLA_QK)), _full((1, GLA_QK))],
        out_specs=(row(GLA_QK), row(GLA_QK), row(GLA_V), row(GLA_V), row(GLA_QK)),
        compiler_params=_params("parallel"),
        name="gla_in",
    )(x, w_main, w_g, w_g2, b_g)


SUBLANES = 8


def _gla_intra(q, k, b):
    n_grp = CHUNK // SUBLANES
    lane = lax.broadcasted_iota(jnp.int32, (SUBLANES, LANES), 1)
    row = lax.broadcasted_iota(jnp.int32, (SUBLANES, LANES), 0)
    grp = lambda t, i: t[i * SUBLANES:(i + 1) * SUBLANES]
    att = [jnp.zeros((SUBLANES, LANES), F32) for _ in range(n_grp)]
    for s in range(CHUNK):
        b_s = b[s:s + 1, :]
        k_s = k[s:s + 1, :]
        for i in range(s // SUBLANES, n_grp):
            e = jnp.exp(grp(b, i) - b_s)
            red = jnp.sum(grp(q, i) * e * k_s, axis=-1, keepdims=True)
            att[i] = jnp.where(lane == s, red, att[i])
    att = [jnp.where(row + i * SUBLANES >= lane, att[i], 0.0) for i in range(n_grp)]
    return jnp.concatenate(att, axis=0)


def _gla_scan_kernel(q_ref, k_ref, v_ref, b_ref, s0_ref, o_ref, s_ref, st_scr, *, n_chunks):
    @pl.when(pl.program_id(1) == 0)
    def _():
        st_scr[...] = s0_ref[0]

    for c in range(n_chunks):
        ts = slice(c * CHUNK, (c + 1) * CHUNK)
        for h in range(GLA_HEADS):
            ks = slice(h * GLA_DK, (h + 1) * GLA_DK)
            vs = slice(h * GLA_DV, (h + 1) * GLA_DV)
            q = q_ref[0, ts, ks]
            k = k_ref[0, ts, ks]
            b = b_ref[0, ts, ks]
            v = v_ref[0, ts, vs]
            att = _gla_intra(q, k, b)
            b_end = b[CHUNK - 1:CHUNK, :]
            st = st_scr[h]
            o_ref[0, ts, vs] = _bdot(att[:, :CHUNK], v) + _bdot_nt(q * jnp.exp(b), st)
            st_scr[h] = st * jnp.exp(b_end) + _bdot_tn(v, k * jnp.exp(b_end - b))

    @pl.when(pl.program_id(1) == pl.num_programs(1) - 1)
    def _():
        s_ref[0] = st_scr[...]


def _gla_scan(q, k, v, la, s0t, tl=128):
    B, Lp, _ = q.shape
    tl = _row_tile(Lp, tl)
    qk_spec = pl.BlockSpec((1, tl, GLA_QK), lambda b, i: (b, i, 0))
    v_spec = pl.BlockSpec((1, tl, GLA_V), lambda b, i: (b, i, 0))
    s_spec = pl.BlockSpec((1, GLA_HEADS, GLA_DV, GLA_DK), lambda b, i: (b, 0, 0, 0))
    return pl.pallas_call(
        functools.partial(_gla_scan_kernel, n_chunks=tl // CHUNK),
        out_shape=(jax.ShapeDtypeStruct((B, Lp, GLA_V), F32),
                   jax.ShapeDtypeStruct((B, GLA_HEADS, GLA_DV, GLA_DK), F32)),
        grid=(B, Lp // tl),
        in_specs=[qk_spec, qk_spec, v_spec, qk_spec, s_spec],
        out_specs=(v_spec, s_spec),
        scratch_shapes=[pltpu.VMEM((GLA_HEADS, GLA_DV, GLA_DK), F32)],
        compiler_params=_params("parallel", "arbitrary"),
        name="gla_scan",
    )(q, k, v, la, s0t)


def _gla_out_kernel(o_ref, r_ref, x_ref, gn_ref, wo_ref, g_ref, b_ref, y_ref):
    parts = []
    for h in range(GLA_HEADS):
        vs = slice(h * GLA_DV, (h + 1) * GLA_DV)
        o = o_ref[:, vs]
        mu = jnp.mean(o, axis=-1, keepdims=True)
        oc = o - mu
        var = jnp.mean(oc * oc, axis=-1, keepdims=True)
        parts.append(oc * lax.rsqrt(var + LN_EPS) * gn_ref[:, vs])
    o = jnp.concatenate(parts, axis=-1)
    r = r_ref[...]
    o = o * (r * jax.nn.sigmoid(r))
    h = _bdot(o, wo_ref[...])
    y_ref[...] = _layer_norm(DEEPNORM_ALPHA * x_ref[...] + h, g_ref[...], b_ref[...])


def _gla_out(o, r, x, gn_g, w_o, ln_g, ln_b, tm=512):
    T = x.shape[0]
    tm = _row_tile(T, tm)
    row = pl.BlockSpec((tm, D_MODEL), lambda i: (i, 0))
    vec = _full((1, D_MODEL))
    return pl.pallas_call(
        _gla_out_kernel,
        out_shape=jax.ShapeDtypeStruct((T, D_MODEL), F32),
        grid=(T // tm,),
        in_specs=[row, row, row, vec, _full((GLA_V, D_MODEL)), vec, vec],
        out_specs=row,
        compiler_params=_params("parallel"),
        name="gla_out",
    )(o, r, x, gn_g, w_o, ln_g, ln_b)


def _swa_kernel(x_ref, k_ref, v_ref, wq_ref, sink_ref, wo_ref, g_ref, b_ref, y_ref,
                *, tl, seq_len, past_valid):
    x = x_ref[0]
    q = (_bdot(x, wq_ref[...]) * (SWA_HEAD_DIM ** -0.5)).astype(BF16)
    kw = tl + WINDOW
    r0 = pl.multiple_of(pl.program_id(1) * tl, tl)
    kt = k_ref[0, pl.ds(r0, kw), :].astype(BF16)
    vt = v_ref[0, pl.ds(r0, kw), :].astype(BF16)
    q_row = lax.broadcasted_iota(jnp.int32, (tl, kw), 0)
    k_col = lax.broadcasted_iota(jnp.int32, (tl, kw), 1)
    band_lo = (q_row // CHUNK) * CHUNK
    pos = r0 + k_col
    ok = (k_col >= band_lo) & (k_col < band_lo + WINDOW + CHUNK) & (pos - WINDOW < seq_len)
    if not past_valid:
        ok = ok & (pos >= WINDOW)
    zeros = jnp.zeros((kw, SWA_HEAD_DIM), BF16)
    pair_out = []
    for g in range(SWA_KV_HEADS):
        kg = kt[:, g * SWA_HEAD_DIM:(g + 1) * SWA_HEAD_DIM]
        vg = vt[:, g * SWA_HEAD_DIM:(g + 1) * SWA_HEAD_DIM]
        k_pad = (jnp.concatenate([kg, zeros], axis=1), jnp.concatenate([zeros, kg], axis=1))
        v_pad = (jnp.concatenate([vg, zeros], axis=1), jnp.concatenate([zeros, vg], axis=1))
        for pair in range(g * SWA_GROUP // 2, (g + 1) * SWA_GROUP // 2):
            qp = q[:, pair * LANES:(pair + 1) * LANES]
            sinks = [sink_ref[2 * pair + j:2 * pair + j + 1, :] for j in range(2)]
            s = [jnp.where(ok, _bdot_nt(qp, k_pad[j]), NEG_INF) for j in range(2)]
            m = [jnp.maximum(jnp.max(s[j], axis=-1, keepdims=True), sinks[j]) for j in range(2)]
            e = [jnp.exp(s[j] - m[j]) for j in range(2)]
            o = [_bdot(e[j], v_pad[j]) for j in range(2)]
            inv = [1.0 / (jnp.sum(e[j], axis=-1, keepdims=True) + jnp.exp(sinks[j] - m[j])) for j in range(2)]
            pair_out.append(o[0] * inv[0] + o[1] * inv[1])
    h_out = _bdot(jnp.concatenate(pair_out, axis=1), wo_ref[...])
    y_ref[0] = _layer_norm(DEEPNORM_ALPHA * x + h_out, g_ref[...], b_ref[...])


def _swa(x, k_all, v_all, w_q, sinks, w_o, ln_g, ln_b, seq_len, past_valid, tl=256):
    B, Lp, _ = x.shape
    tl = _row_tile(Lp, tl)
    x_spec = pl.BlockSpec((1, tl, D_MODEL), lambda b, i: (b, i, 0))
    kv_spec = pl.BlockSpec((1, WINDOW + Lp, SWA_KV), lambda b, i: (b, 0, 0))
    vec = _full((1, D_MODEL))
    return pl.pallas_call(
        functools.partial(_swa_kernel, tl=tl, seq_len=seq_len, past_valid=past_valid),
        out_shape=jax.ShapeDtypeStruct((B, Lp, D_MODEL), F32),
        grid=(B, Lp // tl),
        in_specs=[x_spec, kv_spec, kv_spec, _full((D_MODEL, D_MODEL)), _full((SWA_Q_HEADS, 1)),
                  _full((D_MODEL, D_MODEL)), vec, vec],
        out_specs=x_spec,
        compiler_params=_params("parallel", "arbitrary"),
        name="swa",
    )(x, k_all, v_all, w_q, sinks, w_o, ln_g, ln_b)


def _mem_kernel(x_ref, mk_ref, mv_ref, wq_ref, wo_ref, g_ref, b_ref, y_ref):
    x = x_ref[0]
    q = _bdot(x, wq_ref[...]) * (MEM_HEAD_DIM ** -0.5)
    parts = []
    for h in range(MEM_HEADS):
        hs = slice(h * MEM_HEAD_DIM, (h + 1) * MEM_HEAD_DIM)
        s = _bdot_nt(q[:, hs], mk_ref[0, :, hs])
        m = jnp.max(s, axis=-1, keepdims=True)
        e = jnp.exp(s - m)
        parts.append(_bdot(e, mv_ref[0, :, hs]) * (1.0 / jnp.sum(e, axis=-1, keepdims=True)))
    o = jnp.concatenate(parts, axis=-1)
    h_out = _bdot(o, wo_ref[...])
    y_ref[0] = _layer_norm(DEEPNORM_ALPHA * x + h_out, g_ref[...], b_ref[...])


def _mem_xattn(x, mk, mv, w_q, w_o, ln_g, ln_b, tl=512):
    B, L, _ = x.shape
    tl = _row_tile(L, tl)
    x_spec = pl.BlockSpec((1, tl, D_MODEL), lambda b, i: (b, i, 0))
    m_spec = pl.BlockSpec((1, N_MEM, D_MODEL), lambda b, i: (b, 0, 0))
    vec = _full((1, D_MODEL))
    return pl.pallas_call(
        _mem_kernel,
        out_shape=jax.ShapeDtypeStruct((B, L, D_MODEL), F32),
        grid=(B, L // tl),
        in_specs=[x_spec, m_spec, m_spec, _full((D_MODEL, D_MODEL)), _full((D_MODEL, D_MODEL)), vec, vec],
        out_specs=x_spec,
        compiler_params=_params("parallel", "arbitrary"),
        name="mem_xattn",
    )(x, mk, mv, w_q, w_o, ln_g, ln_b)


def _router_kernel(x_ref, wrt_ref, br_ref, idx_ref, gate_ref, pos_ref, cnt_ref, cnt_scr, *, tm):
    @pl.when(pl.program_id(0) == 0)
    def _():
        cnt_scr[...] = jnp.zeros_like(cnt_scr)

    logits = _bdot_nt(wrt_ref[...], x_ref[...]) + br_ref[...]
    expert = lax.broadcasted_iota(jnp.int32, (N_EXPERTS, tm), 0).astype(F32)
    sel = jnp.zeros((N_EXPERTS, tm), F32)
    l = logits
    vals, hits = [], []
    for j in range(TOP_K):
        m = jnp.max(l, axis=0, keepdims=True)
        idx = jnp.min(jnp.where(l == m, expert, float(N_EXPERTS)), axis=0, keepdims=True)
        hit = expert == idx
        sel = jnp.where(hit, 1.0, sel)
        l = jnp.where(hit, -jnp.inf, l)
        idx_ref[j:j + 1, :] = idx.astype(jnp.int32)
        vals.append(m)
        hits.append(hit)
    e = [jnp.exp(v - vals[0]) for v in vals]
    inv = 1.0 / (e[0] + e[1] + e[2] + e[3])
    r = lax.broadcasted_iota(jnp.int32, (tm, tm), 0)
    c = lax.broadcasted_iota(jnp.int32, (tm, tm), 1)
    before = jnp.dot(sel.astype(BF16), (r < c).astype(BF16), preferred_element_type=F32) + cnt_scr[:, 0:1]
    for j in range(TOP_K):
        gate_ref[j:j + 1, :] = e[j] * inv
        pos_ref[j:j + 1, :] = jnp.sum(jnp.where(hits[j], before, 0.0), axis=0, keepdims=True).astype(jnp.int32)
    cnt_scr[...] = cnt_scr[...] + jnp.sum(sel, axis=1, keepdims=True)
    cnt_ref[...] = cnt_scr[...].astype(jnp.int32)


def _router(x, w_rt, b_r, tm=512):
    T = x.shape[0]
    tm = _row_tile(T, tm)
    small = pl.BlockSpec((TOP_K, tm), lambda i: (0, i))
    return pl.pallas_call(
        functools.partial(_router_kernel, tm=tm),
        out_shape=(jax.ShapeDtypeStruct((TOP_K, T), jnp.int32), jax.ShapeDtypeStruct((TOP_K, T), F32),
                   jax.ShapeDtypeStruct((TOP_K, T), jnp.int32), jax.ShapeDtypeStruct((N_EXPERTS, LANES), jnp.int32)),
        grid=(T // tm,),
        in_specs=[pl.BlockSpec((tm, D_MODEL), lambda i: (i, 0)), _full((N_EXPERTS, D_MODEL)), _full((N_EXPERTS, 1))],
        out_specs=(small, small, small, _full((N_EXPERTS, LANES))),
        scratch_shapes=[pltpu.VMEM((N_EXPERTS, LANES), F32)],
        compiler_params=_params("arbitrary"),
        name="moe_router",
    )(x, w_rt, b_r)


ISSUE_UNROLL = 16


def _dispatch_kernel(pad_end_ref, dest_ref, x_ref, xs_ref, rows_scr, zero_scr, sem, zsem, *, tm, blk):
    @pl.when(pl.program_id(0) == 0)
    def _():
        zero_scr[...] = jnp.zeros_like(zero_scr)

        def fill(e):
            start = pl.multiple_of((pad_end_ref[e] - blk) * ROW_TILES, blk * ROW_TILES)
            return pltpu.make_async_copy(zero_scr, xs_ref.at[pl.ds(start, blk * ROW_TILES), :], zsem)

        def has_block(e):
            return pad_end_ref[e] > (pad_end_ref[e - 1] if e else 0)

        for e in range(N_EXPERTS):
            pl.when(has_block(e))(lambda e=e: fill(e).start())
        for e in range(N_EXPERTS):
            pl.when(has_block(e))(lambda e=e: fill(e).wait())

    _rows_to_tiles(rows_scr, x_ref[...])

    def tile(ref, row):
        return ref.at[pl.ds(pl.multiple_of(row * ROW_TILES, ROW_TILES), ROW_TILES), :]

    def issue(g, carry):
        base = pl.multiple_of(g * (ISSUE_UNROLL // TOP_K), ISSUE_UNROLL // TOP_K)
        for u in range(ISSUE_UNROLL // TOP_K):
            r = base + u
            for j in range(TOP_K):
                pltpu.make_async_copy(tile(rows_scr, r), tile(xs_ref, dest_ref[0, j, r]), sem).start(priority=j % 2)
        return carry

    lax.fori_loop(0, tm * TOP_K // ISSUE_UNROLL, issue, 0)
    for j in range(TOP_K):
        pltpu.make_async_copy(rows_scr, xs_ref.at[pl.ds(0, tm * ROW_TILES), :], sem).wait()


def _dispatch(x, dest_blocks, pad_end, n_slots, tm, blk):
    T = x.shape[0]
    return pl.pallas_call(
        functools.partial(_dispatch_kernel, tm=tm, blk=blk),
        out_shape=jax.ShapeDtypeStruct((n_slots * ROW_TILES, LANES), F32),
        grid_spec=pltpu.PrefetchScalarGridSpec(
            num_scalar_prefetch=1,
            grid=(T // tm,),
            in_specs=[pl.BlockSpec((1, TOP_K, tm), lambda i, pe: (i, 0, 0), memory_space=pltpu.SMEM),
                      pl.BlockSpec((tm, D_MODEL), lambda i, pe: (i, 0))],
            out_specs=pl.BlockSpec(memory_space=pl.ANY),
            scratch_shapes=[pltpu.VMEM((tm * ROW_TILES, LANES), F32), pltpu.VMEM((blk * ROW_TILES, LANES), F32),
                            pltpu.SemaphoreType.DMA, pltpu.SemaphoreType.DMA]),
        compiler_params=_params("arbitrary"),
        name="moe_dispatch",
    )(pad_end, dest_blocks, x)


def _collect_combine_kernel(idx_ref, idx_next_ref, src_ref, gate_ref, x_ref, g_ref, b_ref, y_ref, buf, sem, *, tm):
    i = pl.program_id(0)
    slot = i % 2

    def tile(ref, row):
        return ref.at[pl.ds(pl.multiple_of(row * ROW_TILES, ROW_TILES), ROW_TILES), :]

    def request(ids, s):
        def body(g, carry):
            base = pl.multiple_of(g * (ISSUE_UNROLL // TOP_K), ISSUE_UNROLL // TOP_K)
            for u in range(ISSUE_UNROLL // TOP_K):
                r = base + u
                for j in range(TOP_K):
                    pltpu.make_async_copy(tile(src_ref, ids[0, j, r]), tile(buf.at[s, j], r),
                                          sem.at[s]).start(priority=j % 2)
            return carry
        lax.fori_loop(0, tm * TOP_K // ISSUE_UNROLL, body, 0)

    @pl.when(i == 0)
    def _():
        request(idx_ref, 0)

    @pl.when(i + 1 < pl.num_programs(0))
    def _():
        request(idx_next_ref, 1 - slot)

    for j in range(TOP_K):
        pltpu.make_async_copy(src_ref.at[pl.ds(0, tm * ROW_TILES), :], buf.at[slot, j], sem.at[slot]).wait()
    gates = gate_ref[...]
    acc = DEEPNORM_ALPHA * x_ref[...]
    for j in range(TOP_K):
        acc = acc + gates[:, j:j + 1] * _tiles_to_rows(buf.at[slot, j], tm)
    y_ref[...] = _layer_norm(acc, g_ref[...], b_ref[...])


def _collect_combine(src, idx_blocks, gates, x, ln_g, ln_b, tm):
    T = x.shape[0]
    nt = T // tm
    row = pl.BlockSpec((tm, D_MODEL), lambda i: (i, 0))
    vec = _full((1, D_MODEL))
    return pl.pallas_call(
        functools.partial(_collect_combine_kernel, tm=tm),
        out_shape=jax.ShapeDtypeStruct((T, D_MODEL), F32),
        grid=(nt,),
        in_specs=[pl.BlockSpec((1, TOP_K, tm), lambda i: (i, 0, 0), memory_space=pltpu.SMEM),
                  pl.BlockSpec((1, TOP_K, tm), lambda i: (jnp.minimum(i + 1, nt - 1), 0, 0), memory_space=pltpu.SMEM),
                  pl.BlockSpec(memory_space=pl.ANY),
                  pl.BlockSpec((tm, TOP_K), lambda i: (i, 0)), row, vec, vec],
        out_specs=row,
        scratch_shapes=[pltpu.VMEM((2, TOP_K, tm * ROW_TILES, LANES), F32), pltpu.SemaphoreType.DMA((2,))],
        compiler_params=_params("arbitrary"),
        name="moe_collect_combine",
    )(idx_blocks, idx_blocks, src, gates, x, ln_g, ln_b)


def _expert_kernel(be_ref, nb_ref, xs_ref, w1_ref, b1_ref, w2_ref, b2_ref, o_ref, *, blk):
    @pl.when(pl.program_id(0) < nb_ref[0])
    def _():
        h = _bdot(_tiles_to_rows(xs_ref, blk), w1_ref[0]) + b1_ref[0]
        glu = jnp.minimum(h[:, :D_EXPERT], SWIGLU_LIMIT)
        lin = jnp.clip(h[:, D_EXPERT:], -SWIGLU_LIMIT, SWIGLU_LIMIT)
        a = glu * jax.nn.sigmoid(SWIGLU_ALPHA * glu) * (lin + 1.0)
        _rows_to_tiles(o_ref, _bdot(a, w2_ref[0]) + b2_ref[0])


def _experts(xs, block_e, n_used, w1, b1, w2, b2, blk):
    P = xs.shape[0] // ROW_TILES
    nb = P // blk

    def blk_map(i, be, nu):
        return (jnp.minimum(i, nu[0] - 1), 0)

    def w_map(i, be, nu):
        return (be[jnp.minimum(i, nu[0] - 1)], 0, 0)

    return pl.pallas_call(
        functools.partial(_expert_kernel, blk=blk),
        out_shape=jax.ShapeDtypeStruct((P * ROW_TILES, LANES), F32),
        grid_spec=pltpu.PrefetchScalarGridSpec(
            num_scalar_prefetch=2,
            grid=(nb,),
            in_specs=[pl.BlockSpec((blk * ROW_TILES, LANES), blk_map),
                      pl.BlockSpec((1, D_MODEL, 2 * D_EXPERT), w_map),
                      pl.BlockSpec((1, 1, 2 * D_EXPERT), w_map),
                      pl.BlockSpec((1, D_EXPERT, D_MODEL), w_map),
                      pl.BlockSpec((1, 1, D_MODEL), w_map)],
            out_specs=pl.BlockSpec((blk * ROW_TILES, LANES), blk_map)),
        compiler_params=_params("arbitrary"),
        name="moe_experts",
    )(block_e, n_used, xs, w1, b1, w2, b2)


def _moe(x, w_r, b_r, w1, b1, w2, b2, ln_g, ln_b):
    T = x.shape[0]
    blk = 512 if T >= 4096 else 128
    tm = _row_tile(T, 512)
    n_assign = T * TOP_K
    n_blocks = -(-(n_assign + N_EXPERTS * (blk - 1)) // blk)
    n_slots = n_blocks * blk

    top_i, gates, pos, counts = _router(x, w_r, b_r)
    counts = counts[:, 0]
    padded = (counts + blk - 1) // blk * blk
    pad_end = jnp.cumsum(padded)
    pad_start = pad_end - padded
    dest = pad_start[top_i] + pos
    block_start = jnp.arange(n_blocks, dtype=jnp.int32) * blk
    block_e = jnp.minimum(jnp.sum(pad_end[None, :] <= block_start[:, None], axis=1), N_EXPERTS - 1).astype(jnp.int32)
    n_used = (pad_end[-1:] // blk).astype(jnp.int32)
    dest_blocks = dest.reshape(TOP_K, T // tm, tm).transpose(1, 0, 2)

    xs = _dispatch(x, dest_blocks, pad_end.astype(jnp.int32), n_slots, tm, blk)
    out = _experts(xs, block_e, n_used, w1, b1, w2, b2, blk)
    return _collect_combine(out, dest_blocks, gates.T, x, ln_g, ln_b, tm)


def _pad_rows(t, n):
    return t if n == 0 else jnp.pad(t, ((0, 0), (0, n), (0, 0)))


def _trunk(x, gla_s0, past_k, past_v, past_valid, mem_k, mem_v, w):
    B, L, _ = x.shape
    Lp = -(-L // CHUNK) * CHUNK
    T = B * L
    vec = lambda a: a.reshape(1, -1)
    gla_states = []
    k_all = v_all = new_k = new_v = None
    for layer in range(DEPTH):
        lg, lb = w['ln_g'][layer], w['ln_b'][layer]
        xt = x.reshape(T, D_MODEL)
        if layer < N_A:
            i = layer
            xt = _pad_rows(x, Lp - L).reshape(B * Lp, D_MODEL)
            q, k, v, r, b = _gla_in(xt, w['gla_w_main'][i], w['gla_w_g'][i], w['gla_w_g2'][i],
                                    vec(w['gla_b_g'][i]), L, Lp)
            seq = lambda t: t.reshape(B, Lp, -1)
            s0t = jnp.swapaxes(gla_s0[i], -1, -2)
            o, st = _gla_scan(seq(q), seq(k), seq(v), seq(b), s0t)
            gla_states.append(jnp.swapaxes(st, -1, -2))
            xt = _gla_out(o.reshape(B * Lp, GLA_V), r, xt, vec(w['gla_gn_g'][i]), w['gla_w_o'][i],
                          vec(lg[0]), vec(lb[0]))
            x = xt.reshape(B, Lp, D_MODEL)[:, :L]
        else:
            j = layer - N_A
            if j == 0:
                k_new, v_new = _linear_kv(xt, w['kv_w'][None])
                k_full = jnp.concatenate([past_k, k_new.reshape(B, L, SWA_KV)], axis=1)
                v_full = jnp.concatenate([past_v, v_new.reshape(B, L, SWA_KV)], axis=1)
                new_k, new_v = k_full[:, -WINDOW:], v_full[:, -WINDOW:]
                k_all, v_all = _pad_rows(k_full, Lp - L), _pad_rows(v_full, Lp - L)
            xp = _swa(_pad_rows(x, Lp - L), k_all, v_all, w['swa_w_q'][j], w['swa_sinks'][j].reshape(-1, 1),
                      w['swa_w_o'][j], vec(lg[0]), vec(lb[0]), L, past_valid)
            x = xp[:, :L]
        x = _mem_xattn(x, mem_k[layer], mem_v[layer], w['mem_w_q'][layer], w['mem_w_o'][layer],
                       vec(lg[1]), vec(lb[1]))
        xt = _moe(x.reshape(T, D_MODEL), w['moe_w_r'][layer], w['moe_b_r'][layer], w['moe_w1'][layer],
                  w['moe_b1'][layer], w['moe_w2'][layer], w['moe_b2'][layer], vec(lg[2]), vec(lb[2]))
        x = xt.reshape(B, L, D_MODEL)
    return x, jnp.stack(gla_states), new_k, new_v


def _prep_weights(gla_w_in, gla_w_g2, gla_b_g, gla_gn_g, gla_w_o, kv_w, swa_w_q, swa_sinks, swa_w_o,
                  mem_w_q, mem_w_kv, mem_w_o, moe_w_r, moe_b_r, moe_w1, moe_b1, moe_w2, moe_b2, ln_g, ln_b):
    n_main = 2 * GLA_QK + 2 * GLA_V
    pad_c = lambda a, n: jnp.pad(a, [(0, 0)] * (a.ndim - 1) + [(0, n - a.shape[-1])])
    w = dict(
        gla_w_main=gla_w_in[:, :, :n_main].astype(BF16),
        gla_w_g=pad_c(gla_w_in[:, :, n_main:], LANES).astype(BF16),
        gla_w_g2=jnp.pad(gla_w_g2, ((0, 0), (0, LANES - GLA_GATE_RANK), (0, 0))).astype(BF16),
        gla_b_g=gla_b_g, gla_gn_g=gla_gn_g.reshape(N_A, GLA_V), gla_w_o=gla_w_o.astype(BF16),
        kv_w=kv_w.astype(BF16), swa_w_q=swa_w_q.astype(BF16), swa_sinks=swa_sinks, swa_w_o=swa_w_o.astype(BF16),
        mem_w_q=mem_w_q.astype(BF16), mem_w_kv=mem_w_kv.astype(BF16), mem_w_o=mem_w_o.astype(BF16),
        moe_w_r=jnp.swapaxes(moe_w_r, -1, -2).astype(BF16), moe_b_r=moe_b_r.reshape(DEPTH, N_EXPERTS, 1),
        moe_w1=_w1_relayout(moe_w1.reshape(DEPTH * N_EXPERTS, D_MODEL, 2 * D_EXPERT)).reshape(moe_w1.shape),
        moe_b1=jnp.concatenate([moe_b1[..., 0::2], moe_b1[..., 1::2]], axis=-1).reshape(DEPTH, N_EXPERTS, 1, -1),
        moe_w2=moe_w2.astype(BF16), moe_b2=moe_b2.reshape(DEPTH, N_EXPERTS, 1, D_MODEL),
        ln_g=ln_g, ln_b=ln_b)
    return w


def kernel(x_prompt, x_sample, state_gla, cache_swa_k, cache_swa_v, cache_mem_k, cache_mem_v, mem_prompt, gla_w_in, gla_w_g2, gla_b_g, gla_gn_g, gla_w_o, kv_w, swa_w_q, swa_sinks, swa_w_o, mem_w_q, mem_w_kv, mem_w_o, moe_w_r, moe_b_r, moe_w1, moe_b1, moe_w2, moe_b2, ln_g, ln_b):
    w = _prep_weights(gla_w_in, gla_w_g2, gla_b_g, gla_gn_g, gla_w_o, kv_w, swa_w_q, swa_sinks, swa_w_o,
                      mem_w_q, mem_w_kv, mem_w_o, moe_w_r, moe_b_r, moe_w1, moe_b1, moe_w2, moe_b2, ln_g, ln_b)
    bp = x_prompt.shape[0]
    bs = x_sample.shape[0]
    mem_flat = mem_prompt.reshape(bp * N_MEM, D_MODEL)
    mem_k_p, mem_v_p = _linear_kv(mem_flat, w['mem_w_kv'])
    mem_k_p = mem_k_p.reshape(DEPTH, bp, N_MEM, D_MODEL)
    mem_v_p = mem_v_p.reshape(DEPTH, bp, N_MEM, D_MODEL)
    gla0 = jnp.zeros((N_A, bp, GLA_HEADS, GLA_DK, GLA_DV), F32)
    zero_win = jnp.zeros((bp, WINDOW, SWA_KV), F32)
    y_p, gla_p, k_p, v_p = _trunk(x_prompt, gla0, zero_win, zero_win, False, mem_k_p, mem_v_p, w)
    y_s, gla_s, k_s, v_s = _trunk(x_sample, state_gla, cache_swa_k.reshape(bs, WINDOW, SWA_KV),
                                  cache_swa_v.reshape(bs, WINDOW, SWA_KV), True,
                                  cache_mem_k.reshape(DEPTH, bs, N_MEM, D_MODEL),
                                  cache_mem_v.reshape(DEPTH, bs, N_MEM, D_MODEL), w)
    heads4 = lambda t: t.reshape(t.shape[0], WINDOW, SWA_KV_HEADS, SWA_HEAD_DIM)
    mem5 = lambda t: t.reshape(DEPTH, bp, N_MEM, MEM_HEADS, MEM_HEAD_DIM)
    return (y_p, y_s, gla_p, gla_s, heads4(k_p), heads4(v_p), heads4(k_s), heads4(v_s), mem5(mem_k_p), mem5(mem_v_p))
```

```python
import functools

import jax
import jax.numpy as jnp
from jax import lax
from jax.experimental import pallas as pl
from jax.experimental.pallas import tpu as pltpu

F32 = jnp.float32
BF16 = jnp.bfloat16

D_MODEL = 1024
DEPTH = 4
CHUNK = 64
N_A = DEPTH // 2
GLA_HEADS = 4
GLA_DK = D_MODEL // (2 * GLA_HEADS)
GLA_DV = D_MODEL // GLA_HEADS
GLA_QK = GLA_HEADS * GLA_DK
GLA_V = GLA_HEADS * GLA_DV
GLA_GATE_RANK = 16
GLA_GATE_TEMP = 16.0
SWA_HEAD_DIM = 64
SWA_Q_HEADS = D_MODEL // SWA_HEAD_DIM
SWA_KV_HEADS = 4
SWA_GROUP = SWA_Q_HEADS // SWA_KV_HEADS
SWA_KV = SWA_KV_HEADS * SWA_HEAD_DIM
WINDOW = 128
WINDOW_CHUNKS = WINDOW // CHUNK
N_MEM = 256
MEM_HEADS = 4
MEM_HEAD_DIM = D_MODEL // MEM_HEADS
N_EXPERTS = 32
TOP_K = 4
D_EXPERT = D_MODEL
SWIGLU_ALPHA = 1.702
SWIGLU_LIMIT = 7.0
DEEPNORM_ALPHA = (2 * DEPTH) ** 0.25
LN_EPS = 1e-5
NEG_INF = -1e30

LANES = 128
VMEM_LIMIT = 56 * 1024 * 1024


def _params(*sem):
    return pltpu.CompilerParams(dimension_semantics=sem, vmem_limit_bytes=VMEM_LIMIT)


def _row_tile(n_rows, want):
    t = min(want, n_rows)
    while n_rows % t:
        t //= 2
    return t


def _full(shape):
    nd = len(shape)
    return pl.BlockSpec(shape, lambda *_: (0,) * nd)


def _bdot(a, b):
    return jnp.dot(a.astype(BF16), b.astype(BF16), preferred_element_type=F32)


def _bdot_nt(a, b):
    return lax.dot_general(a.astype(BF16), b.astype(BF16), (((1,), (1,)), ((), ())),
                           preferred_element_type=F32)


def _bdot_tn(a, b):
    return lax.dot_general(a.astype(BF16), b.astype(BF16), (((0,), (0,)), ((), ())),
                           preferred_element_type=F32)


def _layer_norm(z, g, b):
    mu = jnp.mean(z, axis=-1, keepdims=True)
    zc = z - mu
    var = jnp.mean(zc * zc, axis=-1, keepdims=True)
    return zc * lax.rsqrt(var + LN_EPS) * g + b


ROW_TILES = D_MODEL // LANES


def _rows_to_tiles(ref, x):
    n = x.shape[0]
    for c in range(ROW_TILES):
        ref[pl.ds(c, n, stride=ROW_TILES), :] = x[:, c * LANES:(c + 1) * LANES]


def _tiles_to_rows(ref, n):
    return jnp.concatenate([ref[pl.ds(c, n, stride=ROW_TILES), :] for c in range(ROW_TILES)], axis=1)


def _linear_kv_kernel(x_ref, w_ref, k_ref, v_ref):
    y = _bdot(x_ref[...], w_ref[0])
    half = y.shape[1] // 2
    k_ref[0] = y[:, :half]
    v_ref[0] = y[:, half:]


def _linear_kv(x, w, tm=512):
    T, K = x.shape
    G, _, N2 = w.shape
    tm = _row_tile(T, tm)
    out = jax.ShapeDtypeStruct((G, T, N2 // 2), F32)
    o_spec = pl.BlockSpec((1, tm, N2 // 2), lambda g, i: (g, i, 0))
    return pl.pallas_call(
        _linear_kv_kernel,
        out_shape=(out, out),
        grid=(G, T // tm),
        in_specs=[pl.BlockSpec((tm, K), lambda g, i: (i, 0)), pl.BlockSpec((1, K, N2), lambda g, i: (g, 0, 0))],
        out_specs=(o_spec, o_spec),
        compiler_params=_params("parallel", "parallel"),
        name="linear_kv",
    )(x, w)


MXU_DIM = 256


def _w1_relayout_kernel(w_ref, o_ref):
    half = MXU_DIM // 2
    r = lax.broadcasted_iota(jnp.int32, (MXU_DIM, MXU_DIM), 0)
    c = lax.broadcasted_iota(jnp.int32, (MXU_DIM, MXU_DIM), 1)
    src = jnp.where(c < half, 2 * c, 2 * (c - half) + 1)
    perm = (r == src).astype(BF16)
    n_out = o_ref.shape[-1] // 2
    for j in range(w_ref.shape[-1] // MXU_DIM):
        y = jnp.dot(w_ref[0, :, j * MXU_DIM:(j + 1) * MXU_DIM].astype(BF16), perm, preferred_element_type=F32)
        o_ref[0, :, j * half:(j + 1) * half] = y[:, :half].astype(BF16)
        o_ref[0, :, n_out + j * half:n_out + (j + 1) * half] = y[:, half:].astype(BF16)


def _w1_relayout(w1):
    E, D, N = w1.shape
    spec = pl.BlockSpec((1, D, N), lambda e: (e, 0, 0))
    return pl.pallas_call(
        _w1_relayout_kernel,
        out_shape=jax.ShapeDtypeStruct((E, D, N), BF16),
        grid=(E,),
        in_specs=[spec],
        out_specs=spec,
        compiler_params=_params("parallel"),
        name="w1_relayout",
    )(w1)


def _gla_in_kernel(x_ref, w_ref, wg_ref, wg2_ref, bg_ref, q_ref, k_ref, v_ref, r_ref, b_ref,
                   *, tm, seq_len, seq_pad):
    xb = x_ref[...].astype(BF16)
    y = jnp.dot(xb, w_ref[...], preferred_element_type=F32)
    q_ref[...] = y[:, :GLA_QK] * (GLA_DK ** -0.5)
    k_ref[...] = y[:, GLA_QK:2 * GLA_QK]
    v_ref[...] = y[:, 2 * GLA_QK:2 * GLA_QK + GLA_V]
    r_ref[...] = y[:, 2 * GLA_QK + GLA_V:]
    g_lr = jnp.dot(xb, wg_ref[...], preferred_element_type=F32)
    z = _bdot(g_lr, wg2_ref[...]) + bg_ref[...]
    log_a = (jnp.minimum(z, 0.0) - jnp.log(1.0 + jnp.exp(-jnp.abs(z)))) / GLA_GATE_TEMP
    if seq_len < seq_pad:
        pos = (pl.program_id(0) * tm + lax.broadcasted_iota(jnp.int32, (tm, 1), 0)) % seq_pad
        log_a = jnp.where(pos < seq_len, log_a, 0.0)
    row = lax.broadcasted_iota(jnp.int32, (tm, tm), 0)
    col = lax.broadcasted_iota(jnp.int32, (tm, tm), 1)
    tri = ((row // CHUNK == col // CHUNK) & (row >= col)).astype(BF16)
    g1 = log_a.astype(BF16)
    r1 = log_a - g1.astype(F32)
    g2 = r1.astype(BF16)
    g3 = (r1 - g2.astype(F32)).astype(BF16)
    b_ref[...] = (jnp.dot(tri, g1, preferred_element_type=F32) + jnp.dot(tri, g2, preferred_element_type=F32)
                  + jnp.dot(tri, g3, preferred_element_type=F32))


def _gla_in(x, w_main, w_g, w_g2, b_g, seq_len, seq_pad, tm=512):
    T = x.shape[0]
    tm = _row_tile(T, tm)
    assert tm % CHUNK == 0 and seq_pad % CHUNK == 0
    n_main = w_main.shape[1]
    row = lambda n: pl.BlockSpec((tm, n), lambda i: (i, 0))
    return pl.pallas_call(
        functools.partial(_gla_in_kernel, tm=tm, seq_len=seq_len, seq_pad=seq_pad),
        out_shape=(jax.ShapeDtypeStruct((T, GLA_QK), F32), jax.ShapeDtypeStruct((T, GLA_QK), F32),
                   jax.ShapeDtypeStruct((T, GLA_V), F32), jax.ShapeDtypeStruct((T, GLA_V), F32),
                   jax.ShapeDtypeStruct((T, GLA_QK), F32)),
        grid=(T // tm,),
        in_specs=[row(D_MODEL), _full((D_MODEL, n_main)), _full((D_MODEL, LANES)),
                  _full((LANES, GLA_QK)), _full((1, GLA_QK))],
        out_specs=(row(GLA_QK), row(GLA_QK), row(GLA_V), row(GLA_V), row(GLA_QK)),
        compiler_params=_params("parallel"),
        name="gla_in",
    )(x, w_main, w_g, w_g2, b_g)


SUBLANES = 8


def _gla_intra(q, k, b):
    n_grp = CHUNK // SUBLANES
    lane = lax.broadcasted_iota(jnp.int32, (SUBLANES, LANES), 1)
    row = lax.broadcasted_iota(jnp.int32, (SUBLANES, LANES), 0)
    grp = lambda t, i: t[i * SUBLANES:(i + 1) * SUBLANES]
    att = [jnp.zeros((SUBLANES, LANES), F32) for _ in range(n_grp)]
    for s in range(CHUNK):
        b_s = b[s:s + 1, :]
        k_s = k[s:s + 1, :]
        for i in range(s // SUBLANES, n_grp):
            e = jnp.exp(grp(b, i) - b_s)
            red = jnp.sum(grp(q, i) * e * k_s, axis=-1, keepdims=True)
            att[i] = jnp.where(lane == s, red, att[i])
    att = [jnp.where(row + i * SUBLANES >= lane, att[i], 0.0) for i in range(n_grp)]
    return jnp.concatenate(att, axis=0)


def _gla_scan_kernel(q_ref, k_ref, v_ref, b_ref, s0_ref, o_ref, s_ref, st_scr, *, n_chunks):
    @pl.when(pl.program_id(1) == 0)
    def _():
        st_scr[...] = s0_ref[0]

    for c in range(n_chunks):
        ts = slice(c * CHUNK, (c + 1) * CHUNK)
        for h in range(GLA_HEADS):
            ks = slice(h * GLA_DK, (h + 1) * GLA_DK)
            vs = slice(h * GLA_DV, (h + 1) * GLA_DV)
            q = q_ref[0, ts, ks]
            k = k_ref[0, ts, ks]
            b = b_ref[0, ts, ks]
            v = v_ref[0, ts, vs]
            att = _gla_intra(q, k, b)
            b_end = b[CHUNK - 1:CHUNK, :]
            st = st_scr[h]
            o_ref[0, ts, vs] = _bdot(att[:, :CHUNK], v) + _bdot_nt(q * jnp.exp(b), st)
            st_scr[h] = st * jnp.exp(b_end) + _bdot_tn(v, k * jnp.exp(b_end - b))

    @pl.when(pl.program_id(1) == pl.num_programs(1) - 1)
    def _():
        s_ref[0] = st_scr[...]


def _gla_scan(q, k, v, la, s0t, tl=128):
    B, Lp, _ = q.shape
    tl = _row_tile(Lp, tl)
    qk_spec = pl.BlockSpec((1, tl, GLA_QK), lambda b, i: (b, i, 0))
    v_spec = pl.BlockSpec((1, tl, GLA_V), lambda b, i: (b, i, 0))
    s_spec = pl.BlockSpec((1, GLA_HEADS, GLA_DV, GLA_DK), lambda b, i: (b, 0, 0, 0))
    return pl.pallas_call(
        functools.partial(_gla_scan_kernel, n_chunks=tl // CHUNK),
        out_shape=(jax.ShapeDtypeStruct((B, Lp, GLA_V), F32),
                   jax.ShapeDtypeStruct((B, GLA_HEADS, GLA_DV, GLA_DK), F32)),
        grid=(B, Lp // tl),
        in_specs=[qk_spec, qk_spec, v_spec, qk_spec, s_spec],
        out_specs=(v_spec, s_spec),
        scratch_shapes=[pltpu.VMEM((GLA_HEADS, GLA_DV, GLA_DK), F32)],
        compiler_params=_params("parallel", "arbitrary"),
        name="gla_scan",
    )(q, k, v, la, s0t)


def _gla_out_kernel(o_ref, r_ref, x_ref, gn_ref, wo_ref, g_ref, b_ref, y_ref):
    parts = []
    for h in range(GLA_HEADS):
        vs = slice(h * GLA_DV, (h + 1) * GLA_DV)
        o = o_ref[:, vs]
        mu = jnp.mean(o, axis=-1, keepdims=True)
        oc = o - mu
        var = jnp.mean(oc * oc, axis=-1, keepdims=True)
        parts.append(oc * lax.rsqrt(var + LN_EPS) * gn_ref[:, vs])
    o = jnp.concatenate(parts, axis=-1)
    r = r_ref[...]
    o = o * (r * jax.nn.sigmoid(r))
    h = _bdot(o, wo_ref[...])
    y_ref[...] = _layer_norm(DEEPNORM_ALPHA * x_ref[...] + h, g_ref[...], b_ref[...])


def _gla_out(o, r, x, gn_g, w_o, ln_g, ln_b, tm=512):
    T = x.shape[0]
    tm = _row_tile(T, tm)
    row = pl.BlockSpec((tm, D_MODEL), lambda i: (i, 0))
    vec = _full((1, D_MODEL))
    return pl.pallas_call(
        _gla_out_kernel,
        out_shape=jax.ShapeDtypeStruct((T, D_MODEL), F32),
        grid=(T // tm,),
        in_specs=[row, row, row, vec, _full((GLA_V, D_MODEL)), vec, vec],
        out_specs=row,
        compiler_params=_params("parallel"),
        name="gla_out",
    )(o, r, x, gn_g, w_o, ln_g, ln_b)


def _swa_kernel(x_ref, k_ref, v_ref, wq_ref, sink_ref, wo_ref, g_ref, b_ref, y_ref,
                *, tl, seq_len, past_valid):
    x = x_ref[0]
    q = (_bdot(x, wq_ref[...]) * (SWA_HEAD_DIM ** -0.5)).astype(BF16)
    kw = tl + WINDOW
    r0 = pl.multiple_of(pl.program_id(1) * tl, tl)
    kt = k_ref[0, pl.ds(r0, kw), :].astype(BF16)
    vt = v_ref[0, pl.ds(r0, kw), :].astype(BF16)
    q_row = lax.broadcasted_iota(jnp.int32, (tl, kw), 0)
    k_col = lax.broadcasted_iota(jnp.int32, (tl, kw), 1)
    band_lo = (q_row // CHUNK) * CHUNK
    pos = r0 + k_col
    ok = (k_col >= band_lo) & (k_col < band_lo + WINDOW + CHUNK) & (pos - WINDOW < seq_len)
    if not past_valid:
        ok = ok & (pos >= WINDOW)
    zeros = jnp.zeros((kw, SWA_HEAD_DIM), BF16)
    pair_out = []
    for g in range(SWA_KV_HEADS):
        kg = kt[:, g * SWA_HEAD_DIM:(g + 1) * SWA_HEAD_DIM]
        vg = vt[:, g * SWA_HEAD_DIM:(g + 1) * SWA_HEAD_DIM]
        k_pad = (jnp.concatenate([kg, zeros], axis=1), jnp.concatenate([zeros, kg], axis=1))
        v_pad = (jnp.concatenate([vg, zeros], axis=1), jnp.concatenate([zeros, vg], axis=1))
        for pair in range(g * SWA_GROUP // 2, (g + 1) * SWA_GROUP // 2):
            qp = q[:, pair * LANES:(pair + 1) * LANES]
            sinks = [sink_ref[2 * pair + j:2 * pair + j + 1, :] for j in range(2)]
            s = [jnp.where(ok, _bdot_nt(qp, k_pad[j]), NEG_INF) for j in range(2)]
            m = [jnp.maximum(jnp.max(s[j], axis=-1, keepdims=True), sinks[j]) for j in range(2)]
            e = [jnp.exp(s[j] - m[j]) for j in range(2)]
            o = [_bdot(e[j], v_pad[j]) for j in range(2)]
            inv = [1.0 / (jnp.sum(e[j], axis=-1, keepdims=True) + jnp.exp(sinks[j] - m[j])) for j in range(2)]
            pair_out.append(o[0] * inv[0] + o[1] * inv[1])
    h_out = _bdot(jnp.concatenate(pair_out, axis=1), wo_ref[...])
    y_ref[0] = _layer_norm(DEEPNORM_ALPHA * x + h_out, g_ref[...], b_ref[...])


def _swa(x, k_all, v_all, w_q, sinks, w_o, ln_g, ln_b, seq_len, past_valid, tl=256):
    B, Lp, _ = x.shape
    tl = _row_tile(Lp, tl)
    x_spec = pl.BlockSpec((1, tl, D_MODEL), lambda b, i: (b, i, 0))
    kv_spec = pl.BlockSpec((1, WINDOW + Lp, SWA_KV), lambda b, i: (b, 0, 0))
    vec = _full((1, D_MODEL))
    return pl.pallas_call(
        functools.partial(_swa_kernel, tl=tl, seq_len=seq_len, past_valid=past_valid),
        out_shape=jax.ShapeDtypeStruct((B, Lp, D_MODEL), F32),
        grid=(B, Lp // tl),
        in_specs=[x_spec, kv_spec, kv_spec, _full((D_MODEL, D_MODEL)), _full((SWA_Q_HEADS, 1)),
                  _full((D_MODEL, D_MODEL)), vec, vec],
        out_specs=x_spec,
        compiler_params=_params("parallel", "arbitrary"),
        name="swa",
    )(x, k_all, v_all, w_q, sinks, w_o, ln_g, ln_b)


def _mem_kernel(x_ref, mk_ref, mv_ref, wq_ref, wo_ref, g_ref, b_ref, y_ref):
    x = x_ref[0]
    q = _bdot(x, wq_ref[...]) * (MEM_HEAD_DIM ** -0.5)
    parts = []
    for h in range(MEM_HEADS):
        hs = slice(h * MEM_HEAD_DIM, (h + 1) * MEM_HEAD_DIM)
        s = _bdot_nt(q[:, hs], mk_ref[0, :, hs])
        m = jnp.max(s, axis=-1, keepdims=True)
        e = jnp.exp(s - m)
        parts.append(_bdot(e, mv_ref[0, :, hs]) * (1.0 / jnp.sum(e, axis=-1, keepdims=True)))
    o = jnp.concatenate(parts, axis=-1)
    h_out = _bdot(o, wo_ref[...])
    y_ref[0] = _layer_norm(DEEPNORM_ALPHA * x + h_out, g_ref[...], b_ref[...])


def _mem_xattn(x, mk, mv, layer, w_q, w_o, ln_g, ln_b, tl=512):
    B, L, _ = x.shape
    tl = _row_tile(L, tl)
    x_spec = pl.BlockSpec((1, tl, D_MODEL), lambda b, i: (b, i, 0))
    m_spec = pl.BlockSpec((1, N_MEM, D_MODEL), lambda b, i: (layer * B + b, 0, 0))
    vec = _full((1, D_MODEL))
    return pl.pallas_call(
        _mem_kernel,
        out_shape=jax.ShapeDtypeStruct((B, L, D_MODEL), F32),
        grid=(B, L // tl),
        in_specs=[x_spec, m_spec, m_spec, _full((D_MODEL, D_MODEL)), _full((D_MODEL, D_MODEL)), vec, vec],
        out_specs=x_spec,
        compiler_params=_params("parallel", "arbitrary"),
        name="mem_xattn",
    )(x, mk, mv, w_q, w_o, ln_g, ln_b)


def _router_kernel(x_ref, wrt_ref, br_ref, idx_ref, gate_ref, pos_ref, cnt_ref, cnt_scr, *, tm):
    @pl.when(pl.program_id(0) == 0)
    def _():
        cnt_scr[...] = jnp.zeros_like(cnt_scr)

    logits = _bdot_nt(wrt_ref[...], x_ref[...]) + br_ref[...]
    expert = lax.broadcasted_iota(jnp.int32, (N_EXPERTS, tm), 0).astype(F32)
    sel = jnp.zeros((N_EXPERTS, tm), F32)
    l = logits
    vals, hits = [], []
    for j in range(TOP_K):
        m = jnp.max(l, axis=0, keepdims=True)
        idx = jnp.min(jnp.where(l == m, expert, float(N_EXPERTS)), axis=0, keepdims=True)
        hit = expert == idx
        sel = jnp.where(hit, 1.0, sel)
        l = jnp.where(hit, -jnp.inf, l)
        idx_ref[j:j + 1, :] = idx.astype(jnp.int32)
        vals.append(m)
        hits.append(hit)
    e = [jnp.exp(v - vals[0]) for v in vals]
    inv = 1.0 / (e[0] + e[1] + e[2] + e[3])
    r = lax.broadcasted_iota(jnp.int32, (tm, tm), 0)
    c = lax.broadcasted_iota(jnp.int32, (tm, tm), 1)
    before = jnp.dot(sel.astype(BF16), (r < c).astype(BF16), preferred_element_type=F32) + cnt_scr[:, 0:1]
    for j in range(TOP_K):
        gate_ref[j:j + 1, :] = e[j] * inv
        pos_ref[j:j + 1, :] = jnp.sum(jnp.where(hits[j], before, 0.0), axis=0, keepdims=True).astype(jnp.int32)
    cnt_scr[...] = cnt_scr[...] + jnp.sum(sel, axis=1, keepdims=True)
    cnt_ref[...] = cnt_scr[...].astype(jnp.int32)


def _router(x, w_rt, b_r, tm=512):
    T = x.shape[0]
    tm = _row_tile(T, tm)
    small = pl.BlockSpec((TOP_K, tm), lambda i: (0, i))
    return pl.pallas_call(
        functools.partial(_router_kernel, tm=tm),
        out_shape=(jax.ShapeDtypeStruct((TOP_K, T), jnp.int32), jax.ShapeDtypeStruct((TOP_K, T), F32),
                   jax.ShapeDtypeStruct((TOP_K, T), jnp.int32), jax.ShapeDtypeStruct((N_EXPERTS, LANES), jnp.int32)),
        grid=(T // tm,),
        in_specs=[pl.BlockSpec((tm, D_MODEL), lambda i: (i, 0)), _full((N_EXPERTS, D_MODEL)), _full((N_EXPERTS, 1))],
        out_specs=(small, small, small, _full((N_EXPERTS, LANES))),
        scratch_shapes=[pltpu.VMEM((N_EXPERTS, LANES), F32)],
        compiler_params=_params("arbitrary"),
        name="moe_router",
    )(x, w_rt, b_r)


ISSUE_UNROLL = 16


def _dispatch_kernel(pad_end_ref, dest_ref, x_ref, xs_ref, rows_scr, zero_scr, sem, zsem, *, tm, blk):
    @pl.when(pl.program_id(0) == 0)
    def _():
        zero_scr[...] = jnp.zeros_like(zero_scr)

        def fill(e):
            start = pl.multiple_of((pad_end_ref[e] - blk) * ROW_TILES, blk * ROW_TILES)
            return pltpu.make_async_copy(zero_scr, xs_ref.at[pl.ds(start, blk * ROW_TILES), :], zsem)

        def has_block(e):
            return pad_end_ref[e] > (pad_end_ref[e - 1] if e else 0)

        for e in range(N_EXPERTS):
            pl.when(has_block(e))(lambda e=e: fill(e).start())
        for e in range(N_EXPERTS):
            pl.when(has_block(e))(lambda e=e: fill(e).wait())

    _rows_to_tiles(rows_scr, x_ref[...])

    def tile(ref, row):
        return ref.at[pl.ds(pl.multiple_of(row * ROW_TILES, ROW_TILES), ROW_TILES), :]

    def issue(g, carry):
        base = pl.multiple_of(g * (ISSUE_UNROLL // TOP_K), ISSUE_UNROLL // TOP_K)
        for u in range(ISSUE_UNROLL // TOP_K):
            r = base + u
            for j in range(TOP_K):
                pltpu.make_async_copy(tile(rows_scr, r), tile(xs_ref, dest_ref[0, j, r]), sem).start(priority=j % 2)
        return carry

    lax.fori_loop(0, tm * TOP_K // ISSUE_UNROLL, issue, 0)
    for j in range(TOP_K):
        pltpu.make_async_copy(rows_scr, xs_ref.at[pl.ds(0, tm * ROW_TILES), :], sem).wait()


def _dispatch(x, dest_blocks, pad_end, n_slots, tm, blk):
    T = x.shape[0]
    return pl.pallas_call(
        functools.partial(_dispatch_kernel, tm=tm, blk=blk),
        out_shape=jax.ShapeDtypeStruct((n_slots * ROW_TILES, LANES), F32),
        grid_spec=pltpu.PrefetchScalarGridSpec(
            num_scalar_prefetch=1,
            grid=(T // tm,),
            in_specs=[pl.BlockSpec((1, TOP_K, tm), lambda i, pe: (i, 0, 0), memory_space=pltpu.SMEM),
                      pl.BlockSpec((tm, D_MODEL), lambda i, pe: (i, 0))],
            out_specs=pl.BlockSpec(memory_space=pl.ANY),
            scratch_shapes=[pltpu.VMEM((tm * ROW_TILES, LANES), F32), pltpu.VMEM((blk * ROW_TILES, LANES), F32),
                            pltpu.SemaphoreType.DMA, pltpu.SemaphoreType.DMA]),
        compiler_params=_params("arbitrary"),
        name="moe_dispatch",
    )(pad_end, dest_blocks, x)


def _collect_combine_kernel(idx_ref, idx_next_ref, src_ref, gate_ref, x_ref, g_ref, b_ref, y_ref, buf, sem, *, tm):
    i = pl.program_id(0)
    slot = i % 2

    def tile(ref, row):
        return ref.at[pl.ds(pl.multiple_of(row * ROW_TILES, ROW_TILES), ROW_TILES), :]

    def request(ids, s):
        def body(g, carry):
            base = pl.multiple_of(g * (ISSUE_UNROLL // TOP_K), ISSUE_UNROLL // TOP_K)
            for u in range(ISSUE_UNROLL // TOP_K):
                r = base + u
                for j in range(TOP_K):
                    pltpu.make_async_copy(tile(src_ref, ids[0, j, r]), tile(buf.at[s, j], r),
                                          sem.at[s]).start(priority=j % 2)
            return carry
        lax.fori_loop(0, tm * TOP_K // ISSUE_UNROLL, body, 0)

    @pl.when(i == 0)
    def _():
        request(idx_ref, 0)

    @pl.when(i + 1 < pl.num_programs(0))
    def _():
        request(idx_next_ref, 1 - slot)

    for j in range(TOP_K):
        pltpu.make_async_copy(src_ref.at[pl.ds(0, tm * ROW_TILES), :], buf.at[slot, j], sem.at[slot]).wait()
    gates = gate_ref[...]
    acc = DEEPNORM_ALPHA * x_ref[...]
    for j in range(TOP_K):
        acc = acc + gates[:, j:j + 1] * _tiles_to_rows(buf.at[slot, j], tm)
    y_ref[...] = _layer_norm(acc, g_ref[...], b_ref[...])


def _collect_combine(src, idx_blocks, gates, x, ln_g, ln_b, tm):
    T = x.shape[0]
    nt = T // tm
    row = pl.BlockSpec((tm, D_MODEL), lambda i: (i, 0))
    vec = _full((1, D_MODEL))
    return pl.pallas_call(
        functools.partial(_collect_combine_kernel, tm=tm),
        out_shape=jax.ShapeDtypeStruct((T, D_MODEL), F32),
        grid=(nt,),
        in_specs=[pl.BlockSpec((1, TOP_K, tm), lambda i: (i, 0, 0), memory_space=pltpu.SMEM),
                  pl.BlockSpec((1, TOP_K, tm), lambda i: (jnp.minimum(i + 1, nt - 1), 0, 0), memory_space=pltpu.SMEM),
                  pl.BlockSpec(memory_space=pl.ANY),
                  pl.BlockSpec((tm, TOP_K), lambda i: (i, 0)), row, vec, vec],
        out_specs=row,
        scratch_shapes=[pltpu.VMEM((2, TOP_K, tm * ROW_TILES, LANES), F32), pltpu.SemaphoreType.DMA((2,))],
        compiler_params=_params("arbitrary"),
        name="moe_collect_combine",
    )(idx_blocks, idx_blocks, src, gates, x, ln_g, ln_b)


def _expert_kernel(be_ref, nb_ref, xs_ref, w1_ref, b1_ref, w2_ref, b2_ref, o_ref, *, blk):
    @pl.when(pl.program_id(0) < nb_ref[0])
    def _():
        h = _bdot(_tiles_to_rows(xs_ref, blk), w1_ref[0]) + b1_ref[0]
        glu = jnp.minimum(h[:, :D_EXPERT], SWIGLU_LIMIT)
        lin = jnp.clip(h[:, D_EXPERT:], -SWIGLU_LIMIT, SWIGLU_LIMIT)
        a = glu * jax.nn.sigmoid(SWIGLU_ALPHA * glu) * (lin + 1.0)
        _rows_to_tiles(o_ref, _bdot(a, w2_ref[0]) + b2_ref[0])


def _experts(xs, block_e, n_used, w1, b1, w2, b2, blk):
    P = xs.shape[0] // ROW_TILES
    nb = P // blk

    def blk_map(i, be, nu):
        return (jnp.minimum(i, nu[0] - 1), 0)

    def w_map(i, be, nu):
        return (be[jnp.minimum(i, nu[0] - 1)], 0, 0)

    return pl.pallas_call(
        functools.partial(_expert_kernel, blk=blk),
        out_shape=jax.ShapeDtypeStruct((P * ROW_TILES, LANES), F32),
        grid_spec=pltpu.PrefetchScalarGridSpec(
            num_scalar_prefetch=2,
            grid=(nb,),
            in_specs=[pl.BlockSpec((blk * ROW_TILES, LANES), blk_map),
                      pl.BlockSpec((1, D_MODEL, 2 * D_EXPERT), w_map),
                      pl.BlockSpec((1, 1, 2 * D_EXPERT), w_map),
                      pl.BlockSpec((1, D_EXPERT, D_MODEL), w_map),
                      pl.BlockSpec((1, 1, D_MODEL), w_map)],
            out_specs=pl.BlockSpec((blk * ROW_TILES, LANES), blk_map)),
        compiler_params=_params("arbitrary"),
        name="moe_experts",
    )(block_e, n_used, xs, w1, b1, w2, b2)


def _moe(x, layer, w_r, b_r, w1, b1, w2, b2, ln_g, ln_b):
    T = x.shape[0]
    blk = 512 if T >= 4096 else 128
    tm = _row_tile(T, 512)
    n_assign = T * TOP_K
    n_blocks = -(-(n_assign + N_EXPERTS * (blk - 1)) // blk)
    n_slots = n_blocks * blk

    top_i, gates, pos, counts = _router(x, w_r, b_r)
    counts = counts[:, 0]
    padded = (counts + blk - 1) // blk * blk
    pad_end = jnp.cumsum(padded)
    pad_start = pad_end - padded
    is_e = top_i[None] == jnp.arange(N_EXPERTS, dtype=jnp.int32)[:, None, None]
    dest = pos + jnp.sum(jnp.where(is_e, pad_start.astype(jnp.int32)[:, None, None], 0), axis=0)
    block_start = jnp.arange(n_blocks, dtype=jnp.int32) * blk
    block_e = jnp.minimum(jnp.sum(pad_end[None, :] <= block_start[:, None], axis=1), N_EXPERTS - 1).astype(jnp.int32)
    n_used = (pad_end[-1:] // blk).astype(jnp.int32)
    dest_blocks = dest.reshape(TOP_K, T // tm, tm).transpose(1, 0, 2)

    xs = _dispatch(x, dest_blocks, pad_end.astype(jnp.int32), n_slots, tm, blk)
    out = _experts(xs, block_e + layer * N_EXPERTS, n_used, w1, b1, w2, b2, blk)
    return _collect_combine(out, dest_blocks, gates.T, x, ln_g, ln_b, tm)


def _pad_rows(t, n):
    return t if n == 0 else jnp.pad(t, ((0, 0), (0, n), (0, 0)))


def _trunk(x, gla_s0, past_k, past_v, past_valid, mem_k, mem_v, w):
    B, L, _ = x.shape
    Lp = -(-L // CHUNK) * CHUNK
    T = B * L
    vec = lambda a: a.reshape(1, -1)
    gla_states = []
    k_all = v_all = new_k = new_v = None
    for layer in range(DEPTH):
        lg, lb = w['ln_g'][layer], w['ln_b'][layer]
        xt = x.reshape(T, D_MODEL)
        if layer < N_A:
            i = layer
            xt = _pad_rows(x, Lp - L).reshape(B * Lp, D_MODEL)
            q, k, v, r, b = _gla_in(xt, w['gla_w_main'][i], w['gla_w_g'][i], w['gla_w_g2'][i],
                                    vec(w['gla_b_g'][i]), L, Lp)
            seq = lambda t: t.reshape(B, Lp, -1)
            s0t = jnp.swapaxes(gla_s0[i], -1, -2)
            o, st = _gla_scan(seq(q), seq(k), seq(v), seq(b), s0t)
            gla_states.append(jnp.swapaxes(st, -1, -2))
            xt = _gla_out(o.reshape(B * Lp, GLA_V), r, xt, vec(w['gla_gn_g'][i]), w['gla_w_o'][i],
                          vec(lg[0]), vec(lb[0]))
            x = xt.reshape(B, Lp, D_MODEL)[:, :L]
        else:
            j = layer - N_A
            if j == 0:
                k_new, v_new = _linear_kv(xt, w['kv_w'][None])
                k_full = jnp.concatenate([past_k, k_new.reshape(B, L, SWA_KV)], axis=1)
                v_full = jnp.concatenate([past_v, v_new.reshape(B, L, SWA_KV)], axis=1)
                new_k, new_v = k_full[:, -WINDOW:], v_full[:, -WINDOW:]
                k_all, v_all = _pad_rows(k_full, Lp - L), _pad_rows(v_full, Lp - L)
            xp = _swa(_pad_rows(x, Lp - L), k_all, v_all, w['swa_w_q'][j], w['swa_sinks'][j].reshape(-1, 1),
                      w['swa_w_o'][j], vec(lg[0]), vec(lb[0]), L, past_valid)
            x = xp[:, :L]
        x = _mem_xattn(x, mem_k, mem_v, layer, w['mem_w_q'][layer], w['mem_w_o'][layer], vec(lg[1]), vec(lb[1]))
        xt = _moe(x.reshape(T, D_MODEL), layer, w['moe_w_r'][layer], w['moe_b_r'][layer], w['moe_w1'],
                  w['moe_b1'], w['moe_w2'], w['moe_b2'], vec(lg[2]), vec(lb[2]))
        x = xt.reshape(B, L, D_MODEL)
    return x, jnp.stack(gla_states), new_k, new_v


def _prep_weights(gla_w_in, gla_w_g2, gla_b_g, gla_gn_g, gla_w_o, kv_w, swa_w_q, swa_sinks, swa_w_o,
                  mem_w_q, mem_w_kv, mem_w_o, moe_w_r, moe_b_r, moe_w1, moe_b1, moe_w2, moe_b2, ln_g, ln_b):
    n_main = 2 * GLA_QK + 2 * GLA_V
    pad_c = lambda a, n: jnp.pad(a, [(0, 0)] * (a.ndim - 1) + [(0, n - a.shape[-1])])
    w = dict(
        gla_w_main=gla_w_in[:, :, :n_main].astype(BF16),
        gla_w_g=pad_c(gla_w_in[:, :, n_main:], LANES).astype(BF16),
        gla_w_g2=jnp.pad(gla_w_g2, ((0, 0), (0, LANES - GLA_GATE_RANK), (0, 0))).astype(BF16),
        gla_b_g=gla_b_g, gla_gn_g=gla_gn_g.reshape(N_A, GLA_V), gla_w_o=gla_w_o.astype(BF16),
        kv_w=kv_w.astype(BF16), swa_w_q=swa_w_q.astype(BF16), swa_sinks=swa_sinks, swa_w_o=swa_w_o.astype(BF16),
        mem_w_q=mem_w_q.astype(BF16), mem_w_kv=mem_w_kv.astype(BF16), mem_w_o=mem_w_o.astype(BF16),
        moe_w_r=jnp.swapaxes(moe_w_r, -1, -2).astype(BF16), moe_b_r=moe_b_r.reshape(DEPTH, N_EXPERTS, 1),
        moe_w1=_w1_relayout(moe_w1.reshape(DEPTH * N_EXPERTS, D_MODEL, 2 * D_EXPERT)),
        moe_b1=jnp.concatenate([moe_b1[..., 0::2], moe_b1[..., 1::2]], axis=-1).reshape(DEPTH * N_EXPERTS, 1, -1),
        moe_w2=moe_w2.astype(BF16).reshape(DEPTH * N_EXPERTS, D_EXPERT, D_MODEL),
        moe_b2=moe_b2.reshape(DEPTH * N_EXPERTS, 1, D_MODEL),
        ln_g=ln_g, ln_b=ln_b)
    return w


def kernel(x_prompt, x_sample, state_gla, cache_swa_k, cache_swa_v, cache_mem_k, cache_mem_v, mem_prompt, gla_w_in, gla_w_g2, gla_b_g, gla_gn_g, gla_w_o, kv_w, swa_w_q, swa_sinks, swa_w_o, mem_w_q, mem_w_kv, mem_w_o, moe_w_r, moe_b_r, moe_w1, moe_b1, moe_w2, moe_b2, ln_g, ln_b):
    w = _prep_weights(gla_w_in, gla_w_g2, gla_b_g, gla_gn_g, gla_w_o, kv_w, swa_w_q, swa_sinks, swa_w_o,
                      mem_w_q, mem_w_kv, mem_w_o, moe_w_r, moe_b_r, moe_w1, moe_b1, moe_w2, moe_b2, ln_g, ln_b)
    bp = x_prompt.shape[0]
    bs = x_sample.shape[0]
    mem_flat = mem_prompt.reshape(bp * N_MEM, D_MODEL)
    mem_k_p, mem_v_p = _linear_kv(mem_flat, w['mem_w_kv'])
    gla0 = jnp.zeros((N_A, bp, GLA_HEADS, GLA_DK, GLA_DV), F32)
    zero_win = jnp.zeros((bp, WINDOW, SWA_KV), F32)
    y_p, gla_p, k_p, v_p = _trunk(x_prompt, gla0, zero_win, zero_win, False,
                                  mem_k_p.reshape(DEPTH * bp, N_MEM, D_MODEL),
                                  mem_v_p.reshape(DEPTH * bp, N_MEM, D_MODEL), w)
    y_s, gla_s, k_s, v_s = _trunk(x_sample, state_gla, cache_swa_k.reshape(bs, WINDOW, SWA_KV),
                                  cache_swa_v.reshape(bs, WINDOW, SWA_KV), True,
                                  cache_mem_k.reshape(DEPTH * bs, N_MEM, D_MODEL),
                                  cache_mem_v.reshape(DEPTH * bs, N_MEM, D_MODEL), w)
    heads4 = lambda t: t.reshape(t.shape[0], WINDOW, SWA_KV_HEADS, SWA_HEAD_DIM)
    mem5 = lambda t: t.reshape(DEPTH, bp, N_MEM, MEM_HEADS, MEM_HEAD_DIM)
    return (y_p, y_s, gla_p, gla_s, heads4(k_p), heads4(v_p), heads4(k_s), heads4(v_s), mem5(mem_k_p), mem5(mem_v_p))
```

```python
import functools

import jax
import jax.numpy as jnp
from jax import lax
from jax.experimental import pallas as pl
from jax.experimental.pallas import tpu as pltpu

F32 = jnp.float32
BF16 = jnp.bfloat16

D_MODEL = 1024
DEPTH = 4
CHUNK = 64
N_A = DEPTH // 2
GLA_HEADS = 4
GLA_DK = D_MODEL // (2 * GLA_HEADS)
GLA_DV = D_MODEL // GLA_HEADS
GLA_QK = GLA_HEADS * GLA_DK
GLA_V = GLA_HEADS * GLA_DV
GLA_GATE_RANK = 16
GLA_GATE_TEMP = 16.0
SWA_HEAD_DIM = 64
SWA_Q_HEADS = D_MODEL // SWA_HEAD_DIM
SWA_KV_HEADS = 4
SWA_GROUP = SWA_Q_HEADS // SWA_KV_HEADS
SWA_KV = SWA_KV_HEADS * SWA_HEAD_DIM
WINDOW = 128
WINDOW_CHUNKS = WINDOW // CHUNK
N_MEM = 256
MEM_HEADS = 4
MEM_HEAD_DIM = D_MODEL // MEM_HEADS
N_EXPERTS = 32
TOP_K = 4
D_EXPERT = D_MODEL
SWIGLU_ALPHA = 1.702
SWIGLU_LIMIT = 7.0
DEEPNORM_ALPHA = (2 * DEPTH) ** 0.25
LN_EPS = 1e-5
NEG_INF = -1e30

LANES = 128
VMEM_LIMIT = 56 * 1024 * 1024


def _params(*sem):
    return pltpu.CompilerParams(dimension_semantics=sem, vmem_limit_bytes=VMEM_LIMIT)


def _row_tile(n_rows, want):
    t = min(want, n_rows)
    while n_rows % t:
        t //= 2
    return t


def _full(shape):
    nd = len(shape)
    return pl.BlockSpec(shape, lambda *_: (0,) * nd)


def _bdot(a, b):
    return jnp.dot(a.astype(BF16), b.astype(BF16), preferred_element_type=F32)


def _bdot_nt(a, b):
    return lax.dot_general(a.astype(BF16), b.astype(BF16), (((1,), (1,)), ((), ())),
                           preferred_element_type=F32)


def _bdot_tn(a, b):
    return lax.dot_general(a.astype(BF16), b.astype(BF16), (((0,), (0,)), ((), ())),
                           preferred_element_type=F32)


def _layer_norm(z, g, b):
    mu = jnp.mean(z, axis=-1, keepdims=True)
    zc = z - mu
    var = jnp.mean(zc * zc, axis=-1, keepdims=True)
    return zc * lax.rsqrt(var + LN_EPS) * g + b


ROW_TILES = D_MODEL // LANES


def _rows_to_tiles(ref, x):
    n = x.shape[0]
    for c in range(ROW_TILES):
        ref[pl.ds(c, n, stride=ROW_TILES), :] = x[:, c * LANES:(c + 1) * LANES]


def _tiles_to_rows(ref, n):
    return jnp.concatenate([ref[pl.ds(c, n, stride=ROW_TILES), :] for c in range(ROW_TILES)], axis=1)


def _linear_kv_kernel(x_ref, w_ref, k_ref, v_ref):
    y = _bdot(x_ref[...], w_ref[0])
    half = y.shape[1] // 2
    k_ref[0] = y[:, :half]
    v_ref[0] = y[:, half:]


def _linear_kv(x, w, tm=512):
    T, K = x.shape
    G, _, N2 = w.shape
    tm = _row_tile(T, tm)
    out = jax.ShapeDtypeStruct((G, T, N2 // 2), F32)
    o_spec = pl.BlockSpec((1, tm, N2 // 2), lambda g, i: (g, i, 0))
    return pl.pallas_call(
        _linear_kv_kernel,
        out_shape=(out, out),
        grid=(G, T // tm),
        in_specs=[pl.BlockSpec((tm, K), lambda g, i: (i, 0)), pl.BlockSpec((1, K, N2), lambda g, i: (g, 0, 0))],
        out_specs=(o_spec, o_spec),
        compiler_params=_params("parallel", "parallel"),
        name="linear_kv",
    )(x, w)


MXU_DIM = 256


def _w1_relayout_kernel(w_ref, o_ref):
    half = MXU_DIM // 2
    r = lax.broadcasted_iota(jnp.int32, (MXU_DIM, MXU_DIM), 0)
    c = lax.broadcasted_iota(jnp.int32, (MXU_DIM, MXU_DIM), 1)
    src = jnp.where(c < half, 2 * c, 2 * (c - half) + 1)
    perm = (r == src).astype(BF16)
    n_out = o_ref.shape[-1] // 2
    for j in range(w_ref.shape[-1] // MXU_DIM):
        y = jnp.dot(w_ref[0, :, j * MXU_DIM:(j + 1) * MXU_DIM].astype(BF16), perm, preferred_element_type=F32)
        o_ref[0, :, j * half:(j + 1) * half] = y[:, :half].astype(BF16)
        o_ref[0, :, n_out + j * half:n_out + (j + 1) * half] = y[:, half:].astype(BF16)


def _w1_relayout(w1):
    E, D, N = w1.shape
    spec = pl.BlockSpec((1, D, N), lambda e: (e, 0, 0))
    return pl.pallas_call(
        _w1_relayout_kernel,
        out_shape=jax.ShapeDtypeStruct((E, D, N), BF16),
        grid=(E,),
        in_specs=[spec],
        out_specs=spec,
        compiler_params=_params("parallel"),
        name="w1_relayout",
    )(w1)


def _gla_in_kernel(x_ref, w_ref, wg_ref, wg2_ref, bg_ref, q_ref, k_ref, v_ref, r_ref, b_ref,
                   *, tm, seq_len, seq_pad):
    xb = x_ref[...].astype(BF16)
    y = jnp.dot(xb, w_ref[...], preferred_element_type=F32)
    q_ref[...] = y[:, :GLA_QK] * (GLA_DK ** -0.5)
    k_ref[...] = y[:, GLA_QK:2 * GLA_QK]
    v_ref[...] = y[:, 2 * GLA_QK:2 * GLA_QK + GLA_V]
    r_ref[...] = y[:, 2 * GLA_QK + GLA_V:]
    g_lr = jnp.dot(xb, wg_ref[...], preferred_element_type=F32)
    z = _bdot(g_lr, wg2_ref[...]) + bg_ref[...]
    log_a = (jnp.minimum(z, 0.0) - jnp.log(1.0 + jnp.exp(-jnp.abs(z)))) / GLA_GATE_TEMP
    if seq_len < seq_pad:
        pos = (pl.program_id(0) * tm + lax.broadcasted_iota(jnp.int32, (tm, 1), 0)) % seq_pad
        log_a = jnp.where(pos < seq_len, log_a, 0.0)
    row = lax.broadcasted_iota(jnp.int32, (tm, tm), 0)
    col = lax.broadcasted_iota(jnp.int32, (tm, tm), 1)
    tri = ((row // CHUNK == col // CHUNK) & (row >= col)).astype(BF16)
    g1 = log_a.astype(BF16)
    r1 = log_a - g1.astype(F32)
    g2 = r1.astype(BF16)
    g3 = (r1 - g2.astype(F32)).astype(BF16)
    b_ref[...] = (jnp.dot(tri, g1, preferred_element_type=F32) + jnp.dot(tri, g2, preferred_element_type=F32)
                  + jnp.dot(tri, g3, preferred_element_type=F32))


def _gla_in(x, w_main, w_g, w_g2, b_g, seq_len, seq_pad, tm=512):
    T = x.shape[0]
    tm = _row_tile(T, tm)
    assert tm % CHUNK == 0 and seq_pad % CHUNK == 0
    n_main = w_main.shape[1]
    row = lambda n: pl.BlockSpec((tm, n), lambda i: (i, 0))
    return pl.pallas_call(
        functools.partial(_gla_in_kernel, tm=tm, seq_len=seq_len, seq_pad=seq_pad),
        out_shape=(jax.ShapeDtypeStruct((T, GLA_QK), F32), jax.ShapeDtypeStruct((T, GLA_QK), F32),
                   jax.ShapeDtypeStruct((T, GLA_V), F32), jax.ShapeDtypeStruct((T, GLA_V), F32),
                   jax.ShapeDtypeStruct((T, GLA_QK), F32)),
        grid=(T // tm,),
        in_specs=[row(D_MODEL), _full((D_MODEL, n_main)), _full((D_MODEL, LANES)),
                  _full((LANES, GLA_QK)), _full((1, GLA_QK))],
        out_specs=(row(GLA_QK), row(GLA_QK), row(GLA_V), row(GLA_V), row(GLA_QK)),
        compiler_params=_params("parallel"),
        name="gla_in",
    )(x, w_main, w_g, w_g2, b_g)


SUBLANES = 8
GLA_SUB = 16


def _gla_intra(q, k, b):
    n_grp = CHUNK // SUBLANES
    g_sub = GLA_SUB // SUBLANES
    lane = lax.broadcasted_iota(jnp.int32, (SUBLANES, LANES), 1)
    row = lax.broadcasted_iota(jnp.int32, (SUBLANES, LANES), 0)
    grp = lambda t, i: t[i * SUBLANES:(i + 1) * SUBLANES]
    att = [jnp.zeros((SUBLANES, LANES), F32) for _ in range(n_grp)]
    for s in range(CHUNK):
        b_s = b[s:s + 1, :]
        k_s = k[s:s + 1, :]
        for i in range(s // SUBLANES, (s // GLA_SUB + 1) * g_sub):
            e = jnp.exp(grp(b, i) - b_s)
            red = jnp.sum(grp(q, i) * e * k_s, axis=-1, keepdims=True)
            att[i] = jnp.where(lane == s, red, att[i])
    att = [jnp.where(row + i * SUBLANES >= lane, att[i], 0.0) for i in range(n_grp)]
    for blk in range(1, CHUNK // GLA_SUB):
        r0 = blk * GLA_SUB
        c = b[r0 - 1:r0, :]
        k_early = jnp.concatenate([k[:r0] * jnp.exp(c - b[:r0]), jnp.zeros((LANES - r0, GLA_DK), F32)], axis=0)
        off = _bdot_nt(q[r0:r0 + GLA_SUB] * jnp.exp(b[r0:r0 + GLA_SUB] - c), k_early)
        for i in range(g_sub):
            att[blk * g_sub + i] = att[blk * g_sub + i] + grp(off, i)
    return jnp.concatenate(att, axis=0)


def _gla_scan_kernel(q_ref, k_ref, v_ref, b_ref, s0_ref, o_ref, s_ref, st_scr, *, n_chunks):
    @pl.when(pl.program_id(1) == 0)
    def _():
        st_scr[...] = s0_ref[0]

    for c in range(n_chunks):
        ts = slice(c * CHUNK, (c + 1) * CHUNK)
        for h in range(GLA_HEADS):
            ks = slice(h * GLA_DK, (h + 1) * GLA_DK)
            vs = slice(h * GLA_DV, (h + 1) * GLA_DV)
            q = q_ref[0, ts, ks]
            k = k_ref[0, ts, ks]
            b = b_ref[0, ts, ks]
            v = v_ref[0, ts, vs]
            att = _gla_intra(q, k, b)
            b_end = b[CHUNK - 1:CHUNK, :]
            st = st_scr[h]
            o_ref[0, ts, vs] = _bdot(att[:, :CHUNK], v) + _bdot_nt(q * jnp.exp(b), st)
            st_scr[h] = st * jnp.exp(b_end) + _bdot_tn(v, k * jnp.exp(b_end - b))

    @pl.when(pl.program_id(1) == pl.num_programs(1) - 1)
    def _():
        s_ref[0] = st_scr[...]


def _gla_scan(q, k, v, la, s0t, tl=128):
    B, Lp, _ = q.shape
    tl = _row_tile(Lp, tl)
    qk_spec = pl.BlockSpec((1, tl, GLA_QK), lambda b, i: (b, i, 0))
    v_spec = pl.BlockSpec((1, tl, GLA_V), lambda b, i: (b, i, 0))
    s_spec = pl.BlockSpec((1, GLA_HEADS, GLA_DV, GLA_DK), lambda b, i: (b, 0, 0, 0))
    return pl.pallas_call(
        functools.partial(_gla_scan_kernel, n_chunks=tl // CHUNK),
        out_shape=(jax.ShapeDtypeStruct((B, Lp, GLA_V), F32),
                   jax.ShapeDtypeStruct((B, GLA_HEADS, GLA_DV, GLA_DK), F32)),
        grid=(B, Lp // tl),
        in_specs=[qk_spec, qk_spec, v_spec, qk_spec, s_spec],
        out_specs=(v_spec, s_spec),
        scratch_shapes=[pltpu.VMEM((GLA_HEADS, GLA_DV, GLA_DK), F32)],
        compiler_params=_params("parallel", "arbitrary"),
        name="gla_scan",
    )(q, k, v, la, s0t)


def _gla_out_kernel(o_ref, r_ref, x_ref, gn_ref, wo_ref, g_ref, b_ref, y_ref):
    parts = []
    for h in range(GLA_HEADS):
        vs = slice(h * GLA_DV, (h + 1) * GLA_DV)
        o = o_ref[:, vs]
        mu = jnp.mean(o, axis=-1, keepdims=True)
        oc = o - mu
        var = jnp.mean(oc * oc, axis=-1, keepdims=True)
        parts.append(oc * lax.rsqrt(var + LN_EPS) * gn_ref[:, vs])
    o = jnp.concatenate(parts, axis=-1)
    r = r_ref[...]
    o = o * (r * jax.nn.sigmoid(r))
    h = _bdot(o, wo_ref[...])
    y_ref[...] = _layer_norm(DEEPNORM_ALPHA * x_ref[...] + h, g_ref[...], b_ref[...])


def _gla_out(o, r, x, gn_g, w_o, ln_g, ln_b, tm=512):
    T = x.shape[0]
    tm = _row_tile(T, tm)
    row = pl.BlockSpec((tm, D_MODEL), lambda i: (i, 0))
    vec = _full((1, D_MODEL))
    return pl.pallas_call(
        _gla_out_kernel,
        out_shape=jax.ShapeDtypeStruct((T, D_MODEL), F32),
        grid=(T // tm,),
        in_specs=[row, row, row, vec, _full((GLA_V, D_MODEL)), vec, vec],
        out_specs=row,
        compiler_params=_params("parallel"),
        name="gla_out",
    )(o, r, x, gn_g, w_o, ln_g, ln_b)


def _swa_kernel(x_ref, k_ref, v_ref, wq_ref, sink_ref, wo_ref, g_ref, b_ref, y_ref,
                *, tl, seq_len, past_valid):
    x = x_ref[0]
    q = (_bdot(x, wq_ref[...]) * (SWA_HEAD_DIM ** -0.5)).astype(BF16)
    kw = tl + WINDOW
    r0 = pl.multiple_of(pl.program_id(1) * tl, tl)
    kt = k_ref[0, pl.ds(r0, kw), :].astype(BF16)
    vt = v_ref[0, pl.ds(r0, kw), :].astype(BF16)
    q_row = lax.broadcasted_iota(jnp.int32, (tl, kw), 0)
    k_col = lax.broadcasted_iota(jnp.int32, (tl, kw), 1)
    band_lo = (q_row // CHUNK) * CHUNK
    pos = r0 + k_col
    ok = (k_col >= band_lo) & (k_col < band_lo + WINDOW + CHUNK) & (pos - WINDOW < seq_len)
    if not past_valid:
        ok = ok & (pos >= WINDOW)
    zeros = jnp.zeros((kw, SWA_HEAD_DIM), BF16)
    pair_out = []
    for g in range(SWA_KV_HEADS):
        kg = kt[:, g * SWA_HEAD_DIM:(g + 1) * SWA_HEAD_DIM]
        vg = vt[:, g * SWA_HEAD_DIM:(g + 1) * SWA_HEAD_DIM]
        k_pad = (jnp.concatenate([kg, zeros], axis=1), jnp.concatenate([zeros, kg], axis=1))
        v_pad = (jnp.concatenate([vg, zeros], axis=1), jnp.concatenate([zeros, vg], axis=1))
        for pair in range(g * SWA_GROUP // 2, (g + 1) * SWA_GROUP // 2):
            qp = q[:, pair * LANES:(pair + 1) * LANES]
            sinks = [sink_ref[2 * pair + j:2 * pair + j + 1, :] for j in range(2)]
            s = [jnp.where(ok, _bdot_nt(qp, k_pad[j]), NEG_INF) for j in range(2)]
            m = [jnp.maximum(jnp.max(s[j], axis=-1, keepdims=True), sinks[j]) for j in range(2)]
            e = [jnp.exp(s[j] - m[j]) for j in range(2)]
            o = [_bdot(e[j], v_pad[j]) for j in range(2)]
            inv = [1.0 / (jnp.sum(e[j], axis=-1, keepdims=True) + jnp.exp(sinks[j] - m[j])) for j in range(2)]
            pair_out.append(o[0] * inv[0] + o[1] * inv[1])
    h_out = _bdot(jnp.concatenate(pair_out, axis=1), wo_ref[...])
    y_ref[0] = _layer_norm(DEEPNORM_ALPHA * x + h_out, g_ref[...], b_ref[...])


def _swa(x, k_all, v_all, w_q, sinks, w_o, ln_g, ln_b, seq_len, past_valid, tl=256):
    B, Lp, _ = x.shape
    tl = _row_tile(Lp, tl)
    x_spec = pl.BlockSpec((1, tl, D_MODEL), lambda b, i: (b, i, 0))
    kv_spec = pl.BlockSpec((1, WINDOW + Lp, SWA_KV), lambda b, i: (b, 0, 0))
    vec = _full((1, D_MODEL))
    return pl.pallas_call(
        functools.partial(_swa_kernel, tl=tl, seq_len=seq_len, past_valid=past_valid),
        out_shape=jax.ShapeDtypeStruct((B, Lp, D_MODEL), F32),
        grid=(B, Lp // tl),
        in_specs=[x_spec, kv_spec, kv_spec, _full((D_MODEL, D_MODEL)), _full((SWA_Q_HEADS, 1)),
                  _full((D_MODEL, D_MODEL)), vec, vec],
        out_specs=x_spec,
        compiler_params=_params("parallel", "arbitrary"),
        name="swa",
    )(x, k_all, v_all, w_q, sinks, w_o, ln_g, ln_b)


def _mem_kernel(x_ref, mk_ref, mv_ref, wq_ref, wo_ref, g_ref, b_ref, y_ref):
    x = x_ref[0]
    q = _bdot(x, wq_ref[...]) * (MEM_HEAD_DIM ** -0.5)
    parts = []
    for h in range(MEM_HEADS):
        hs = slice(h * MEM_HEAD_DIM, (h + 1) * MEM_HEAD_DIM)
        s = _bdot_nt(q[:, hs], mk_ref[0, :, hs])
        m = jnp.max(s, axis=-1, keepdims=True)
        e = jnp.exp(s - m)
        parts.append(_bdot(e, mv_ref[0, :, hs]) * (1.0 / jnp.sum(e, axis=-1, keepdims=True)))
    o = jnp.concatenate(parts, axis=-1)
    h_out = _bdot(o, wo_ref[...])
    y_ref[0] = _layer_norm(DEEPNORM_ALPHA * x + h_out, g_ref[...], b_ref[...])


def _mem_xattn(x, mk, mv, layer, w_q, w_o, ln_g, ln_b, tl=512):
    B, L, _ = x.shape
    tl = _row_tile(L, tl)
    x_spec = pl.BlockSpec((1, tl, D_MODEL), lambda b, i: (b, i, 0))
    m_spec = pl.BlockSpec((1, N_MEM, D_MODEL), lambda b, i: (layer * B + b, 0, 0))
    vec = _full((1, D_MODEL))
    return pl.pallas_call(
        _mem_kernel,
        out_shape=jax.ShapeDtypeStruct((B, L, D_MODEL), F32),
        grid=(B, L // tl),
        in_specs=[x_spec, m_spec, m_spec, _full((D_MODEL, D_MODEL)), _full((D_MODEL, D_MODEL)), vec, vec],
        out_specs=x_spec,
        compiler_params=_params("parallel", "arbitrary"),
        name="mem_xattn",
    )(x, mk, mv, w_q, w_o, ln_g, ln_b)


def _router_kernel(x_ref, wrt_ref, br_ref, idx_ref, gate_ref, pos_ref, cnt_ref, cnt_scr, *, tm):
    @pl.when(pl.program_id(0) == 0)
    def _():
        cnt_scr[...] = jnp.zeros_like(cnt_scr)

    logits = _bdot_nt(wrt_ref[...], x_ref[...]) + br_ref[...]
    expert = lax.broadcasted_iota(jnp.int32, (N_EXPERTS, tm), 0).astype(F32)
    sel = jnp.zeros((N_EXPERTS, tm), F32)
    l = logits
    vals, hits = [], []
    for j in range(TOP_K):
        m = jnp.max(l, axis=0, keepdims=True)
        idx = jnp.min(jnp.where(l == m, expert, float(N_EXPERTS)), axis=0, keepdims=True)
        hit = expert == idx
        sel = jnp.where(hit, 1.0, sel)
        l = jnp.where(hit, -jnp.inf, l)
        idx_ref[j:j + 1, :] = idx.astype(jnp.int32)
        vals.append(m)
        hits.append(hit)
    e = [jnp.exp(v - vals[0]) for v in vals]
    inv = 1.0 / (e[0] + e[1] + e[2] + e[3])
    r = lax.broadcasted_iota(jnp.int32, (tm, tm), 0)
    c = lax.broadcasted_iota(jnp.int32, (tm, tm), 1)
    before = jnp.dot(sel.astype(BF16), (r < c).astype(BF16), preferred_element_type=F32) + cnt_scr[:, 0:1]
    for j in range(TOP_K):
        gate_ref[j:j + 1, :] = e[j] * inv
        pos_ref[j:j + 1, :] = jnp.sum(jnp.where(hits[j], before, 0.0), axis=0, keepdims=True).astype(jnp.int32)
    cnt_scr[...] = cnt_scr[...] + jnp.sum(sel, axis=1, keepdims=True)
    cnt_ref[...] = cnt_scr[...].astype(jnp.int32)


def _router(x, w_rt, b_r, tm=512):
    T = x.shape[0]
    tm = _row_tile(T, tm)
    small = pl.BlockSpec((TOP_K, tm), lambda i: (0, i))
    return pl.pallas_call(
        functools.partial(_router_kernel, tm=tm),
        out_shape=(jax.ShapeDtypeStruct((TOP_K, T), jnp.int32), jax.ShapeDtypeStruct((TOP_K, T), F32),
                   jax.ShapeDtypeStruct((TOP_K, T), jnp.int32), jax.ShapeDtypeStruct((N_EXPERTS, LANES), jnp.int32)),
        grid=(T // tm,),
        in_specs=[pl.BlockSpec((tm, D_MODEL), lambda i: (i, 0)), _full((N_EXPERTS, D_MODEL)), _full((N_EXPERTS, 1))],
        out_specs=(small, small, small, _full((N_EXPERTS, LANES))),
        scratch_shapes=[pltpu.VMEM((N_EXPERTS, LANES), F32)],
        compiler_params=_params("arbitrary"),
        name="moe_router",
    )(x, w_rt, b_r)


ISSUE_UNROLL = 16


def _dispatch_kernel(pad_end_ref, dest_ref, x_ref, xs_ref, rows_scr, zero_scr, sem, zsem, *, tm, blk):
    @pl.when(pl.program_id(0) == 0)
    def _():
        zero_scr[...] = jnp.zeros_like(zero_scr)

        def fill(e):
            start = pl.multiple_of((pad_end_ref[e] - blk) * ROW_TILES, blk * ROW_TILES)
            return pltpu.make_async_copy(zero_scr, xs_ref.at[pl.ds(start, blk * ROW_TILES), :], zsem)

        def has_block(e):
            return pad_end_ref[e] > (pad_end_ref[e - 1] if e else 0)

        for e in range(N_EXPERTS):
            pl.when(has_block(e))(lambda e=e: fill(e).start())
        for e in range(N_EXPERTS):
            pl.when(has_block(e))(lambda e=e: fill(e).wait())

    _rows_to_tiles(rows_scr, x_ref[...])

    def tile(ref, row):
        return ref.at[pl.ds(pl.multiple_of(row * ROW_TILES, ROW_TILES), ROW_TILES), :]

    def issue(g, carry):
        base = pl.multiple_of(g * (ISSUE_UNROLL // TOP_K), ISSUE_UNROLL // TOP_K)
        for u in range(ISSUE_UNROLL // TOP_K):
            r = base + u
            for j in range(TOP_K):
                pltpu.make_async_copy(tile(rows_scr, r), tile(xs_ref, dest_ref[0, j, r]), sem).start(priority=j % 2)
        return carry

    lax.fori_loop(0, tm * TOP_K // ISSUE_UNROLL, issue, 0)
    for j in range(TOP_K):
        pltpu.make_async_copy(rows_scr, xs_ref.at[pl.ds(0, tm * ROW_TILES), :], sem).wait()


def _dispatch(x, dest_blocks, pad_end, n_slots, tm, blk):
    T = x.shape[0]
    return pl.pallas_call(
        functools.partial(_dispatch_kernel, tm=tm, blk=blk),
        out_shape=jax.ShapeDtypeStruct((n_slots * ROW_TILES, LANES), F32),
        grid_spec=pltpu.PrefetchScalarGridSpec(
            num_scalar_prefetch=1,
            grid=(T // tm,),
            in_specs=[pl.BlockSpec((1, TOP_K, tm), lambda i, pe: (i, 0, 0), memory_space=pltpu.SMEM),
                      pl.BlockSpec((tm, D_MODEL), lambda i, pe: (i, 0))],
            out_specs=pl.BlockSpec(memory_space=pl.ANY),
            scratch_shapes=[pltpu.VMEM((tm * ROW_TILES, LANES), F32), pltpu.VMEM((blk * ROW_TILES, LANES), F32),
                            pltpu.SemaphoreType.DMA, pltpu.SemaphoreType.DMA]),
        compiler_params=_params("arbitrary"),
        name="moe_dispatch",
    )(pad_end, dest_blocks, x)


def _collect_combine_kernel(idx_ref, idx_next_ref, src_ref, gate_ref, x_ref, g_ref, b_ref, y_ref, buf, sem, *, tm):
    i = pl.program_id(0)
    slot = i % 2

    def tile(ref, row):
        return ref.at[pl.ds(pl.multiple_of(row * ROW_TILES, ROW_TILES), ROW_TILES), :]

    def request(ids, s):
        def body(g, carry):
            base = pl.multiple_of(g * (ISSUE_UNROLL // TOP_K), ISSUE_UNROLL // TOP_K)
            for u in range(ISSUE_UNROLL // TOP_K):
                r = base + u
                for j in range(TOP_K):
                    pltpu.make_async_copy(tile(src_ref, ids[0, j, r]), tile(buf.at[s, j], r),
                                          sem.at[s]).start(priority=j % 2)
            return carry
        lax.fori_loop(0, tm * TOP_K // ISSUE_UNROLL, body, 0)

    @pl.when(i == 0)
    def _():
        request(idx_ref, 0)

    @pl.when(i + 1 < pl.num_programs(0))
    def _():
        request(idx_next_ref, 1 - slot)

    for j in range(TOP_K):
        pltpu.make_async_copy(src_ref.at[pl.ds(0, tm * ROW_TILES), :], buf.at[slot, j], sem.at[slot]).wait()
    gates = gate_ref[...]
    acc = DEEPNORM_ALPHA * x_ref[...]
    for j in range(TOP_K):
        acc = acc + gates[:, j:j + 1] * _tiles_to_rows(buf.at[slot, j], tm)
    y_ref[...] = _layer_norm(acc, g_ref[...], b_ref[...])


def _collect_combine(src, idx_blocks, gates, x, ln_g, ln_b, tm):
    T = x.shape[0]
    nt = T // tm
    row = pl.BlockSpec((tm, D_MODEL), lambda i: (i, 0))
    vec = _full((1, D_MODEL))
    return pl.pallas_call(
        functools.partial(_collect_combine_kernel, tm=tm),
        out_shape=jax.ShapeDtypeStruct((T, D_MODEL), F32),
        grid=(nt,),
        in_specs=[pl.BlockSpec((1, TOP_K, tm), lambda i: (i, 0, 0), memory_space=pltpu.SMEM),
                  pl.BlockSpec((1, TOP_K, tm), lambda i: (jnp.minimum(i + 1, nt - 1), 0, 0), memory_space=pltpu.SMEM),
                  pl.BlockSpec(memory_space=pl.ANY),
                  pl.BlockSpec((tm, TOP_K), lambda i: (i, 0)), row, vec, vec],
        out_specs=row,
        scratch_shapes=[pltpu.VMEM((2, TOP_K, tm * ROW_TILES, LANES), F32), pltpu.SemaphoreType.DMA((2,))],
        compiler_params=_params("arbitrary"),
        name="moe_collect_combine",
    )(idx_blocks, idx_blocks, src, gates, x, ln_g, ln_b)


def _expert_kernel(be_ref, nb_ref, xs_ref, w1_ref, b1_ref, w2_ref, b2_ref, o_ref, *, blk):
    @pl.when(pl.program_id(0) < nb_ref[0])
    def _():
        h = _bdot(_tiles_to_rows(xs_ref, blk), w1_ref[0]) + b1_ref[0]
        glu = jnp.minimum(h[:, :D_EXPERT], SWIGLU_LIMIT)
        lin = jnp.clip(h[:, D_EXPERT:], -SWIGLU_LIMIT, SWIGLU_LIMIT)
        a = glu * jax.nn.sigmoid(SWIGLU_ALPHA * glu) * (lin + 1.0)
        _rows_to_tiles(o_ref, _bdot(a, w2_ref[0]) + b2_ref[0])


def _experts(xs, block_e, n_used, w1, b1, w2, b2, blk):
    P = xs.shape[0] // ROW_TILES
    nb = P // blk

    def blk_map(i, be, nu):
        return (jnp.minimum(i, nu[0] - 1), 0)

    def w_map(i, be, nu):
        return (be[jnp.minimum(i, nu[0] - 1)], 0, 0)

    return pl.pallas_call(
        functools.partial(_expert_kernel, blk=blk),
        out_shape=jax.ShapeDtypeStruct((P * ROW_TILES, LANES), F32),
        grid_spec=pltpu.PrefetchScalarGridSpec(
            num_scalar_prefetch=2,
            grid=(nb,),
            in_specs=[pl.BlockSpec((blk * ROW_TILES, LANES), blk_map),
                      pl.BlockSpec((1, D_MODEL, 2 * D_EXPERT), w_map),
                      pl.BlockSpec((1, 1, 2 * D_EXPERT), w_map),
                      pl.BlockSpec((1, D_EXPERT, D_MODEL), w_map),
                      pl.BlockSpec((1, 1, D_MODEL), w_map)],
            out_specs=pl.BlockSpec((blk * ROW_TILES, LANES), blk_map)),
        compiler_params=_params("arbitrary"),
        name="moe_experts",
    )(block_e, n_used, xs, w1, b1, w2, b2)


def _moe(x, layer, w_r, b_r, w1, b1, w2, b2, ln_g, ln_b):
    T = x.shape[0]
    blk = 1024 if T >= 4096 else 128
    tm = _row_tile(T, 512)
    n_assign = T * TOP_K
    n_blocks = -(-(n_assign + N_EXPERTS * (blk - 1)) // blk)
    n_slots = n_blocks * blk

    top_i, gates, pos, counts = _router(x, w_r, b_r)
    counts = counts[:, 0]
    padded = (counts + blk - 1) // blk * blk
    pad_end = jnp.cumsum(padded)
    pad_start = pad_end - padded
    is_e = top_i[None] == jnp.arange(N_EXPERTS, dtype=jnp.int32)[:, None, None]
    dest = pos + jnp.sum(jnp.where(is_e, pad_start.astype(jnp.int32)[:, None, None], 0), axis=0)
    block_start = jnp.arange(n_blocks, dtype=jnp.int32) * blk
    block_e = jnp.minimum(jnp.sum(pad_end[None, :] <= block_start[:, None], axis=1), N_EXPERTS - 1).astype(jnp.int32)
    n_used = (pad_end[-1:] // blk).astype(jnp.int32)
    dest_blocks = dest.reshape(TOP_K, T // tm, tm).transpose(1, 0, 2)

    xs = _dispatch(x, dest_blocks, pad_end.astype(jnp.int32), n_slots, tm, blk)
    out = _experts(xs, block_e + layer * N_EXPERTS, n_used, w1, b1, w2, b2, blk)
    return _collect_combine(out, dest_blocks, gates.T, x, ln_g, ln_b, tm)


def _pad_rows(t, n):
    return t if n == 0 else jnp.pad(t, ((0, 0), (0, n), (0, 0)))


def _trunk(x, gla_s0, past_k, past_v, past_valid, mem_k, mem_v, w):
    B, L, _ = x.shape
    Lp = -(-L // CHUNK) * CHUNK
    T = B * L
    vec = lambda a: a.reshape(1, -1)
    gla_states = []
    k_all = v_all = new_k = new_v = None
    for layer in range(DEPTH):
        lg, lb = w['ln_g'][layer], w['ln_b'][layer]
        xt = x.reshape(T, D_MODEL)
        if layer < N_A:
            i = layer
            xt = _pad_rows(x, Lp - L).reshape(B * Lp, D_MODEL)
            q, k, v, r, b = _gla_in(xt, w['gla_w_main'][i], w['gla_w_g'][i], w['gla_w_g2'][i],
                                    vec(w['gla_b_g'][i]), L, Lp)
            seq = lambda t: t.reshape(B, Lp, -1)
            s0t = jnp.swapaxes(gla_s0[i], -1, -2)
            o, st = _gla_scan(seq(q), seq(k), seq(v), seq(b), s0t)
            gla_states.append(jnp.swapaxes(st, -1, -2))
            xt = _gla_out(o.reshape(B * Lp, GLA_V), r, xt, vec(w['gla_gn_g'][i]), w['gla_w_o'][i],
                          vec(lg[0]), vec(lb[0]))
            x = xt.reshape(B, Lp, D_MODEL)[:, :L]
        else:
            j = layer - N_A
            if j == 0:
                k_new, v_new = _linear_kv(xt, w['kv_w'][None])
                k_full = jnp.concatenate([past_k, k_new.reshape(B, L, SWA_KV)], axis=1)
                v_full = jnp.concatenate([past_v, v_new.reshape(B, L, SWA_KV)], axis=1)
                new_k, new_v = k_full[:, -WINDOW:], v_full[:, -WINDOW:]
                k_all, v_all = _pad_rows(k_full, Lp - L), _pad_rows(v_full, Lp - L)
            xp = _swa(_pad_rows(x, Lp - L), k_all, v_all, w['swa_w_q'][j], w['swa_sinks'][j].reshape(-1, 1),
                      w['swa_w_o'][j], vec(lg[0]), vec(lb[0]), L, past_valid)
            x = xp[:, :L]
        x = _mem_xattn(x, mem_k, mem_v, layer, w['mem_w_q'][layer], w['mem_w_o'][layer], vec(lg[1]), vec(lb[1]))
        xt = _moe(x.reshape(T, D_MODEL), layer, w['moe_w_r'][layer], w['moe_b_r'][layer], w['moe_w1'],
                  w['moe_b1'], w['moe_w2'], w['moe_b2'], vec(lg[2]), vec(lb[2]))
        x = xt.reshape(B, L, D_MODEL)
    return x, jnp.stack(gla_states), new_k, new_v


def _prep_weights(gla_w_in, gla_w_g2, gla_b_g, gla_gn_g, gla_w_o, kv_w, swa_w_q, swa_sinks, swa_w_o,
                  mem_w_q, mem_w_kv, mem_w_o, moe_w_r, moe_b_r, moe_w1, moe_b1, moe_w2, moe_b2, ln_g, ln_b):
    n_main = 2 * GLA_QK + 2 * GLA_V
    pad_c = lambda a, n: jnp.pad(a, [(0, 0)] * (a.ndim - 1) + [(0, n - a.shape[-1])])
    w = dict(
        gla_w_main=gla_w_in[:, :, :n_main].astype(BF16),
        gla_w_g=pad_c(gla_w_in[:, :, n_main:], LANES).astype(BF16),
        gla_w_g2=jnp.pad(gla_w_g2, ((0, 0), (0, LANES - GLA_GATE_RANK), (0, 0))).astype(BF16),
        gla_b_g=gla_b_g, gla_gn_g=gla_gn_g.reshape(N_A, GLA_V), gla_w_o=gla_w_o.astype(BF16),
        kv_w=kv_w.astype(BF16), swa_w_q=swa_w_q.astype(BF16), swa_sinks=swa_sinks, swa_w_o=swa_w_o.astype(BF16),
        mem_w_q=mem_w_q.astype(BF16), mem_w_kv=mem_w_kv.astype(BF16), mem_w_o=mem_w_o.astype(BF16),
        moe_w_r=jnp.swapaxes(moe_w_r, -1, -2).astype(BF16), moe_b_r=moe_b_r.reshape(DEPTH, N_EXPERTS, 1),
        moe_w1=_w1_relayout(moe_w1.reshape(DEPTH * N_EXPERTS, D_MODEL, 2 * D_EXPERT)),
        moe_b1=jnp.concatenate([moe_b1[..., 0::2], moe_b1[..., 1::2]], axis=-1).reshape(DEPTH * N_EXPERTS, 1, -1),
        moe_w2=moe_w2.astype(BF16).reshape(DEPTH * N_EXPERTS, D_EXPERT, D_MODEL),
        moe_b2=moe_b2.reshape(DEPTH * N_EXPERTS, 1, D_MODEL),
        ln_g=ln_g, ln_b=ln_b)
    return w


def kernel(x_prompt, x_sample, state_gla, cache_swa_k, cache_swa_v, cache_mem_k, cache_mem_v, mem_prompt, gla_w_in, gla_w_g2, gla_b_g, gla_gn_g, gla_w_o, kv_w, swa_w_q, swa_sinks, swa_w_o, mem_w_q, mem_w_kv, mem_w_o, moe_w_r, moe_b_r, moe_w1, moe_b1, moe_w2, moe_b2, ln_g, ln_b):
    w = _prep_weights(gla_w_in, gla_w_g2, gla_b_g, gla_gn_g, gla_w_o, kv_w, swa_w_q, swa_sinks, swa_w_o,
                      mem_w_q, mem_w_kv, mem_w_o, moe_w_r, moe_b_r, moe_w1, moe_b1, moe_w2, moe_b2, ln_g, ln_b)
    bp = x_prompt.shape[0]
    bs = x_sample.shape[0]
    mem_flat = mem_prompt.reshape(bp * N_MEM, D_MODEL)
    mem_k_p, mem_v_p = _linear_kv(mem_flat, w['mem_w_kv'])
    gla0 = jnp.zeros((N_A, bp, GLA_HEADS, GLA_DK, GLA_DV), F32)
    zero_win = jnp.zeros((bp, WINDOW, SWA_KV), F32)
    y_p, gla_p, k_p, v_p = _trunk(x_prompt, gla0, zero_win, zero_win, False,
                                  mem_k_p.reshape(DEPTH * bp, N_MEM, D_MODEL),
                                  mem_v_p.reshape(DEPTH * bp, N_MEM, D_MODEL), w)
    y_s, gla_s, k_s, v_s = _trunk(x_sample, state_gla, cache_swa_k.reshape(bs, WINDOW, SWA_KV),
                                  cache_swa_v.reshape(bs, WINDOW, SWA_KV), True,
                                  cache_mem_k.reshape(DEPTH * bs, N_MEM, D_MODEL),
                                  cache_mem_v.reshape(DEPTH * bs, N_MEM, D_MODEL), w)
    heads4 = lambda t: t.reshape(t.shape[0], WINDOW, SWA_KV_HEADS, SWA_HEAD_DIM)
    mem5 = lambda t: t.reshape(DEPTH, bp, N_MEM, MEM_HEADS, MEM_HEAD_DIM)
    return (y_p, y_s, gla_p, gla_s, heads4(k_p), heads4(v_p), heads4(k_s), heads4(v_s), mem5(mem_k_p), mem5(mem_v_p))
```

```python
import functools

import jax
import jax.numpy as jnp
from jax import lax
from jax.experimental import pallas as pl
from jax.experimental.pallas import tpu as pltpu

F32 = jnp.float32
BF16 = jnp.bfloat16

D_MODEL = 1024
DEPTH = 4
CHUNK = 64
N_A = DEPTH // 2
GLA_HEADS = 4
GLA_DK = D_MODEL // (2 * GLA_HEADS)
GLA_DV = D_MODEL // GLA_HEADS
GLA_QK = GLA_HEADS * GLA_DK
GLA_V = GLA_HEADS * GLA_DV
GLA_GATE_RANK = 16
GLA_GATE_TEMP = 16.0
SWA_HEAD_DIM = 64
SWA_Q_HEADS = D_MODEL // SWA_HEAD_DIM
SWA_KV_HEADS = 4
SWA_GROUP = SWA_Q_HEADS // SWA_KV_HEADS
SWA_KV = SWA_KV_HEADS * SWA_HEAD_DIM
WINDOW = 128
WINDOW_CHUNKS = WINDOW // CHUNK
N_MEM = 256
MEM_HEADS = 4
MEM_HEAD_DIM = D_MODEL // MEM_HEADS
N_EXPERTS = 32
TOP_K = 4
D_EXPERT = D_MODEL
SWIGLU_ALPHA = 1.702
SWIGLU_LIMIT = 7.0
DEEPNORM_ALPHA = (2 * DEPTH) ** 0.25
LN_EPS = 1e-5
NEG_INF = -1e30

LANES = 128
VMEM_LIMIT = 56 * 1024 * 1024


def _params(*sem):
    return pltpu.CompilerParams(dimension_semantics=sem, vmem_limit_bytes=VMEM_LIMIT)


def _row_tile(n_rows, want):
    t = min(want, n_rows)
    while n_rows % t:
        t //= 2
    return t


def _full(shape):
    nd = len(shape)
    return pl.BlockSpec(shape, lambda *_: (0,) * nd)


def _bdot(a, b):
    return jnp.dot(a.astype(BF16), b.astype(BF16), preferred_element_type=F32)


def _bdot_nt(a, b):
    return lax.dot_general(a.astype(BF16), b.astype(BF16), (((1,), (1,)), ((), ())),
                           preferred_element_type=F32)


def _bdot_tn(a, b):
    return lax.dot_general(a.astype(BF16), b.astype(BF16), (((0,), (0,)), ((), ())),
                           preferred_element_type=F32)


def _layer_norm(z, g, b):
    mu = jnp.mean(z, axis=-1, keepdims=True)
    zc = z - mu
    var = jnp.mean(zc * zc, axis=-1, keepdims=True)
    return zc * lax.rsqrt(var + LN_EPS) * g + b


ROW_TILES = D_MODEL // LANES


def _rows_to_tiles(ref, x):
    n = x.shape[0]
    for c in range(ROW_TILES):
        ref[pl.ds(c, n, stride=ROW_TILES), :] = x[:, c * LANES:(c + 1) * LANES]


def _tiles_to_rows(ref, n):
    return jnp.concatenate([ref[pl.ds(c, n, stride=ROW_TILES), :] for c in range(ROW_TILES)], axis=1)


def _linear_kv_kernel(x_ref, w_ref, k_ref, v_ref):
    y = _bdot(x_ref[...], w_ref[0])
    half = y.shape[1] // 2
    k_ref[0] = y[:, :half]
    v_ref[0] = y[:, half:]


def _linear_kv(x, w, tm=512):
    T, K = x.shape
    G, _, N2 = w.shape
    tm = _row_tile(T, tm)
    out = jax.ShapeDtypeStruct((G, T, N2 // 2), F32)
    o_spec = pl.BlockSpec((1, tm, N2 // 2), lambda g, i: (g, i, 0))
    return pl.pallas_call(
        _linear_kv_kernel,
        out_shape=(out, out),
        grid=(G, T // tm),
        in_specs=[pl.BlockSpec((tm, K), lambda g, i: (i, 0)), pl.BlockSpec((1, K, N2), lambda g, i: (g, 0, 0))],
        out_specs=(o_spec, o_spec),
        compiler_params=_params("parallel", "parallel"),
        name="linear_kv",
    )(x, w)


MXU_DIM = 256


def _w1_relayout_kernel(w_ref, o_ref):
    half = MXU_DIM // 2
    r = lax.broadcasted_iota(jnp.int32, (MXU_DIM, MXU_DIM), 0)
    c = lax.broadcasted_iota(jnp.int32, (MXU_DIM, MXU_DIM), 1)
    src = jnp.where(c < half, 2 * c, 2 * (c - half) + 1)
    perm = (r == src).astype(BF16)
    n_out = o_ref.shape[-1] // 2
    for j in range(w_ref.shape[-1] // MXU_DIM):
        y = jnp.dot(w_ref[0, :, j * MXU_DIM:(j + 1) * MXU_DIM].astype(BF16), perm, preferred_element_type=F32)
        o_ref[0, :, j * half:(j + 1) * half] = y[:, :half].astype(BF16)
        o_ref[0, :, n_out + j * half:n_out + (j + 1) * half] = y[:, half:].astype(BF16)


def _w1_relayout(w1):
    E, D, N = w1.shape
    spec = pl.BlockSpec((1, D, N), lambda e: (e, 0, 0))
    return pl.pallas_call(
        _w1_relayout_kernel,
        out_shape=jax.ShapeDtypeStruct((E, D, N), BF16),
        grid=(E,),
        in_specs=[spec],
        out_specs=spec,
        compiler_params=_params("parallel"),
        name="w1_relayout",
    )(w1)


def _gla_in_kernel(x_ref, w_ref, wg_ref, wg2_ref, bg_ref, q_ref, k_ref, v_ref, r_ref, b_ref,
                   *, tm, seq_len, seq_pad):
    xb = x_ref[...].astype(BF16)
    y = jnp.dot(xb, w_ref[...], preferred_element_type=F32)
    q_ref[...] = y[:, :GLA_QK] * (GLA_DK ** -0.5)
    k_ref[...] = y[:, GLA_QK:2 * GLA_QK]
    v_ref[...] = y[:, 2 * GLA_QK:2 * GLA_QK + GLA_V]
    r_ref[...] = y[:, 2 * GLA_QK + GLA_V:]
    g_lr = jnp.dot(xb, wg_ref[...], preferred_element_type=F32)
    z = _bdot(g_lr, wg2_ref[...]) + bg_ref[...]
    log_a = (jnp.minimum(z, 0.0) - jnp.log(1.0 + jnp.exp(-jnp.abs(z)))) / GLA_GATE_TEMP
    if seq_len < seq_pad:
        pos = (pl.program_id(0) * tm + lax.broadcasted_iota(jnp.int32, (tm, 1), 0)) % seq_pad
        log_a = jnp.where(pos < seq_len, log_a, 0.0)
    row = lax.broadcasted_iota(jnp.int32, (tm, tm), 0)
    col = lax.broadcasted_iota(jnp.int32, (tm, tm), 1)
    tri = ((row // CHUNK == col // CHUNK) & (row >= col)).astype(BF16)
    g1 = log_a.astype(BF16)
    r1 = log_a - g1.astype(F32)
    g2 = r1.astype(BF16)
    g3 = (r1 - g2.astype(F32)).astype(BF16)
    b_ref[...] = (jnp.dot(tri, g1, preferred_element_type=F32) + jnp.dot(tri, g2, preferred_element_type=F32)
                  + jnp.dot(tri, g3, preferred_element_type=F32))


def _gla_in(x, w_main, w_g, w_g2, b_g, seq_len, seq_pad, tm=512):
    T = x.shape[0]
    tm = _row_tile(T, tm)
    assert tm % CHUNK == 0 and seq_pad % CHUNK == 0
    n_main = w_main.shape[1]
    row = lambda n: pl.BlockSpec((tm, n), lambda i: (i, 0))
    return pl.pallas_call(
        functools.partial(_gla_in_kernel, tm=tm, seq_len=seq_len, seq_pad=seq_pad),
        out_shape=(jax.ShapeDtypeStruct((T, GLA_QK), F32), jax.ShapeDtypeStruct((T, GLA_QK), F32),
                   jax.ShapeDtypeStruct((T, GLA_V), F32), jax.ShapeDtypeStruct((T, GLA_V), F32),
                   jax.ShapeDtypeStruct((T, GLA_QK), F32)),
        grid=(T // tm,),
        in_specs=[row(D_MODEL), _full((D_MODEL, n_main)), _full((D_MODEL, LANES)),
                  _full((LANES, GLA_QK)), _full((1, GLA_QK))],
        out_specs=(row(GLA_QK), row(GLA_QK), row(GLA_V), row(GLA_V), row(GLA_QK)),
        compiler_params=_params("parallel"),
        name="gla_in",
    )(x, w_main, w_g, w_g2, b_g)


SUBLANES = 8
GLA_SUB = 16


def _gla_intra(q, k, b):
    n_grp = CHUNK // SUBLANES
    g_sub = GLA_SUB // SUBLANES
    lane = lax.broadcasted_iota(jnp.int32, (SUBLANES, LANES), 1)
    row = lax.broadcasted_iota(jnp.int32, (SUBLANES, LANES), 0)
    grp = lambda t, i: t[i * SUBLANES:(i + 1) * SUBLANES]
    att = [jnp.zeros((SUBLANES, LANES), F32) for _ in range(n_grp)]
    for s in range(CHUNK):
        b_s = b[s:s + 1, :]
        k_s = k[s:s + 1, :]
        for i in range(s // SUBLANES, (s // GLA_SUB + 1) * g_sub):
            e = jnp.exp(grp(b, i) - b_s)
            red = jnp.sum(grp(q, i) * e * k_s, axis=-1, keepdims=True)
            att[i] = jnp.where(lane == s, red, att[i])
    att = [jnp.where(row + i * SUBLANES >= lane, att[i], 0.0) for i in range(n_grp)]
    for blk in range(1, CHUNK // GLA_SUB):
        r0 = blk * GLA_SUB
        c = b[r0 - 1:r0, :]
        k_early = jnp.concatenate([k[:r0] * jnp.exp(c - b[:r0]), jnp.zeros((LANES - r0, GLA_DK), F32)], axis=0)
        off = _bdot_nt(q[r0:r0 + GLA_SUB] * jnp.exp(b[r0:r0 + GLA_SUB] - c), k_early)
        for i in range(g_sub):
            att[blk * g_sub + i] = att[blk * g_sub + i] + grp(off, i)
    return jnp.concatenate(att, axis=0)


def _gla_scan_kernel(q_ref, k_ref, v_ref, b_ref, s0_ref, o_ref, s_ref, st_scr, *, n_chunks):
    @pl.when(pl.program_id(1) == 0)
    def _():
        st_scr[...] = s0_ref[0]

    for c in range(n_chunks):
        ts = slice(c * CHUNK, (c + 1) * CHUNK)
        for h in range(GLA_HEADS):
            ks = slice(h * GLA_DK, (h + 1) * GLA_DK)
            vs = slice(h * GLA_DV, (h + 1) * GLA_DV)
            q = q_ref[0, ts, ks]
            k = k_ref[0, ts, ks]
            b = b_ref[0, ts, ks]
            v = v_ref[0, ts, vs]
            att = _gla_intra(q, k, b)
            b_end = b[CHUNK - 1:CHUNK, :]
            st = st_scr[h]
            o_ref[0, ts, vs] = _bdot(att[:, :CHUNK], v) + _bdot_nt(q * jnp.exp(b), st)
            st_scr[h] = st * jnp.exp(b_end) + _bdot_tn(v, k * jnp.exp(b_end - b))

    @pl.when(pl.program_id(1) == pl.num_programs(1) - 1)
    def _():
        s_ref[0] = st_scr[...]


def _gla_scan(q, k, v, la, s0t, tl=128):
    B, Lp, _ = q.shape
    tl = _row_tile(Lp, tl)
    qk_spec = pl.BlockSpec((1, tl, GLA_QK), lambda b, i: (b, i, 0))
    v_spec = pl.BlockSpec((1, tl, GLA_V), lambda b, i: (b, i, 0))
    s_spec = pl.BlockSpec((1, GLA_HEADS, GLA_DV, GLA_DK), lambda b, i: (b, 0, 0, 0))
    return pl.pallas_call(
        functools.partial(_gla_scan_kernel, n_chunks=tl // CHUNK),
        out_shape=(jax.ShapeDtypeStruct((B, Lp, GLA_V), F32),
                   jax.ShapeDtypeStruct((B, GLA_HEADS, GLA_DV, GLA_DK), F32)),
        grid=(B, Lp // tl),
        in_specs=[qk_spec, qk_spec, v_spec, qk_spec, s_spec],
        out_specs=(v_spec, s_spec),
        scratch_shapes=[pltpu.VMEM((GLA_HEADS, GLA_DV, GLA_DK), F32)],
        compiler_params=_params("parallel", "arbitrary"),
        name="gla_scan",
    )(q, k, v, la, s0t)


def _gla_out_kernel(o_ref, r_ref, x_ref, gn_ref, wo_ref, g_ref, b_ref, y_ref):
    parts = []
    for h in range(GLA_HEADS):
        vs = slice(h * GLA_DV, (h + 1) * GLA_DV)
        o = o_ref[:, vs]
        mu = jnp.mean(o, axis=-1, keepdims=True)
        oc = o - mu
        var = jnp.mean(oc * oc, axis=-1, keepdims=True)
        parts.append(oc * lax.rsqrt(var + LN_EPS) * gn_ref[:, vs])
    o = jnp.concatenate(parts, axis=-1)
    r = r_ref[...]
    o = o * (r * jax.nn.sigmoid(r))
    h = _bdot(o, wo_ref[...])
    y_ref[...] = _layer_norm(DEEPNORM_ALPHA * x_ref[...] + h, g_ref[...], b_ref[...])


def _gla_out(o, r, x, gn_g, w_o, ln_g, ln_b, tm=512):
    T = x.shape[0]
    tm = _row_tile(T, tm)
    row = pl.BlockSpec((tm, D_MODEL), lambda i: (i, 0))
    vec = _full((1, D_MODEL))
    return pl.pallas_call(
        _gla_out_kernel,
        out_shape=jax.ShapeDtypeStruct((T, D_MODEL), F32),
        grid=(T // tm,),
        in_specs=[row, row, row, vec, _full((GLA_V, D_MODEL)), vec, vec],
        out_specs=row,
        compiler_params=_params("parallel"),
        name="gla_out",
    )(o, r, x, gn_g, w_o, ln_g, ln_b)


def _swa_kernel(x_ref, k_ref, v_ref, wq_ref, sink_ref, wo_ref, g_ref, b_ref, y_ref,
                *, tl, seq_len, past_valid):
    x = x_ref[0]
    q = (_bdot(x, wq_ref[...]) * (SWA_HEAD_DIM ** -0.5)).astype(BF16)
    kw = tl + WINDOW
    r0 = pl.multiple_of(pl.program_id(1) * tl, tl)
    kt = k_ref[0, pl.ds(r0, kw), :].astype(BF16)
    vt = v_ref[0, pl.ds(r0, kw), :].astype(BF16)
    qs = min(tl, WINDOW)
    kws = qs + WINDOW
    q_row = lax.broadcasted_iota(jnp.int32, (qs, kws), 0)
    k_col = lax.broadcasted_iota(jnp.int32, (qs, kws), 1)
    band_lo = (q_row // CHUNK) * CHUNK
    in_band = (k_col >= band_lo) & (k_col < band_lo + WINDOW + CHUNK)
    oks = []
    for u in range(tl // qs):
        pos = r0 + u * qs + k_col
        ok = in_band & (pos - WINDOW < seq_len)
        oks.append(ok if past_valid else ok & (pos >= WINDOW))
    zeros = jnp.zeros((kw, SWA_HEAD_DIM), BF16)
    pair_out = [[] for _ in range(SWA_Q_HEADS // 2)]
    for g in range(SWA_KV_HEADS):
        kg = kt[:, g * SWA_HEAD_DIM:(g + 1) * SWA_HEAD_DIM]
        vg = vt[:, g * SWA_HEAD_DIM:(g + 1) * SWA_HEAD_DIM]
        k_pad = (jnp.concatenate([kg, zeros], axis=1), jnp.concatenate([zeros, kg], axis=1))
        v_pad = (jnp.concatenate([vg, zeros], axis=1), jnp.concatenate([zeros, vg], axis=1))
        for pair in range(g * SWA_GROUP // 2, (g + 1) * SWA_GROUP // 2):
            sinks = [sink_ref[2 * pair + j:2 * pair + j + 1, :] for j in range(2)]
            work = [(u, j) for u in range(tl // qs) for j in range(2)]
            kwin = [slice(u * qs, u * qs + kws) for u in range(tl // qs)]
            qp = [q[u * qs:(u + 1) * qs, pair * LANES:(pair + 1) * LANES] for u in range(tl // qs)]
            s = [jnp.where(oks[u], _bdot_nt(qp[u], k_pad[j][kwin[u]]), NEG_INF) for u, j in work]
            m = [jnp.maximum(jnp.max(s_i, axis=-1, keepdims=True), sinks[j]) for s_i, (u, j) in zip(s, work)]
            e = [jnp.exp(s_i - m_i) for s_i, m_i in zip(s, m)]
            o = [_bdot(e_i, v_pad[j][kwin[u]]) for e_i, (u, j) in zip(e, work)]
            inv = [1.0 / (jnp.sum(e_i, axis=-1, keepdims=True) + jnp.exp(sinks[j] - m_i))
                   for e_i, m_i, (u, j) in zip(e, m, work)]
            on = [o_i * inv_i for o_i, inv_i in zip(o, inv)]
            for u in range(tl // qs):
                pair_out[pair].append(on[2 * u] + on[2 * u + 1])
    pair_out = [jnp.concatenate(p, axis=0) for p in pair_out]
    h_out = _bdot(jnp.concatenate(pair_out, axis=1), wo_ref[...])
    y_ref[0] = _layer_norm(DEEPNORM_ALPHA * x + h_out, g_ref[...], b_ref[...])


def _swa(x, k_all, v_all, w_q, sinks, w_o, ln_g, ln_b, seq_len, past_valid, tl=256):
    B, Lp, _ = x.shape
    tl = _row_tile(Lp, tl)
    x_spec = pl.BlockSpec((1, tl, D_MODEL), lambda b, i: (b, i, 0))
    kv_spec = pl.BlockSpec((1, WINDOW + Lp, SWA_KV), lambda b, i: (b, 0, 0))
    vec = _full((1, D_MODEL))
    return pl.pallas_call(
        functools.partial(_swa_kernel, tl=tl, seq_len=seq_len, past_valid=past_valid),
        out_shape=jax.ShapeDtypeStruct((B, Lp, D_MODEL), F32),
        grid=(B, Lp // tl),
        in_specs=[x_spec, kv_spec, kv_spec, _full((D_MODEL, D_MODEL)), _full((SWA_Q_HEADS, 1)),
                  _full((D_MODEL, D_MODEL)), vec, vec],
        out_specs=x_spec,
        compiler_params=_params("parallel", "arbitrary"),
        name="swa",
    )(x, k_all, v_all, w_q, sinks, w_o, ln_g, ln_b)


def _mem_kernel(x_ref, mk_ref, mv_ref, wq_ref, wo_ref, g_ref, b_ref, y_ref):
    x = x_ref[0]
    q = _bdot(x, wq_ref[...]) * (MEM_HEAD_DIM ** -0.5)
    parts = []
    for h in range(MEM_HEADS):
        hs = slice(h * MEM_HEAD_DIM, (h + 1) * MEM_HEAD_DIM)
        s = _bdot_nt(q[:, hs], mk_ref[0, :, hs])
        m = jnp.max(s, axis=-1, keepdims=True)
        e = jnp.exp(s - m)
        parts.append(_bdot(e, mv_ref[0, :, hs]) * (1.0 / jnp.sum(e, axis=-1, keepdims=True)))
    o = jnp.concatenate(parts, axis=-1)
    h_out = _bdot(o, wo_ref[...])
    y_ref[0] = _layer_norm(DEEPNORM_ALPHA * x + h_out, g_ref[...], b_ref[...])


def _mem_xattn(x, mk, mv, layer, w_q, w_o, ln_g, ln_b, tl=512):
    B, L, _ = x.shape
    tl = _row_tile(L, tl)
    x_spec = pl.BlockSpec((1, tl, D_MODEL), lambda b, i: (b, i, 0))
    m_spec = pl.BlockSpec((1, N_MEM, D_MODEL), lambda b, i: (layer * B + b, 0, 0))
    vec = _full((1, D_MODEL))
    return pl.pallas_call(
        _mem_kernel,
        out_shape=jax.ShapeDtypeStruct((B, L, D_MODEL), F32),
        grid=(B, L // tl),
        in_specs=[x_spec, m_spec, m_spec, _full((D_MODEL, D_MODEL)), _full((D_MODEL, D_MODEL)), vec, vec],
        out_specs=x_spec,
        compiler_params=_params("parallel", "arbitrary"),
        name="mem_xattn",
    )(x, mk, mv, w_q, w_o, ln_g, ln_b)


def _router_kernel(x_ref, wrt_ref, br_ref, idx_ref, gate_ref, pos_ref, cnt_ref, cnt_scr, *, tm):
    @pl.when(pl.program_id(0) == 0)
    def _():
        cnt_scr[...] = jnp.zeros_like(cnt_scr)

    logits = _bdot_nt(wrt_ref[...], x_ref[...]) + br_ref[...]
    expert = lax.broadcasted_iota(jnp.int32, (N_EXPERTS, tm), 0).astype(F32)
    sel = jnp.zeros((N_EXPERTS, tm), F32)
    l = logits
    vals, hits = [], []
    for j in range(TOP_K):
        m = jnp.max(l, axis=0, keepdims=True)
        idx = jnp.min(jnp.where(l == m, expert, float(N_EXPERTS)), axis=0, keepdims=True)
        hit = expert == idx
        sel = jnp.where(hit, 1.0, sel)
        l = jnp.where(hit, -jnp.inf, l)
        idx_ref[j:j + 1, :] = idx.astype(jnp.int32)
        vals.append(m)
        hits.append(hit)
    e = [jnp.exp(v - vals[0]) for v in vals]
    inv = 1.0 / (e[0] + e[1] + e[2] + e[3])
    r = lax.broadcasted_iota(jnp.int32, (tm, tm), 0)
    c = lax.broadcasted_iota(jnp.int32, (tm, tm), 1)
    before = jnp.dot(sel.astype(BF16), (r < c).astype(BF16), preferred_element_type=F32) + cnt_scr[:, 0:1]
    for j in range(TOP_K):
        gate_ref[j:j + 1, :] = e[j] * inv
        pos_ref[j:j + 1, :] = jnp.sum(jnp.where(hits[j], before, 0.0), axis=0, keepdims=True).astype(jnp.int32)
    cnt_scr[...] = cnt_scr[...] + jnp.sum(sel, axis=1, keepdims=True)
    cnt_ref[...] = cnt_scr[...].astype(jnp.int32)


def _router(x, w_rt, b_r, tm=512):
    T = x.shape[0]
    tm = _row_tile(T, tm)
    small = pl.BlockSpec((TOP_K, tm), lambda i: (0, i))
    return pl.pallas_call(
        functools.partial(_router_kernel, tm=tm),
        out_shape=(jax.ShapeDtypeStruct((TOP_K, T), jnp.int32), jax.ShapeDtypeStruct((TOP_K, T), F32),
                   jax.ShapeDtypeStruct((TOP_K, T), jnp.int32), jax.ShapeDtypeStruct((N_EXPERTS, LANES), jnp.int32)),
        grid=(T // tm,),
        in_specs=[pl.BlockSpec((tm, D_MODEL), lambda i: (i, 0)), _full((N_EXPERTS, D_MODEL)), _full((N_EXPERTS, 1))],
        out_specs=(small, small, small, _full((N_EXPERTS, LANES))),
        scratch_shapes=[pltpu.VMEM((N_EXPERTS, LANES), F32)],
        compiler_params=_params("arbitrary"),
        name="moe_router",
    )(x, w_rt, b_r)


ISSUE_UNROLL = 16


def _dispatch_kernel(pad_end_ref, dest_ref, x_ref, xs_ref, rows_scr, zero_scr, sem, zsem, *, tm, blk):
    @pl.when(pl.program_id(0) == 0)
    def _():
        zero_scr[...] = jnp.zeros_like(zero_scr)

        def fill(e):
            start = pl.multiple_of((pad_end_ref[e] - blk) * ROW_TILES, blk * ROW_TILES)
            return pltpu.make_async_copy(zero_scr, xs_ref.at[pl.ds(start, blk * ROW_TILES), :], zsem)

        def has_block(e):
            return pad_end_ref[e] > (pad_end_ref[e - 1] if e else 0)

        for e in range(N_EXPERTS):
            pl.when(has_block(e))(lambda e=e: fill(e).start())
        for e in range(N_EXPERTS):
            pl.when(has_block(e))(lambda e=e: fill(e).wait())

    _rows_to_tiles(rows_scr, x_ref[...])

    def tile(ref, row):
        return ref.at[pl.ds(pl.multiple_of(row * ROW_TILES, ROW_TILES), ROW_TILES), :]

    def issue(g, carry):
        base = pl.multiple_of(g * (ISSUE_UNROLL // TOP_K), ISSUE_UNROLL // TOP_K)
        for u in range(ISSUE_UNROLL // TOP_K):
            r = base + u
            for j in range(TOP_K):
                pltpu.make_async_copy(tile(rows_scr, r), tile(xs_ref, dest_ref[0, j, r]), sem).start(priority=j % 2)
        return carry

    lax.fori_loop(0, tm * TOP_K // ISSUE_UNROLL, issue, 0)
    for j in range(TOP_K):
        pltpu.make_async_copy(rows_scr, xs_ref.at[pl.ds(0, tm * ROW_TILES), :], sem).wait()


def _dispatch(x, dest_blocks, pad_end, n_slots, tm, blk):
    T = x.shape[0]
    return pl.pallas_call(
        functools.partial(_dispatch_kernel, tm=tm, blk=blk),
        out_shape=jax.ShapeDtypeStruct((n_slots * ROW_TILES, LANES), F32),
        grid_spec=pltpu.PrefetchScalarGridSpec(
            num_scalar_prefetch=1,
            grid=(T // tm,),
            in_specs=[pl.BlockSpec((1, TOP_K, tm), lambda i, pe: (i, 0, 0), memory_space=pltpu.SMEM),
                      pl.BlockSpec((tm, D_MODEL), lambda i, pe: (i, 0))],
            out_specs=pl.BlockSpec(memory_space=pl.ANY),
            scratch_shapes=[pltpu.VMEM((tm * ROW_TILES, LANES), F32), pltpu.VMEM((blk * ROW_TILES, LANES), F32),
                            pltpu.SemaphoreType.DMA, pltpu.SemaphoreType.DMA]),
        compiler_params=_params("arbitrary"),
        name="moe_dispatch",
    )(pad_end, dest_blocks, x)


def _collect_combine_kernel(idx_ref, idx_next_ref, src_ref, gate_ref, x_ref, g_ref, b_ref, y_ref, buf, sem, *, tm):
    i = pl.program_id(0)
    slot = i % 2

    def tile(ref, row):
        return ref.at[pl.ds(pl.multiple_of(row * ROW_TILES, ROW_TILES), ROW_TILES), :]

    def request(ids, s):
        def body(g, carry):
            base = pl.multiple_of(g * (ISSUE_UNROLL // TOP_K), ISSUE_UNROLL // TOP_K)
            for u in range(ISSUE_UNROLL // TOP_K):
                r = base + u
                for j in range(TOP_K):
                    pltpu.make_async_copy(tile(src_ref, ids[0, j, r]), tile(buf.at[s, j], r),
                                          sem.at[s]).start(priority=j % 2)
            return carry
        lax.fori_loop(0, tm * TOP_K // ISSUE_UNROLL, body, 0)

    @pl.when(i == 0)
    def _():
        request(idx_ref, 0)

    @pl.when(i + 1 < pl.num_programs(0))
    def _():
        request(idx_next_ref, 1 - slot)

    for j in range(TOP_K):
        pltpu.make_async_copy(src_ref.at[pl.ds(0, tm * ROW_TILES), :], buf.at[slot, j], sem.at[slot]).wait()
    gates = gate_ref[...]
    acc = DEEPNORM_ALPHA * x_ref[...]
    for j in range(TOP_K):
        acc = acc + gates[:, j:j + 1] * _tiles_to_rows(buf.at[slot, j], tm)
    y_ref[...] = _layer_norm(acc, g_ref[...], b_ref[...])


def _collect_combine(src, idx_blocks, gates, x, ln_g, ln_b, tm):
    T = x.shape[0]
    nt = T // tm
    row = pl.BlockSpec((tm, D_MODEL), lambda i: (i, 0))
    vec = _full((1, D_MODEL))
    return pl.pallas_call(
        functools.partial(_collect_combine_kernel, tm=tm),
        out_shape=jax.ShapeDtypeStruct((T, D_MODEL), F32),
        grid=(nt,),
        in_specs=[pl.BlockSpec((1, TOP_K, tm), lambda i: (i, 0, 0), memory_space=pltpu.SMEM),
                  pl.BlockSpec((1, TOP_K, tm), lambda i: (jnp.minimum(i + 1, nt - 1), 0, 0), memory_space=pltpu.SMEM),
                  pl.BlockSpec(memory_space=pl.ANY),
                  pl.BlockSpec((tm, TOP_K), lambda i: (i, 0)), row, vec, vec],
        out_specs=row,
        scratch_shapes=[pltpu.VMEM((2, TOP_K, tm * ROW_TILES, LANES), F32), pltpu.SemaphoreType.DMA((2,))],
        compiler_params=_params("arbitrary"),
        name="moe_collect_combine",
    )(idx_blocks, idx_blocks, src, gates, x, ln_g, ln_b)


def _expert_kernel(be_ref, nb_ref, xs_ref, w1_ref, b1_ref, w2_ref, b2_ref, o_ref, *, blk):
    @pl.when(pl.program_id(0) < nb_ref[0])
    def _():
        h = _bdot(_tiles_to_rows(xs_ref, blk), w1_ref[0]) + b1_ref[0]
        glu = jnp.minimum(h[:, :D_EXPERT], SWIGLU_LIMIT)
        lin = jnp.clip(h[:, D_EXPERT:], -SWIGLU_LIMIT, SWIGLU_LIMIT)
        a = glu * jax.nn.sigmoid(SWIGLU_ALPHA * glu) * (lin + 1.0)
        _rows_to_tiles(o_ref, _bdot(a, w2_ref[0]) + b2_ref[0])


def _experts(xs, block_e, n_used, w1, b1, w2, b2, blk):
    P = xs.shape[0] // ROW_TILES
    nb = P // blk

    def blk_map(i, be, nu):
        return (jnp.minimum(i, nu[0] - 1), 0)

    def w_map(i, be, nu):
        return (be[jnp.minimum(i, nu[0] - 1)], 0, 0)

    return pl.pallas_call(
        functools.partial(_expert_kernel, blk=blk),
        out_shape=jax.ShapeDtypeStruct((P * ROW_TILES, LANES), F32),
        grid_spec=pltpu.PrefetchScalarGridSpec(
            num_scalar_prefetch=2,
            grid=(nb,),
            in_specs=[pl.BlockSpec((blk * ROW_TILES, LANES), blk_map),
                      pl.BlockSpec((1, D_MODEL, 2 * D_EXPERT), w_map),
                      pl.BlockSpec((1, 1, 2 * D_EXPERT), w_map),
                      pl.BlockSpec((1, D_EXPERT, D_MODEL), w_map),
                      pl.BlockSpec((1, 1, D_MODEL), w_map)],
            out_specs=pl.BlockSpec((blk * ROW_TILES, LANES), blk_map)),
        compiler_params=_params("arbitrary"),
        name="moe_experts",
    )(block_e, n_used, xs, w1, b1, w2, b2)


def _moe(x, layer, w_r, b_r, w1, b1, w2, b2, ln_g, ln_b):
    T = x.shape[0]
    blk = 1024 if T >= 4096 else 128
    tm = _row_tile(T, 512)
    n_assign = T * TOP_K
    n_blocks = -(-(n_assign + N_EXPERTS * (blk - 1)) // blk)
    n_slots = n_blocks * blk

    top_i, gates, pos, counts = _router(x, w_r, b_r)
    counts = counts[:, 0]
    padded = (counts + blk - 1) // blk * blk
    pad_end = jnp.cumsum(padded)
    pad_start = pad_end - padded
    is_e = top_i[None] == jnp.arange(N_EXPERTS, dtype=jnp.int32)[:, None, None]
    dest = pos + jnp.sum(jnp.where(is_e, pad_start.astype(jnp.int32)[:, None, None], 0), axis=0)
    block_start = jnp.arange(n_blocks, dtype=jnp.int32) * blk
    block_e = jnp.minimum(jnp.sum(pad_end[None, :] <= block_start[:, None], axis=1), N_EXPERTS - 1).astype(jnp.int32)
    n_used = (pad_end[-1:] // blk).astype(jnp.int32)
    dest_blocks = dest.reshape(TOP_K, T // tm, tm).transpose(1, 0, 2)

    xs = _dispatch(x, dest_blocks, pad_end.astype(jnp.int32), n_slots, tm, blk)
    out = _experts(xs, block_e + layer * N_EXPERTS, n_used, w1, b1, w2, b2, blk)
    return _collect_combine(out, dest_blocks, gates.T, x, ln_g, ln_b, tm)


def _pad_rows(t, n):
    return t if n == 0 else jnp.pad(t, ((0, 0), (0, n), (0, 0)))


def _trunk(x, gla_s0, past_k, past_v, past_valid, mem_k, mem_v, w):
    B, L, _ = x.shape
    Lp = -(-L // CHUNK) * CHUNK
    T = B * L
    vec = lambda a: a.reshape(1, -1)
    gla_states = []
    k_all = v_all = new_k = new_v = None
    for layer in range(DEPTH):
        lg, lb = w['ln_g'][layer], w['ln_b'][layer]
        xt = x.reshape(T, D_MODEL)
        if layer < N_A:
            i = layer
            xt = _pad_rows(x, Lp - L).reshape(B * Lp, D_MODEL)
            q, k, v, r, b = _gla_in(xt, w['gla_w_main'][i], w['gla_w_g'][i], w['gla_w_g2'][i],
                                    vec(w['gla_b_g'][i]), L, Lp)
            seq = lambda t: t.reshape(B, Lp, -1)
            s0t = jnp.swapaxes(gla_s0[i], -1, -2)
            o, st = _gla_scan(seq(q), seq(k), seq(v), seq(b), s0t)
            gla_states.append(jnp.swapaxes(st, -1, -2))
            xt = _gla_out(o.reshape(B * Lp, GLA_V), r, xt, vec(w['gla_gn_g'][i]), w['gla_w_o'][i],
                          vec(lg[0]), vec(lb[0]))
            x = xt.reshape(B, Lp, D_MODEL)[:, :L]
        else:
            j = layer - N_A
            if j == 0:
                k_new, v_new = _linear_kv(xt, w['kv_w'][None])
                k_full = jnp.concatenate([past_k, k_new.reshape(B, L, SWA_KV)], axis=1)
                v_full = jnp.concatenate([past_v, v_new.reshape(B, L, SWA_KV)], axis=1)
                new_k, new_v = k_full[:, -WINDOW:], v_full[:, -WINDOW:]
                k_all, v_all = _pad_rows(k_full, Lp - L), _pad_rows(v_full, Lp - L)
            xp = _swa(_pad_rows(x, Lp - L), k_all, v_all, w['swa_w_q'][j], w['swa_sinks'][j].reshape(-1, 1),
                      w['swa_w_o'][j], vec(lg[0]), vec(lb[0]), L, past_valid)
            x = xp[:, :L]
        x = _mem_xattn(x, mem_k, mem_v, layer, w['mem_w_q'][layer], w['mem_w_o'][layer], vec(lg[1]), vec(lb[1]))
        xt = _moe(x.reshape(T, D_MODEL), layer, w['moe_w_r'][layer], w['moe_b_r'][layer], w['moe_w1'],
                  w['moe_b1'], w['moe_w2'], w['moe_b2'], vec(lg[2]), vec(lb[2]))
        x = xt.reshape(B, L, D_MODEL)
    return x, jnp.stack(gla_states), new_k, new_v


def _prep_weights(gla_w_in, gla_w_g2, gla_b_g, gla_gn_g, gla_w_o, kv_w, swa_w_q, swa_sinks, swa_w_o,
                  mem_w_q, mem_w_kv, mem_w_o, moe_w_r, moe_b_r, moe_w1, moe_b1, moe_w2, moe_b2, ln_g, ln_b):
    n_main = 2 * GLA_QK + 2 * GLA_V
    pad_c = lambda a, n: jnp.pad(a, [(0, 0)] * (a.ndim - 1) + [(0, n - a.shape[-1])])
    w = dict(
        gla_w_main=gla_w_in[:, :, :n_main].astype(BF16),
        gla_w_g=pad_c(gla_w_in[:, :, n_main:], LANES).astype(BF16),
        gla_w_g2=jnp.pad(gla_w_g2, ((0, 0), (0, LANES - GLA_GATE_RANK), (0, 0))).astype(BF16),
        gla_b_g=gla_b_g, gla_gn_g=gla_gn_g.reshape(N_A, GLA_V), gla_w_o=gla_w_o.astype(BF16),
        kv_w=kv_w.astype(BF16), swa_w_q=swa_w_q.astype(BF16), swa_sinks=swa_sinks, swa_w_o=swa_w_o.astype(BF16),
        mem_w_q=mem_w_q.astype(BF16), mem_w_kv=mem_w_kv.astype(BF16), mem_w_o=mem_w_o.astype(BF16),
        moe_w_r=jnp.swapaxes(moe_w_r, -1, -2).astype(BF16), moe_b_r=moe_b_r.reshape(DEPTH, N_EXPERTS, 1),
        moe_w1=_w1_relayout(moe_w1.reshape(DEPTH * N_EXPERTS, D_MODEL, 2 * D_EXPERT)),
        moe_b1=jnp.concatenate([moe_b1[..., 0::2], moe_b1[..., 1::2]], axis=-1).reshape(DEPTH * N_EXPERTS, 1, -1),
        moe_w2=moe_w2.astype(BF16).reshape(DEPTH * N_EXPERTS, D_EXPERT, D_MODEL),
        moe_b2=moe_b2.reshape(DEPTH * N_EXPERTS, 1, D_MODEL),
        ln_g=ln_g, ln_b=ln_b)
    return w


def kernel(x_prompt, x_sample, state_gla, cache_swa_k, cache_swa_v, cache_mem_k, cache_mem_v, mem_prompt, gla_w_in, gla_w_g2, gla_b_g, gla_gn_g, gla_w_o, kv_w, swa_w_q, swa_sinks, swa_w_o, mem_w_q, mem_w_kv, mem_w_o, moe_w_r, moe_b_r, moe_w1, moe_b1, moe_w2, moe_b2, ln_g, ln_b):
    w = _prep_weights(gla_w_in, gla_w_g2, gla_b_g, gla_gn_g, gla_w_o, kv_w, swa_w_q, swa_sinks, swa_w_o,
                      mem_w_q, mem_w_kv, mem_w_o, moe_w_r, moe_b_r, moe_w1, moe_b1, moe_w2, moe_b2, ln_g, ln_b)
    bp = x_prompt.shape[0]
    bs = x_sample.shape[0]
    mem_flat = mem_prompt.reshape(bp * N_MEM, D_MODEL)
    mem_k_p, mem_v_p = _linear_kv(mem_flat, w['mem_w_kv'])
    gla0 = jnp.zeros((N_A, bp, GLA_HEADS, GLA_DK, GLA_DV), F32)
    zero_win = jnp.zeros((bp, WINDOW, SWA_KV), F32)
    y_p, gla_p, k_p, v_p = _trunk(x_prompt, gla0, zero_win, zero_win, False,
                                  mem_k_p.reshape(DEPTH * bp, N_MEM, D_MODEL),
                                  mem_v_p.reshape(DEPTH * bp, N_MEM, D_MODEL), w)
    y_s, gla_s, k_s, v_s = _trunk(x_sample, state_gla, cache_swa_k.reshape(bs, WINDOW, SWA_KV),
                                  cache_swa_v.reshape(bs, WINDOW, SWA_KV), True,
                                  cache_mem_k.reshape(DEPTH * bs, N_MEM, D_MODEL),
                                  cache_mem_v.reshape(DEPTH * bs, N_MEM, D_MODEL), w)
    heads4 = lambda t: t.reshape(t.shape[0], WINDOW, SWA_KV_HEADS, SWA_HEAD_DIM)
    mem5 = lambda t: t.reshape(DEPTH, bp, N_MEM, MEM_HEADS, MEM_HEAD_DIM)
    return (y_p, y_s, gla_p, gla_s, heads4(k_p), heads4(v_p), heads4(k_s), heads4(v_s), mem5(mem_k_p), mem5(mem_v_p))
```

```python
import functools

import jax
import jax.numpy as jnp
from jax import lax
from jax.experimental import pallas as pl
from jax.experimental.pallas import tpu as pltpu

F32 = jnp.float32
BF16 = jnp.bfloat16

D_MODEL = 1024
DEPTH = 4
CHUNK = 64
N_A = DEPTH // 2
GLA_HEADS = 4
GLA_DK = D_MODEL // (2 * GLA_HEADS)
GLA_DV = D_MODEL // GLA_HEADS
GLA_QK = GLA_HEADS * GLA_DK
GLA_V = GLA_HEADS * GLA_DV
GLA_GATE_RANK = 16
GLA_GATE_TEMP = 16.0
SWA_HEAD_DIM = 64
SWA_Q_HEADS = D_MODEL // SWA_HEAD_DIM
SWA_KV_HEADS = 4
SWA_GROUP = SWA_Q_HEADS // SWA_KV_HEADS
SWA_KV = SWA_KV_HEADS * SWA_HEAD_DIM
WINDOW = 128
WINDOW_CHUNKS = WINDOW // CHUNK
N_MEM = 256
MEM_HEADS = 4
MEM_HEAD_DIM = D_MODEL // MEM_HEADS
N_EXPERTS = 32
TOP_K = 4
D_EXPERT = D_MODEL
SWIGLU_ALPHA = 1.702
SWIGLU_LIMIT = 7.0
DEEPNORM_ALPHA = (2 * DEPTH) ** 0.25
LN_EPS = 1e-5
NEG_INF = -1e30

LANES = 128
VMEM_LIMIT = 56 * 1024 * 1024


def _params(*sem):
    return pltpu.CompilerParams(dimension_semantics=sem, vmem_limit_bytes=VMEM_LIMIT)


def _row_tile(n_rows, want):
    t = min(want, n_rows)
    while n_rows % t:
        t //= 2
    return t


def _full(shape):
    nd = len(shape)
    return pl.BlockSpec(shape, lambda *_: (0,) * nd)


def _bdot(a, b):
    return jnp.dot(a.astype(BF16), b.astype(BF16), preferred_element_type=F32)


def _bdot_nt(a, b):
    return lax.dot_general(a.astype(BF16), b.astype(BF16), (((1,), (1,)), ((), ())),
                           preferred_element_type=F32)


def _bdot_tn(a, b):
    return lax.dot_general(a.astype(BF16), b.astype(BF16), (((0,), (0,)), ((), ())),
                           preferred_element_type=F32)


def _layer_norm(z, g, b):
    mu = jnp.mean(z, axis=-1, keepdims=True)
    zc = z - mu
    var = jnp.mean(zc * zc, axis=-1, keepdims=True)
    return zc * lax.rsqrt(var + LN_EPS) * g + b


ROW_TILES = D_MODEL // LANES


def _rows_to_tiles(ref, x):
    n = x.shape[0]
    for c in range(ROW_TILES):
        ref[pl.ds(c, n, stride=ROW_TILES), :] = x[:, c * LANES:(c + 1) * LANES]


def _tiles_to_rows(ref, n):
    return jnp.concatenate([ref[pl.ds(c, n, stride=ROW_TILES), :] for c in range(ROW_TILES)], axis=1)


def _linear_kv_kernel(x_ref, w_ref, k_ref, v_ref):
    y = _bdot(x_ref[...], w_ref[0])
    half = y.shape[1] // 2
    k_ref[0] = y[:, :half]
    v_ref[0] = y[:, half:]


def _linear_kv(x, w, tm=512):
    T, K = x.shape
    G, _, N2 = w.shape
    tm = _row_tile(T, tm)
    out = jax.ShapeDtypeStruct((G, T, N2 // 2), F32)
    o_spec = pl.BlockSpec((1, tm, N2 // 2), lambda g, i: (g, i, 0))
    return pl.pallas_call(
        _linear_kv_kernel,
        out_shape=(out, out),
        grid=(G, T // tm),
        in_specs=[pl.BlockSpec((tm, K), lambda g, i: (i, 0)), pl.BlockSpec((1, K, N2), lambda g, i: (g, 0, 0))],
        out_specs=(o_spec, o_spec),
        compiler_params=_params("parallel", "parallel"),
        name="linear_kv",
    )(x, w)


MXU_DIM = 256


def _w1_relayout_kernel(w_ref, o_ref):
    half = MXU_DIM // 2
    r = lax.broadcasted_iota(jnp.int32, (MXU_DIM, MXU_DIM), 0)
    c = lax.broadcasted_iota(jnp.int32, (MXU_DIM, MXU_DIM), 1)
    src = jnp.where(c < half, 2 * c, 2 * (c - half) + 1)
    perm = (r == src).astype(BF16)
    n_out = o_ref.shape[-1] // 2
    for j in range(w_ref.shape[-1] // MXU_DIM):
        y = jnp.dot(w_ref[0, :, j * MXU_DIM:(j + 1) * MXU_DIM].astype(BF16), perm, preferred_element_type=F32)
        o_ref[0, :, j * half:(j + 1) * half] = y[:, :half].astype(BF16)
        o_ref[0, :, n_out + j * half:n_out + (j + 1) * half] = y[:, half:].astype(BF16)


def _w1_relayout(w1):
    E, D, N = w1.shape
    spec = pl.BlockSpec((1, D, N), lambda e: (e, 0, 0))
    return pl.pallas_call(
        _w1_relayout_kernel,
        out_shape=jax.ShapeDtypeStruct((E, D, N), BF16),
        grid=(E,),
        in_specs=[spec],
        out_specs=spec,
        compiler_params=_params("parallel"),
        name="w1_relayout",
    )(w1)


def _gla_in_kernel(x_ref, w_ref, wg_ref, wg2_ref, bg_ref, q_ref, k_ref, v_ref, r_ref, b_ref,
                   *, tm, seq_len, seq_pad):
    xb = x_ref[...].astype(BF16)
    y = jnp.dot(xb, w_ref[...], preferred_element_type=F32)
    q_ref[...] = y[:, :GLA_QK] * (GLA_DK ** -0.5)
    k_ref[...] = y[:, GLA_QK:2 * GLA_QK]
    v_ref[...] = y[:, 2 * GLA_QK:2 * GLA_QK + GLA_V]
    r_ref[...] = y[:, 2 * GLA_QK + GLA_V:]
    g_lr = jnp.dot(xb, wg_ref[...], preferred_element_type=F32)
    z = _bdot(g_lr, wg2_ref[...]) + bg_ref[...]
    log_a = (jnp.minimum(z, 0.0) - jnp.log(1.0 + jnp.exp(-jnp.abs(z)))) / GLA_GATE_TEMP
    if seq_len < seq_pad:
        pos = (pl.program_id(0) * tm + lax.broadcasted_iota(jnp.int32, (tm, 1), 0)) % seq_pad
        log_a = jnp.where(pos < seq_len, log_a, 0.0)
    row = lax.broadcasted_iota(jnp.int32, (tm, tm), 0)
    col = lax.broadcasted_iota(jnp.int32, (tm, tm), 1)
    tri = ((row // CHUNK == col // CHUNK) & (row >= col)).astype(BF16)
    g1 = log_a.astype(BF16)
    r1 = log_a - g1.astype(F32)
    g2 = r1.astype(BF16)
    g3 = (r1 - g2.astype(F32)).astype(BF16)
    b_ref[...] = (jnp.dot(tri, g1, preferred_element_type=F32) + jnp.dot(tri, g2, preferred_element_type=F32)
                  + jnp.dot(tri, g3, preferred_element_type=F32))


def _gla_in(x, w_main, w_g, w_g2, b_g, seq_len, seq_pad, tm=512):
    T = x.shape[0]
    tm = _row_tile(T, tm)
    assert tm % CHUNK == 0 and seq_pad % CHUNK == 0
    n_main = w_main.shape[1]
    row = lambda n: pl.BlockSpec((tm, n), lambda i: (i, 0))
    return pl.pallas_call(
        functools.partial(_gla_in_kernel, tm=tm, seq_len=seq_len, seq_pad=seq_pad),
        out_shape=(jax.ShapeDtypeStruct((T, GLA_QK), F32), jax.ShapeDtypeStruct((T, GLA_QK), F32),
                   jax.ShapeDtypeStruct((T, GLA_V), F32), jax.ShapeDtypeStruct((T, GLA_V), F32),
                   jax.ShapeDtypeStruct((T, GLA_QK), F32)),
        grid=(T // tm,),
        in_specs=[row(D_MODEL), _full((D_MODEL, n_main)), _full((D_MODEL, LANES)),
                  _full((LANES, GLA_QK)), _full((1, GLA_QK))],
        out_specs=(row(GLA_QK), row(GLA_QK), row(GLA_V), row(GLA_V), row(GLA_QK)),
        compiler_params=_params("parallel"),
        name="gla_in",
    )(x, w_main, w_g, w_g2, b_g)


SUBLANES = 8
GLA_SUB = 16


def _gla_intra(q, k, b):
    n_grp = CHUNK // SUBLANES
    g_sub = GLA_SUB // SUBLANES
    lane = lax.broadcasted_iota(jnp.int32, (SUBLANES, LANES), 1)
    row = lax.broadcasted_iota(jnp.int32, (SUBLANES, LANES), 0)
    grp = lambda t, i: t[i * SUBLANES:(i + 1) * SUBLANES]
    att = [jnp.zeros((SUBLANES, LANES), F32) for _ in range(n_grp)]
    for s in range(CHUNK):
        b_s = b[s:s + 1, :]
        k_s = k[s:s + 1, :]
        for i in range(s // SUBLANES, (s // GLA_SUB + 1) * g_sub):
            e = jnp.exp(grp(b, i) - b_s)
            red = jnp.sum(grp(q, i) * e * k_s, axis=-1, keepdims=True)
            att[i] = jnp.where(lane == s, red, att[i])
    att = [jnp.where(row + i * SUBLANES >= lane, att[i], 0.0) for i in range(n_grp)]
    for blk in range(1, CHUNK // GLA_SUB):
        r0 = blk * GLA_SUB
        c = b[r0 - 1:r0, :]
        k_early = jnp.concatenate([k[:r0] * jnp.exp(c - b[:r0]), jnp.zeros((LANES - r0, GLA_DK), F32)], axis=0)
        off = _bdot_nt(q[r0:r0 + GLA_SUB] * jnp.exp(b[r0:r0 + GLA_SUB] - c), k_early)
        for i in range(g_sub):
            att[blk * g_sub + i] = att[blk * g_sub + i] + grp(off, i)
    return jnp.concatenate(att, axis=0)


def _gla_scan_kernel(q_ref, k_ref, v_ref, b_ref, s0_ref, o_ref, s_ref, st_scr, *, n_chunks):
    @pl.when(pl.program_id(1) == 0)
    def _():
        st_scr[...] = s0_ref[0]

    for c in range(n_chunks):
        ts = slice(c * CHUNK, (c + 1) * CHUNK)
        for h in range(GLA_HEADS):
            ks = slice(h * GLA_DK, (h + 1) * GLA_DK)
            vs = slice(h * GLA_DV, (h + 1) * GLA_DV)
            q = q_ref[0, ts, ks]
            k = k_ref[0, ts, ks]
            b = b_ref[0, ts, ks]
            v = v_ref[0, ts, vs]
            att = _gla_intra(q, k, b)
            b_end = b[CHUNK - 1:CHUNK, :]
            st = st_scr[h]
            o_ref[0, ts, vs] = _bdot(att[:, :CHUNK], v) + _bdot_nt(q * jnp.exp(b), st)
            st_scr[h] = st * jnp.exp(b_end) + _bdot_tn(v, k * jnp.exp(b_end - b))

    @pl.when(pl.program_id(1) == pl.num_programs(1) - 1)
    def _():
        s_ref[0] = st_scr[...]


def _gla_scan(q, k, v, la, s0t, tl=128):
    B, Lp, _ = q.shape
    tl = _row_tile(Lp, tl)
    qk_spec = pl.BlockSpec((1, tl, GLA_QK), lambda b, i: (b, i, 0))
    v_spec = pl.BlockSpec((1, tl, GLA_V), lambda b, i: (b, i, 0))
    s_spec = pl.BlockSpec((1, GLA_HEADS, GLA_DV, GLA_DK), lambda b, i: (b, 0, 0, 0))
    return pl.pallas_call(
        functools.partial(_gla_scan_kernel, n_chunks=tl // CHUNK),
        out_shape=(jax.ShapeDtypeStruct((B, Lp, GLA_V), F32),
                   jax.ShapeDtypeStruct((B, GLA_HEADS, GLA_DV, GLA_DK), F32)),
        grid=(B, Lp // tl),
        in_specs=[qk_spec, qk_spec, v_spec, qk_spec, s_spec],
        out_specs=(v_spec, s_spec),
        scratch_shapes=[pltpu.VMEM((GLA_HEADS, GLA_DV, GLA_DK), F32)],
        compiler_params=_params("parallel", "arbitrary"),
        name="gla_scan",
    )(q, k, v, la, s0t)


def _gla_out_kernel(o_ref, r_ref, x_ref, gn_ref, wo_ref, g_ref, b_ref, y_ref):
    parts = []
    for h in range(GLA_HEADS):
        vs = slice(h * GLA_DV, (h + 1) * GLA_DV)
        o = o_ref[:, vs]
        mu = jnp.mean(o, axis=-1, keepdims=True)
        oc = o - mu
        var = jnp.mean(oc * oc, axis=-1, keepdims=True)
        parts.append(oc * lax.rsqrt(var + LN_EPS) * gn_ref[:, vs])
    o = jnp.concatenate(parts, axis=-1)
    r = r_ref[...]
    o = o * (r * jax.nn.sigmoid(r))
    h = _bdot(o, wo_ref[...])
    y_ref[...] = _layer_norm(DEEPNORM_ALPHA * x_ref[...] + h, g_ref[...], b_ref[...])


def _gla_out(o, r, x, gn_g, w_o, ln_g, ln_b, tm=512):
    T = x.shape[0]
    tm = _row_tile(T, tm)
    row = pl.BlockSpec((tm, D_MODEL), lambda i: (i, 0))
    vec = _full((1, D_MODEL))
    return pl.pallas_call(
        _gla_out_kernel,
        out_shape=jax.ShapeDtypeStruct((T, D_MODEL), F32),
        grid=(T // tm,),
        in_specs=[row, row, row, vec, _full((GLA_V, D_MODEL)), vec, vec],
        out_specs=row,
        compiler_params=_params("parallel"),
        name="gla_out",
    )(o, r, x, gn_g, w_o, ln_g, ln_b)


def _swa_kernel(x_ref, k_ref, v_ref, wq_ref, sink_ref, wo_ref, g_ref, b_ref, y_ref,
                *, tl, seq_len, past_valid):
    x = x_ref[0]
    q = (_bdot(x, wq_ref[...]) * (SWA_HEAD_DIM ** -0.5)).astype(BF16)
    kw = tl + WINDOW
    r0 = pl.multiple_of(pl.program_id(1) * tl, tl)
    kt = k_ref[0, pl.ds(r0, kw), :].astype(BF16)
    vt = v_ref[0, pl.ds(r0, kw), :].astype(BF16)
    qs = min(tl, WINDOW)
    kws = qs + WINDOW
    q_row = lax.broadcasted_iota(jnp.int32, (qs, kws), 0)
    k_col = lax.broadcasted_iota(jnp.int32, (qs, kws), 1)
    band_lo = (q_row // CHUNK) * CHUNK
    in_band = (k_col >= band_lo) & (k_col < band_lo + WINDOW + CHUNK)
    oks = []
    for u in range(tl // qs):
        pos = r0 + u * qs + k_col
        ok = in_band & (pos - WINDOW < seq_len)
        oks.append(ok if past_valid else ok & (pos >= WINDOW))
    zeros = jnp.zeros((kw, SWA_HEAD_DIM), BF16)
    pair_out = [[] for _ in range(SWA_Q_HEADS // 2)]
    for g in range(SWA_KV_HEADS):
        kg = kt[:, g * SWA_HEAD_DIM:(g + 1) * SWA_HEAD_DIM]
        vg = vt[:, g * SWA_HEAD_DIM:(g + 1) * SWA_HEAD_DIM]
        k_pad = (jnp.concatenate([kg, zeros], axis=1), jnp.concatenate([zeros, kg], axis=1))
        v_pad = (jnp.concatenate([vg, zeros], axis=1), jnp.concatenate([zeros, vg], axis=1))
        for pair in range(g * SWA_GROUP // 2, (g + 1) * SWA_GROUP // 2):
            sinks = [sink_ref[2 * pair + j:2 * pair + j + 1, :] for j in range(2)]
            work = [(u, j) for u in range(tl // qs) for j in range(2)]
            kwin = [slice(u * qs, u * qs + kws) for u in range(tl // qs)]
            qp = [q[u * qs:(u + 1) * qs, pair * LANES:(pair + 1) * LANES] for u in range(tl // qs)]
            s = [jnp.where(oks[u], _bdot_nt(qp[u], k_pad[j][kwin[u]]), NEG_INF) for u, j in work]
            m = [jnp.maximum(jnp.max(s_i, axis=-1, keepdims=True), sinks[j]) for s_i, (u, j) in zip(s, work)]
            e = [jnp.exp(s_i - m_i) for s_i, m_i in zip(s, m)]
            o = [_bdot(e_i, v_pad[j][kwin[u]]) for e_i, (u, j) in zip(e, work)]
            inv = [1.0 / (jnp.sum(e_i, axis=-1, keepdims=True) + jnp.exp(sinks[j] - m_i))
                   for e_i, m_i, (u, j) in zip(e, m, work)]
            on = [o_i * inv_i for o_i, inv_i in zip(o, inv)]
            for u in range(tl // qs):
                pair_out[pair].append(on[2 * u] + on[2 * u + 1])
    pair_out = [jnp.concatenate(p, axis=0) for p in pair_out]
    h_out = _bdot(jnp.concatenate(pair_out, axis=1), wo_ref[...])
    y_ref[0] = _layer_norm(DEEPNORM_ALPHA * x + h_out, g_ref[...], b_ref[...])


def _swa(x, k_all, v_all, w_q, sinks, w_o, ln_g, ln_b, seq_len, past_valid, tl=256):
    B, Lp, _ = x.shape
    tl = _row_tile(Lp, tl)
    x_spec = pl.BlockSpec((1, tl, D_MODEL), lambda b, i: (b, i, 0))
    kv_spec = pl.BlockSpec((1, WINDOW + Lp, SWA_KV), lambda b, i: (b, 0, 0))
    vec = _full((1, D_MODEL))
    return pl.pallas_call(
        functools.partial(_swa_kernel, tl=tl, seq_len=seq_len, past_valid=past_valid),
        out_shape=jax.ShapeDtypeStruct((B, Lp, D_MODEL), F32),
        grid=(B, Lp // tl),
        in_specs=[x_spec, kv_spec, kv_spec, _full((D_MODEL, D_MODEL)), _full((SWA_Q_HEADS, 1)),
                  _full((D_MODEL, D_MODEL)), vec, vec],
        out_specs=x_spec,
        compiler_params=_params("parallel", "arbitrary"),
        name="swa",
    )(x, k_all, v_all, w_q, sinks, w_o, ln_g, ln_b)


def _mem_kernel(x_ref, mk_ref, mv_ref, wq_ref, wo_ref, g_ref, b_ref, y_ref):
    x = x_ref[0]
    q = _bdot(x, wq_ref[...]) * (MEM_HEAD_DIM ** -0.5)
    parts = []
    for h in range(MEM_HEADS):
        hs = slice(h * MEM_HEAD_DIM, (h + 1) * MEM_HEAD_DIM)
        s = _bdot_nt(q[:, hs], mk_ref[0, :, hs])
        m = jnp.max(s, axis=-1, keepdims=True)
        e = jnp.exp(s - m)
        parts.append(_bdot(e, mv_ref[0, :, hs]) * (1.0 / jnp.sum(e, axis=-1, keepdims=True)))
    o = jnp.concatenate(parts, axis=-1)
    h_out = _bdot(o, wo_ref[...])
    y_ref[0] = _layer_norm(DEEPNORM_ALPHA * x + h_out, g_ref[...], b_ref[...])


def _mem_xattn(x, mk, mv, layer, w_q, w_o, ln_g, ln_b, tl=512):
    B, L, _ = x.shape
    tl = _row_tile(L, tl)
    x_spec = pl.BlockSpec((1, tl, D_MODEL), lambda b, i: (b, i, 0))
    m_spec = pl.BlockSpec((1, N_MEM, D_MODEL), lambda b, i: (layer * B + b, 0, 0))
    vec = _full((1, D_MODEL))
    return pl.pallas_call(
        _mem_kernel,
        out_shape=jax.ShapeDtypeStruct((B, L, D_MODEL), F32),
        grid=(B, L // tl),
        in_specs=[x_spec, m_spec, m_spec, _full((D_MODEL, D_MODEL)), _full((D_MODEL, D_MODEL)), vec, vec],
        out_specs=x_spec,
        compiler_params=_params("parallel", "arbitrary"),
        name="mem_xattn",
    )(x, mk, mv, w_q, w_o, ln_g, ln_b)


def _router_kernel(x_ref, wrt_ref, br_ref, idx_ref, gate_ref, pos_ref, cnt_ref, cnt_scr, *, tm):
    @pl.when(pl.program_id(0) == 0)
    def _():
        cnt_scr[...] = jnp.zeros_like(cnt_scr)

    logits = _bdot_nt(wrt_ref[...], x_ref[...]) + br_ref[...]
    expert = lax.broadcasted_iota(jnp.int32, (N_EXPERTS, tm), 0).astype(F32)
    sel = jnp.zeros((N_EXPERTS, tm), F32)
    l = logits
    vals, hits = [], []
    for j in range(TOP_K):
        m = jnp.max(l, axis=0, keepdims=True)
        idx = jnp.min(jnp.where(l == m, expert, float(N_EXPERTS)), axis=0, keepdims=True)
        hit = expert == idx
        sel = jnp.where(hit, 1.0, sel)
        l = jnp.where(hit, -jnp.inf, l)
        idx_ref[j:j + 1, :] = idx.astype(jnp.int32)
        vals.append(m)
        hits.append(hit)
    e = [jnp.exp(v - vals[0]) for v in vals]
    inv = 1.0 / (e[0] + e[1] + e[2] + e[3])
    r = lax.broadcasted_iota(jnp.int32, (tm, tm), 0)
    c = lax.broadcasted_iota(jnp.int32, (tm, tm), 1)
    before = jnp.dot(sel.astype(BF16), (r < c).astype(BF16), preferred_element_type=F32) + cnt_scr[:, 0:1]
    for j in range(TOP_K):
        gate_ref[j:j + 1, :] = e[j] * inv
        pos_ref[j:j + 1, :] = jnp.sum(jnp.where(hits[j], before, 0.0), axis=0, keepdims=True).astype(jnp.int32)
    cnt_scr[...] = cnt_scr[...] + jnp.sum(sel, axis=1, keepdims=True)
    cnt_ref[...] = cnt_scr[...].astype(jnp.int32)


def _router(x, w_rt, b_r, tm=512):
    T = x.shape[0]
    tm = _row_tile(T, tm)
    small = pl.BlockSpec((TOP_K, tm), lambda i: (0, i))
    return pl.pallas_call(
        functools.partial(_router_kernel, tm=tm),
        out_shape=(jax.ShapeDtypeStruct((TOP_K, T), jnp.int32), jax.ShapeDtypeStruct((TOP_K, T), F32),
                   jax.ShapeDtypeStruct((TOP_K, T), jnp.int32), jax.ShapeDtypeStruct((N_EXPERTS, LANES), jnp.int32)),
        grid=(T // tm,),
        in_specs=[pl.BlockSpec((tm, D_MODEL), lambda i: (i, 0)), _full((N_EXPERTS, D_MODEL)), _full((N_EXPERTS, 1))],
        out_specs=(small, small, small, _full((N_EXPERTS, LANES))),
        scratch_shapes=[pltpu.VMEM((N_EXPERTS, LANES), F32)],
        compiler_params=_params("arbitrary"),
        name="moe_router",
    )(x, w_rt, b_r)


ISSUE_UNROLL = 16


def _dispatch_kernel(pad_end_ref, dest_ref, x_ref, xs_ref, rows_scr, zero_scr, sem, zsem, *, tm, blk):
    @pl.when(pl.program_id(0) == 0)
    def _():
        zero_scr[...] = jnp.zeros_like(zero_scr)

        def fill(e):
            start = pl.multiple_of((pad_end_ref[e] - blk) * ROW_TILES, blk * ROW_TILES)
            return pltpu.make_async_copy(zero_scr, xs_ref.at[pl.ds(start, blk * ROW_TILES), :], zsem)

        def has_block(e):
            return pad_end_ref[e] > (pad_end_ref[e - 1] if e else 0)

        for e in range(N_EXPERTS):
            pl.when(has_block(e))(lambda e=e: fill(e).start())
        for e in range(N_EXPERTS):
            pl.when(has_block(e))(lambda e=e: fill(e).wait())

    _rows_to_tiles(rows_scr, x_ref[...])

    def tile(ref, row):
        return ref.at[pl.ds(pl.multiple_of(row * ROW_TILES, ROW_TILES), ROW_TILES), :]

    def issue(g, carry):
        base = pl.multiple_of(g * (ISSUE_UNROLL // TOP_K), ISSUE_UNROLL // TOP_K)
        for u in range(ISSUE_UNROLL // TOP_K):
            r = base + u
            for j in range(TOP_K):
                pltpu.make_async_copy(tile(rows_scr, r), tile(xs_ref, dest_ref[0, 0, r * TOP_K + j]), sem).start(priority=j % 2)
        return carry

    lax.fori_loop(0, tm * TOP_K // ISSUE_UNROLL, issue, 0)
    for j in range(TOP_K):
        pltpu.make_async_copy(rows_scr, xs_ref.at[pl.ds(0, tm * ROW_TILES), :], sem).wait()


def _dispatch(x, dest_blocks, pad_end, n_slots, tm, blk):
    T = x.shape[0]
    return pl.pallas_call(
        functools.partial(_dispatch_kernel, tm=tm, blk=blk),
        out_shape=jax.ShapeDtypeStruct((n_slots * ROW_TILES, LANES), F32),
        grid_spec=pltpu.PrefetchScalarGridSpec(
            num_scalar_prefetch=1,
            grid=(T // tm,),
            in_specs=[pl.BlockSpec((1, 1, tm * TOP_K), lambda i, pe: (i, 0, 0), memory_space=pltpu.SMEM),
                      pl.BlockSpec((tm, D_MODEL), lambda i, pe: (i, 0))],
            out_specs=pl.BlockSpec(memory_space=pl.ANY),
            scratch_shapes=[pltpu.VMEM((tm * ROW_TILES, LANES), F32), pltpu.VMEM((blk * ROW_TILES, LANES), F32),
                            pltpu.SemaphoreType.DMA, pltpu.SemaphoreType.DMA]),
        compiler_params=_params("arbitrary"),
        name="moe_dispatch",
    )(pad_end, dest_blocks, x)


def _collect_combine_kernel(idx_ref, idx_next_ref, src_ref, gate_ref, x_ref, g_ref, b_ref, y_ref, buf, sem, *, tm):
    i = pl.program_id(0)
    slot = i % 2

    def tile(ref, row):
        return ref.at[pl.ds(pl.multiple_of(row * ROW_TILES, ROW_TILES), ROW_TILES), :]

    def request(ids, s):
        def body(g, carry):
            base = pl.multiple_of(g * (ISSUE_UNROLL // TOP_K), ISSUE_UNROLL // TOP_K)
            for u in range(ISSUE_UNROLL // TOP_K):
                r = base + u
                for j in range(TOP_K):
                    pltpu.make_async_copy(tile(src_ref, ids[0, 0, r * TOP_K + j]), tile(buf.at[s, j], r),
                                          sem.at[s]).start(priority=j % 2)
            return carry
        lax.fori_loop(0, tm * TOP_K // ISSUE_UNROLL, body, 0)

    @pl.when(i == 0)
    def _():
        request(idx_ref, 0)

    @pl.when(i + 1 < pl.num_programs(0))
    def _():
        request(idx_next_ref, 1 - slot)

    for j in range(TOP_K):
        pltpu.make_async_copy(src_ref.at[pl.ds(0, tm * ROW_TILES), :], buf.at[slot, j], sem.at[slot]).wait()
    gates = gate_ref[...]
    acc = DEEPNORM_ALPHA * x_ref[...]
    for j in range(TOP_K):
        acc = acc + gates[:, j:j + 1] * _tiles_to_rows(buf.at[slot, j], tm)
    y_ref[...] = _layer_norm(acc, g_ref[...], b_ref[...])


def _collect_combine(src, idx_blocks, gates, x, ln_g, ln_b, tm):
    T = x.shape[0]
    nt = T // tm
    row = pl.BlockSpec((tm, D_MODEL), lambda i: (i, 0))
    vec = _full((1, D_MODEL))
    return pl.pallas_call(
        functools.partial(_collect_combine_kernel, tm=tm),
        out_shape=jax.ShapeDtypeStruct((T, D_MODEL), F32),
        grid=(nt,),
        in_specs=[pl.BlockSpec((1, 1, tm * TOP_K), lambda i: (i, 0, 0), memory_space=pltpu.SMEM),
                  pl.BlockSpec((1, 1, tm * TOP_K), lambda i: (jnp.minimum(i + 1, nt - 1), 0, 0), memory_space=pltpu.SMEM),
                  pl.BlockSpec(memory_space=pl.ANY),
                  pl.BlockSpec((tm, TOP_K), lambda i: (i, 0)), row, vec, vec],
        out_specs=row,
        scratch_shapes=[pltpu.VMEM((2, TOP_K, tm * ROW_TILES, LANES), F32), pltpu.SemaphoreType.DMA((2,))],
        compiler_params=_params("arbitrary"),
        name="moe_collect_combine",
    )(idx_blocks, idx_blocks, src, gates, x, ln_g, ln_b)


def _expert_kernel(be_ref, nb_ref, xs_ref, w1_ref, b1_ref, w2_ref, b2_ref, o_ref, *, blk):
    @pl.when(pl.program_id(0) < nb_ref[0])
    def _():
        h = _bdot(_tiles_to_rows(xs_ref, blk), w1_ref[0]) + b1_ref[0]
        glu = jnp.minimum(h[:, :D_EXPERT], SWIGLU_LIMIT)
        lin = jnp.clip(h[:, D_EXPERT:], -SWIGLU_LIMIT, SWIGLU_LIMIT)
        a = glu * jax.nn.sigmoid(SWIGLU_ALPHA * glu) * (lin + 1.0)
        _rows_to_tiles(o_ref, _bdot(a, w2_ref[0]) + b2_ref[0])


def _experts(xs, block_e, n_used, w1, b1, w2, b2, blk):
    P = xs.shape[0] // ROW_TILES
    nb = P // blk

    def blk_map(i, be, nu):
        return (jnp.minimum(i, nu[0] - 1), 0)

    def w_map(i, be, nu):
        return (be[jnp.minimum(i, nu[0] - 1)], 0, 0)

    return pl.pallas_call(
        functools.partial(_expert_kernel, blk=blk),
        out_shape=jax.ShapeDtypeStruct((P * ROW_TILES, LANES), F32),
        grid_spec=pltpu.PrefetchScalarGridSpec(
            num_scalar_prefetch=2,
            grid=(nb,),
            in_specs=[pl.BlockSpec((blk * ROW_TILES, LANES), blk_map),
                      pl.BlockSpec((1, D_MODEL, 2 * D_EXPERT), w_map),
                      pl.BlockSpec((1, 1, 2 * D_EXPERT), w_map),
                      pl.BlockSpec((1, D_EXPERT, D_MODEL), w_map),
                      pl.BlockSpec((1, 1, D_MODEL), w_map)],
            out_specs=pl.BlockSpec((blk * ROW_TILES, LANES), blk_map)),
        compiler_params=_params("arbitrary"),
        name="moe_experts",
    )(block_e, n_used, xs, w1, b1, w2, b2)


def _moe(x, layer, w_r, b_r, w1, b1, w2, b2, ln_g, ln_b):
    T = x.shape[0]
    blk = 1024 if T >= 4096 else 128
    tm = _row_tile(T, 512)
    n_assign = T * TOP_K
    n_blocks = -(-(n_assign + N_EXPERTS * (blk - 1)) // blk)
    n_slots = n_blocks * blk

    top_i, gates, pos, counts = _router(x, w_r, b_r)
    counts = counts[:, 0]
    padded = (counts + blk - 1) // blk * blk
    pad_end = jnp.cumsum(padded)
    pad_start = pad_end - padded
    is_e = top_i[None] == jnp.arange(N_EXPERTS, dtype=jnp.int32)[:, None, None]
    dest = pos + jnp.sum(jnp.where(is_e, pad_start.astype(jnp.int32)[:, None, None], 0), axis=0)
    block_start = jnp.arange(n_blocks, dtype=jnp.int32) * blk
    block_e = jnp.minimum(jnp.sum(pad_end[None, :] <= block_start[:, None], axis=1), N_EXPERTS - 1).astype(jnp.int32)
    n_used = (pad_end[-1:] // blk).astype(jnp.int32)
    dest_blocks = dest.T.reshape(T // tm, 1, tm * TOP_K)

    xs = _dispatch(x, dest_blocks, pad_end.astype(jnp.int32), n_slots, tm, blk)
    out = _experts(xs, block_e + layer * N_EXPERTS, n_used, w1, b1, w2, b2, blk)
    return _collect_combine(out, dest_blocks, gates.T, x, ln_g, ln_b, tm)


def _pad_rows(t, n):
    return t if n == 0 else jnp.pad(t, ((0, 0), (0, n), (0, 0)))


def _trunk(x, gla_s0, past_k, past_v, past_valid, mem_k, mem_v, w):
    B, L, _ = x.shape
    Lp = -(-L // CHUNK) * CHUNK
    T = B * L
    vec = lambda a: a.reshape(1, -1)
    gla_states = []
    k_all = v_all = new_k = new_v = None
    for layer in range(DEPTH):
        lg, lb = w['ln_g'][layer], w['ln_b'][layer]
        xt = x.reshape(T, D_MODEL)
        if layer < N_A:
            i = layer
            xt = _pad_rows(x, Lp - L).reshape(B * Lp, D_MODEL)
            q, k, v, r, b = _gla_in(xt, w['gla_w_main'][i], w['gla_w_g'][i], w['gla_w_g2'][i],
                                    vec(w['gla_b_g'][i]), L, Lp)
            seq = lambda t: t.reshape(B, Lp, -1)
            s0t = jnp.swapaxes(gla_s0[i], -1, -2)
            o, st = _gla_scan(seq(q), seq(k), seq(v), seq(b), s0t)
            gla_states.append(jnp.swapaxes(st, -1, -2))
            xt = _gla_out(o.reshape(B * Lp, GLA_V), r, xt, vec(w['gla_gn_g'][i]), w['gla_w_o'][i],
                          vec(lg[0]), vec(lb[0]))
            x = xt.reshape(B, Lp, D_MODEL)[:, :L]
        else:
            j = layer - N_A
            if j == 0:
                k_new, v_new = _linear_kv(xt, w['kv_w'][None])
                k_full = jnp.concatenate([past_k, k_new.reshape(B, L, SWA_KV)], axis=1)
                v_full = jnp.concatenate([past_v, v_new.reshape(B, L, SWA_KV)], axis=1)
                new_k, new_v = k_full[:, -WINDOW:], v_full[:, -WINDOW:]
                k_all, v_all = _pad_rows(k_full, Lp - L), _pad_rows(v_full, Lp - L)
            xp = _swa(_pad_rows(x, Lp - L), k_all, v_all, w['swa_w_q'][j], w['swa_sinks'][j].reshape(-1, 1),
                      w['swa_w_o'][j], vec(lg[0]), vec(lb[0]), L, past_valid)
            x = xp[:, :L]
        x = _mem_xattn(x, mem_k, mem_v, layer, w['mem_w_q'][layer], w['mem_w_o'][layer], vec(lg[1]), vec(lb[1]))
        xt = _moe(x.reshape(T, D_MODEL), layer, w['moe_w_r'][layer], w['moe_b_r'][layer], w['moe_w1'],
                  w['moe_b1'], w['moe_w2'], w['moe_b2'], vec(lg[2]), vec(lb[2]))
        x = xt.reshape(B, L, D_MODEL)
    return x, jnp.stack(gla_states), new_k, new_v


def _prep_weights(gla_w_in, gla_w_g2, gla_b_g, gla_gn_g, gla_w_o, kv_w, swa_w_q, swa_sinks, swa_w_o,
                  mem_w_q, mem_w_kv, mem_w_o, moe_w_r, moe_b_r, moe_w1, moe_b1, moe_w2, moe_b2, ln_g, ln_b):
    n_main = 2 * GLA_QK + 2 * GLA_V
    pad_c = lambda a, n: jnp.pad(a, [(0, 0)] * (a.ndim - 1) + [(0, n - a.shape[-1])])
    w = dict(
        gla_w_main=gla_w_in[:, :, :n_main].astype(BF16),
        gla_w_g=pad_c(gla_w_in[:, :, n_main:], LANES).astype(BF16),
        gla_w_g2=jnp.pad(gla_w_g2, ((0, 0), (0, LANES - GLA_GATE_RANK), (0, 0))).astype(BF16),
        gla_b_g=gla_b_g, gla_gn_g=gla_gn_g.reshape(N_A, GLA_V), gla_w_o=gla_w_o.astype(BF16),
        kv_w=kv_w.astype(BF16), swa_w_q=swa_w_q.astype(BF16), swa_sinks=swa_sinks, swa_w_o=swa_w_o.astype(BF16),
        mem_w_q=mem_w_q.astype(BF16), mem_w_kv=mem_w_kv.astype(BF16), mem_w_o=mem_w_o.astype(BF16),
        moe_w_r=jnp.swapaxes(moe_w_r, -1, -2).astype(BF16), moe_b_r=moe_b_r.reshape(DEPTH, N_EXPERTS, 1),
        moe_w1=_w1_relayout(moe_w1.reshape(DEPTH * N_EXPERTS, D_MODEL, 2 * D_EXPERT)),
        moe_b1=jnp.concatenate([moe_b1[..., 0::2], moe_b1[..., 1::2]], axis=-1).reshape(DEPTH * N_EXPERTS, 1, -1),
        moe_w2=moe_w2.astype(BF16).reshape(DEPTH * N_EXPERTS, D_EXPERT, D_MODEL),
        moe_b2=moe_b2.reshape(DEPTH * N_EXPERTS, 1, D_MODEL),
        ln_g=ln_g, ln_b=ln_b)
    return w


def kernel(x_prompt, x_sample, state_gla, cache_swa_k, cache_swa_v, cache_mem_k, cache_mem_v, mem_prompt, gla_w_in, gla_w_g2, gla_b_g, gla_gn_g, gla_w_o, kv_w, swa_w_q, swa_sinks, swa_w_o, mem_w_q, mem_w_kv, mem_w_o, moe_w_r, moe_b_r, moe_w1, moe_b1, moe_w2, moe_b2, ln_g, ln_b):
    w = _prep_weights(gla_w_in, gla_w_g2, gla_b_g, gla_gn_g, gla_w_o, kv_w, swa_w_q, swa_sinks, swa_w_o,
                      mem_w_q, mem_w_kv, mem_w_o, moe_w_r, moe_b_r, moe_w1, moe_b1, moe_w2, moe_b2, ln_g, ln_b)
    bp = x_prompt.shape[0]
    bs = x_sample.shape[0]
    mem_flat = mem_prompt.reshape(bp * N_MEM, D_MODEL)
    mem_k_p, mem_v_p = _linear_kv(mem_flat, w['mem_w_kv'])
    gla0 = jnp.zeros((N_A, bp, GLA_HEADS, GLA_DK, GLA_DV), F32)
    zero_win = jnp.zeros((bp, WINDOW, SWA_KV), F32)
    y_p, gla_p, k_p, v_p = _trunk(x_prompt, gla0, zero_win, zero_win, False,
                                  mem_k_p.reshape(DEPTH * bp, N_MEM, D_MODEL),
                                  mem_v_p.reshape(DEPTH * bp, N_MEM, D_MODEL), w)
    y_s, gla_s, k_s, v_s = _trunk(x_sample, state_gla, cache_swa_k.reshape(bs, WINDOW, SWA_KV),
                                  cache_swa_v.reshape(bs, WINDOW, SWA_KV), True,
                                  cache_mem_k.reshape(DEPTH * bs, N_MEM, D_MODEL),
                                  cache_mem_v.reshape(DEPTH * bs, N_MEM, D_MODEL), w)
    heads4 = lambda t: t.reshape(t.shape[0], WINDOW, SWA_KV_HEADS, SWA_HEAD_DIM)
    mem5 = lambda t: t.reshape(DEPTH, bp, N_MEM, MEM_HEADS, MEM_HEAD_DIM)
    return (y_p, y_s, gla_p, gla_s, heads4(k_p), heads4(v_p), heads4(k_s), heads4(v_s), mem5(mem_k_p), mem5(mem_v_p))
```

```python
import functools

import jax
import jax.numpy as jnp
from jax import lax
from jax.experimental import pallas as pl
from jax.experimental.pallas import tpu as pltpu

F32 = jnp.float32
BF16 = jnp.bfloat16

D_MODEL = 1024
DEPTH = 4
CHUNK = 64
N_A = DEPTH // 2
GLA_HEADS = 4
GLA_DK = D_MODEL // (2 * GLA_HEADS)
GLA_DV = D_MODEL // GLA_HEADS
GLA_QK = GLA_HEADS * GLA_DK
GLA_V = GLA_HEADS * GLA_DV
GLA_GATE_RANK = 16
GLA_GATE_TEMP = 16.0
SWA_HEAD_DIM = 64
SWA_Q_HEADS = D_MODEL // SWA_HEAD_DIM
SWA_KV_HEADS = 4
SWA_GROUP = SWA_Q_HEADS // SWA_KV_HEADS
SWA_KV = SWA_KV_HEADS * SWA_HEAD_DIM
WINDOW = 128
N_MEM = 256
MEM_HEADS = 4
MEM_HEAD_DIM = D_MODEL // MEM_HEADS
N_EXPERTS = 32
TOP_K = 4
D_EXPERT = D_MODEL
SWIGLU_ALPHA = 1.702
SWIGLU_LIMIT = 7.0
DEEPNORM_ALPHA = (2 * DEPTH) ** 0.25
LN_EPS = 1e-5
NEG_INF = -1e30

LANES = 128
VMEM_LIMIT = 56 * 1024 * 1024


def _params(*sem):
    return pltpu.CompilerParams(dimension_semantics=sem, vmem_limit_bytes=VMEM_LIMIT)


def _row_tile(n_rows, want):
    t = min(want, n_rows)
    while n_rows % t:
        t //= 2
    return t


def _full(shape):
    nd = len(shape)
    return pl.BlockSpec(shape, lambda *_: (0,) * nd)


def _bdot(a, b):
    return jnp.dot(a.astype(BF16), b.astype(BF16), preferred_element_type=F32)


def _bdot_nt(a, b):
    return lax.dot_general(a.astype(BF16), b.astype(BF16), (((1,), (1,)), ((), ())),
                           preferred_element_type=F32)


def _bdot_tn(a, b):
    return lax.dot_general(a.astype(BF16), b.astype(BF16), (((0,), (0,)), ((), ())),
                           preferred_element_type=F32)


def _layer_norm(z, g, b):
    mu = jnp.mean(z, axis=-1, keepdims=True)
    zc = z - mu
    var = jnp.mean(zc * zc, axis=-1, keepdims=True)
    return zc * lax.rsqrt(var + LN_EPS) * g + b


ROW_TILES = D_MODEL // LANES


def _rows_to_tiles(ref, x):
    n = x.shape[0]
    for c in range(ROW_TILES):
        ref[pl.ds(c, n, stride=ROW_TILES), :] = x[:, c * LANES:(c + 1) * LANES]


def _tiles_to_rows(ref, n):
    return jnp.concatenate([ref[pl.ds(c, n, stride=ROW_TILES), :] for c in range(ROW_TILES)], axis=1)


def _linear_kv_kernel(x_ref, w_ref, k_ref, v_ref):
    y = _bdot(x_ref[...], w_ref[0])
    half = y.shape[1] // 2
    k_ref[0] = y[:, :half]
    v_ref[0] = y[:, half:]


def _linear_kv(x, w, tm=512):
    T, K = x.shape
    G, _, N2 = w.shape
    tm = _row_tile(T, tm)
    out = jax.ShapeDtypeStruct((G, T, N2 // 2), F32)
    o_spec = pl.BlockSpec((1, tm, N2 // 2), lambda g, i: (g, i, 0))
    return pl.pallas_call(
        _linear_kv_kernel,
        out_shape=(out, out),
        grid=(G, T // tm),
        in_specs=[pl.BlockSpec((tm, K), lambda g, i: (i, 0)), pl.BlockSpec((1, K, N2), lambda g, i: (g, 0, 0))],
        out_specs=(o_spec, o_spec),
        compiler_params=_params("parallel", "parallel"),
        name="linear_kv",
    )(x, w)


MXU_DIM = 256


def _w1_relayout_kernel(w_ref, o_ref):
    half = MXU_DIM // 2
    r = lax.broadcasted_iota(jnp.int32, (MXU_DIM, MXU_DIM), 0)
    c = lax.broadcasted_iota(jnp.int32, (MXU_DIM, MXU_DIM), 1)
    src = jnp.where(c < half, 2 * c, 2 * (c - half) + 1)
    perm = (r == src).astype(BF16)
    n_out = o_ref.shape[-1] // 2
    for j in range(w_ref.shape[-1] // MXU_DIM):
        y = jnp.dot(w_ref[0, :, j * MXU_DIM:(j + 1) * MXU_DIM].astype(BF16), perm, preferred_element_type=F32)
        o_ref[0, :, j * half:(j + 1) * half] = y[:, :half].astype(BF16)
        o_ref[0, :, n_out + j * half:n_out + (j + 1) * half] = y[:, half:].astype(BF16)


def _w1_relayout(w1):
    E, D, N = w1.shape
    spec = pl.BlockSpec((1, D, N), lambda e: (e, 0, 0))
    return pl.pallas_call(
        _w1_relayout_kernel,
        out_shape=jax.ShapeDtypeStruct((E, D, N), BF16),
        grid=(E,),
        in_specs=[spec],
        out_specs=spec,
        compiler_params=_params("parallel"),
        name="w1_relayout",
    )(w1)


def _gla_in_kernel(x_ref, w_ref, wg_ref, wg2_ref, bg_ref, q_ref, k_ref, v_ref, r_ref, b_ref,
                   *, tm, seq_len, seq_pad):
    xb = x_ref[...].astype(BF16)
    y = jnp.dot(xb, w_ref[...], preferred_element_type=F32)
    q_ref[...] = y[:, :GLA_QK] * (GLA_DK ** -0.5)
    k_ref[...] = y[:, GLA_QK:2 * GLA_QK]
    v_ref[...] = y[:, 2 * GLA_QK:2 * GLA_QK + GLA_V]
    r_ref[...] = y[:, 2 * GLA_QK + GLA_V:]
    g_lr = jnp.dot(xb, wg_ref[...], preferred_element_type=F32)
    z = _bdot(g_lr, wg2_ref[...]) + bg_ref[...]
    log_a = (jnp.minimum(z, 0.0) - jnp.log(1.0 + jnp.exp(-jnp.abs(z)))) / GLA_GATE_TEMP
    if seq_len < seq_pad:
        pos = (pl.program_id(0) * tm + lax.broadcasted_iota(jnp.int32, (tm, 1), 0)) % seq_pad
        log_a = jnp.where(pos < seq_len, log_a, 0.0)
    row = lax.broadcasted_iota(jnp.int32, (tm, tm), 0)
    col = lax.broadcasted_iota(jnp.int32, (tm, tm), 1)
    tri = ((row // CHUNK == col // CHUNK) & (row >= col)).astype(BF16)
    g1 = log_a.astype(BF16)
    r1 = log_a - g1.astype(F32)
    g2 = r1.astype(BF16)
    g3 = (r1 - g2.astype(F32)).astype(BF16)
    b_ref[...] = (jnp.dot(tri, g1, preferred_element_type=F32) + jnp.dot(tri, g2, preferred_element_type=F32)
                  + jnp.dot(tri, g3, preferred_element_type=F32))


def _gla_in(x, w_main, w_g, w_g2, b_g, seq_len, seq_pad, tm=512):
    T = x.shape[0]
    tm = _row_tile(T, tm)
    assert tm % CHUNK == 0 and seq_pad % CHUNK == 0
    n_main = w_main.shape[1]
    row = lambda n: pl.BlockSpec((tm, n), lambda i: (i, 0))
    return pl.pallas_call(
        functools.partial(_gla_in_kernel, tm=tm, seq_len=seq_len, seq_pad=seq_pad),
        out_shape=(jax.ShapeDtypeStruct((T, GLA_QK), F32), jax.ShapeDtypeStruct((T, GLA_QK), F32),
                   jax.ShapeDtypeStruct((T, GLA_V), F32), jax.ShapeDtypeStruct((T, GLA_V), F32),
                   jax.ShapeDtypeStruct((T, GLA_QK), F32)),
        grid=(T // tm,),
        in_specs=[row(D_MODEL), _full((D_MODEL, n_main)), _full((D_MODEL, LANES)),
                  _full((LANES, GLA_QK)), _full((1, GLA_QK))],
        out_specs=(row(GLA_QK), row(GLA_QK), row(GLA_V), row(GLA_V), row(GLA_QK)),
        compiler_params=_params("parallel"),
        name="gla_in",
    )(x, w_main, w_g, w_g2, b_g)


SUBLANES = 8
GLA_SUB = 16


def _gla_intra(q, k, b):
    n_grp = CHUNK // SUBLANES
    g_sub = GLA_SUB // SUBLANES
    lane = lax.broadcasted_iota(jnp.int32, (SUBLANES, LANES), 1)
    row = lax.broadcasted_iota(jnp.int32, (SUBLANES, LANES), 0)
    grp = lambda t, i: t[i * SUBLANES:(i + 1) * SUBLANES]
    att = [jnp.zeros((SUBLANES, LANES), F32) for _ in range(n_grp)]
    for s in range(CHUNK):
        b_s = b[s:s + 1, :]
        k_s = k[s:s + 1, :]
        for i in range(s // SUBLANES, (s // GLA_SUB + 1) * g_sub):
            e = jnp.exp(grp(b, i) - b_s)
            red = jnp.sum(grp(q, i) * e * k_s, axis=-1, keepdims=True)
            att[i] = jnp.where(lane == s, red, att[i])
    att = [jnp.where(row + i * SUBLANES >= lane, att[i], 0.0) for i in range(n_grp)]
    for blk in range(1, CHUNK // GLA_SUB):
        r0 = blk * GLA_SUB
        c = b[r0 - 1:r0, :]
        k_early = jnp.concatenate([k[:r0] * jnp.exp(c - b[:r0]), jnp.zeros((LANES - r0, GLA_DK), F32)], axis=0)
        off = _bdot_nt(q[r0:r0 + GLA_SUB] * jnp.exp(b[r0:r0 + GLA_SUB] - c), k_early)
        for i in range(g_sub):
            att[blk * g_sub + i] = att[blk * g_sub + i] + grp(off, i)
    return jnp.concatenate(att, axis=0)


def _gla_scan_kernel(q_ref, k_ref, v_ref, b_ref, s0_ref, o_ref, s_ref, st_scr, *, n_chunks):
    @pl.when(pl.program_id(1) == 0)
    def _():
        st_scr[...] = s0_ref[0]

    for c in range(n_chunks):
        ts = slice(c * CHUNK, (c + 1) * CHUNK)
        for h in range(GLA_HEADS):
            ks = slice(h * GLA_DK, (h + 1) * GLA_DK)
            vs = slice(h * GLA_DV, (h + 1) * GLA_DV)
            q = q_ref[0, ts, ks]
            k = k_ref[0, ts, ks]
            b = b_ref[0, ts, ks]
            v = v_ref[0, ts, vs]
            att = _gla_intra(q, k, b)
            b_end = b[CHUNK - 1:CHUNK, :]
            st = st_scr[h]
            o_ref[0, ts, vs] = _bdot(att[:, :CHUNK], v) + _bdot_nt(q * jnp.exp(b), st)
            st_scr[h] = st * jnp.exp(b_end) + _bdot_tn(v, k * jnp.exp(b_end - b))

    @pl.when(pl.program_id(1) == pl.num_programs(1) - 1)
    def _():
        s_ref[0] = st_scr[...]


def _gla_scan(q, k, v, la, s0t, tl=128):
    B, Lp, _ = q.shape
    tl = _row_tile(Lp, tl)
    qk_spec = pl.BlockSpec((1, tl, GLA_QK), lambda b, i: (b, i, 0))
    v_spec = pl.BlockSpec((1, tl, GLA_V), lambda b, i: (b, i, 0))
    s_spec = pl.BlockSpec((1, GLA_HEADS, GLA_DV, GLA_DK), lambda b, i: (b, 0, 0, 0))
    return pl.pallas_call(
        functools.partial(_gla_scan_kernel, n_chunks=tl // CHUNK),
        out_shape=(jax.ShapeDtypeStruct((B, Lp, GLA_V), F32),
                   jax.ShapeDtypeStruct((B, GLA_HEADS, GLA_DV, GLA_DK), F32)),
        grid=(B, Lp // tl),
        in_specs=[qk_spec, qk_spec, v_spec, qk_spec, s_spec],
        out_specs=(v_spec, s_spec),
        scratch_shapes=[pltpu.VMEM((GLA_HEADS, GLA_DV, GLA_DK), F32)],
        compiler_params=_params("parallel", "arbitrary"),
        name="gla_scan",
    )(q, k, v, la, s0t)


def _gla_out_kernel(o_ref, r_ref, x_ref, gn_ref, wo_ref, g_ref, b_ref, y_ref):
    parts = []
    for h in range(GLA_HEADS):
        vs = slice(h * GLA_DV, (h + 1) * GLA_DV)
        o = o_ref[:, vs]
        mu = jnp.mean(o, axis=-1, keepdims=True)
        oc = o - mu
        var = jnp.mean(oc * oc, axis=-1, keepdims=True)
        parts.append(oc * lax.rsqrt(var + LN_EPS) * gn_ref[:, vs])
    o = jnp.concatenate(parts, axis=-1)
    r = r_ref[...]
    o = o * (r * jax.nn.sigmoid(r))
    h = _bdot(o, wo_ref[...])
    y_ref[...] = _layer_norm(DEEPNORM_ALPHA * x_ref[...] + h, g_ref[...], b_ref[...])


def _gla_out(o, r, x, gn_g, w_o, ln_g, ln_b, tm=512):
    T = x.shape[0]
    tm = _row_tile(T, tm)
    row = pl.BlockSpec((tm, D_MODEL), lambda i: (i, 0))
    vec = _full((1, D_MODEL))
    return pl.pallas_call(
        _gla_out_kernel,
        out_shape=jax.ShapeDtypeStruct((T, D_MODEL), F32),
        grid=(T // tm,),
        in_specs=[row, row, row, vec, _full((GLA_V, D_MODEL)), vec, vec],
        out_specs=row,
        compiler_params=_params("parallel"),
        name="gla_out",
    )(o, r, x, gn_g, w_o, ln_g, ln_b)


def _swa_kernel(x_ref, k_ref, v_ref, wq_ref, sink_ref, wo_ref, g_ref, b_ref, y_ref,
                *, tl, seq_len, past_valid):
    x = x_ref[0]
    q = (_bdot(x, wq_ref[...]) * (SWA_HEAD_DIM ** -0.5)).astype(BF16)
    kw = tl + WINDOW
    r0 = pl.multiple_of(pl.program_id(1) * tl, tl)
    kt = k_ref[0, pl.ds(r0, kw), :].astype(BF16)
    vt = v_ref[0, pl.ds(r0, kw), :].astype(BF16)
    qs = min(tl, WINDOW)
    kws = qs + WINDOW
    q_row = lax.broadcasted_iota(jnp.int32, (qs, kws), 0)
    k_col = lax.broadcasted_iota(jnp.int32, (qs, kws), 1)
    band_lo = (q_row // CHUNK) * CHUNK
    in_band = (k_col >= band_lo) & (k_col < band_lo + WINDOW + CHUNK)
    oks = []
    for u in range(tl // qs):
        pos = r0 + u * qs + k_col
        ok = in_band & (pos - WINDOW < seq_len)
        oks.append(ok if past_valid else ok & (pos >= WINDOW))
    zeros = jnp.zeros((kw, SWA_HEAD_DIM), BF16)
    pair_out = [[] for _ in range(SWA_Q_HEADS // 2)]
    for g in range(SWA_KV_HEADS):
        kg = kt[:, g * SWA_HEAD_DIM:(g + 1) * SWA_HEAD_DIM]
        vg = vt[:, g * SWA_HEAD_DIM:(g + 1) * SWA_HEAD_DIM]
        k_pad = (jnp.concatenate([kg, zeros], axis=1), jnp.concatenate([zeros, kg], axis=1))
        v_pad = (jnp.concatenate([vg, zeros], axis=1), jnp.concatenate([zeros, vg], axis=1))
        for pair in range(g * SWA_GROUP // 2, (g + 1) * SWA_GROUP // 2):
            sinks = [sink_ref[2 * pair + j:2 * pair + j + 1, :] for j in range(2)]
            work = [(u, j) for u in range(tl // qs) for j in range(2)]
            kwin = [slice(u * qs, u * qs + kws) for u in range(tl // qs)]
            qp = [q[u * qs:(u + 1) * qs, pair * LANES:(pair + 1) * LANES] for u in range(tl // qs)]
            s = [jnp.where(oks[u], _bdot_nt(qp[u], k_pad[j][kwin[u]]), NEG_INF) for u, j in work]
            m = [jnp.maximum(jnp.max(s_i, axis=-1, keepdims=True), sinks[j]) for s_i, (u, j) in zip(s, work)]
            e = [jnp.exp(s_i - m_i) for s_i, m_i in zip(s, m)]
            o = [_bdot(e_i, v_pad[j][kwin[u]]) for e_i, (u, j) in zip(e, work)]
            inv = [1.0 / (jnp.sum(e_i, axis=-1, keepdims=True) + jnp.exp(sinks[j] - m_i))
                   for e_i, m_i, (u, j) in zip(e, m, work)]
            on = [o_i * inv_i for o_i, inv_i in zip(o, inv)]
            for u in range(tl // qs):
                pair_out[pair].append(on[2 * u] + on[2 * u + 1])
    pair_out = [jnp.concatenate(p, axis=0) for p in pair_out]
    h_out = _bdot(jnp.concatenate(pair_out, axis=1), wo_ref[...])
    y_ref[0] = _layer_norm(DEEPNORM_ALPHA * x + h_out, g_ref[...], b_ref[...])


def _swa(x, k_all, v_all, w_q, sinks, w_o, ln_g, ln_b, seq_len, past_valid, tl=256):
    B, Lp, _ = x.shape
    tl = _row_tile(Lp, tl)
    x_spec = pl.BlockSpec((1, tl, D_MODEL), lambda b, i: (b, i, 0))
    kv_spec = pl.BlockSpec((1, WINDOW + Lp, SWA_KV), lambda b, i: (b, 0, 0))
    vec = _full((1, D_MODEL))
    return pl.pallas_call(
        functools.partial(_swa_kernel, tl=tl, seq_len=seq_len, past_valid=past_valid),
        out_shape=jax.ShapeDtypeStruct((B, Lp, D_MODEL), F32),
        grid=(B, Lp // tl),
        in_specs=[x_spec, kv_spec, kv_spec, _full((D_MODEL, D_MODEL)), _full((SWA_Q_HEADS, 1)),
                  _full((D_MODEL, D_MODEL)), vec, vec],
        out_specs=x_spec,
        compiler_params=_params("parallel", "arbitrary"),
        name="swa",
    )(x, k_all, v_all, w_q, sinks, w_o, ln_g, ln_b)


def _mem_kernel(x_ref, mk_ref, mv_ref, wq_ref, wo_ref, g_ref, b_ref, y_ref):
    x = x_ref[0]
    q = _bdot(x, wq_ref[...]) * (MEM_HEAD_DIM ** -0.5)
    parts = []
    for h in range(MEM_HEADS):
        hs = slice(h * MEM_HEAD_DIM, (h + 1) * MEM_HEAD_DIM)
        s = _bdot_nt(q[:, hs], mk_ref[0, :, hs])
        m = jnp.max(s, axis=-1, keepdims=True)
        e = jnp.exp(s - m)
        parts.append(_bdot(e, mv_ref[0, :, hs]) * (1.0 / jnp.sum(e, axis=-1, keepdims=True)))
    o = jnp.concatenate(parts, axis=-1)
    h_out = _bdot(o, wo_ref[...])
    y_ref[0] = _layer_norm(DEEPNORM_ALPHA * x + h_out, g_ref[...], b_ref[...])


def _mem_xattn(x, mk, mv, layer, w_q, w_o, ln_g, ln_b, tl=512):
    B, L, _ = x.shape
    tl = _row_tile(L, tl)
    x_spec = pl.BlockSpec((1, tl, D_MODEL), lambda b, i: (b, i, 0))
    m_spec = pl.BlockSpec((1, N_MEM, D_MODEL), lambda b, i: (layer * B + b, 0, 0))
    vec = _full((1, D_MODEL))
    return pl.pallas_call(
        _mem_kernel,
        out_shape=jax.ShapeDtypeStruct((B, L, D_MODEL), F32),
        grid=(B, L // tl),
        in_specs=[x_spec, m_spec, m_spec, _full((D_MODEL, D_MODEL)), _full((D_MODEL, D_MODEL)), vec, vec],
        out_specs=x_spec,
        compiler_params=_params("parallel", "arbitrary"),
        name="mem_xattn",
    )(x, mk, mv, w_q, w_o, ln_g, ln_b)


def _router_kernel(x_ref, wrt_ref, br_ref, idx_ref, gate_ref, pos_ref, cnt_ref, cnt_scr, *, tm):
    @pl.when(pl.program_id(0) == 0)
    def _():
        cnt_scr[...] = jnp.zeros_like(cnt_scr)

    logits = _bdot_nt(wrt_ref[...], x_ref[...]) + br_ref[...]
    expert = lax.broadcasted_iota(jnp.int32, (N_EXPERTS, tm), 0).astype(F32)
    sel = jnp.zeros((N_EXPERTS, tm), F32)
    l = logits
    vals, hits = [], []
    for j in range(TOP_K):
        m = jnp.max(l, axis=0, keepdims=True)
        idx = jnp.min(jnp.where(l == m, expert, float(N_EXPERTS)), axis=0, keepdims=True)
        hit = expert == idx
        sel = jnp.where(hit, 1.0, sel)
        l = jnp.where(hit, -jnp.inf, l)
        idx_ref[j:j + 1, :] = idx.astype(jnp.int32)
        vals.append(m)
        hits.append(hit)
    e = [jnp.exp(v - vals[0]) for v in vals]
    inv = 1.0 / (e[0] + e[1] + e[2] + e[3])
    r = lax.broadcasted_iota(jnp.int32, (tm, tm), 0)
    c = lax.broadcasted_iota(jnp.int32, (tm, tm), 1)
    before = jnp.dot(sel.astype(BF16), (r < c).astype(BF16), preferred_element_type=F32) + cnt_scr[:, 0:1]
    for j in range(TOP_K):
        gate_ref[j:j + 1, :] = e[j] * inv
        pos_ref[j:j + 1, :] = jnp.sum(jnp.where(hits[j], before, 0.0), axis=0, keepdims=True).astype(jnp.int32)
    cnt_scr[...] = cnt_scr[...] + jnp.sum(sel, axis=1, keepdims=True)
    cnt_ref[...] = cnt_scr[...].astype(jnp.int32)


def _router(x, w_rt, b_r, tm=512):
    T = x.shape[0]
    tm = _row_tile(T, tm)
    small = pl.BlockSpec((TOP_K, tm), lambda i: (0, i))
    return pl.pallas_call(
        functools.partial(_router_kernel, tm=tm),
        out_shape=(jax.ShapeDtypeStruct((TOP_K, T), jnp.int32), jax.ShapeDtypeStruct((TOP_K, T), F32),
                   jax.ShapeDtypeStruct((TOP_K, T), jnp.int32), jax.ShapeDtypeStruct((N_EXPERTS, LANES), jnp.int32)),
        grid=(T // tm,),
        in_specs=[pl.BlockSpec((tm, D_MODEL), lambda i: (i, 0)), _full((N_EXPERTS, D_MODEL)), _full((N_EXPERTS, 1))],
        out_specs=(small, small, small, _full((N_EXPERTS, LANES))),
        scratch_shapes=[pltpu.VMEM((N_EXPERTS, LANES), F32)],
        compiler_params=_params("arbitrary"),
        name="moe_router",
    )(x, w_rt, b_r)


ISSUE_UNROLL = 16


def _dispatch_kernel(pad_end_ref, dest_ref, x_ref, xs_ref, rows_scr, zero_scr, sem, zsem, *, tm, blk):
    @pl.when(pl.program_id(0) == 0)
    def _():
        zero_scr[...] = jnp.zeros_like(zero_scr)

        def fill(e):
            start = pl.multiple_of((pad_end_ref[e] - blk) * ROW_TILES, blk * ROW_TILES)
            return pltpu.make_async_copy(zero_scr, xs_ref.at[pl.ds(start, blk * ROW_TILES), :], zsem)

        def has_block(e):
            return pad_end_ref[e] > (pad_end_ref[e - 1] if e else 0)

        for e in range(N_EXPERTS):
            pl.when(has_block(e))(lambda e=e: fill(e).start())
        for e in range(N_EXPERTS):
            pl.when(has_block(e))(lambda e=e: fill(e).wait())

    _rows_to_tiles(rows_scr, x_ref[...])

    def tile(ref, row):
        return ref.at[pl.ds(pl.multiple_of(row * ROW_TILES, ROW_TILES), ROW_TILES), :]

    def issue(g, carry):
        base = pl.multiple_of(g * (ISSUE_UNROLL // TOP_K), ISSUE_UNROLL // TOP_K)
        for u in range(ISSUE_UNROLL // TOP_K):
            r = base + u
            for j in range(TOP_K):
                pltpu.make_async_copy(tile(rows_scr, r), tile(xs_ref, dest_ref[0, 0, r * TOP_K + j]), sem).start(priority=j % 2)
        return carry

    lax.fori_loop(0, tm * TOP_K // ISSUE_UNROLL, issue, 0)
    for j in range(TOP_K):
        pltpu.make_async_copy(rows_scr, xs_ref.at[pl.ds(0, tm * ROW_TILES), :], sem).wait()


def _dispatch(x, dest_blocks, pad_end, n_slots, tm, blk):
    T = x.shape[0]
    return pl.pallas_call(
        functools.partial(_dispatch_kernel, tm=tm, blk=blk),
        out_shape=jax.ShapeDtypeStruct((n_slots * ROW_TILES, LANES), F32),
        grid_spec=pltpu.PrefetchScalarGridSpec(
            num_scalar_prefetch=1,
            grid=(T // tm,),
            in_specs=[pl.BlockSpec((1, 1, tm * TOP_K), lambda i, pe: (i, 0, 0), memory_space=pltpu.SMEM),
                      pl.BlockSpec((tm, D_MODEL), lambda i, pe: (i, 0))],
            out_specs=pl.BlockSpec(memory_space=pl.ANY),
            scratch_shapes=[pltpu.VMEM((tm * ROW_TILES, LANES), F32), pltpu.VMEM((blk * ROW_TILES, LANES), F32),
                            pltpu.SemaphoreType.DMA, pltpu.SemaphoreType.DMA]),
        compiler_params=_params("arbitrary"),
        name="moe_dispatch",
    )(pad_end, dest_blocks, x)


def _collect_combine_kernel(idx_ref, idx_next_ref, src_ref, gate_ref, x_ref, g_ref, b_ref, y_ref, buf, sem, *, tm):
    i = pl.program_id(0)
    slot = i % 2

    def tile(ref, row):
        return ref.at[pl.ds(pl.multiple_of(row * ROW_TILES, ROW_TILES), ROW_TILES), :]

    def request(ids, s):
        def body(g, carry):
            base = pl.multiple_of(g * (ISSUE_UNROLL // TOP_K), ISSUE_UNROLL // TOP_K)
            for u in range(ISSUE_UNROLL // TOP_K):
                r = base + u
                for j in range(TOP_K):
                    pltpu.make_async_copy(tile(src_ref, ids[0, 0, r * TOP_K + j]), tile(buf.at[s, j], r),
                                          sem.at[s]).start(priority=j % 2)
            return carry
        lax.fori_loop(0, tm * TOP_K // ISSUE_UNROLL, body, 0)

    @pl.when(i == 0)
    def _():
        request(idx_ref, 0)

    @pl.when(i + 1 < pl.num_programs(0))
    def _():
        request(idx_next_ref, 1 - slot)

    for j in range(TOP_K):
        pltpu.make_async_copy(src_ref.at[pl.ds(0, tm * ROW_TILES), :], buf.at[slot, j], sem.at[slot]).wait()
    gates = gate_ref[...]
    acc = DEEPNORM_ALPHA * x_ref[...]
    for j in range(TOP_K):
        acc = acc + gates[:, j:j + 1] * _tiles_to_rows(buf.at[slot, j], tm)
    y_ref[...] = _layer_norm(acc, g_ref[...], b_ref[...])


def _collect_combine(src, idx_blocks, gates, x, ln_g, ln_b, tm):
    T = x.shape[0]
    nt = T // tm
    row = pl.BlockSpec((tm, D_MODEL), lambda i: (i, 0))
    vec = _full((1, D_MODEL))
    return pl.pallas_call(
        functools.partial(_collect_combine_kernel, tm=tm),
        out_shape=jax.ShapeDtypeStruct((T, D_MODEL), F32),
        grid=(nt,),
        in_specs=[pl.BlockSpec((1, 1, tm * TOP_K), lambda i: (i, 0, 0), memory_space=pltpu.SMEM),
                  pl.BlockSpec((1, 1, tm * TOP_K), lambda i: (jnp.minimum(i + 1, nt - 1), 0, 0), memory_space=pltpu.SMEM),
                  pl.BlockSpec(memory_space=pl.ANY),
                  pl.BlockSpec((tm, TOP_K), lambda i: (i, 0)), row, vec, vec],
        out_specs=row,
        scratch_shapes=[pltpu.VMEM((2, TOP_K, tm * ROW_TILES, LANES), F32), pltpu.SemaphoreType.DMA((2,))],
        compiler_params=_params("arbitrary"),
        name="moe_collect_combine",
    )(idx_blocks, idx_blocks, src, gates, x, ln_g, ln_b)


def _expert_kernel(be_ref, nb_ref, xs_ref, w1_ref, b1_ref, w2_ref, b2_ref, o_ref, *, blk):
    @pl.when(pl.program_id(0) < nb_ref[0])
    def _():
        h = _bdot(_tiles_to_rows(xs_ref, blk), w1_ref[0]) + b1_ref[0]
        glu = jnp.minimum(h[:, :D_EXPERT], SWIGLU_LIMIT)
        lin = jnp.clip(h[:, D_EXPERT:], -SWIGLU_LIMIT, SWIGLU_LIMIT)
        a = glu * jax.nn.sigmoid(SWIGLU_ALPHA * glu) * (lin + 1.0)
        _rows_to_tiles(o_ref, _bdot(a, w2_ref[0]) + b2_ref[0])


def _experts(xs, block_e, n_used, w1, b1, w2, b2, blk):
    P = xs.shape[0] // ROW_TILES
    nb = P // blk

    def blk_map(i, be, nu):
        return (jnp.minimum(i, nu[0] - 1), 0)

    def w_map(i, be, nu):
        return (be[jnp.minimum(i, nu[0] - 1)], 0, 0)

    return pl.pallas_call(
        functools.partial(_expert_kernel, blk=blk),
        out_shape=jax.ShapeDtypeStruct((P * ROW_TILES, LANES), F32),
        grid_spec=pltpu.PrefetchScalarGridSpec(
            num_scalar_prefetch=2,
            grid=(nb,),
            in_specs=[pl.BlockSpec((blk * ROW_TILES, LANES), blk_map),
                      pl.BlockSpec((1, D_MODEL, 2 * D_EXPERT), w_map),
                      pl.BlockSpec((1, 1, 2 * D_EXPERT), w_map),
                      pl.BlockSpec((1, D_EXPERT, D_MODEL), w_map),
                      pl.BlockSpec((1, 1, D_MODEL), w_map)],
            out_specs=pl.BlockSpec((blk * ROW_TILES, LANES), blk_map)),
        compiler_params=_params("arbitrary"),
        name="moe_experts",
    )(block_e, n_used, xs, w1, b1, w2, b2)


def _moe(x, layer, w_r, b_r, w1, b1, w2, b2, ln_g, ln_b):
    T = x.shape[0]
    blk = 1024 if T >= 4096 else 128
    tm = _row_tile(T, 512)
    n_assign = T * TOP_K
    n_blocks = -(-(n_assign + N_EXPERTS * (blk - 1)) // blk)
    n_slots = n_blocks * blk

    top_i, gates, pos, counts = _router(x, w_r, b_r)
    counts = counts[:, 0]
    padded = (counts + blk - 1) // blk * blk
    pad_end = jnp.cumsum(padded)
    pad_start = pad_end - padded
    is_e = top_i[None] == jnp.arange(N_EXPERTS, dtype=jnp.int32)[:, None, None]
    dest = pos + jnp.sum(jnp.where(is_e, pad_start.astype(jnp.int32)[:, None, None], 0), axis=0)
    block_start = jnp.arange(n_blocks, dtype=jnp.int32) * blk
    block_e = jnp.minimum(jnp.sum(pad_end[None, :] <= block_start[:, None], axis=1), N_EXPERTS - 1).astype(jnp.int32)
    n_used = (pad_end[-1:] // blk).astype(jnp.int32)
    dest_blocks = dest.T.reshape(T // tm, 1, tm * TOP_K)

    xs = _dispatch(x, dest_blocks, pad_end.astype(jnp.int32), n_slots, tm, blk)
    out = _experts(xs, block_e + layer * N_EXPERTS, n_used, w1, b1, w2, b2, blk)
    return _collect_combine(out, dest_blocks, gates.T, x, ln_g, ln_b, tm)


def _pad_rows(t, n):
    return t if n == 0 else jnp.pad(t, ((0, 0), (0, n), (0, 0)))


def _trunk(x, gla_s0, past_k, past_v, past_valid, mem_k, mem_v, w):
    B, L, _ = x.shape
    Lp = -(-L // CHUNK) * CHUNK
    T = B * L
    vec = lambda a: a.reshape(1, -1)
    gla_states = []
    k_all = v_all = new_k = new_v = None
    for layer in range(DEPTH):
        lg, lb = w['ln_g'][layer], w['ln_b'][layer]
        xt = x.reshape(T, D_MODEL)
        if layer < N_A:
            i = layer
            xt = _pad_rows(x, Lp - L).reshape(B * Lp, D_MODEL)
            q, k, v, r, b = _gla_in(xt, w['gla_w_main'][i], w['gla_w_g'][i], w['gla_w_g2'][i],
                                    vec(w['gla_b_g'][i]), L, Lp)
            seq = lambda t: t.reshape(B, Lp, -1)
            s0t = jnp.swapaxes(gla_s0[i], -1, -2)
            o, st = _gla_scan(seq(q), seq(k), seq(v), seq(b), s0t)
            gla_states.append(jnp.swapaxes(st, -1, -2))
            xt = _gla_out(o.reshape(B * Lp, GLA_V), r, xt, vec(w['gla_gn_g'][i]), w['gla_w_o'][i],
                          vec(lg[0]), vec(lb[0]))
            x = xt.reshape(B, Lp, D_MODEL)[:, :L]
        else:
            j = layer - N_A
            if j == 0:
                k_new, v_new = _linear_kv(xt, w['kv_w'][None])
                k_full = jnp.concatenate([past_k, k_new.reshape(B, L, SWA_KV)], axis=1)
                v_full = jnp.concatenate([past_v, v_new.reshape(B, L, SWA_KV)], axis=1)
                new_k, new_v = k_full[:, -WINDOW:], v_full[:, -WINDOW:]
                k_all, v_all = _pad_rows(k_full, Lp - L), _pad_rows(v_full, Lp - L)
            xp = _swa(_pad_rows(x, Lp - L), k_all, v_all, w['swa_w_q'][j], w['swa_sinks'][j].reshape(-1, 1),
                      w['swa_w_o'][j], vec(lg[0]), vec(lb[0]), L, past_valid)
            x = xp[:, :L]
        x = _mem_xattn(x, mem_k, mem_v, layer, w['mem_w_q'][layer], w['mem_w_o'][layer], vec(lg[1]), vec(lb[1]))
        xt = _moe(x.reshape(T, D_MODEL), layer, w['moe_w_r'][layer], w['moe_b_r'][layer], w['moe_w1'],
                  w['moe_b1'], w['moe_w2'], w['moe_b2'], vec(lg[2]), vec(lb[2]))
        x = xt.reshape(B, L, D_MODEL)
    return x, jnp.stack(gla_states), new_k, new_v


def _prep_weights(gla_w_in, gla_w_g2, gla_b_g, gla_gn_g, gla_w_o, kv_w, swa_w_q, swa_sinks, swa_w_o,
                  mem_w_q, mem_w_kv, mem_w_o, moe_w_r, moe_b_r, moe_w1, moe_b1, moe_w2, moe_b2, ln_g, ln_b):
    n_main = 2 * GLA_QK + 2 * GLA_V
    pad_c = lambda a, n: jnp.pad(a, [(0, 0)] * (a.ndim - 1) + [(0, n - a.shape[-1])])
    w = dict(
        gla_w_main=gla_w_in[:, :, :n_main].astype(BF16),
        gla_w_g=pad_c(gla_w_in[:, :, n_main:], LANES).astype(BF16),
        gla_w_g2=jnp.pad(gla_w_g2, ((0, 0), (0, LANES - GLA_GATE_RANK), (0, 0))).astype(BF16),
        gla_b_g=gla_b_g, gla_gn_g=gla_gn_g.reshape(N_A, GLA_V), gla_w_o=gla_w_o.astype(BF16),
        kv_w=kv_w.astype(BF16), swa_w_q=swa_w_q.astype(BF16), swa_sinks=swa_sinks, swa_w_o=swa_w_o.astype(BF16),
        mem_w_q=mem_w_q.astype(BF16), mem_w_kv=mem_w_kv.astype(BF16), mem_w_o=mem_w_o.astype(BF16),
        moe_w_r=jnp.swapaxes(moe_w_r, -1, -2).astype(BF16), moe_b_r=moe_b_r.reshape(DEPTH, N_EXPERTS, 1),
        moe_w1=_w1_relayout(moe_w1.reshape(DEPTH * N_EXPERTS, D_MODEL, 2 * D_EXPERT)),
        moe_b1=jnp.concatenate([moe_b1[..., 0::2], moe_b1[..., 1::2]], axis=-1).reshape(DEPTH * N_EXPERTS, 1, -1),
        moe_w2=moe_w2.astype(BF16).reshape(DEPTH * N_EXPERTS, D_EXPERT, D_MODEL),
        moe_b2=moe_b2.reshape(DEPTH * N_EXPERTS, 1, D_MODEL),
        ln_g=ln_g, ln_b=ln_b)
    return w


def kernel(x_prompt, x_sample, state_gla, cache_swa_k, cache_swa_v, cache_mem_k, cache_mem_v, mem_prompt, gla_w_in, gla_w_g2, gla_b_g, gla_gn_g, gla_w_o, kv_w, swa_w_q, swa_sinks, swa_w_o, mem_w_q, mem_w_kv, mem_w_o, moe_w_r, moe_b_r, moe_w1, moe_b1, moe_w2, moe_b2, ln_g, ln_b):
    w = _prep_weights(gla_w_in, gla_w_g2, gla_b_g, gla_gn_g, gla_w_o, kv_w, swa_w_q, swa_sinks, swa_w_o,
                      mem_w_q, mem_w_kv, mem_w_o, moe_w_r, moe_b_r, moe_w1, moe_b1, moe_w2, moe_b2, ln_g, ln_b)
    bp = x_prompt.shape[0]
    bs = x_sample.shape[0]
    mem_flat = mem_prompt.reshape(bp * N_MEM, D_MODEL)
    mem_k_p, mem_v_p = _linear_kv(mem_flat, w['mem_w_kv'])
    gla0 = jnp.zeros((N_A, bp, GLA_HEADS, GLA_DK, GLA_DV), F32)
    zero_win = jnp.zeros((bp, WINDOW, SWA_KV), F32)
    y_p, gla_p, k_p, v_p = _trunk(x_prompt, gla0, zero_win, zero_win, False,
                                  mem_k_p.reshape(DEPTH * bp, N_MEM, D_MODEL),
                                  mem_v_p.reshape(DEPTH * bp, N_MEM, D_MODEL), w)
    y_s, gla_s, k_s, v_s = _trunk(x_sample, state_gla, cache_swa_k.reshape(bs, WINDOW, SWA_KV),
                                  cache_swa_v.reshape(bs, WINDOW, SWA_KV), True,
                                  cache_mem_k.reshape(DEPTH * bs, N_MEM, D_MODEL),
                                  cache_mem_v.reshape(DEPTH * bs, N_MEM, D_MODEL), w)
    heads4 = lambda t: t.reshape(t.shape[0], WINDOW, SWA_KV_HEADS, SWA_HEAD_DIM)
    mem5 = lambda t: t.reshape(DEPTH, bp, N_MEM, MEM_HEADS, MEM_HEAD_DIM)
    return (y_p, y_s, gla_p, gla_s, heads4(k_p), heads4(v_p), heads4(k_s), heads4(v_s), mem5(mem_k_p), mem5(mem_v_p))
```

```python
import functools

import jax
import jax.numpy as jnp
from jax import lax
from jax.experimental import pallas as pl
from jax.experimental.pallas import tpu as pltpu

F32 = jnp.float32
BF16 = jnp.bfloat16

D_MODEL = 1024
DEPTH = 4
CHUNK = 64
N_A = DEPTH // 2
GLA_HEADS = 4
GLA_DK = D_MODEL // (2 * GLA_HEADS)
GLA_DV = D_MODEL // GLA_HEADS
GLA_QK = GLA_HEADS * GLA_DK
GLA_V = GLA_HEADS * GLA_DV
GLA_GATE_RANK = 16
GLA_GATE_TEMP = 16.0
SWA_HEAD_DIM = 64
SWA_Q_HEADS = D_MODEL // SWA_HEAD_DIM
SWA_KV_HEADS = 4
SWA_GROUP = SWA_Q_HEADS // SWA_KV_HEADS
SWA_KV = SWA_KV_HEADS * SWA_HEAD_DIM
WINDOW = 128
N_MEM = 256
MEM_HEADS = 4
MEM_HEAD_DIM = D_MODEL // MEM_HEADS
N_EXPERTS = 32
TOP_K = 4
D_EXPERT = D_MODEL
SWIGLU_ALPHA = 1.702
SWIGLU_LIMIT = 7.0
DEEPNORM_ALPHA = (2 * DEPTH) ** 0.25
LN_EPS = 1e-5
NEG_INF = -1e30

LANES = 128
VMEM_LIMIT = 56 * 1024 * 1024


def _params(*sem):
    return pltpu.CompilerParams(dimension_semantics=sem, vmem_limit_bytes=VMEM_LIMIT)


def _row_tile(n_rows, want):
    t = min(want, n_rows)
    while n_rows % t:
        t //= 2
    return t


def _full(shape):
    nd = len(shape)
    return pl.BlockSpec(shape, lambda *_: (0,) * nd)


def _bdot(a, b):
    return jnp.dot(a.astype(BF16), b.astype(BF16), preferred_element_type=F32)


def _bdot_nt(a, b):
    return lax.dot_general(a.astype(BF16), b.astype(BF16), (((1,), (1,)), ((), ())),
                           preferred_element_type=F32)


def _bdot_tn(a, b):
    return lax.dot_general(a.astype(BF16), b.astype(BF16), (((0,), (0,)), ((), ())),
                           preferred_element_type=F32)


def _layer_norm(z, g, b):
    mu = jnp.mean(z, axis=-1, keepdims=True)
    zc = z - mu
    var = jnp.mean(zc * zc, axis=-1, keepdims=True)
    return zc * lax.rsqrt(var + LN_EPS) * g + b


ROW_TILES = D_MODEL // LANES


def _rows_to_tiles(ref, x):
    n = x.shape[0]
    for c in range(ROW_TILES):
        ref[pl.ds(c, n, stride=ROW_TILES), :] = x[:, c * LANES:(c + 1) * LANES]


def _tiles_to_rows(ref, n):
    return jnp.concatenate([ref[pl.ds(c, n, stride=ROW_TILES), :] for c in range(ROW_TILES)], axis=1)


def _linear_kv_kernel(x_ref, w_ref, k_ref, v_ref):
    y = _bdot(x_ref[...], w_ref[0])
    half = y.shape[1] // 2
    k_ref[0] = y[:, :half]
    v_ref[0] = y[:, half:]


def _linear_kv(x, w, tm=512):
    T, K = x.shape
    G, _, N2 = w.shape
    tm = _row_tile(T, tm)
    out = jax.ShapeDtypeStruct((G, T, N2 // 2), F32)
    o_spec = pl.BlockSpec((1, tm, N2 // 2), lambda g, i: (g, i, 0))
    return pl.pallas_call(
        _linear_kv_kernel,
        out_shape=(out, out),
        grid=(G, T // tm),
        in_specs=[pl.BlockSpec((tm, K), lambda g, i: (i, 0)), pl.BlockSpec((1, K, N2), lambda g, i: (g, 0, 0))],
        out_specs=(o_spec, o_spec),
        compiler_params=_params("parallel", "parallel"),
        name="linear_kv",
    )(x, w)


MXU_DIM = 256


def _w1_relayout_kernel(w_ref, o_ref):
    half = MXU_DIM // 2
    r = lax.broadcasted_iota(jnp.int32, (MXU_DIM, MXU_DIM), 0)
    c = lax.broadcasted_iota(jnp.int32, (MXU_DIM, MXU_DIM), 1)
    src = jnp.where(c < half, 2 * c, 2 * (c - half) + 1)
    perm = (r == src).astype(BF16)
    n_out = o_ref.shape[-1] // 2
    for j in range(w_ref.shape[-1] // MXU_DIM):
        y = jnp.dot(w_ref[0, :, j * MXU_DIM:(j + 1) * MXU_DIM].astype(BF16), perm, preferred_element_type=F32)
        o_ref[0, :, j * half:(j + 1) * half] = y[:, :half].astype(BF16)
        o_ref[0, :, n_out + j * half:n_out + (j + 1) * half] = y[:, half:].astype(BF16)


def _w1_relayout(w1):
    E, D, N = w1.shape
    spec = pl.BlockSpec((1, D, N), lambda e: (e, 0, 0))
    return pl.pallas_call(
        _w1_relayout_kernel,
        out_shape=jax.ShapeDtypeStruct((E, D, N), BF16),
        grid=(E,),
        in_specs=[spec],
        out_specs=spec,
        compiler_params=_params("parallel"),
        name="w1_relayout",
    )(w1)


def _gla_in_kernel(x_ref, w_ref, wg_ref, wg2_ref, bg_ref, q_ref, k_ref, v_ref, r_ref, b_ref,
                   *, tm, seq_len, seq_pad):
    xb = x_ref[...].astype(BF16)
    y = jnp.dot(xb, w_ref[...], preferred_element_type=F32)
    q_ref[...] = y[:, :GLA_QK] * (GLA_DK ** -0.5)
    k_ref[...] = y[:, GLA_QK:2 * GLA_QK]
    v_ref[...] = y[:, 2 * GLA_QK:2 * GLA_QK + GLA_V]
    r_ref[...] = y[:, 2 * GLA_QK + GLA_V:]
    g_lr = jnp.dot(xb, wg_ref[...], preferred_element_type=F32)
    z = _bdot(g_lr, wg2_ref[...]) + bg_ref[...]
    log_a = (jnp.minimum(z, 0.0) - jnp.log(1.0 + jnp.exp(-jnp.abs(z)))) / GLA_GATE_TEMP
    if seq_len < seq_pad:
        pos = (pl.program_id(0) * tm + lax.broadcasted_iota(jnp.int32, (tm, 1), 0)) % seq_pad
        log_a = jnp.where(pos < seq_len, log_a, 0.0)
    row = lax.broadcasted_iota(jnp.int32, (tm, tm), 0)
    col = lax.broadcasted_iota(jnp.int32, (tm, tm), 1)
    tri = ((row // CHUNK == col // CHUNK) & (row >= col)).astype(BF16)
    g1 = log_a.astype(BF16)
    r1 = log_a - g1.astype(F32)
    g2 = r1.astype(BF16)
    g3 = (r1 - g2.astype(F32)).astype(BF16)
    b_ref[...] = (jnp.dot(tri, g1, preferred_element_type=F32) + jnp.dot(tri, g2, preferred_element_type=F32)
                  + jnp.dot(tri, g3, preferred_element_type=F32))


def _gla_in(x, w_main, w_g, w_g2, b_g, seq_len, seq_pad, tm=512):
    T = x.shape[0]
    tm = _row_tile(T, tm)
    assert tm % CHUNK == 0 and seq_pad % CHUNK == 0
    n_main = w_main.shape[1]
    row = lambda n: pl.BlockSpec((tm, n), lambda i: (i, 0))
    return pl.pallas_call(
        functools.partial(_gla_in_kernel, tm=tm, seq_len=seq_len, seq_pad=seq_pad),
        out_shape=(jax.ShapeDtypeStruct((T, GLA_QK), F32), jax.ShapeDtypeStruct((T, GLA_QK), F32),
                   jax.ShapeDtypeStruct((T, GLA_V), F32), jax.ShapeDtypeStruct((T, GLA_V), F32),
                   jax.ShapeDtypeStruct((T, GLA_QK), F32)),
        grid=(T // tm,),
        in_specs=[row(D_MODEL), _full((D_MODEL, n_main)), _full((D_MODEL, LANES)),
                  _full((LANES, GLA_QK)), _full((1, GLA_QK))],
        out_specs=(row(GLA_QK), row(GLA_QK), row(GLA_V), row(GLA_V), row(GLA_QK)),
        compiler_params=_params("parallel"),
        name="gla_in",
    )(x, w_main, w_g, w_g2, b_g)


SUBLANES = 8
GLA_SUB = 16


def _gla_intra(q, k, b):
    n_grp = CHUNK // SUBLANES
    g_sub = GLA_SUB // SUBLANES
    lane = lax.broadcasted_iota(jnp.int32, (SUBLANES, LANES), 1)
    row = lax.broadcasted_iota(jnp.int32, (SUBLANES, LANES), 0)
    grp = lambda t, i: t[i * SUBLANES:(i + 1) * SUBLANES]
    att = [jnp.zeros((SUBLANES, LANES), F32) for _ in range(n_grp)]
    for s in range(CHUNK):
        b_s = b[s:s + 1, :]
        k_s = k[s:s + 1, :]
        for i in range(s // SUBLANES, (s // GLA_SUB + 1) * g_sub):
            e = jnp.exp(grp(b, i) - b_s)
            red = jnp.sum(grp(q, i) * e * k_s, axis=-1, keepdims=True)
            att[i] = jnp.where(lane == s, red, att[i])
    att = [jnp.where(row + i * SUBLANES >= lane, att[i], 0.0) for i in range(n_grp)]
    for blk in range(1, CHUNK // GLA_SUB):
        r0 = blk * GLA_SUB
        c = b[r0 - 1:r0, :]
        k_early = jnp.concatenate([k[:r0] * jnp.exp(c - b[:r0]), jnp.zeros((LANES - r0, GLA_DK), F32)], axis=0)
        off = _bdot_nt(q[r0:r0 + GLA_SUB] * jnp.exp(b[r0:r0 + GLA_SUB] - c), k_early)
        for i in range(g_sub):
            att[blk * g_sub + i] = att[blk * g_sub + i] + grp(off, i)
    return jnp.concatenate(att, axis=0)


def _gla_scan_kernel(q_ref, k_ref, v_ref, b_ref, s0_ref, r_ref, x_ref, gn_ref, wo_ref, g_ref, lb_ref,
                     y_ref, s_ref, st_scr, o_scr, *, n_chunks):
    @pl.when(pl.program_id(1) == 0)
    def _():
        st_scr[...] = s0_ref[0]

    for c in range(n_chunks):
        ts = slice(c * CHUNK, (c + 1) * CHUNK)
        for h in range(GLA_HEADS):
            ks = slice(h * GLA_DK, (h + 1) * GLA_DK)
            vs = slice(h * GLA_DV, (h + 1) * GLA_DV)
            q = q_ref[0, ts, ks]
            k = k_ref[0, ts, ks]
            b = b_ref[0, ts, ks]
            v = v_ref[0, ts, vs]
            att = _gla_intra(q, k, b)
            b_end = b[CHUNK - 1:CHUNK, :]
            st = st_scr[h]
            o_scr[ts, vs] = _bdot(att[:, :CHUNK], v) + _bdot_nt(q * jnp.exp(b), st)
            st_scr[h] = st * jnp.exp(b_end) + _bdot_tn(v, k * jnp.exp(b_end - b))

    @pl.when(pl.program_id(1) == pl.num_programs(1) - 1)
    def _():
        s_ref[0] = st_scr[...]

    parts = []
    for h in range(GLA_HEADS):
        vs = slice(h * GLA_DV, (h + 1) * GLA_DV)
        o = o_scr[:, vs]
        mu = jnp.mean(o, axis=-1, keepdims=True)
        oc = o - mu
        var = jnp.mean(oc * oc, axis=-1, keepdims=True)
        parts.append(oc * lax.rsqrt(var + LN_EPS) * gn_ref[:, vs])
    r = r_ref[0]
    o = jnp.concatenate(parts, axis=-1) * (r * jax.nn.sigmoid(r))
    h_out = _bdot(o, wo_ref[...])
    y_ref[0] = _layer_norm(DEEPNORM_ALPHA * x_ref[0] + h_out, g_ref[...], lb_ref[...])


def _gla_scan(q, k, v, b, s0t, r, x, gn_g, w_o, ln_g, ln_b, tl=256):
    B, Lp, _ = q.shape
    tl = _row_tile(Lp, tl)
    qk_spec = pl.BlockSpec((1, tl, GLA_QK), lambda b, i: (b, i, 0))
    v_spec = pl.BlockSpec((1, tl, GLA_V), lambda b, i: (b, i, 0))
    s_spec = pl.BlockSpec((1, GLA_HEADS, GLA_DV, GLA_DK), lambda b, i: (b, 0, 0, 0))
    vec = _full((1, D_MODEL))
    return pl.pallas_call(
        functools.partial(_gla_scan_kernel, n_chunks=tl // CHUNK),
        out_shape=(jax.ShapeDtypeStruct((B, Lp, D_MODEL), F32),
                   jax.ShapeDtypeStruct((B, GLA_HEADS, GLA_DV, GLA_DK), F32)),
        grid=(B, Lp // tl),
        in_specs=[qk_spec, qk_spec, v_spec, qk_spec, s_spec, v_spec, v_spec, vec, _full((GLA_V, D_MODEL)), vec, vec],
        out_specs=(v_spec, s_spec),
        scratch_shapes=[pltpu.VMEM((GLA_HEADS, GLA_DV, GLA_DK), F32), pltpu.VMEM((tl, GLA_V), F32)],
        compiler_params=_params("parallel", "arbitrary"),
        name="gla_scan",
    )(q, k, v, b, s0t, r, x, gn_g, w_o, ln_g, ln_b)


def _swa_kernel(x_ref, k_ref, v_ref, wq_ref, sink_ref, wo_ref, g_ref, b_ref, y_ref,
                *, tl, seq_len, past_valid):
    x = x_ref[0]
    q = (_bdot(x, wq_ref[...]) * (SWA_HEAD_DIM ** -0.5)).astype(BF16)
    kw = tl + WINDOW
    r0 = pl.multiple_of(pl.program_id(1) * tl, tl)
    kt = k_ref[0, pl.ds(r0, kw), :].astype(BF16)
    vt = v_ref[0, pl.ds(r0, kw), :].astype(BF16)
    qs = min(tl, WINDOW)
    kws = qs + WINDOW
    q_row = lax.broadcasted_iota(jnp.int32, (qs, kws), 0)
    k_col = lax.broadcasted_iota(jnp.int32, (qs, kws), 1)
    band_lo = (q_row // CHUNK) * CHUNK
    in_band = (k_col >= band_lo) & (k_col < band_lo + WINDOW + CHUNK)
    oks = []
    for u in range(tl // qs):
        pos = r0 + u * qs + k_col
        ok = in_band & (pos - WINDOW < seq_len)
        oks.append(ok if past_valid else ok & (pos >= WINDOW))
    zeros = jnp.zeros((kw, SWA_HEAD_DIM), BF16)
    pair_out = [[] for _ in range(SWA_Q_HEADS // 2)]
    for g in range(SWA_KV_HEADS):
        kg = kt[:, g * SWA_HEAD_DIM:(g + 1) * SWA_HEAD_DIM]
        vg = vt[:, g * SWA_HEAD_DIM:(g + 1) * SWA_HEAD_DIM]
        k_pad = (jnp.concatenate([kg, zeros], axis=1), jnp.concatenate([zeros, kg], axis=1))
        v_pad = (jnp.concatenate([vg, zeros], axis=1), jnp.concatenate([zeros, vg], axis=1))
        for pair in range(g * SWA_GROUP // 2, (g + 1) * SWA_GROUP // 2):
            sinks = [sink_ref[2 * pair + j:2 * pair + j + 1, :] for j in range(2)]
            work = [(u, j) for u in range(tl // qs) for j in range(2)]
            kwin = [slice(u * qs, u * qs + kws) for u in range(tl // qs)]
            qp = [q[u * qs:(u + 1) * qs, pair * LANES:(pair + 1) * LANES] for u in range(tl // qs)]
            s = [jnp.where(oks[u], _bdot_nt(qp[u], k_pad[j][kwin[u]]), NEG_INF) for u, j in work]
            m = [jnp.maximum(jnp.max(s_i, axis=-1, keepdims=True), sinks[j]) for s_i, (u, j) in zip(s, work)]
            e = [jnp.exp(s_i - m_i) for s_i, m_i in zip(s, m)]
            o = [_bdot(e_i, v_pad[j][kwin[u]]) for e_i, (u, j) in zip(e, work)]
            inv = [1.0 / (jnp.sum(e_i, axis=-1, keepdims=True) + jnp.exp(sinks[j] - m_i))
                   for e_i, m_i, (u, j) in zip(e, m, work)]
            on = [o_i * inv_i for o_i, inv_i in zip(o, inv)]
            for u in range(tl // qs):
                pair_out[pair].append(on[2 * u] + on[2 * u + 1])
    pair_out = [jnp.concatenate(p, axis=0) for p in pair_out]
    h_out = _bdot(jnp.concatenate(pair_out, axis=1), wo_ref[...])
    y_ref[0] = _layer_norm(DEEPNORM_ALPHA * x + h_out, g_ref[...], b_ref[...])


def _swa(x, k_all, v_all, w_q, sinks, w_o, ln_g, ln_b, seq_len, past_valid, tl=256):
    B, Lp, _ = x.shape
    tl = _row_tile(Lp, tl)
    x_spec = pl.BlockSpec((1, tl, D_MODEL), lambda b, i: (b, i, 0))
    kv_spec = pl.BlockSpec((1, WINDOW + Lp, SWA_KV), lambda b, i: (b, 0, 0))
    vec = _full((1, D_MODEL))
    return pl.pallas_call(
        functools.partial(_swa_kernel, tl=tl, seq_len=seq_len, past_valid=past_valid),
        out_shape=jax.ShapeDtypeStruct((B, Lp, D_MODEL), F32),
        grid=(B, Lp // tl),
        in_specs=[x_spec, kv_spec, kv_spec, _full((D_MODEL, D_MODEL)), _full((SWA_Q_HEADS, 1)),
                  _full((D_MODEL, D_MODEL)), vec, vec],
        out_specs=x_spec,
        compiler_params=_params("parallel", "arbitrary"),
        name="swa",
    )(x, k_all, v_all, w_q, sinks, w_o, ln_g, ln_b)


def _mem_kernel(x_ref, mk_ref, mv_ref, wq_ref, wo_ref, g_ref, b_ref, y_ref):
    x = x_ref[0]
    q = _bdot(x, wq_ref[...]) * (MEM_HEAD_DIM ** -0.5)
    parts = []
    for h in range(MEM_HEADS):
        hs = slice(h * MEM_HEAD_DIM, (h + 1) * MEM_HEAD_DIM)
        s = _bdot_nt(q[:, hs], mk_ref[0, :, hs])
        m = jnp.max(s, axis=-1, keepdims=True)
        e = jnp.exp(s - m)
        parts.append(_bdot(e, mv_ref[0, :, hs]) * (1.0 / jnp.sum(e, axis=-1, keepdims=True)))
    o = jnp.concatenate(parts, axis=-1)
    h_out = _bdot(o, wo_ref[...])
    y_ref[0] = _layer_norm(DEEPNORM_ALPHA * x + h_out, g_ref[...], b_ref[...])


def _mem_xattn(x, mk, mv, layer, w_q, w_o, ln_g, ln_b, tl=512):
    B, L, _ = x.shape
    tl = _row_tile(L, tl)
    x_spec = pl.BlockSpec((1, tl, D_MODEL), lambda b, i: (b, i, 0))
    m_spec = pl.BlockSpec((1, N_MEM, D_MODEL), lambda b, i: (layer * B + b, 0, 0))
    vec = _full((1, D_MODEL))
    return pl.pallas_call(
        _mem_kernel,
        out_shape=jax.ShapeDtypeStruct((B, L, D_MODEL), F32),
        grid=(B, L // tl),
        in_specs=[x_spec, m_spec, m_spec, _full((D_MODEL, D_MODEL)), _full((D_MODEL, D_MODEL)), vec, vec],
        out_specs=x_spec,
        compiler_params=_params("parallel", "arbitrary"),
        name="mem_xattn",
    )(x, mk, mv, w_q, w_o, ln_g, ln_b)


def _router_kernel(x_ref, wrt_ref, br_ref, idx_ref, gate_ref, pos_ref, cnt_ref, cnt_scr, *, tm):
    @pl.when(pl.program_id(0) == 0)
    def _():
        cnt_scr[...] = jnp.zeros_like(cnt_scr)

    logits = _bdot_nt(wrt_ref[...], x_ref[...]) + br_ref[...]
    expert = lax.broadcasted_iota(jnp.int32, (N_EXPERTS, tm), 0).astype(F32)
    sel = jnp.zeros((N_EXPERTS, tm), F32)
    l = logits
    vals, hits = [], []
    for j in range(TOP_K):
        m = jnp.max(l, axis=0, keepdims=True)
        idx = jnp.min(jnp.where(l == m, expert, float(N_EXPERTS)), axis=0, keepdims=True)
        hit = expert == idx
        sel = jnp.where(hit, 1.0, sel)
        l = jnp.where(hit, -jnp.inf, l)
        idx_ref[j:j + 1, :] = idx.astype(jnp.int32)
        vals.append(m)
        hits.append(hit)
    e = [jnp.exp(v - vals[0]) for v in vals]
    inv = 1.0 / (e[0] + e[1] + e[2] + e[3])
    r = lax.broadcasted_iota(jnp.int32, (tm, tm), 0)
    c = lax.broadcasted_iota(jnp.int32, (tm, tm), 1)
    before = jnp.dot(sel.astype(BF16), (r < c).astype(BF16), preferred_element_type=F32) + cnt_scr[:, 0:1]
    for j in range(TOP_K):
        gate_ref[j:j + 1, :] = e[j] * inv
        pos_ref[j:j + 1, :] = jnp.sum(jnp.where(hits[j], before, 0.0), axis=0, keepdims=True).astype(jnp.int32)
    cnt_scr[...] = cnt_scr[...] + jnp.sum(sel, axis=1, keepdims=True)
    cnt_ref[...] = cnt_scr[...].astype(jnp.int32)


def _router(x, w_rt, b_r, tm=512):
    T = x.shape[0]
    tm = _row_tile(T, tm)
    small = pl.BlockSpec((TOP_K, tm), lambda i: (0, i))
    return pl.pallas_call(
        functools.partial(_router_kernel, tm=tm),
        out_shape=(jax.ShapeDtypeStruct((TOP_K, T), jnp.int32), jax.ShapeDtypeStruct((TOP_K, T), F32),
                   jax.ShapeDtypeStruct((TOP_K, T), jnp.int32), jax.ShapeDtypeStruct((N_EXPERTS, LANES), jnp.int32)),
        grid=(T // tm,),
        in_specs=[pl.BlockSpec((tm, D_MODEL), lambda i: (i, 0)), _full((N_EXPERTS, D_MODEL)), _full((N_EXPERTS, 1))],
        out_specs=(small, small, small, _full((N_EXPERTS, LANES))),
        scratch_shapes=[pltpu.VMEM((N_EXPERTS, LANES), F32)],
        compiler_params=_params("arbitrary"),
        name="moe_router",
    )(x, w_rt, b_r)


ISSUE_UNROLL = 16


def _dispatch_kernel(pad_end_ref, dest_ref, x_ref, xs_ref, rows_scr, zero_scr, sem, zsem, *, tm, blk):
    @pl.when(pl.program_id(0) == 0)
    def _():
        zero_scr[...] = jnp.zeros_like(zero_scr)

        def fill(e):
            start = pl.multiple_of((pad_end_ref[e] - blk) * ROW_TILES, blk * ROW_TILES)
            return pltpu.make_async_copy(zero_scr, xs_ref.at[pl.ds(start, blk * ROW_TILES), :], zsem)

        def has_block(e):
            return pad_end_ref[e] > (pad_end_ref[e - 1] if e else 0)

        for e in range(N_EXPERTS):
            pl.when(has_block(e))(lambda e=e: fill(e).start())
        for e in range(N_EXPERTS):
            pl.when(has_block(e))(lambda e=e: fill(e).wait())

    _rows_to_tiles(rows_scr, x_ref[...])

    def tile(ref, row):
        return ref.at[pl.ds(pl.multiple_of(row * ROW_TILES, ROW_TILES), ROW_TILES), :]

    def issue(g, carry):
        base = pl.multiple_of(g * (ISSUE_UNROLL // TOP_K), ISSUE_UNROLL // TOP_K)
        for u in range(ISSUE_UNROLL // TOP_K):
            r = base + u
            for j in range(TOP_K):
                pltpu.make_async_copy(tile(rows_scr, r), tile(xs_ref, dest_ref[0, 0, r * TOP_K + j]), sem).start(priority=j % 2)
        return carry

    lax.fori_loop(0, tm * TOP_K // ISSUE_UNROLL, issue, 0)
    for j in range(TOP_K):
        pltpu.make_async_copy(rows_scr, xs_ref.at[pl.ds(0, tm * ROW_TILES), :], sem).wait()


def _dispatch(x, dest_blocks, pad_end, n_slots, tm, blk):
    T = x.shape[0]
    return pl.pallas_call(
        functools.partial(_dispatch_kernel, tm=tm, blk=blk),
        out_shape=jax.ShapeDtypeStruct((n_slots * ROW_TILES, LANES), F32),
        grid_spec=pltpu.PrefetchScalarGridSpec(
            num_scalar_prefetch=1,
            grid=(T // tm,),
            in_specs=[pl.BlockSpec((1, 1, tm * TOP_K), lambda i, pe: (i, 0, 0), memory_space=pltpu.SMEM),
                      pl.BlockSpec((tm, D_MODEL), lambda i, pe: (i, 0))],
            out_specs=pl.BlockSpec(memory_space=pl.ANY),
            scratch_shapes=[pltpu.VMEM((tm * ROW_TILES, LANES), F32), pltpu.VMEM((blk * ROW_TILES, LANES), F32),
                            pltpu.SemaphoreType.DMA, pltpu.SemaphoreType.DMA]),
        compiler_params=_params("arbitrary"),
        name="moe_dispatch",
    )(pad_end, dest_blocks, x)


def _collect_combine_kernel(idx_ref, idx_next_ref, src_ref, gate_ref, x_ref, g_ref, b_ref, y_ref, buf, sem, *, tm):
    i = pl.program_id(0)
    slot = i % 2

    def tile(ref, row):
        return ref.at[pl.ds(pl.multiple_of(row * ROW_TILES, ROW_TILES), ROW_TILES), :]

    def request(ids, s):
        def body(g, carry):
            base = pl.multiple_of(g * (ISSUE_UNROLL // TOP_K), ISSUE_UNROLL // TOP_K)
            for u in range(ISSUE_UNROLL // TOP_K):
                r = base + u
                for j in range(TOP_K):
                    pltpu.make_async_copy(tile(src_ref, ids[0, 0, r * TOP_K + j]), tile(buf.at[s, j], r),
                                          sem.at[s]).start(priority=j % 2)
            return carry
        lax.fori_loop(0, tm * TOP_K // ISSUE_UNROLL, body, 0)

    @pl.when(i == 0)
    def _():
        request(idx_ref, 0)

    @pl.when(i + 1 < pl.num_programs(0))
    def _():
        request(idx_next_ref, 1 - slot)

    for j in range(TOP_K):
        pltpu.make_async_copy(src_ref.at[pl.ds(0, tm * ROW_TILES), :], buf.at[slot, j], sem.at[slot]).wait()
    gates = gate_ref[...]
    acc = DEEPNORM_ALPHA * x_ref[...]
    for j in range(TOP_K):
        acc = acc + gates[:, j:j + 1] * _tiles_to_rows(buf.at[slot, j], tm)
    y_ref[...] = _layer_norm(acc, g_ref[...], b_ref[...])


def _collect_combine(src, idx_blocks, gates, x, ln_g, ln_b, tm):
    T = x.shape[0]
    nt = T // tm
    row = pl.BlockSpec((tm, D_MODEL), lambda i: (i, 0))
    vec = _full((1, D_MODEL))
    return pl.pallas_call(
        functools.partial(_collect_combine_kernel, tm=tm),
        out_shape=jax.ShapeDtypeStruct((T, D_MODEL), F32),
        grid=(nt,),
        in_specs=[pl.BlockSpec((1, 1, tm * TOP_K), lambda i: (i, 0, 0), memory_space=pltpu.SMEM),
                  pl.BlockSpec((1, 1, tm * TOP_K), lambda i: (jnp.minimum(i + 1, nt - 1), 0, 0), memory_space=pltpu.SMEM),
                  pl.BlockSpec(memory_space=pl.ANY),
                  pl.BlockSpec((tm, TOP_K), lambda i: (i, 0)), row, vec, vec],
        out_specs=row,
        scratch_shapes=[pltpu.VMEM((2, TOP_K, tm * ROW_TILES, LANES), F32), pltpu.SemaphoreType.DMA((2,))],
        compiler_params=_params("arbitrary"),
        name="moe_collect_combine",
    )(idx_blocks, idx_blocks, src, gates, x, ln_g, ln_b)


def _expert_kernel(be_ref, nb_ref, xs_ref, w1_ref, b1_ref, w2_ref, b2_ref, o_ref, *, blk):
    @pl.when(pl.program_id(0) < nb_ref[0])
    def _():
        h = _bdot(_tiles_to_rows(xs_ref, blk), w1_ref[0]) + b1_ref[0]
        glu = jnp.minimum(h[:, :D_EXPERT], SWIGLU_LIMIT)
        lin = jnp.clip(h[:, D_EXPERT:], -SWIGLU_LIMIT, SWIGLU_LIMIT)
        a = glu * jax.nn.sigmoid(SWIGLU_ALPHA * glu) * (lin + 1.0)
        _rows_to_tiles(o_ref, _bdot(a, w2_ref[0]) + b2_ref[0])


def _experts(xs, block_e, n_used, w1, b1, w2, b2, blk):
    P = xs.shape[0] // ROW_TILES
    nb = P // blk

    def blk_map(i, be, nu):
        return (jnp.minimum(i, nu[0] - 1), 0)

    def w_map(i, be, nu):
        return (be[jnp.minimum(i, nu[0] - 1)], 0, 0)

    return pl.pallas_call(
        functools.partial(_expert_kernel, blk=blk),
        out_shape=jax.ShapeDtypeStruct((P * ROW_TILES, LANES), F32),
        grid_spec=pltpu.PrefetchScalarGridSpec(
            num_scalar_prefetch=2,
            grid=(nb,),
            in_specs=[pl.BlockSpec((blk * ROW_TILES, LANES), blk_map),
                      pl.BlockSpec((1, D_MODEL, 2 * D_EXPERT), w_map),
                      pl.BlockSpec((1, 1, 2 * D_EXPERT), w_map),
                      pl.BlockSpec((1, D_EXPERT, D_MODEL), w_map),
                      pl.BlockSpec((1, 1, D_MODEL), w_map)],
            out_specs=pl.BlockSpec((blk * ROW_TILES, LANES), blk_map)),
        compiler_params=_params("arbitrary"),
        name="moe_experts",
    )(block_e, n_used, xs, w1, b1, w2, b2)


def _moe(x, layer, w_r, b_r, w1, b1, w2, b2, ln_g, ln_b):
    T = x.shape[0]
    blk = 1024 if T >= 4096 else 128
    tm = _row_tile(T, 512)
    n_assign = T * TOP_K
    n_blocks = -(-(n_assign + N_EXPERTS * (blk - 1)) // blk)
    n_slots = n_blocks * blk

    top_i, gates, pos, counts = _router(x, w_r, b_r)
    counts = counts[:, 0]
    padded = (counts + blk - 1) // blk * blk
    pad_end = jnp.cumsum(padded)
    pad_start = pad_end - padded
    is_e = top_i[None] == jnp.arange(N_EXPERTS, dtype=jnp.int32)[:, None, None]
    dest = pos + jnp.sum(jnp.where(is_e, pad_start.astype(jnp.int32)[:, None, None], 0), axis=0)
    block_start = jnp.arange(n_blocks, dtype=jnp.int32) * blk
    block_e = jnp.minimum(jnp.sum(pad_end[None, :] <= block_start[:, None], axis=1), N_EXPERTS - 1).astype(jnp.int32)
    n_used = (pad_end[-1:] // blk).astype(jnp.int32)
    dest_blocks = dest.T.reshape(T // tm, 1, tm * TOP_K)

    xs = _dispatch(x, dest_blocks, pad_end.astype(jnp.int32), n_slots, tm, blk)
    out = _experts(xs, block_e + layer * N_EXPERTS, n_used, w1, b1, w2, b2, blk)
    return _collect_combine(out, dest_blocks, gates.T, x, ln_g, ln_b, tm)


def _pad_rows(t, n):
    return t if n == 0 else jnp.pad(t, ((0, 0), (0, n), (0, 0)))


def _trunk(x, gla_s0, past_k, past_v, past_valid, mem_k, mem_v, w):
    B, L, _ = x.shape
    Lp = -(-L // CHUNK) * CHUNK
    T = B * L
    vec = lambda a: a.reshape(1, -1)
    gla_states = []
    k_all = v_all = new_k = new_v = None
    for layer in range(DEPTH):
        lg, lb = w['ln_g'][layer], w['ln_b'][layer]
        xt = x.reshape(T, D_MODEL)
        if layer < N_A:
            i = layer
            xt = _pad_rows(x, Lp - L).reshape(B * Lp, D_MODEL)
            q, k, v, r, b = _gla_in(xt, w['gla_w_main'][i], w['gla_w_g'][i], w['gla_w_g2'][i],
                                    vec(w['gla_b_g'][i]), L, Lp)
            seq = lambda t: t.reshape(B, Lp, -1)
            s0t = jnp.swapaxes(gla_s0[i], -1, -2)
            y, st = _gla_scan(seq(q), seq(k), seq(v), seq(b), s0t, seq(r), seq(xt), vec(w['gla_gn_g'][i]),
                              w['gla_w_o'][i], vec(lg[0]), vec(lb[0]))
            gla_states.append(jnp.swapaxes(st, -1, -2))
            x = y[:, :L]
        else:
            j = layer - N_A
            if j == 0:
                k_new, v_new = _linear_kv(xt, w['kv_w'][None])
                k_full = jnp.concatenate([past_k, k_new.reshape(B, L, SWA_KV)], axis=1)
                v_full = jnp.concatenate([past_v, v_new.reshape(B, L, SWA_KV)], axis=1)
                new_k, new_v = k_full[:, -WINDOW:], v_full[:, -WINDOW:]
                k_all, v_all = _pad_rows(k_full, Lp - L), _pad_rows(v_full, Lp - L)
            xp = _swa(_pad_rows(x, Lp - L), k_all, v_all, w['swa_w_q'][j], w['swa_sinks'][j].reshape(-1, 1),
                      w['swa_w_o'][j], vec(lg[0]), vec(lb[0]), L, past_valid)
            x = xp[:, :L]
        x = _mem_xattn(x, mem_k, mem_v, layer, w['mem_w_q'][layer], w['mem_w_o'][layer], vec(lg[1]), vec(lb[1]))
        xt = _moe(x.reshape(T, D_MODEL), layer, w['moe_w_r'][layer], w['moe_b_r'][layer], w['moe_w1'],
                  w['moe_b1'], w['moe_w2'], w['moe_b2'], vec(lg[2]), vec(lb[2]))
        x = xt.reshape(B, L, D_MODEL)
    return x, jnp.stack(gla_states), new_k, new_v


def _prep_weights(gla_w_in, gla_w_g2, gla_b_g, gla_gn_g, gla_w_o, kv_w, swa_w_q, swa_sinks, swa_w_o,
                  mem_w_q, mem_w_kv, mem_w_o, moe_w_r, moe_b_r, moe_w1, moe_b1, moe_w2, moe_b2, ln_g, ln_b):
    n_main = 2 * GLA_QK + 2 * GLA_V
    pad_c = lambda a, n: jnp.pad(a, [(0, 0)] * (a.ndim - 1) + [(0, n - a.shape[-1])])
    w = dict(
        gla_w_main=gla_w_in[:, :, :n_main].astype(BF16),
        gla_w_g=pad_c(gla_w_in[:, :, n_main:], LANES).astype(BF16),
        gla_w_g2=jnp.pad(gla_w_g2, ((0, 0), (0, LANES - GLA_GATE_RANK), (0, 0))).astype(BF16),
        gla_b_g=gla_b_g, gla_gn_g=gla_gn_g.reshape(N_A, GLA_V), gla_w_o=gla_w_o.astype(BF16),
        kv_w=kv_w.astype(BF16), swa_w_q=swa_w_q.astype(BF16), swa_sinks=swa_sinks, swa_w_o=swa_w_o.astype(BF16),
        mem_w_q=mem_w_q.astype(BF16), mem_w_kv=mem_w_kv.astype(BF16), mem_w_o=mem_w_o.astype(BF16),
        moe_w_r=jnp.swapaxes(moe_w_r, -1, -2).astype(BF16), moe_b_r=moe_b_r.reshape(DEPTH, N_EXPERTS, 1),
        moe_w1=_w1_relayout(moe_w1.reshape(DEPTH * N_EXPERTS, D_MODEL, 2 * D_EXPERT)),
        moe_b1=jnp.concatenate([moe_b1[..., 0::2], moe_b1[..., 1::2]], axis=-1).reshape(DEPTH * N_EXPERTS, 1, -1),
        moe_w2=moe_w2.astype(BF16).reshape(DEPTH * N_EXPERTS, D_EXPERT, D_MODEL),
        moe_b2=moe_b2.reshape(DEPTH * N_EXPERTS, 1, D_MODEL),
        ln_g=ln_g, ln_b=ln_b)
    return w


def kernel(x_prompt, x_sample, state_gla, cache_swa_k, cache_swa_v, cache_mem_k, cache_mem_v, mem_prompt, gla_w_in, gla_w_g2, gla_b_g, gla_gn_g, gla_w_o, kv_w, swa_w_q, swa_sinks, swa_w_o, mem_w_q, mem_w_kv, mem_w_o, moe_w_r, moe_b_r, moe_w1, moe_b1, moe_w2, moe_b2, ln_g, ln_b):
    w = _prep_weights(gla_w_in, gla_w_g2, gla_b_g, gla_gn_g, gla_w_o, kv_w, swa_w_q, swa_sinks, swa_w_o,
                      mem_w_q, mem_w_kv, mem_w_o, moe_w_r, moe_b_r, moe_w1, moe_b1, moe_w2, moe_b2, ln_g, ln_b)
    bp = x_prompt.shape[0]
    bs = x_sample.shape[0]
    mem_flat = mem_prompt.reshape(bp * N_MEM, D_MODEL)
    mem_k_p, mem_v_p = _linear_kv(mem_flat, w['mem_w_kv'])
    gla0 = jnp.zeros((N_A, bp, GLA_HEADS, GLA_DK, GLA_DV), F32)
    zero_win = jnp.zeros((bp, WINDOW, SWA_KV), F32)
    y_p, gla_p, k_p, v_p = _trunk(x_prompt, gla0, zero_win, zero_win, False,
                                  mem_k_p.reshape(DEPTH * bp, N_MEM, D_MODEL),
                                  mem_v_p.reshape(DEPTH * bp, N_MEM, D_MODEL), w)
    y_s, gla_s, k_s, v_s = _trunk(x_sample, state_gla, cache_swa_k.reshape(bs, WINDOW, SWA_KV),
                                  cache_swa_v.reshape(bs, WINDOW, SWA_KV), True,
                                  cache_mem_k.reshape(DEPTH * bs, N_MEM, D_MODEL),
                                  cache_mem_v.reshape(DEPTH * bs, N_MEM, D_MODEL), w)
    heads4 = lambda t: t.reshape(t.shape[0], WINDOW, SWA_KV_HEADS, SWA_HEAD_DIM)
    mem5 = lambda t: t.reshape(DEPTH, bp, N_MEM, MEM_HEADS, MEM_HEAD_DIM)
    return (y_p, y_s, gla_p, gla_s, heads4(k_p), heads4(v_p), heads4(k_s), heads4(v_s), mem5(mem_k_p), mem5(mem_v_p))
```

```python
import functools

import jax
import jax.numpy as jnp
from jax import lax
from jax.experimental import pallas as pl
from jax.experimental.pallas import tpu as pltpu

F32 = jnp.float32
BF16 = jnp.bfloat16

D_MODEL = 1024
DEPTH = 4
CHUNK = 64
N_A = DEPTH // 2
GLA_HEADS = 4
GLA_DK = D_MODEL // (2 * GLA_HEADS)
GLA_DV = D_MODEL // GLA_HEADS
GLA_QK = GLA_HEADS * GLA_DK
GLA_V = GLA_HEADS * GLA_DV
GLA_GATE_RANK = 16
GLA_GATE_TEMP = 16.0
SWA_HEAD_DIM = 64
SWA_Q_HEADS = D_MODEL // SWA_HEAD_DIM
SWA_KV_HEADS = 4
SWA_GROUP = SWA_Q_HEADS // SWA_KV_HEADS
SWA_KV = SWA_KV_HEADS * SWA_HEAD_DIM
WINDOW = 128
N_MEM = 256
MEM_HEADS = 4
MEM_HEAD_DIM = D_MODEL // MEM_HEADS
N_EXPERTS = 32
TOP_K = 4
D_EXPERT = D_MODEL
SWIGLU_ALPHA = 1.702
SWIGLU_LIMIT = 7.0
DEEPNORM_ALPHA = (2 * DEPTH) ** 0.25
LN_EPS = 1e-5
NEG_INF = -1e30

LANES = 128
VMEM_LIMIT = 56 * 1024 * 1024


def _params(*sem):
    return pltpu.CompilerParams(dimension_semantics=sem, vmem_limit_bytes=VMEM_LIMIT)


def _row_tile(n_rows, want):
    t = min(want, n_rows)
    while n_rows % t:
        t //= 2
    return t


def _full(shape):
    nd = len(shape)
    return pl.BlockSpec(shape, lambda *_: (0,) * nd)


def _bdot(a, b):
    return jnp.dot(a.astype(BF16), b.astype(BF16), preferred_element_type=F32)


def _bdot_nt(a, b):
    return lax.dot_general(a.astype(BF16), b.astype(BF16), (((1,), (1,)), ((), ())),
                           preferred_element_type=F32)


def _bdot_tn(a, b):
    return lax.dot_general(a.astype(BF16), b.astype(BF16), (((0,), (0,)), ((), ())),
                           preferred_element_type=F32)


def _layer_norm(z, g, b):
    mu = jnp.mean(z, axis=-1, keepdims=True)
    zc = z - mu
    var = jnp.mean(zc * zc, axis=-1, keepdims=True)
    return zc * lax.rsqrt(var + LN_EPS) * g + b


ROW_TILES = D_MODEL // LANES


def _rows_to_tiles(ref, x):
    n = x.shape[0]
    for c in range(ROW_TILES):
        ref[pl.ds(c, n, stride=ROW_TILES), :] = x[:, c * LANES:(c + 1) * LANES]


def _tiles_to_rows(ref, n):
    return jnp.concatenate([ref[pl.ds(c, n, stride=ROW_TILES), :] for c in range(ROW_TILES)], axis=1)


def _linear_kv_kernel(x_ref, w_ref, k_ref, v_ref):
    y = _bdot(x_ref[...], w_ref[0])
    half = y.shape[1] // 2
    k_ref[0] = y[:, :half]
    v_ref[0] = y[:, half:]


def _linear_kv(x, w, tm=512):
    T, K = x.shape
    G, _, N2 = w.shape
    tm = _row_tile(T, tm)
    out = jax.ShapeDtypeStruct((G, T, N2 // 2), F32)
    o_spec = pl.BlockSpec((1, tm, N2 // 2), lambda g, i: (g, i, 0))
    return pl.pallas_call(
        _linear_kv_kernel,
        out_shape=(out, out),
        grid=(G, T // tm),
        in_specs=[pl.BlockSpec((tm, K), lambda g, i: (i, 0)), pl.BlockSpec((1, K, N2), lambda g, i: (g, 0, 0))],
        out_specs=(o_spec, o_spec),
        compiler_params=_params("parallel", "parallel"),
        name="linear_kv",
    )(x, w)


MXU_DIM = 256


def _w1_relayout_kernel(w_ref, o_ref):
    half = MXU_DIM // 2
    r = lax.broadcasted_iota(jnp.int32, (MXU_DIM, MXU_DIM), 0)
    c = lax.broadcasted_iota(jnp.int32, (MXU_DIM, MXU_DIM), 1)
    src = jnp.where(c < half, 2 * c, 2 * (c - half) + 1)
    perm = (r == src).astype(BF16)
    n_out = o_ref.shape[-1] // 2
    for j in range(w_ref.shape[-1] // MXU_DIM):
        y = jnp.dot(w_ref[0, :, j * MXU_DIM:(j + 1) * MXU_DIM].astype(BF16), perm, preferred_element_type=F32)
        o_ref[0, :, j * half:(j + 1) * half] = y[:, :half].astype(BF16)
        o_ref[0, :, n_out + j * half:n_out + (j + 1) * half] = y[:, half:].astype(BF16)


def _w1_relayout(w1):
    E, D, N = w1.shape
    spec = pl.BlockSpec((1, D, N), lambda e: (e, 0, 0))
    return pl.pallas_call(
        _w1_relayout_kernel,
        out_shape=jax.ShapeDtypeStruct((E, D, N), BF16),
        grid=(E,),
        in_specs=[spec],
        out_specs=spec,
        compiler_params=_params("parallel"),
        name="w1_relayout",
    )(w1)


def _gla_in_kernel(x_ref, w_ref, wg_ref, wg2_ref, bg_ref, q_ref, k_ref, v_ref, r_ref, b_ref,
                   *, tm, seq_len, seq_pad):
    xb = x_ref[...].astype(BF16)
    y = jnp.dot(xb, w_ref[...], preferred_element_type=F32)
    q_ref[...] = y[:, :GLA_QK] * (GLA_DK ** -0.5)
    k_ref[...] = y[:, GLA_QK:2 * GLA_QK]
    v_ref[...] = y[:, 2 * GLA_QK:2 * GLA_QK + GLA_V]
    r_ref[...] = y[:, 2 * GLA_QK + GLA_V:]
    g_lr = jnp.dot(xb, wg_ref[...], preferred_element_type=F32)
    z = _bdot(g_lr, wg2_ref[...]) + bg_ref[...]
    log_a = (jnp.minimum(z, 0.0) - jnp.log(1.0 + jnp.exp(-jnp.abs(z)))) / GLA_GATE_TEMP
    if seq_len < seq_pad:
        pos = (pl.program_id(0) * tm + lax.broadcasted_iota(jnp.int32, (tm, 1), 0)) % seq_pad
        log_a = jnp.where(pos < seq_len, log_a, 0.0)
    row = lax.broadcasted_iota(jnp.int32, (tm, tm), 0)
    col = lax.broadcasted_iota(jnp.int32, (tm, tm), 1)
    tri = ((row // CHUNK == col // CHUNK) & (row >= col)).astype(BF16)
    g1 = log_a.astype(BF16)
    r1 = log_a - g1.astype(F32)
    g2 = r1.astype(BF16)
    g3 = (r1 - g2.astype(F32)).astype(BF16)
    b_ref[...] = (jnp.dot(tri, g1, preferred_element_type=F32) + jnp.dot(tri, g2, preferred_element_type=F32)
                  + jnp.dot(tri, g3, preferred_element_type=F32))


def _gla_in(x, w_main, w_g, w_g2, b_g, seq_len, seq_pad, tm=512):
    T = x.shape[0]
    tm = _row_tile(T, tm)
    assert tm % CHUNK == 0 and seq_pad % CHUNK == 0
    n_main = w_main.shape[1]
    row = lambda n: pl.BlockSpec((tm, n), lambda i: (i, 0))
    return pl.pallas_call(
        functools.partial(_gla_in_kernel, tm=tm, seq_len=seq_len, seq_pad=seq_pad),
        out_shape=(jax.ShapeDtypeStruct((T, GLA_QK), F32), jax.ShapeDtypeStruct((T, GLA_QK), F32),
                   jax.ShapeDtypeStruct((T, GLA_V), F32), jax.ShapeDtypeStruct((T, GLA_V), F32),
                   jax.ShapeDtypeStruct((T, GLA_QK), F32)),
        grid=(T // tm,),
        in_specs=[row(D_MODEL), _full((D_MODEL, n_main)), _full((D_MODEL, LANES)),
                  _full((LANES, GLA_QK)), _full((1, GLA_QK))],
        out_specs=(row(GLA_QK), row(GLA_QK), row(GLA_V), row(GLA_V), row(GLA_QK)),
        compiler_params=_params("parallel"),
        name="gla_in",
    )(x, w_main, w_g, w_g2, b_g)


SUBLANES = 8
GLA_SUB = 16


def _gla_intra(q, k, b):
    n_grp = CHUNK // SUBLANES
    g_sub = GLA_SUB // SUBLANES
    lane = lax.broadcasted_iota(jnp.int32, (SUBLANES, LANES), 1)
    row = lax.broadcasted_iota(jnp.int32, (SUBLANES, LANES), 0)
    grp = lambda t, i: t[i * SUBLANES:(i + 1) * SUBLANES]
    att = [jnp.zeros((SUBLANES, LANES), F32) for _ in range(n_grp)]
    for s in range(CHUNK):
        b_s = b[s:s + 1, :]
        k_s = k[s:s + 1, :]
        for i in range(s // SUBLANES, (s // GLA_SUB + 1) * g_sub):
            e = jnp.exp(grp(b, i) - b_s)
            red = jnp.sum(grp(q, i) * e * k_s, axis=-1, keepdims=True)
            att[i] = jnp.where(lane == s, red, att[i])
    att = [jnp.where(row + i * SUBLANES >= lane, att[i], 0.0) for i in range(n_grp)]
    for blk in range(1, CHUNK // GLA_SUB):
        r0 = blk * GLA_SUB
        c = b[r0 - 1:r0, :]
        k_early = jnp.concatenate([k[:r0] * jnp.exp(c - b[:r0]), jnp.zeros((LANES - r0, GLA_DK), F32)], axis=0)
        off = _bdot_nt(q[r0:r0 + GLA_SUB] * jnp.exp(b[r0:r0 + GLA_SUB] - c), k_early)
        for i in range(g_sub):
            att[blk * g_sub + i] = att[blk * g_sub + i] + grp(off, i)
    return jnp.concatenate(att, axis=0)


def _gla_scan_kernel(q_ref, k_ref, v_ref, b_ref, s0_ref, r_ref, x_ref, gn_ref, wo_ref, g_ref, lb_ref,
                     y_ref, s_ref, st_scr, o_scr, *, n_chunks):
    @pl.when(pl.program_id(1) == 0)
    def _():
        st_scr[...] = s0_ref[0]

    for c in range(n_chunks):
        ts = slice(c * CHUNK, (c + 1) * CHUNK)
        for h in range(GLA_HEADS):
            ks = slice(h * GLA_DK, (h + 1) * GLA_DK)
            vs = slice(h * GLA_DV, (h + 1) * GLA_DV)
            q = q_ref[0, ts, ks]
            k = k_ref[0, ts, ks]
            b = b_ref[0, ts, ks]
            v = v_ref[0, ts, vs]
            att = _gla_intra(q, k, b)
            b_end = b[CHUNK - 1:CHUNK, :]
            st = st_scr[h]
            o_scr[ts, vs] = _bdot(att[:, :CHUNK], v) + _bdot_nt(q * jnp.exp(b), st)
            st_scr[h] = st * jnp.exp(b_end) + _bdot_tn(v, k * jnp.exp(b_end - b))

    @pl.when(pl.program_id(1) == pl.num_programs(1) - 1)
    def _():
        s_ref[0] = st_scr[...]

    parts = []
    for h in range(GLA_HEADS):
        vs = slice(h * GLA_DV, (h + 1) * GLA_DV)
        o = o_scr[:, vs]
        mu = jnp.mean(o, axis=-1, keepdims=True)
        oc = o - mu
        var = jnp.mean(oc * oc, axis=-1, keepdims=True)
        parts.append(oc * lax.rsqrt(var + LN_EPS) * gn_ref[:, vs])
    r = r_ref[0]
    o = jnp.concatenate(parts, axis=-1) * (r * jax.nn.sigmoid(r))
    h_out = _bdot(o, wo_ref[...])
    y_ref[0] = _layer_norm(DEEPNORM_ALPHA * x_ref[0] + h_out, g_ref[...], lb_ref[...])


def _gla_scan(q, k, v, b, s0t, r, x, gn_g, w_o, ln_g, ln_b, tl=256):
    B, Lp, _ = q.shape
    tl = _row_tile(Lp, tl)
    qk_spec = pl.BlockSpec((1, tl, GLA_QK), lambda b, i: (b, i, 0))
    v_spec = pl.BlockSpec((1, tl, GLA_V), lambda b, i: (b, i, 0))
    s_spec = pl.BlockSpec((1, GLA_HEADS, GLA_DV, GLA_DK), lambda b, i: (b, 0, 0, 0))
    vec = _full((1, D_MODEL))
    return pl.pallas_call(
        functools.partial(_gla_scan_kernel, n_chunks=tl // CHUNK),
        out_shape=(jax.ShapeDtypeStruct((B, Lp, D_MODEL), F32),
                   jax.ShapeDtypeStruct((B, GLA_HEADS, GLA_DV, GLA_DK), F32)),
        grid=(B, Lp // tl),
        in_specs=[qk_spec, qk_spec, v_spec, qk_spec, s_spec, v_spec, v_spec, vec, _full((GLA_V, D_MODEL)), vec, vec],
        out_specs=(v_spec, s_spec),
        scratch_shapes=[pltpu.VMEM((GLA_HEADS, GLA_DV, GLA_DK), F32), pltpu.VMEM((tl, GLA_V), F32)],
        compiler_params=_params("parallel", "arbitrary"),
        name="gla_scan",
    )(q, k, v, b, s0t, r, x, gn_g, w_o, ln_g, ln_b)


def _swa_kernel(x_ref, k_ref, v_ref, wq_ref, sink_ref, wo_ref, g_ref, b_ref, y_ref,
                *, tl, seq_len, past_valid):
    x = x_ref[0]
    q = (_bdot(x, wq_ref[...]) * (SWA_HEAD_DIM ** -0.5)).astype(BF16)
    kw = tl + WINDOW
    r0 = pl.multiple_of(pl.program_id(1) * tl, tl)
    kt = k_ref[0, pl.ds(r0, kw), :].astype(BF16)
    vt = v_ref[0, pl.ds(r0, kw), :].astype(BF16)
    qs = min(tl, WINDOW)
    kws = qs + WINDOW
    q_row = lax.broadcasted_iota(jnp.int32, (qs, kws), 0)
    k_col = lax.broadcasted_iota(jnp.int32, (qs, kws), 1)
    band_lo = (q_row // CHUNK) * CHUNK
    in_band = (k_col >= band_lo) & (k_col < band_lo + WINDOW + CHUNK)
    oks = []
    for u in range(tl // qs):
        pos = r0 + u * qs + k_col
        ok = in_band & (pos - WINDOW < seq_len)
        oks.append(ok if past_valid else ok & (pos >= WINDOW))
    zeros = jnp.zeros((kw, SWA_HEAD_DIM), BF16)
    pair_out = [[] for _ in range(SWA_Q_HEADS // 2)]
    for g in range(SWA_KV_HEADS):
        kg = kt[:, g * SWA_HEAD_DIM:(g + 1) * SWA_HEAD_DIM]
        vg = vt[:, g * SWA_HEAD_DIM:(g + 1) * SWA_HEAD_DIM]
        k_pad = (jnp.concatenate([kg, zeros], axis=1), jnp.concatenate([zeros, kg], axis=1))
        v_pad = (jnp.concatenate([vg, zeros], axis=1), jnp.concatenate([zeros, vg], axis=1))
        for pair in range(g * SWA_GROUP // 2, (g + 1) * SWA_GROUP // 2):
            sinks = [sink_ref[2 * pair + j:2 * pair + j + 1, :] for j in range(2)]
            work = [(u, j) for u in range(tl // qs) for j in range(2)]
            kwin = [slice(u * qs, u * qs + kws) for u in range(tl // qs)]
            qp = [q[u * qs:(u + 1) * qs, pair * LANES:(pair + 1) * LANES] for u in range(tl // qs)]
            s = [jnp.where(oks[u], _bdot_nt(qp[u], k_pad[j][kwin[u]]), NEG_INF) for u, j in work]
            m = [jnp.maximum(jnp.max(s_i, axis=-1, keepdims=True), sinks[j]) for s_i, (u, j) in zip(s, work)]
            e = [jnp.exp(s_i - m_i) for s_i, m_i in zip(s, m)]
            o = [_bdot(e_i, v_pad[j][kwin[u]]) for e_i, (u, j) in zip(e, work)]
            inv = [1.0 / (jnp.sum(e_i, axis=-1, keepdims=True) + jnp.exp(sinks[j] - m_i))
                   for e_i, m_i, (u, j) in zip(e, m, work)]
            on = [o_i * inv_i for o_i, inv_i in zip(o, inv)]
            for u in range(tl // qs):
                pair_out[pair].append(on[2 * u] + on[2 * u + 1])
    pair_out = [jnp.concatenate(p, axis=0) for p in pair_out]
    h_out = _bdot(jnp.concatenate(pair_out, axis=1), wo_ref[...])
    y_ref[0] = _layer_norm(DEEPNORM_ALPHA * x + h_out, g_ref[...], b_ref[...])


def _swa(x, k_all, v_all, w_q, sinks, w_o, ln_g, ln_b, seq_len, past_valid, tl=256):
    B, Lp, _ = x.shape
    tl = _row_tile(Lp, tl)
    x_spec = pl.BlockSpec((1, tl, D_MODEL), lambda b, i: (b, i, 0))
    kv_spec = pl.BlockSpec((1, WINDOW + Lp, SWA_KV), lambda b, i: (b, 0, 0))
    vec = _full((1, D_MODEL))
    return pl.pallas_call(
        functools.partial(_swa_kernel, tl=tl, seq_len=seq_len, past_valid=past_valid),
        out_shape=jax.ShapeDtypeStruct((B, Lp, D_MODEL), F32),
        grid=(B, Lp // tl),
        in_specs=[x_spec, kv_spec, kv_spec, _full((D_MODEL, D_MODEL)), _full((SWA_Q_HEADS, 1)),
                  _full((D_MODEL, D_MODEL)), vec, vec],
        out_specs=x_spec,
        compiler_params=_params("parallel", "arbitrary"),
        name="swa",
    )(x, k_all, v_all, w_q, sinks, w_o, ln_g, ln_b)


def _mem_kernel(x_ref, mk_ref, mv_ref, wq_ref, wo_ref, g_ref, b_ref, y_ref):
    x = x_ref[0]
    q = _bdot(x, wq_ref[...]) * (MEM_HEAD_DIM ** -0.5)
    parts = []
    for h in range(MEM_HEADS):
        hs = slice(h * MEM_HEAD_DIM, (h + 1) * MEM_HEAD_DIM)
        s = _bdot_nt(q[:, hs], mk_ref[0, :, hs])
        m = jnp.max(s, axis=-1, keepdims=True)
        e = jnp.exp(s - m)
        parts.append(_bdot(e, mv_ref[0, :, hs]) * (1.0 / jnp.sum(e, axis=-1, keepdims=True)))
    o = jnp.concatenate(parts, axis=-1)
    h_out = _bdot(o, wo_ref[...])
    y_ref[0] = _layer_norm(DEEPNORM_ALPHA * x + h_out, g_ref[...], b_ref[...])


def _mem_xattn(x, mk, mv, layer, w_q, w_o, ln_g, ln_b, tl=1024):
    B, L, _ = x.shape
    tl = _row_tile(L, tl)
    x_spec = pl.BlockSpec((1, tl, D_MODEL), lambda b, i: (b, i, 0))
    m_spec = pl.BlockSpec((1, N_MEM, D_MODEL), lambda b, i: (layer * B + b, 0, 0))
    vec = _full((1, D_MODEL))
    return pl.pallas_call(
        _mem_kernel,
        out_shape=jax.ShapeDtypeStruct((B, L, D_MODEL), F32),
        grid=(B, L // tl),
        in_specs=[x_spec, m_spec, m_spec, _full((D_MODEL, D_MODEL)), _full((D_MODEL, D_MODEL)), vec, vec],
        out_specs=x_spec,
        compiler_params=_params("parallel", "arbitrary"),
        name="mem_xattn",
    )(x, mk, mv, w_q, w_o, ln_g, ln_b)


def _router_kernel(x_ref, wrt_ref, br_ref, idx_ref, gate_ref, pos_ref, cnt_ref, cnt_scr, *, tm):
    @pl.when(pl.program_id(0) == 0)
    def _():
        cnt_scr[...] = jnp.zeros_like(cnt_scr)

    logits = _bdot_nt(wrt_ref[...], x_ref[...]) + br_ref[...]
    expert = lax.broadcasted_iota(jnp.int32, (N_EXPERTS, tm), 0).astype(F32)
    sel = jnp.zeros((N_EXPERTS, tm), F32)
    l = logits
    vals, hits = [], []
    for j in range(TOP_K):
        m = jnp.max(l, axis=0, keepdims=True)
        idx = jnp.min(jnp.where(l == m, expert, float(N_EXPERTS)), axis=0, keepdims=True)
        hit = expert == idx
        sel = jnp.where(hit, 1.0, sel)
        l = jnp.where(hit, -jnp.inf, l)
        idx_ref[j:j + 1, :] = idx.astype(jnp.int32)
        vals.append(m)
        hits.append(hit)
    e = [jnp.exp(v - vals[0]) for v in vals]
    inv = 1.0 / (e[0] + e[1] + e[2] + e[3])
    r = lax.broadcasted_iota(jnp.int32, (tm, tm), 0)
    c = lax.broadcasted_iota(jnp.int32, (tm, tm), 1)
    before = jnp.dot(sel.astype(BF16), (r < c).astype(BF16), preferred_element_type=F32) + cnt_scr[:, 0:1]
    for j in range(TOP_K):
        gate_ref[j:j + 1, :] = e[j] * inv
        pos_ref[j:j + 1, :] = jnp.sum(jnp.where(hits[j], before, 0.0), axis=0, keepdims=True).astype(jnp.int32)
    cnt_scr[...] = cnt_scr[...] + jnp.sum(sel, axis=1, keepdims=True)
    cnt_ref[...] = cnt_scr[...].astype(jnp.int32)


def _router(x, w_rt, b_r, tm=512):
    T = x.shape[0]
    tm = _row_tile(T, tm)
    small = pl.BlockSpec((TOP_K, tm), lambda i: (0, i))
    return pl.pallas_call(
        functools.partial(_router_kernel, tm=tm),
        out_shape=(jax.ShapeDtypeStruct((TOP_K, T), jnp.int32), jax.ShapeDtypeStruct((TOP_K, T), F32),
                   jax.ShapeDtypeStruct((TOP_K, T), jnp.int32), jax.ShapeDtypeStruct((N_EXPERTS, LANES), jnp.int32)),
        grid=(T // tm,),
        in_specs=[pl.BlockSpec((tm, D_MODEL), lambda i: (i, 0)), _full((N_EXPERTS, D_MODEL)), _full((N_EXPERTS, 1))],
        out_specs=(small, small, small, _full((N_EXPERTS, LANES))),
        scratch_shapes=[pltpu.VMEM((N_EXPERTS, LANES), F32)],
        compiler_params=_params("arbitrary"),
        name="moe_router",
    )(x, w_rt, b_r)


ISSUE_UNROLL = 16


def _dispatch_kernel(pad_end_ref, dest_ref, x_ref, xs_ref, rows_scr, zero_scr, sem, zsem, *, tm, blk):
    @pl.when(pl.program_id(0) == 0)
    def _():
        zero_scr[...] = jnp.zeros_like(zero_scr)

        def fill(e):
            start = pl.multiple_of((pad_end_ref[e] - blk) * ROW_TILES, blk * ROW_TILES)
            return pltpu.make_async_copy(zero_scr, xs_ref.at[pl.ds(start, blk * ROW_TILES), :], zsem)

        def has_block(e):
            return pad_end_ref[e] > (pad_end_ref[e - 1] if e else 0)

        for e in range(N_EXPERTS):
            pl.when(has_block(e))(lambda e=e: fill(e).start())
        for e in range(N_EXPERTS):
            pl.when(has_block(e))(lambda e=e: fill(e).wait())

    _rows_to_tiles(rows_scr, x_ref[...])

    def tile(ref, row):
        return ref.at[pl.ds(pl.multiple_of(row * ROW_TILES, ROW_TILES), ROW_TILES), :]

    def issue(g, carry):
        base = pl.multiple_of(g * (ISSUE_UNROLL // TOP_K), ISSUE_UNROLL // TOP_K)
        for u in range(ISSUE_UNROLL // TOP_K):
            r = base + u
            for j in range(TOP_K):
                pltpu.make_async_copy(tile(rows_scr, r), tile(xs_ref, dest_ref[0, 0, r * TOP_K + j]), sem).start(priority=j % 2)
        return carry

    lax.fori_loop(0, tm * TOP_K // ISSUE_UNROLL, issue, 0)
    for j in range(TOP_K):
        pltpu.make_async_copy(rows_scr, xs_ref.at[pl.ds(0, tm * ROW_TILES), :], sem).wait()


def _dispatch(x, dest_blocks, pad_end, n_slots, tm, blk):
    T = x.shape[0]
    return pl.pallas_call(
        functools.partial(_dispatch_kernel, tm=tm, blk=blk),
        out_shape=jax.ShapeDtypeStruct((n_slots * ROW_TILES, LANES), F32),
        grid_spec=pltpu.PrefetchScalarGridSpec(
            num_scalar_prefetch=1,
            grid=(T // tm,),
            in_specs=[pl.BlockSpec((1, 1, tm * TOP_K), lambda i, pe: (i, 0, 0), memory_space=pltpu.SMEM),
                      pl.BlockSpec((tm, D_MODEL), lambda i, pe: (i, 0))],
            out_specs=pl.BlockSpec(memory_space=pl.ANY),
            scratch_shapes=[pltpu.VMEM((tm * ROW_TILES, LANES), F32), pltpu.VMEM((blk * ROW_TILES, LANES), F32),
                            pltpu.SemaphoreType.DMA, pltpu.SemaphoreType.DMA]),
        compiler_params=_params("arbitrary"),
        name="moe_dispatch",
    )(pad_end, dest_blocks, x)


def _collect_combine_kernel(idx_ref, idx_next_ref, src_ref, gate_ref, x_ref, g_ref, b_ref, y_ref, buf, sem, *, tm):
    i = pl.program_id(0)
    slot = i % 2

    def tile(ref, row):
        return ref.at[pl.ds(pl.multiple_of(row * ROW_TILES, ROW_TILES), ROW_TILES), :]

    def request(ids, s):
        def body(g, carry):
            base = pl.multiple_of(g * (ISSUE_UNROLL // TOP_K), ISSUE_UNROLL // TOP_K)
            for u in range(ISSUE_UNROLL // TOP_K):
                r = base + u
                for j in range(TOP_K):
                    pltpu.make_async_copy(tile(src_ref, ids[0, 0, r * TOP_K + j]), tile(buf.at[s, j], r),
                                          sem.at[s]).start(priority=j % 2)
            return carry
        lax.fori_loop(0, tm * TOP_K // ISSUE_UNROLL, body, 0)

    @pl.when(i == 0)
    def _():
        request(idx_ref, 0)

    @pl.when(i + 1 < pl.num_programs(0))
    def _():
        request(idx_next_ref, 1 - slot)

    for j in range(TOP_K):
        pltpu.make_async_copy(src_ref.at[pl.ds(0, tm * ROW_TILES), :], buf.at[slot, j], sem.at[slot]).wait()
    gates = gate_ref[...]
    acc = DEEPNORM_ALPHA * x_ref[...]
    for j in range(TOP_K):
        acc = acc + gates[:, j:j + 1] * _tiles_to_rows(buf.at[slot, j], tm)
    y_ref[...] = _layer_norm(acc, g_ref[...], b_ref[...])


def _collect_combine(src, idx_blocks, gates, x, ln_g, ln_b, tm):
    T = x.shape[0]
    nt = T // tm
    row = pl.BlockSpec((tm, D_MODEL), lambda i: (i, 0))
    vec = _full((1, D_MODEL))
    return pl.pallas_call(
        functools.partial(_collect_combine_kernel, tm=tm),
        out_shape=jax.ShapeDtypeStruct((T, D_MODEL), F32),
        grid=(nt,),
        in_specs=[pl.BlockSpec((1, 1, tm * TOP_K), lambda i: (i, 0, 0), memory_space=pltpu.SMEM),
                  pl.BlockSpec((1, 1, tm * TOP_K), lambda i: (jnp.minimum(i + 1, nt - 1), 0, 0), memory_space=pltpu.SMEM),
                  pl.BlockSpec(memory_space=pl.ANY),
                  pl.BlockSpec((tm, TOP_K), lambda i: (i, 0)), row, vec, vec],
        out_specs=row,
        scratch_shapes=[pltpu.VMEM((2, TOP_K, tm * ROW_TILES, LANES), F32), pltpu.SemaphoreType.DMA((2,))],
        compiler_params=_params("arbitrary"),
        name="moe_collect_combine",
    )(idx_blocks, idx_blocks, src, gates, x, ln_g, ln_b)


def _expert_kernel(be_ref, nb_ref, xs_ref, w1_ref, b1_ref, w2_ref, b2_ref, o_ref, *, blk):
    @pl.when(pl.program_id(0) < nb_ref[0])
    def _():
        h = _bdot(_tiles_to_rows(xs_ref, blk), w1_ref[0]) + b1_ref[0]
        glu = jnp.minimum(h[:, :D_EXPERT], SWIGLU_LIMIT)
        lin = jnp.clip(h[:, D_EXPERT:], -SWIGLU_LIMIT, SWIGLU_LIMIT)
        a = glu * jax.nn.sigmoid(SWIGLU_ALPHA * glu) * (lin + 1.0)
        _rows_to_tiles(o_ref, _bdot(a, w2_ref[0]) + b2_ref[0])


def _experts(xs, block_e, n_used, w1, b1, w2, b2, blk):
    P = xs.shape[0] // ROW_TILES
    nb = P // blk

    def blk_map(i, be, nu):
        return (jnp.minimum(i, nu[0] - 1), 0)

    def w_map(i, be, nu):
        return (be[jnp.minimum(i, nu[0] - 1)], 0, 0)

    return pl.pallas_call(
        functools.partial(_expert_kernel, blk=blk),
        out_shape=jax.ShapeDtypeStruct((P * ROW_TILES, LANES), F32),
        grid_spec=pltpu.PrefetchScalarGridSpec(
            num_scalar_prefetch=2,
            grid=(nb,),
            in_specs=[pl.BlockSpec((blk * ROW_TILES, LANES), blk_map),
                      pl.BlockSpec((1, D_MODEL, 2 * D_EXPERT), w_map),
                      pl.BlockSpec((1, 1, 2 * D_EXPERT), w_map),
                      pl.BlockSpec((1, D_EXPERT, D_MODEL), w_map),
                      pl.BlockSpec((1, 1, D_MODEL), w_map)],
            out_specs=pl.BlockSpec((blk * ROW_TILES, LANES), blk_map)),
        compiler_params=_params("arbitrary"),
        name="moe_experts",
    )(block_e, n_used, xs, w1, b1, w2, b2)


def _moe(x, layer, w_r, b_r, w1, b1, w2, b2, ln_g, ln_b):
    T = x.shape[0]
    blk = 1024 if T >= 4096 else 128
    tm = _row_tile(T, 512)
    n_assign = T * TOP_K
    n_blocks = -(-(n_assign + N_EXPERTS * (blk - 1)) // blk)
    n_slots = n_blocks * blk

    top_i, gates, pos, counts = _router(x, w_r, b_r)
    counts = counts[:, 0]
    padded = (counts + blk - 1) // blk * blk
    pad_end = jnp.cumsum(padded)
    pad_start = pad_end - padded
    is_e = top_i[None] == jnp.arange(N_EXPERTS, dtype=jnp.int32)[:, None, None]
    dest = pos + jnp.sum(jnp.where(is_e, pad_start.astype(jnp.int32)[:, None, None], 0), axis=0)
    block_start = jnp.arange(n_blocks, dtype=jnp.int32) * blk
    block_e = jnp.minimum(jnp.sum(pad_end[None, :] <= block_start[:, None], axis=1), N_EXPERTS - 1).astype(jnp.int32)
    n_used = (pad_end[-1:] // blk).astype(jnp.int32)
    dest_blocks = dest.T.reshape(T // tm, 1, tm * TOP_K)

    xs = _dispatch(x, dest_blocks, pad_end.astype(jnp.int32), n_slots, tm, blk)
    out = _experts(xs, block_e + layer * N_EXPERTS, n_used, w1, b1, w2, b2, blk)
    return _collect_combine(out, dest_blocks, gates.T, x, ln_g, ln_b, tm)


def _pad_rows(t, n):
    return t if n == 0 else jnp.pad(t, ((0, 0), (0, n), (0, 0)))


def _trunk(x, gla_s0, past_k, past_v, past_valid, mem_k, mem_v, w):
    B, L, _ = x.shape
    Lp = -(-L // CHUNK) * CHUNK
    T = B * L
    vec = lambda a: a.reshape(1, -1)
    gla_states = []
    k_all = v_all = new_k = new_v = None
    for layer in range(DEPTH):
        lg, lb = w['ln_g'][layer], w['ln_b'][layer]
        xt = x.reshape(T, D_MODEL)
        if layer < N_A:
            i = layer
            xt = _pad_rows(x, Lp - L).reshape(B * Lp, D_MODEL)
            q, k, v, r, b = _gla_in(xt, w['gla_w_main'][i], w['gla_w_g'][i], w['gla_w_g2'][i],
                                    vec(w['gla_b_g'][i]), L, Lp)
            seq = lambda t: t.reshape(B, Lp, -1)
            s0t = jnp.swapaxes(gla_s0[i], -1, -2)
            y, st = _gla_scan(seq(q), seq(k), seq(v), seq(b), s0t, seq(r), seq(xt), vec(w['gla_gn_g'][i]),
                              w['gla_w_o'][i], vec(lg[0]), vec(lb[0]))
            gla_states.append(jnp.swapaxes(st, -1, -2))
            x = y[:, :L]
        else:
            j = layer - N_A
            if j == 0:
                k_new, v_new = _linear_kv(xt, w['kv_w'][None])
                k_full = jnp.concatenate([past_k, k_new.reshape(B, L, SWA_KV)], axis=1)
                v_full = jnp.concatenate([past_v, v_new.reshape(B, L, SWA_KV)], axis=1)
                new_k, new_v = k_full[:, -WINDOW:], v_full[:, -WINDOW:]
                k_all, v_all = _pad_rows(k_full, Lp - L), _pad_rows(v_full, Lp - L)
            xp = _swa(_pad_rows(x, Lp - L), k_all, v_all, w['swa_w_q'][j], w['swa_sinks'][j].reshape(-1, 1),
                      w['swa_w_o'][j], vec(lg[0]), vec(lb[0]), L, past_valid)
            x = xp[:, :L]
        x = _mem_xattn(x, mem_k, mem_v, layer, w['mem_w_q'][layer], w['mem_w_o'][layer], vec(lg[1]), vec(lb[1]))
        xt = _moe(x.reshape(T, D_MODEL), layer, w['moe_w_r'][layer], w['moe_b_r'][layer], w['moe_w1'],
                  w['moe_b1'], w['moe_w2'], w['moe_b2'], vec(lg[2]), vec(lb[2]))
        x = xt.reshape(B, L, D_MODEL)
    return x, jnp.stack(gla_states), new_k, new_v


def _prep_weights(gla_w_in, gla_w_g2, gla_b_g, gla_gn_g, gla_w_o, kv_w, swa_w_q, swa_sinks, swa_w_o,
                  mem_w_q, mem_w_kv, mem_w_o, moe_w_r, moe_b_r, moe_w1, moe_b1, moe_w2, moe_b2, ln_g, ln_b):
    n_main = 2 * GLA_QK + 2 * GLA_V
    pad_c = lambda a, n: jnp.pad(a, [(0, 0)] * (a.ndim - 1) + [(0, n - a.shape[-1])])
    w = dict(
        gla_w_main=gla_w_in[:, :, :n_main].astype(BF16),
        gla_w_g=pad_c(gla_w_in[:, :, n_main:], LANES).astype(BF16),
        gla_w_g2=jnp.pad(gla_w_g2, ((0, 0), (0, LANES - GLA_GATE_RANK), (0, 0))).astype(BF16),
        gla_b_g=gla_b_g, gla_gn_g=gla_gn_g.reshape(N_A, GLA_V), gla_w_o=gla_w_o.astype(BF16),
        kv_w=kv_w.astype(BF16), swa_w_q=swa_w_q.astype(BF16), swa_sinks=swa_sinks, swa_w_o=swa_w_o.astype(BF16),
        mem_w_q=mem_w_q.astype(BF16), mem_w_kv=mem_w_kv.astype(BF16), mem_w_o=mem_w_o.astype(BF16),
        moe_w_r=jnp.swapaxes(moe_w_r, -1, -2).astype(BF16), moe_b_r=moe_b_r.reshape(DEPTH, N_EXPERTS, 1),
        moe_w1=_w1_relayout(moe_w1.reshape(DEPTH * N_EXPERTS, D_MODEL, 2 * D_EXPERT)),
        moe_b1=jnp.concatenate([moe_b1[..., 0::2], moe_b1[..., 1::2]], axis=-1).reshape(DEPTH * N_EXPERTS, 1, -1),
        moe_w2=moe_w2.astype(BF16).reshape(DEPTH * N_EXPERTS, D_EXPERT, D_MODEL),
        moe_b2=moe_b2.reshape(DEPTH * N_EXPERTS, 1, D_MODEL),
        ln_g=ln_g, ln_b=ln_b)
    return w


def kernel(x_prompt, x_sample, state_gla, cache_swa_k, cache_swa_v, cache_mem_k, cache_mem_v, mem_prompt, gla_w_in, gla_w_g2, gla_b_g, gla_gn_g, gla_w_o, kv_w, swa_w_q, swa_sinks, swa_w_o, mem_w_q, mem_w_kv, mem_w_o, moe_w_r, moe_b_r, moe_w1, moe_b1, moe_w2, moe_b2, ln_g, ln_b):
    w = _prep_weights(gla_w_in, gla_w_g2, gla_b_g, gla_gn_g, gla_w_o, kv_w, swa_w_q, swa_sinks, swa_w_o,
                      mem_w_q, mem_w_kv, mem_w_o, moe_w_r, moe_b_r, moe_w1, moe_b1, moe_w2, moe_b2, ln_g, ln_b)
    bp = x_prompt.shape[0]
    bs = x_sample.shape[0]
    mem_flat = mem_prompt.reshape(bp * N_MEM, D_MODEL)
    mem_k_p, mem_v_p = _linear_kv(mem_flat, w['mem_w_kv'])
    gla0 = jnp.zeros((N_A, bp, GLA_HEADS, GLA_DK, GLA_DV), F32)
    zero_win = jnp.zeros((bp, WINDOW, SWA_KV), F32)
    y_p, gla_p, k_p, v_p = _trunk(x_prompt, gla0, zero_win, zero_win, False,
                                  mem_k_p.reshape(DEPTH * bp, N_MEM, D_MODEL),
                                  mem_v_p.reshape(DEPTH * bp, N_MEM, D_MODEL), w)
    y_s, gla_s, k_s, v_s = _trunk(x_sample, state_gla, cache_swa_k.reshape(bs, WINDOW, SWA_KV),
                                  cache_swa_v.reshape(bs, WINDOW, SWA_KV), True,
                                  cache_mem_k.reshape(DEPTH * bs, N_MEM, D_MODEL),
                                  cache_mem_v.reshape(DEPTH * bs, N_MEM, D_MODEL), w)
    heads4 = lambda t: t.reshape(t.shape[0], WINDOW, SWA_KV_HEADS, SWA_HEAD_DIM)
    mem5 = lambda t: t.reshape(DEPTH, bp, N_MEM, MEM_HEADS, MEM_HEAD_DIM)
    return (y_p, y_s, gla_p, gla_s, heads4(k_p), heads4(v_p), heads4(k_s), heads4(v_s), mem5(mem_k_p), mem5(mem_v_p))
```

```python
import functools

import jax
import jax.numpy as jnp
from jax import lax
from jax.experimental import pallas as pl
from jax.experimental.pallas import tpu as pltpu

F32 = jnp.float32
BF16 = jnp.bfloat16

D_MODEL = 1024
DEPTH = 4
CHUNK = 64
N_A = DEPTH // 2
GLA_HEADS = 4
GLA_DK = D_MODEL // (2 * GLA_HEADS)
GLA_DV = D_MODEL // GLA_HEADS
GLA_QK = GLA_HEADS * GLA_DK
GLA_V = GLA_HEADS * GLA_DV
GLA_GATE_RANK = 16
GLA_GATE_TEMP = 16.0
SWA_HEAD_DIM = 64
SWA_Q_HEADS = D_MODEL // SWA_HEAD_DIM
SWA_KV_HEADS = 4
SWA_GROUP = SWA_Q_HEADS // SWA_KV_HEADS
SWA_KV = SWA_KV_HEADS * SWA_HEAD_DIM
WINDOW = 128
N_MEM = 256
MEM_HEADS = 4
MEM_HEAD_DIM = D_MODEL // MEM_HEADS
N_EXPERTS = 32
TOP_K = 4
D_EXPERT = D_MODEL
SWIGLU_ALPHA = 1.702
SWIGLU_LIMIT = 7.0
DEEPNORM_ALPHA = (2 * DEPTH) ** 0.25
LN_EPS = 1e-5
NEG_INF = -1e30

LANES = 128
VMEM_LIMIT = 56 * 1024 * 1024


def _params(*sem):
    return pltpu.CompilerParams(dimension_semantics=sem, vmem_limit_bytes=VMEM_LIMIT)


def _row_tile(n_rows, want):
    t = min(want, n_rows)
    while n_rows % t:
        t //= 2
    return t


def _full(shape):
    nd = len(shape)
    return pl.BlockSpec(shape, lambda *_: (0,) * nd)


def _bdot(a, b):
    return jnp.dot(a.astype(BF16), b.astype(BF16), preferred_element_type=F32)


def _bdot_nt(a, b):
    return lax.dot_general(a.astype(BF16), b.astype(BF16), (((1,), (1,)), ((), ())),
                           preferred_element_type=F32)


def _bdot_tn(a, b):
    return lax.dot_general(a.astype(BF16), b.astype(BF16), (((0,), (0,)), ((), ())),
                           preferred_element_type=F32)


def _layer_norm(z, g, b):
    mu = jnp.mean(z, axis=-1, keepdims=True)
    zc = z - mu
    var = jnp.mean(zc * zc, axis=-1, keepdims=True)
    return zc * lax.rsqrt(var + LN_EPS) * g + b


ROW_TILES = D_MODEL // LANES


def _rows_to_tiles(ref, x):
    n = x.shape[0]
    for c in range(ROW_TILES):
        ref[pl.ds(c, n, stride=ROW_TILES), :] = x[:, c * LANES:(c + 1) * LANES]


def _tiles_to_rows(ref, n):
    return jnp.concatenate([ref[pl.ds(c, n, stride=ROW_TILES), :] for c in range(ROW_TILES)], axis=1)


def _linear_kv_kernel(x_ref, w_ref, k_ref, v_ref):
    y = _bdot(x_ref[...], w_ref[0])
    half = y.shape[1] // 2
    k_ref[0] = y[:, :half]
    v_ref[0] = y[:, half:]


def _linear_kv(x, w, tm=512):
    T, K = x.shape
    G, _, N2 = w.shape
    tm = _row_tile(T, tm)
    out = jax.ShapeDtypeStruct((G, T, N2 // 2), F32)
    o_spec = pl.BlockSpec((1, tm, N2 // 2), lambda g, i: (g, i, 0))
    return pl.pallas_call(
        _linear_kv_kernel,
        out_shape=(out, out),
        grid=(G, T // tm),
        in_specs=[pl.BlockSpec((tm, K), lambda g, i: (i, 0)), pl.BlockSpec((1, K, N2), lambda g, i: (g, 0, 0))],
        out_specs=(o_spec, o_spec),
        compiler_params=_params("parallel", "parallel"),
        name="linear_kv",
    )(x, w)


MXU_DIM = 256


def _w1_relayout_kernel(w_ref, o_ref):
    half = MXU_DIM // 2
    r = lax.broadcasted_iota(jnp.int32, (MXU_DIM, MXU_DIM), 0)
    c = lax.broadcasted_iota(jnp.int32, (MXU_DIM, MXU_DIM), 1)
    src = jnp.where(c < half, 2 * c, 2 * (c - half) + 1)
    perm = (r == src).astype(BF16)
    n_out = o_ref.shape[-1] // 2
    for j in range(w_ref.shape[-1] // MXU_DIM):
        y = jnp.dot(w_ref[0, :, j * MXU_DIM:(j + 1) * MXU_DIM].astype(BF16), perm, preferred_element_type=F32)
        o_ref[0, :, j * half:(j + 1) * half] = y[:, :half].astype(BF16)
        o_ref[0, :, n_out + j * half:n_out + (j + 1) * half] = y[:, half:].astype(BF16)


def _w1_relayout(w1):
    E, D, N = w1.shape
    spec = pl.BlockSpec((1, D, N), lambda e: (e, 0, 0))
    return pl.pallas_call(
        _w1_relayout_kernel,
        out_shape=jax.ShapeDtypeStruct((E, D, N), BF16),
        grid=(E,),
        in_specs=[spec],
        out_specs=spec,
        compiler_params=_params("parallel"),
        name="w1_relayout",
    )(w1)


def _gla_in_kernel(x_ref, w_ref, wg_ref, wg2_ref, bg_ref, q_ref, k_ref, v_ref, r_ref, b_ref,
                   *, tm, seq_len, seq_pad):
    xb = x_ref[...].astype(BF16)
    y = jnp.dot(xb, w_ref[...], preferred_element_type=F32)
    q_ref[...] = y[:, :GLA_QK] * (GLA_DK ** -0.5)
    k_ref[...] = y[:, GLA_QK:2 * GLA_QK]
    v_ref[...] = y[:, 2 * GLA_QK:2 * GLA_QK + GLA_V]
    r_ref[...] = y[:, 2 * GLA_QK + GLA_V:]
    g_lr = jnp.dot(xb, wg_ref[...], preferred_element_type=F32)
    z = _bdot(g_lr, wg2_ref[...]) + bg_ref[...]
    log_a = (jnp.minimum(z, 0.0) - jnp.log(1.0 + jnp.exp(-jnp.abs(z)))) / GLA_GATE_TEMP
    if seq_len < seq_pad:
        pos = (pl.program_id(0) * tm + lax.broadcasted_iota(jnp.int32, (tm, 1), 0)) % seq_pad
        log_a = jnp.where(pos < seq_len, log_a, 0.0)
    row = lax.broadcasted_iota(jnp.int32, (tm, tm), 0)
    col = lax.broadcasted_iota(jnp.int32, (tm, tm), 1)
    tri = ((row // CHUNK == col // CHUNK) & (row >= col)).astype(BF16)
    g1 = log_a.astype(BF16)
    r1 = log_a - g1.astype(F32)
    g2 = r1.astype(BF16)
    g3 = (r1 - g2.astype(F32)).astype(BF16)
    b_ref[...] = (jnp.dot(tri, g1, preferred_element_type=F32) + jnp.dot(tri, g2, preferred_element_type=F32)
                  + jnp.dot(tri, g3, preferred_element_type=F32))


def _gla_in(x, w_main, w_g, w_g2, b_g, seq_len, seq_pad, tm=512):
    T = x.shape[0]
    tm = _row_tile(T, tm)
    assert tm % CHUNK == 0 and seq_pad % CHUNK == 0
    n_main = w_main.shape[1]
    row = lambda n: pl.BlockSpec((tm, n), lambda i: (i, 0))
    return pl.pallas_call(
        functools.partial(_gla_in_kernel, tm=tm, seq_len=seq_len, seq_pad=seq_pad),
        out_shape=(jax.ShapeDtypeStruct((T, GLA_QK), F32), jax.ShapeDtypeStruct((T, GLA_QK), F32),
                   jax.ShapeDtypeStruct((T, GLA_V), F32), jax.ShapeDtypeStruct((T, GLA_V), F32),
                   jax.ShapeDtypeStruct((T, GLA_QK), F32)),
        grid=(T // tm,),
        in_specs=[row(D_MODEL), _full((D_MODEL, n_main)), _full((D_MODEL, LANES)),
                  _full((LANES, GLA_QK)), _full((1, GLA_QK))],
        out_specs=(row(GLA_QK), row(GLA_QK), row(GLA_V), row(GLA_V), row(GLA_QK)),
        compiler_params=_params("parallel"),
        name="gla_in",
    )(x, w_main, w_g, w_g2, b_g)


SUBLANES = 8
GLA_SUB = 16


def _gla_intra(q, k, b):
    n_grp = CHUNK // SUBLANES
    g_sub = GLA_SUB // SUBLANES
    lane = lax.broadcasted_iota(jnp.int32, (SUBLANES, LANES), 1)
    row = lax.broadcasted_iota(jnp.int32, (SUBLANES, LANES), 0)
    grp = lambda t, i: t[i * SUBLANES:(i + 1) * SUBLANES]
    att = [jnp.zeros((SUBLANES, LANES), F32) for _ in range(n_grp)]
    for s in range(CHUNK):
        b_s = b[s:s + 1, :]
        k_s = k[s:s + 1, :]
        for i in range(s // SUBLANES, (s // GLA_SUB + 1) * g_sub):
            e = jnp.exp(grp(b, i) - b_s)
            red = jnp.sum(grp(q, i) * e * k_s, axis=-1, keepdims=True)
            att[i] = jnp.where(lane == s, red, att[i])
    att = [jnp.where(row + i * SUBLANES >= lane, att[i], 0.0) for i in range(n_grp)]
    for blk in range(1, CHUNK // GLA_SUB):
        r0 = blk * GLA_SUB
        c = b[r0 - 1:r0, :]
        k_early = jnp.concatenate([k[:r0] * jnp.exp(c - b[:r0]), jnp.zeros((LANES - r0, GLA_DK), F32)], axis=0)
        off = _bdot_nt(q[r0:r0 + GLA_SUB] * jnp.exp(b[r0:r0 + GLA_SUB] - c), k_early)
        for i in range(g_sub):
            att[blk * g_sub + i] = att[blk * g_sub + i] + grp(off, i)
    return jnp.concatenate(att, axis=0)


def _gla_scan_kernel(q_ref, k_ref, v_ref, b_ref, s0_ref, r_ref, x_ref, gn_ref, wo_ref, g_ref, lb_ref,
                     y_ref, s_ref, st_scr, o_scr, *, n_chunks):
    @pl.when(pl.program_id(1) == 0)
    def _():
        st_scr[...] = s0_ref[0]

    for c in range(n_chunks):
        ts = slice(c * CHUNK, (c + 1) * CHUNK)
        for h in range(GLA_HEADS):
            ks = slice(h * GLA_DK, (h + 1) * GLA_DK)
            vs = slice(h * GLA_DV, (h + 1) * GLA_DV)
            q = q_ref[0, ts, ks]
            k = k_ref[0, ts, ks]
            b = b_ref[0, ts, ks]
            v = v_ref[0, ts, vs]
            att = _gla_intra(q, k, b)
            b_end = b[CHUNK - 1:CHUNK, :]
            st = st_scr[h]
            o_scr[ts, vs] = _bdot(att[:, :CHUNK], v) + _bdot_nt(q * jnp.exp(b), st)
            st_scr[h] = st * jnp.exp(b_end) + _bdot_tn(v, k * jnp.exp(b_end - b))

    @pl.when(pl.program_id(1) == pl.num_programs(1) - 1)
    def _():
        s_ref[0] = st_scr[...]

    parts = []
    for h in range(GLA_HEADS):
        vs = slice(h * GLA_DV, (h + 1) * GLA_DV)
        o = o_scr[:, vs]
        mu = jnp.mean(o, axis=-1, keepdims=True)
        oc = o - mu
        var = jnp.mean(oc * oc, axis=-1, keepdims=True)
        parts.append(oc * lax.rsqrt(var + LN_EPS) * gn_ref[:, vs])
    r = r_ref[0]
    o = jnp.concatenate(parts, axis=-1) * (r * jax.nn.sigmoid(r))
    h_out = _bdot(o, wo_ref[...])
    y_ref[0] = _layer_norm(DEEPNORM_ALPHA * x_ref[0] + h_out, g_ref[...], lb_ref[...])


def _gla_scan(q, k, v, b, s0t, r, x, gn_g, w_o, ln_g, ln_b, tl=256):
    B, Lp, _ = q.shape
    tl = _row_tile(Lp, tl)
    qk_spec = pl.BlockSpec((1, tl, GLA_QK), lambda b, i: (b, i, 0))
    v_spec = pl.BlockSpec((1, tl, GLA_V), lambda b, i: (b, i, 0))
    s_spec = pl.BlockSpec((1, GLA_HEADS, GLA_DV, GLA_DK), lambda b, i: (b, 0, 0, 0))
    vec = _full((1, D_MODEL))
    return pl.pallas_call(
        functools.partial(_gla_scan_kernel, n_chunks=tl // CHUNK),
        out_shape=(jax.ShapeDtypeStruct((B, Lp, D_MODEL), F32),
                   jax.ShapeDtypeStruct((B, GLA_HEADS, GLA_DV, GLA_DK), F32)),
        grid=(B, Lp // tl),
        in_specs=[qk_spec, qk_spec, v_spec, qk_spec, s_spec, v_spec, v_spec, vec, _full((GLA_V, D_MODEL)), vec, vec],
        out_specs=(v_spec, s_spec),
        scratch_shapes=[pltpu.VMEM((GLA_HEADS, GLA_DV, GLA_DK), F32), pltpu.VMEM((tl, GLA_V), F32)],
        compiler_params=_params("parallel", "arbitrary"),
        name="gla_scan",
    )(q, k, v, b, s0t, r, x, gn_g, w_o, ln_g, ln_b)


def _swa_kernel(x_ref, k_ref, v_ref, wq_ref, sink_ref, wo_ref, g_ref, b_ref, y_ref,
                *, tl, seq_len, past_valid):
    x = x_ref[0]
    q = (_bdot(x, wq_ref[...]) * (SWA_HEAD_DIM ** -0.5)).astype(BF16)
    kw = tl + WINDOW
    r0 = pl.multiple_of(pl.program_id(1) * tl, tl)
    kt = k_ref[0, pl.ds(r0, kw), :].astype(BF16)
    vt = v_ref[0, pl.ds(r0, kw), :].astype(BF16)
    qs = min(tl, WINDOW)
    kws = qs + WINDOW
    q_row = lax.broadcasted_iota(jnp.int32, (qs, kws), 0)
    k_col = lax.broadcasted_iota(jnp.int32, (qs, kws), 1)
    band_lo = (q_row // CHUNK) * CHUNK
    in_band = (k_col >= band_lo) & (k_col < band_lo + WINDOW + CHUNK)
    oks = []
    for u in range(tl // qs):
        pos = r0 + u * qs + k_col
        ok = in_band & (pos - WINDOW < seq_len)
        oks.append(ok if past_valid else ok & (pos >= WINDOW))
    zeros = jnp.zeros((kw, SWA_HEAD_DIM), BF16)
    pair_out = [[] for _ in range(SWA_Q_HEADS // 2)]
    for g in range(SWA_KV_HEADS):
        kg = kt[:, g * SWA_HEAD_DIM:(g + 1) * SWA_HEAD_DIM]
        vg = vt[:, g * SWA_HEAD_DIM:(g + 1) * SWA_HEAD_DIM]
        k_pad = (jnp.concatenate([kg, zeros], axis=1), jnp.concatenate([zeros, kg], axis=1))
        v_pad = (jnp.concatenate([vg, zeros], axis=1), jnp.concatenate([zeros, vg], axis=1))
        for pair in range(g * SWA_GROUP // 2, (g + 1) * SWA_GROUP // 2):
            sinks = [sink_ref[2 * pair + j:2 * pair + j + 1, :] for j in range(2)]
            work = [(u, j) for u in range(tl // qs) for j in range(2)]
            kwin = [slice(u * qs, u * qs + kws) for u in range(tl // qs)]
            qp = [q[u * qs:(u + 1) * qs, pair * LANES:(pair + 1) * LANES] for u in range(tl // qs)]
            s = [jnp.where(oks[u], _bdot_nt(qp[u], k_pad[j][kwin[u]]), NEG_INF) for u, j in work]
            m = [jnp.maximum(jnp.max(s_i, axis=-1, keepdims=True), sinks[j]) for s_i, (u, j) in zip(s, work)]
            e = [jnp.exp(s_i - m_i) for s_i, m_i in zip(s, m)]
            o = [_bdot(e_i, v_pad[j][kwin[u]]) for e_i, (u, j) in zip(e, work)]
            inv = [1.0 / (jnp.sum(e_i, axis=-1, keepdims=True) + jnp.exp(sinks[j] - m_i))
                   for e_i, m_i, (u, j) in zip(e, m, work)]
            on = [o_i * inv_i for o_i, inv_i in zip(o, inv)]
            for u in range(tl // qs):
                pair_out[pair].append(on[2 * u] + on[2 * u + 1])
    pair_out = [jnp.concatenate(p, axis=0) for p in pair_out]
    h_out = _bdot(jnp.concatenate(pair_out, axis=1), wo_ref[...])
    y_ref[0] = _layer_norm(DEEPNORM_ALPHA * x + h_out, g_ref[...], b_ref[...])


def _swa(x, k_all, v_all, w_q, sinks, w_o, ln_g, ln_b, seq_len, past_valid, tl=256):
    B, Lp, _ = x.shape
    tl = _row_tile(Lp, tl)
    x_spec = pl.BlockSpec((1, tl, D_MODEL), lambda b, i: (b, i, 0))
    kv_spec = pl.BlockSpec((1, WINDOW + Lp, SWA_KV), lambda b, i: (b, 0, 0))
    vec = _full((1, D_MODEL))
    return pl.pallas_call(
        functools.partial(_swa_kernel, tl=tl, seq_len=seq_len, past_valid=past_valid),
        out_shape=jax.ShapeDtypeStruct((B, Lp, D_MODEL), F32),
        grid=(B, Lp // tl),
        in_specs=[x_spec, kv_spec, kv_spec, _full((D_MODEL, D_MODEL)), _full((SWA_Q_HEADS, 1)),
                  _full((D_MODEL, D_MODEL)), vec, vec],
        out_specs=x_spec,
        compiler_params=_params("parallel", "arbitrary"),
        name="swa",
    )(x, k_all, v_all, w_q, sinks, w_o, ln_g, ln_b)


def _mem_kernel(x_ref, mk_ref, mv_ref, wq_ref, wo_ref, g_ref, b_ref, y_ref):
    x = x_ref[0]
    q = _bdot(x, wq_ref[...]) * (MEM_HEAD_DIM ** -0.5)
    parts = []
    for h in range(MEM_HEADS):
        hs = slice(h * MEM_HEAD_DIM, (h + 1) * MEM_HEAD_DIM)
        s = _bdot_nt(q[:, hs], mk_ref[0, :, hs])
        m = jnp.max(s, axis=-1, keepdims=True)
        e = jnp.exp(s - m)
        parts.append(_bdot(e, mv_ref[0, :, hs]) * (1.0 / jnp.sum(e, axis=-1, keepdims=True)))
    o = jnp.concatenate(parts, axis=-1)
    h_out = _bdot(o, wo_ref[...])
    y_ref[0] = _layer_norm(DEEPNORM_ALPHA * x + h_out, g_ref[...], b_ref[...])


def _mem_xattn(x, mk, mv, layer, w_q, w_o, ln_g, ln_b, tl=1024):
    B, L, _ = x.shape
    tl = _row_tile(L, tl)
    x_spec = pl.BlockSpec((1, tl, D_MODEL), lambda b, i: (b, i, 0))
    m_spec = pl.BlockSpec((1, N_MEM, D_MODEL), lambda b, i: (layer * B + b, 0, 0))
    vec = _full((1, D_MODEL))
    return pl.pallas_call(
        _mem_kernel,
        out_shape=jax.ShapeDtypeStruct((B, L, D_MODEL), F32),
        grid=(B, L // tl),
        in_specs=[x_spec, m_spec, m_spec, _full((D_MODEL, D_MODEL)), _full((D_MODEL, D_MODEL)), vec, vec],
        out_specs=x_spec,
        compiler_params=_params("parallel", "arbitrary"),
        name="mem_xattn",
    )(x, mk, mv, w_q, w_o, ln_g, ln_b)


def _router_kernel(x_ref, wrt_ref, br_ref, idx_ref, gate_ref, pos_ref, cnt_ref, cnt_scr, *, tm):
    @pl.when(pl.program_id(0) == 0)
    def _():
        cnt_scr[...] = jnp.zeros_like(cnt_scr)

    logits = _bdot_nt(wrt_ref[...], x_ref[...]) + br_ref[...]
    expert = lax.broadcasted_iota(jnp.int32, (N_EXPERTS, tm), 0).astype(F32)
    sel = jnp.zeros((N_EXPERTS, tm), F32)
    l = logits
    vals, hits = [], []
    for j in range(TOP_K):
        m = jnp.max(l, axis=0, keepdims=True)
        idx = jnp.min(jnp.where(l == m, expert, float(N_EXPERTS)), axis=0, keepdims=True)
        hit = expert == idx
        sel = jnp.where(hit, 1.0, sel)
        l = jnp.where(hit, -jnp.inf, l)
        idx_ref[j:j + 1, :] = idx.astype(jnp.int32)
        vals.append(m)
        hits.append(hit)
    e = [jnp.exp(v - vals[0]) for v in vals]
    inv = 1.0 / (e[0] + e[1] + e[2] + e[3])
    r = lax.broadcasted_iota(jnp.int32, (tm, tm), 0)
    c = lax.broadcasted_iota(jnp.int32, (tm, tm), 1)
    before = jnp.dot(sel.astype(BF16), (r < c).astype(BF16), preferred_element_type=F32) + cnt_scr[:, 0:1]
    for j in range(TOP_K):
        gate_ref[j:j + 1, :] = e[j] * inv
        pos_ref[j:j + 1, :] = jnp.sum(jnp.where(hits[j], before, 0.0), axis=0, keepdims=True).astype(jnp.int32)
    cnt_scr[...] = cnt_scr[...] + jnp.sum(sel, axis=1, keepdims=True)
    cnt_ref[...] = cnt_scr[...].astype(jnp.int32)


def _router(x, w_rt, b_r, tm=1024):
    T = x.shape[0]
    tm = _row_tile(T, tm)
    small = pl.BlockSpec((TOP_K, tm), lambda i: (0, i))
    return pl.pallas_call(
        functools.partial(_router_kernel, tm=tm),
        out_shape=(jax.ShapeDtypeStruct((TOP_K, T), jnp.int32), jax.ShapeDtypeStruct((TOP_K, T), F32),
                   jax.ShapeDtypeStruct((TOP_K, T), jnp.int32), jax.ShapeDtypeStruct((N_EXPERTS, LANES), jnp.int32)),
        grid=(T // tm,),
        in_specs=[pl.BlockSpec((tm, D_MODEL), lambda i: (i, 0)), _full((N_EXPERTS, D_MODEL)), _full((N_EXPERTS, 1))],
        out_specs=(small, small, small, _full((N_EXPERTS, LANES))),
        scratch_shapes=[pltpu.VMEM((N_EXPERTS, LANES), F32)],
        compiler_params=_params("arbitrary"),
        name="moe_router",
    )(x, w_rt, b_r)


ISSUE_UNROLL = 16


def _dispatch_kernel(pad_end_ref, dest_ref, x_ref, xs_ref, rows_scr, zero_scr, sem, zsem, *, tm, blk):
    @pl.when(pl.program_id(0) == 0)
    def _():
        zero_scr[...] = jnp.zeros_like(zero_scr)

        def fill(e):
            start = pl.multiple_of((pad_end_ref[e] - blk) * ROW_TILES, blk * ROW_TILES)
            return pltpu.make_async_copy(zero_scr, xs_ref.at[pl.ds(start, blk * ROW_TILES), :], zsem)

        def has_block(e):
            return pad_end_ref[e] > (pad_end_ref[e - 1] if e else 0)

        for e in range(N_EXPERTS):
            pl.when(has_block(e))(lambda e=e: fill(e).start())
        for e in range(N_EXPERTS):
            pl.when(has_block(e))(lambda e=e: fill(e).wait())

    _rows_to_tiles(rows_scr, x_ref[...])

    def tile(ref, row):
        return ref.at[pl.ds(pl.multiple_of(row * ROW_TILES, ROW_TILES), ROW_TILES), :]

    def issue(g, carry):
        base = pl.multiple_of(g * (ISSUE_UNROLL // TOP_K), ISSUE_UNROLL // TOP_K)
        for u in range(ISSUE_UNROLL // TOP_K):
            r = base + u
            for j in range(TOP_K):
                pltpu.make_async_copy(tile(rows_scr, r), tile(xs_ref, dest_ref[0, 0, r * TOP_K + j]), sem).start(priority=j % 2)
        return carry

    lax.fori_loop(0, tm * TOP_K // ISSUE_UNROLL, issue, 0)
    for j in range(TOP_K):
        pltpu.make_async_copy(rows_scr, xs_ref.at[pl.ds(0, tm * ROW_TILES), :], sem).wait()


def _dispatch(x, dest_blocks, pad_end, n_slots, tm, blk):
    T = x.shape[0]
    return pl.pallas_call(
        functools.partial(_dispatch_kernel, tm=tm, blk=blk),
        out_shape=jax.ShapeDtypeStruct((n_slots * ROW_TILES, LANES), F32),
        grid_spec=pltpu.PrefetchScalarGridSpec(
            num_scalar_prefetch=1,
            grid=(T // tm,),
            in_specs=[pl.BlockSpec((1, 1, tm * TOP_K), lambda i, pe: (i, 0, 0), memory_space=pltpu.SMEM),
                      pl.BlockSpec((tm, D_MODEL), lambda i, pe: (i, 0))],
            out_specs=pl.BlockSpec(memory_space=pl.ANY),
            scratch_shapes=[pltpu.VMEM((tm * ROW_TILES, LANES), F32), pltpu.VMEM((blk * ROW_TILES, LANES), F32),
                            pltpu.SemaphoreType.DMA, pltpu.SemaphoreType.DMA]),
        compiler_params=_params("arbitrary"),
        name="moe_dispatch",
    )(pad_end, dest_blocks, x)


def _collect_combine_kernel(idx_ref, idx_next_ref, src_ref, gate_ref, x_ref, g_ref, b_ref, y_ref, buf, sem, *, tm):
    i = pl.program_id(0)
    slot = i % 2

    def tile(ref, row):
        return ref.at[pl.ds(pl.multiple_of(row * ROW_TILES, ROW_TILES), ROW_TILES), :]

    def request(ids, s):
        def body(g, carry):
            base = pl.multiple_of(g * (ISSUE_UNROLL // TOP_K), ISSUE_UNROLL // TOP_K)
            for u in range(ISSUE_UNROLL // TOP_K):
                r = base + u
                for j in range(TOP_K):
                    pltpu.make_async_copy(tile(src_ref, ids[0, 0, r * TOP_K + j]), tile(buf.at[s, j], r),
                                          sem.at[s]).start(priority=j % 2)
            return carry
        lax.fori_loop(0, tm * TOP_K // ISSUE_UNROLL, body, 0)

    @pl.when(i == 0)
    def _():
        request(idx_ref, 0)

    @pl.when(i + 1 < pl.num_programs(0))
    def _():
        request(idx_next_ref, 1 - slot)

    for j in range(TOP_K):
        pltpu.make_async_copy(src_ref.at[pl.ds(0, tm * ROW_TILES), :], buf.at[slot, j], sem.at[slot]).wait()
    gates = gate_ref[...]
    acc = DEEPNORM_ALPHA * x_ref[...]
    for j in range(TOP_K):
        acc = acc + gates[:, j:j + 1] * _tiles_to_rows(buf.at[slot, j], tm)
    y_ref[...] = _layer_norm(acc, g_ref[...], b_ref[...])


def _collect_combine(src, idx_blocks, gates, x, ln_g, ln_b, tm):
    T = x.shape[0]
    nt = T // tm
    row = pl.BlockSpec((tm, D_MODEL), lambda i: (i, 0))
    vec = _full((1, D_MODEL))
    return pl.pallas_call(
        functools.partial(_collect_combine_kernel, tm=tm),
        out_shape=jax.ShapeDtypeStruct((T, D_MODEL), F32),
        grid=(nt,),
        in_specs=[pl.BlockSpec((1, 1, tm * TOP_K), lambda i: (i, 0, 0), memory_space=pltpu.SMEM),
                  pl.BlockSpec((1, 1, tm * TOP_K), lambda i: (jnp.minimum(i + 1, nt - 1), 0, 0), memory_space=pltpu.SMEM),
                  pl.BlockSpec(memory_space=pl.ANY),
                  pl.BlockSpec((tm, TOP_K), lambda i: (i, 0)), row, vec, vec],
        out_specs=row,
        scratch_shapes=[pltpu.VMEM((2, TOP_K, tm * ROW_TILES, LANES), F32), pltpu.SemaphoreType.DMA((2,))],
        compiler_params=_params("arbitrary"),
        name="moe_collect_combine",
    )(idx_blocks, idx_blocks, src, gates, x, ln_g, ln_b)


def _expert_kernel(be_ref, nb_ref, xs_ref, w1_ref, b1_ref, w2_ref, b2_ref, o_ref, *, blk):
    @pl.when(pl.program_id(0) < nb_ref[0])
    def _():
        h = _bdot(_tiles_to_rows(xs_ref, blk), w1_ref[0]) + b1_ref[0]
        glu = jnp.minimum(h[:, :D_EXPERT], SWIGLU_LIMIT)
        lin = jnp.clip(h[:, D_EXPERT:], -SWIGLU_LIMIT, SWIGLU_LIMIT)
        a = glu * jax.nn.sigmoid(SWIGLU_ALPHA * glu) * (lin + 1.0)
        _rows_to_tiles(o_ref, _bdot(a, w2_ref[0]) + b2_ref[0])


def _experts(xs, block_e, n_used, w1, b1, w2, b2, blk):
    P = xs.shape[0] // ROW_TILES
    nb = P // blk

    def blk_map(i, be, nu):
        return (jnp.minimum(i, nu[0] - 1), 0)

    def w_map(i, be, nu):
        return (be[jnp.minimum(i, nu[0] - 1)], 0, 0)

    return pl.pallas_call(
        functools.partial(_expert_kernel, blk=blk),
        out_shape=jax.ShapeDtypeStruct((P * ROW_TILES, LANES), F32),
        grid_spec=pltpu.PrefetchScalarGridSpec(
            num_scalar_prefetch=2,
            grid=(nb,),
            in_specs=[pl.BlockSpec((blk * ROW_TILES, LANES), blk_map),
                      pl.BlockSpec((1, D_MODEL, 2 * D_EXPERT), w_map),
                      pl.BlockSpec((1, 1, 2 * D_EXPERT), w_map),
                      pl.BlockSpec((1, D_EXPERT, D_MODEL), w_map),
                      pl.BlockSpec((1, 1, D_MODEL), w_map)],
            out_specs=pl.BlockSpec((blk * ROW_TILES, LANES), blk_map)),
        compiler_params=_params("arbitrary"),
        name="moe_experts",
    )(block_e, n_used, xs, w1, b1, w2, b2)


def _moe(x, layer, w_r, b_r, w1, b1, w2, b2, ln_g, ln_b):
    T = x.shape[0]
    blk = 1024 if T >= 4096 else 128
    tm = _row_tile(T, 512)
    n_assign = T * TOP_K
    n_blocks = -(-(n_assign + N_EXPERTS * (blk - 1)) // blk)
    n_slots = n_blocks * blk

    top_i, gates, pos, counts = _router(x, w_r, b_r)
    counts = counts[:, 0]
    padded = (counts + blk - 1) // blk * blk
    pad_end = jnp.cumsum(padded)
    pad_start = pad_end - padded
    is_e = top_i[None] == jnp.arange(N_EXPERTS, dtype=jnp.int32)[:, None, None]
    dest = pos + jnp.sum(jnp.where(is_e, pad_start.astype(jnp.int32)[:, None, None], 0), axis=0)
    block_start = jnp.arange(n_blocks, dtype=jnp.int32) * blk
    block_e = jnp.minimum(jnp.sum(pad_end[None, :] <= block_start[:, None], axis=1), N_EXPERTS - 1).astype(jnp.int32)
    n_used = (pad_end[-1:] // blk).astype(jnp.int32)
    dest_blocks = dest.T.reshape(T // tm, 1, tm * TOP_K)

    xs = _dispatch(x, dest_blocks, pad_end.astype(jnp.int32), n_slots, tm, blk)
    out = _experts(xs, block_e + layer * N_EXPERTS, n_used, w1, b1, w2, b2, blk)
    return _collect_combine(out, dest_blocks, gates.T, x, ln_g, ln_b, tm)


def _pad_rows(t, n):
    return t if n == 0 else jnp.pad(t, ((0, 0), (0, n), (0, 0)))


def _trunk(x, gla_s0, past_k, past_v, past_valid, mem_k, mem_v, w):
    B, L, _ = x.shape
    Lp = -(-L // CHUNK) * CHUNK
    T = B * L
    vec = lambda a: a.reshape(1, -1)
    gla_states = []
    k_all = v_all = new_k = new_v = None
    for layer in range(DEPTH):
        lg, lb = w['ln_g'][layer], w['ln_b'][layer]
        xt = x.reshape(T, D_MODEL)
        if layer < N_A:
            i = layer
            xt = _pad_rows(x, Lp - L).reshape(B * Lp, D_MODEL)
            q, k, v, r, b = _gla_in(xt, w['gla_w_main'][i], w['gla_w_g'][i], w['gla_w_g2'][i],
                                    vec(w['gla_b_g'][i]), L, Lp)
            seq = lambda t: t.reshape(B, Lp, -1)
            s0t = jnp.swapaxes(gla_s0[i], -1, -2)
            y, st = _gla_scan(seq(q), seq(k), seq(v), seq(b), s0t, seq(r), seq(xt), vec(w['gla_gn_g'][i]),
                              w['gla_w_o'][i], vec(lg[0]), vec(lb[0]))
            gla_states.append(jnp.swapaxes(st, -1, -2))
            x = y[:, :L]
        else:
            j = layer - N_A
            if j == 0:
                k_new, v_new = _linear_kv(xt, w['kv_w'][None])
                k_full = jnp.concatenate([past_k, k_new.reshape(B, L, SWA_KV)], axis=1)
                v_full = jnp.concatenate([past_v, v_new.reshape(B, L, SWA_KV)], axis=1)
                new_k, new_v = k_full[:, -WINDOW:], v_full[:, -WINDOW:]
                k_all, v_all = _pad_rows(k_full, Lp - L), _pad_rows(v_full, Lp - L)
            xp = _swa(_pad_rows(x, Lp - L), k_all, v_all, w['swa_w_q'][j], w['swa_sinks'][j].reshape(-1, 1),
                      w['swa_w_o'][j], vec(lg[0]), vec(lb[0]), L, past_valid)
            x = xp[:, :L]
        x = _mem_xattn(x, mem_k, mem_v, layer, w['mem_w_q'][layer], w['mem_w_o'][layer], vec(lg[1]), vec(lb[1]))
        xt = _moe(x.reshape(T, D_MODEL), layer, w['moe_w_r'][layer], w['moe_b_r'][layer], w['moe_w1'],
                  w['moe_b1'], w['moe_w2'], w['moe_b2'], vec(lg[2]), vec(lb[2]))
        x = xt.reshape(B, L, D_MODEL)
    return x, jnp.stack(gla_states), new_k, new_v


def _prep_weights(gla_w_in, gla_w_g2, gla_b_g, gla_gn_g, gla_w_o, kv_w, swa_w_q, swa_sinks, swa_w_o,
                  mem_w_q, mem_w_kv, mem_w_o, moe_w_r, moe_b_r, moe_w1, moe_b1, moe_w2, moe_b2, ln_g, ln_b):
    n_main = 2 * GLA_QK + 2 * GLA_V
    pad_c = lambda a, n: jnp.pad(a, [(0, 0)] * (a.ndim - 1) + [(0, n - a.shape[-1])])
    w = dict(
        gla_w_main=gla_w_in[:, :, :n_main].astype(BF16),
        gla_w_g=pad_c(gla_w_in[:, :, n_main:], LANES).astype(BF16),
        gla_w_g2=jnp.pad(gla_w_g2, ((0, 0), (0, LANES - GLA_GATE_RANK), (0, 0))).astype(BF16),
        gla_b_g=gla_b_g, gla_gn_g=gla_gn_g.reshape(N_A, GLA_V), gla_w_o=gla_w_o.astype(BF16),
        kv_w=kv_w.astype(BF16), swa_w_q=swa_w_q.astype(BF16), swa_sinks=swa_sinks, swa_w_o=swa_w_o.astype(BF16),
        mem_w_q=mem_w_q.astype(BF16), mem_w_kv=mem_w_kv.astype(BF16), mem_w_o=mem_w_o.astype(BF16),
        moe_w_r=jnp.swapaxes(moe_w_r, -1, -2).astype(BF16), moe_b_r=moe_b_r.reshape(DEPTH, N_EXPERTS, 1),
        moe_w1=_w1_relayout(moe_w1.reshape(DEPTH * N_EXPERTS, D_MODEL, 2 * D_EXPERT)),
        moe_b1=jnp.concatenate([moe_b1[..., 0::2], moe_b1[..., 1::2]], axis=-1).reshape(DEPTH * N_EXPERTS, 1, -1),
        moe_w2=moe_w2.astype(BF16).reshape(DEPTH * N_EXPERTS, D_EXPERT, D_MODEL),
        moe_b2=moe_b2.reshape(DEPTH * N_EXPERTS, 1, D_MODEL),
        ln_g=ln_g, ln_b=ln_b)
    return w


def kernel(x_prompt, x_sample, state_gla, cache_swa_k, cache_swa_v, cache_mem_k, cache_mem_v, mem_prompt, gla_w_in, gla_w_g2, gla_b_g, gla_gn_g, gla_w_o, kv_w, swa_w_q, swa_sinks, swa_w_o, mem_w_q, mem_w_kv, mem_w_o, moe_w_r, moe_b_r, moe_w1, moe_b1, moe_w2, moe_b2, ln_g, ln_b):
    w = _prep_weights(gla_w_in, gla_w_g2, gla_b_g, gla_gn_g, gla_w_o, kv_w, swa_w_q, swa_sinks, swa_w_o,
                      mem_w_q, mem_w_kv, mem_w_o, moe_w_r, moe_b_r, moe_w1, moe_b1, moe_w2, moe_b2, ln_g, ln_b)
    bp = x_prompt.shape[0]
    bs = x_sample.shape[0]
    mem_flat = mem_prompt.reshape(bp * N_MEM, D_MODEL)
    mem_k_p, mem_v_p = _linear_kv(mem_flat, w['mem_w_kv'])
    gla0 = jnp.zeros((N_A, bp, GLA_HEADS, GLA_DK, GLA_DV), F32)
    zero_win = jnp.zeros((bp, WINDOW, SWA_KV), F32)
    y_p, gla_p, k_p, v_p = _trunk(x_prompt, gla0, zero_win, zero_win, False,
                                  mem_k_p.reshape(DEPTH * bp, N_MEM, D_MODEL),
                                  mem_v_p.reshape(DEPTH * bp, N_MEM, D_MODEL), w)
    y_s, gla_s, k_s, v_s = _trunk(x_sample, state_gla, cache_swa_k.reshape(bs, WINDOW, SWA_KV),
                                  cache_swa_v.reshape(bs, WINDOW, SWA_KV), True,
                                  cache_mem_k.reshape(DEPTH * bs, N_MEM, D_MODEL),
                                  cache_mem_v.reshape(DEPTH * bs, N_MEM, D_MODEL), w)
    heads4 = lambda t: t.reshape(t.shape[0], WINDOW, SWA_KV_HEADS, SWA_HEAD_DIM)
    mem5 = lambda t: t.reshape(DEPTH, bp, N_MEM, MEM_HEADS, MEM_HEAD_DIM)
    return (y_p, y_s, gla_p, gla_s, heads4(k_p), heads4(v_p), heads4(k_s), heads4(v_s), mem5(mem_k_p), mem5(mem_v_p))
```
